```python
import math
import jax
import jax.numpy as jnp
from jax import lax
import numpy as np

D_MODEL = 1024
BATCH = 2
SEQ = 16384
DEPTH = 4

N_MIXERS = 3
NORM_EPS = 1e-6

GM_CHUNK = 128
GM_WIDTH = D_MODEL
GM_GROUPS = 8
GM_GROUP_DIM = GM_WIDTH // GM_GROUPS

DN_QK_HEADS = 4
DN_V_HEADS = 8
DN_HEAD_DIM = D_MODEL // DN_V_HEADS
DN_CONV = 4
DN_CHUNK = 64
DN_KEY_WIDTH = DN_QK_HEADS * DN_HEAD_DIM
DN_VAL_WIDTH = DN_V_HEADS * DN_HEAD_DIM
DN_CONV_WIDTH = 2 * DN_KEY_WIDTH + DN_VAL_WIDTH
DN_PROJ_WIDTH = DN_CONV_WIDTH + DN_VAL_WIDTH + 2 * DN_V_HEADS

SWA_HEAD_DIM = 64
SWA_Q_HEADS = D_MODEL // SWA_HEAD_DIM
SWA_KV_HEADS = 4
SWA_GROUP = SWA_Q_HEADS // SWA_KV_HEADS
SWA_WINDOW = 128
SWA_BLOCK = 128
SWA_PROJ_WIDTH = (SWA_Q_HEADS + 2 * SWA_KV_HEADS) * SWA_HEAD_DIM

MOE_GROUPS = 4
MOE_EXPERTS_PER_GROUP = 8
MOE_EXPERTS = MOE_GROUPS * MOE_EXPERTS_PER_GROUP
MOE_TOP_K = 2
MOE_FF = D_MODEL // 2
MOE_BLOCK = 128

kernel_name = 'hybrid_gmlp_deltanet_swa_hmoe'


def rms_norm(x, gain):
    xf = x.astype(jnp.float32)
    y = xf * lax.rsqrt(jnp.mean(xf * xf, axis=-1, keepdims=True) + NORM_EPS)
    return (y * gain.astype(jnp.float32)).astype(x.dtype)


def l2_norm(x):
    xf = x.astype(jnp.float32)
    return xf * lax.rsqrt(jnp.sum(xf * xf, axis=-1, keepdims=True) + NORM_EPS)


def causal_depthwise_conv(x, w):
    s = x.shape[1]
    taps = w.shape[0]
    xp = jnp.pad(x, ((0, 0), (taps - 1, 0), (0, 0)))
    return sum(xp[:, j:j + s] * w[j] for j in range(taps))


def gmlp_mixer(h, w_in, v_norm, w_s, b_s, w_out):
    b, s, _ = h.shape
    z = jax.nn.gelu(h @ w_in)
    u, v = jnp.split(z, 2, axis=-1)
    v = rms_norm(v, v_norm)
    v = v.reshape(b, s // GM_CHUNK, GM_CHUNK, GM_GROUPS, GM_GROUP_DIM)
    causal = jnp.tril(jnp.ones((GM_CHUNK, GM_CHUNK), dtype=bool))
    w_causal = jnp.where(causal, w_s, 0).astype(v.dtype)
    sv = jnp.einsum('gts,bcsgd->bctgd', w_causal, v) + b_s.T[:, :, None]
    return (u * sv.reshape(b, s, GM_WIDTH)) @ w_out


def _to_chunks(t):
    b, s, h, d = t.shape
    return t.reshape(b, s // DN_CHUNK, DN_CHUNK, h, d).transpose(0, 3, 1, 2, 4)


def chunk_gated_delta_rule(q, k, v, g, beta):
    b, s, h, dk = q.shape
    dv = v.shape[-1]
    n = s // DN_CHUNK
    q = _to_chunks(q) * dk ** -0.5
    k = _to_chunks(k)
    v = _to_chunks(v)
    beta = beta.reshape(b, n, DN_CHUNK, h).transpose(0, 3, 1, 2)
    gc = jnp.cumsum(g.reshape(b, n, DN_CHUNK, h).transpose(0, 3, 1, 2), axis=-1)
    lower = jnp.tril(jnp.ones((DN_CHUNK, DN_CHUNK), dtype=bool))
    strict = jnp.tril(jnp.ones((DN_CHUNK, DN_CHUNK), dtype=bool), k=-1)
    diff = gc[..., :, None] - gc[..., None, :]
    decay = jnp.where(lower, jnp.exp(jnp.where(lower, diff, 0.0)), 0.0)
    k_beta = k * beta[..., None]
    v_beta = v * beta[..., None]
    kk = jnp.einsum('bhncd,bhnsd->bhncs', k_beta, k) * decay
    tri = jnp.where(strict, kk, 0.0) + jnp.eye(DN_CHUNK, dtype=jnp.float32)
    u = lax.linalg.triangular_solve(tri, v_beta, left_side=True, lower=True, unit_diagonal=True)
    w = lax.linalg.triangular_solve(tri, k_beta * jnp.exp(gc)[..., None],
                                    left_side=True, lower=True, unit_diagonal=True)
    qk = jnp.einsum('bhncd,bhnsd->bhncs', q, k) * decay
    q_dec = q * jnp.exp(gc)[..., None]
    k_dec = k * jnp.exp(gc[..., -1:] - gc)[..., None]
    g_last = jnp.exp(gc[..., -1])

    def step(state, inp):
        u_n, w_n, qk_n, qd_n, kd_n, gl_n = inp
        v_new = u_n - jnp.einsum('bhcd,bhde->bhce', w_n, state)
        o_n = jnp.einsum('bhcd,bhde->bhce', qd_n, state) + jnp.einsum('bhcs,bhse->bhce', qk_n, v_new)
        state = state * gl_n[..., None, None] + jnp.einsum('bhcd,bhce->bhde', kd_n, v_new)
        return state, o_n

    xs = tuple(jnp.moveaxis(t, 2, 0) for t in (u, w, qk, q_dec, k_dec, g_last))
    state0 = jnp.zeros((b, h, dk, dv), jnp.float32)
    _, o = lax.scan(step, state0, xs)
    return o.transpose(1, 0, 3, 2, 4).reshape(b, s, h, dv)


def deltanet_mixer(h, w_in, conv_w, a_log, dt_bias, o_norm, w_out):
    b, s, _ = h.shape
    proj = h @ w_in
    split_at = [DN_CONV_WIDTH, DN_CONV_WIDTH + DN_VAL_WIDTH, DN_CONV_WIDTH + DN_VAL_WIDTH + DN_V_HEADS]
    qkv, z, beta_in, alpha_in = jnp.split(proj, split_at, axis=-1)
    qkv = jax.nn.silu(causal_depthwise_conv(qkv, conv_w))
    q, k, v = jnp.split(qkv, [DN_KEY_WIDTH, 2 * DN_KEY_WIDTH], axis=-1)
    rep = DN_V_HEADS // DN_QK_HEADS
    q = jnp.repeat(l2_norm(q.reshape(b, s, DN_QK_HEADS, DN_HEAD_DIM)), rep, axis=2)
    k = jnp.repeat(l2_norm(k.reshape(b, s, DN_QK_HEADS, DN_HEAD_DIM)), rep, axis=2)
    v = v.reshape(b, s, DN_V_HEADS, DN_HEAD_DIM).astype(jnp.float32)
    beta = jax.nn.sigmoid(beta_in.astype(jnp.float32))
    g = -jnp.exp(a_log.astype(jnp.float32)) * jax.nn.softplus(
        alpha_in.astype(jnp.float32) + dt_bias.astype(jnp.float32))
    o = chunk_gated_delta_rule(q, k, v, g, beta)
    o = rms_norm(o, o_norm) * jax.nn.silu(z.reshape(b, s, DN_V_HEADS, DN_HEAD_DIM).astype(jnp.float32))
    return o.reshape(b, s, DN_VAL_WIDTH).astype(h.dtype) @ w_out


def swa_mixer(h, w_in, q_norm, k_norm, sinks, w_out):
    b, s, _ = h.shape
    nb = s // SWA_BLOCK
    hd = SWA_HEAD_DIM
    proj = h @ w_in
    q, k, v = jnp.split(proj, [SWA_Q_HEADS * hd, (SWA_Q_HEADS + SWA_KV_HEADS) * hd], axis=-1)
    q = rms_norm(q.reshape(b, s, SWA_Q_HEADS, hd), q_norm).astype(jnp.float32) * hd ** -0.5
    k = rms_norm(k.reshape(b, s, SWA_KV_HEADS, hd), k_norm).astype(jnp.float32)
    v = v.reshape(b, s, SWA_KV_HEADS, hd).astype(jnp.float32)
    q = q.reshape(b, nb, SWA_BLOCK, SWA_KV_HEADS, SWA_GROUP, hd)
    k = k.reshape(b, nb, SWA_BLOCK, SWA_KV_HEADS, hd)
    v = v.reshape(b, nb, SWA_BLOCK, SWA_KV_HEADS, hd)

    def with_prev(t):
        prev = jnp.concatenate([jnp.zeros_like(t[:, :1]), t[:, :-1]], axis=1)
        return jnp.concatenate([prev, t], axis=2)

    kk, vv = with_prev(k), with_prev(v)
    scores = jnp.einsum('bnqhgd,bnkhd->bnhgqk', q, kk)
    qi = jnp.arange(SWA_BLOCK)[:, None]
    kj = jnp.arange(2 * SWA_BLOCK)[None, :]
    offset = SWA_BLOCK + qi - kj
    band = (offset >= 0) & (offset < SWA_WINDOW)
    has_key = (jnp.arange(nb) > 0)[:, None, None] | (kj >= SWA_BLOCK)[None]
    mask = band[None] & has_key
    scores = jnp.where(mask[None, :, None, None], scores, -jnp.inf)
    sink = sinks.astype(jnp.float32).reshape(SWA_KV_HEADS, SWA_GROUP)[None, None, :, :, None, None]
    m = jnp.maximum(scores.max(axis=-1, keepdims=True), sink)
    p = jnp.exp(scores - m)
    denom = p.sum(axis=-1, keepdims=True) + jnp.exp(sink - m)
    out = jnp.einsum('bnhgqk,bnkhd->bnqhgd', p / denom, vv)
    return out.reshape(b, s, SWA_Q_HEADS * hd).astype(h.dtype) @ w_out


def hierarchical_moe(h, router_w, router_b, w_in, w_out):
    b, s, d = h.shape
    t = b * s
    xt = h.reshape(t, d)
    logits = (xt @ router_w + router_b).astype(jnp.float32)
    p_group = jax.nn.softmax(logits[:, :MOE_GROUPS], axis=-1)
    pg_top, g_sel = lax.top_k(p_group, 1)
    exp_logits = logits[:, MOE_GROUPS:].reshape(t, MOE_GROUPS, MOE_EXPERTS_PER_GROUP)
    sel_logits = jnp.take_along_axis(exp_logits, g_sel[:, :, None], axis=1)[:, 0]
    p_exp = jax.nn.softmax(sel_logits, axis=-1)
    pe_top, e_sel = lax.top_k(p_exp, MOE_TOP_K)
    gates = pg_top * pe_top / pe_top.sum(axis=-1, keepdims=True)
    expert_id = g_sel * MOE_EXPERTS_PER_GROUP + e_sel

    n_assign = t * MOE_TOP_K
    flat_e = expert_id.reshape(n_assign)
    flat_tok = jnp.repeat(jnp.arange(t, dtype=jnp.int32), MOE_TOP_K)
    flat_gate = gates.reshape(n_assign)
    order = jnp.argsort(flat_e)
    e_sorted = flat_e[order]
    tok_sorted = flat_tok[order]
    gate_sorted = flat_gate[order]
    counts = jnp.bincount(flat_e, length=MOE_EXPERTS)
    padded = (counts + MOE_BLOCK - 1) // MOE_BLOCK * MOE_BLOCK
    pad_end = jnp.cumsum(padded)
    pad_start = pad_end - padded
    seg_start = jnp.cumsum(counts) - counts
    dest = pad_start[e_sorted] + jnp.arange(n_assign) - seg_start[e_sorted]
    n_blocks = n_assign // MOE_BLOCK + MOE_EXPERTS
    slot_tok = jnp.full((n_blocks * MOE_BLOCK,), t, dtype=jnp.int32).at[dest].set(tok_sorted)
    x_pad = jnp.concatenate([xt, jnp.zeros((1, d), xt.dtype)], axis=0)
    x_blocks = x_pad[slot_tok].reshape(n_blocks, MOE_BLOCK, d)
    block_e = jnp.minimum(jnp.searchsorted(pad_end, jnp.arange(n_blocks) * MOE_BLOCK, side='right'),
                          MOE_EXPERTS - 1)

    def expert_block(args):
        xb, e = args
        a, gl = jnp.split(xb @ w_in[e], 2, axis=-1)
        return (jax.nn.silu(a) * gl) @ w_out[e]

    y_blocks = lax.map(expert_block, (x_blocks, block_e))
    y_slots = y_blocks.reshape(n_blocks * MOE_BLOCK, d)
    contrib = y_slots[dest] * gate_sorted[:, None].astype(h.dtype)
    out = jax.ops.segment_sum(contrib, tok_sorted, num_segments=t)
    return out.reshape(b, s, d)


def setup_inputs(seed: int = 0) -> dict:
    key = jax.random.key(seed)
    keys = iter(jax.random.split(key, 128))

    def normal(shape, scale):
        return jax.random.normal(next(keys), shape, jnp.float32) * scale

    def gain(n):
        return 1.0 + 0.05 * normal((n,), 1.0)

    p = {}
    p['x'] = normal((BATCH, SEQ, D_MODEL), 1.0)
    p['c'] = normal((BATCH, D_MODEL), 1.0)
    for i in range(DEPTH):
        pre = 'l%d_' % i
        p[pre + 'norm_mix'] = gain(D_MODEL)
        p[pre + 'norm_ffn'] = gain(D_MODEL)
        p[pre + 'ada_w'] = normal((D_MODEL, 6 * D_MODEL), 0.5 * D_MODEL ** -0.5)
        p[pre + 'ada_b'] = normal((6 * D_MODEL,), 0.02)
        kind = i % N_MIXERS
        if kind == 0:
            p[pre + 'gm_w_in'] = normal((D_MODEL, 2 * GM_WIDTH), D_MODEL ** -0.5)
            p[pre + 'gm_v_norm'] = gain(GM_WIDTH)
            p[pre + 'gm_w_s'] = normal((GM_GROUPS, GM_CHUNK, GM_CHUNK), GM_CHUNK ** -0.5)
            p[pre + 'gm_b_s'] = 1.0 + normal((GM_GROUPS, GM_CHUNK), 0.1)
            p[pre + 'gm_w_out'] = normal((GM_WIDTH, D_MODEL), GM_WIDTH ** -0.5)
        elif kind == 1:
            p[pre + 'dn_w_in'] = normal((D_MODEL, DN_PROJ_WIDTH), D_MODEL ** -0.5)
            p[pre + 'dn_conv_w'] = normal((DN_CONV, DN_CONV_WIDTH), DN_CONV ** -0.5)
            p[pre + 'dn_a_log'] = jnp.log(jax.random.uniform(next(keys), (DN_V_HEADS,), jnp.float32, 1.0, 16.0))
            dt = jnp.exp(jax.random.uniform(next(keys), (DN_V_HEADS,), jnp.float32,
                                            math.log(1e-3), math.log(1e-1)))
            p[pre + 'dn_dt_bias'] = dt + jnp.log(-jnp.expm1(-dt))
            p[pre + 'dn_o_norm'] = gain(DN_HEAD_DIM)
            p[pre + 'dn_w_out'] = normal((DN_VAL_WIDTH, D_MODEL), DN_VAL_WIDTH ** -0.5)
        else:
            p[pre + 'swa_w_in'] = normal((D_MODEL, SWA_PROJ_WIDTH), D_MODEL ** -0.5)
            p[pre + 'swa_q_norm'] = gain(SWA_HEAD_DIM)
            p[pre + 'swa_k_norm'] = gain(SWA_HEAD_DIM)
            p[pre + 'swa_sinks'] = normal((SWA_Q_HEADS,), 1.0)
            p[pre + 'swa_w_out'] = normal((SWA_Q_HEADS * SWA_HEAD_DIM, D_MODEL), (SWA_Q_HEADS * SWA_HEAD_DIM) ** -0.5)
        p[pre + 'router_w'] = normal((D_MODEL, MOE_GROUPS + MOE_EXPERTS), D_MODEL ** -0.5)
        p[pre + 'router_b'] = normal((MOE_GROUPS + MOE_EXPERTS,), 0.01)
        p[pre + 'expert_w_in'] = normal((MOE_EXPERTS, D_MODEL, 2 * MOE_FF), D_MODEL ** -0.5)
        p[pre + 'expert_w_out'] = normal((MOE_EXPERTS, MOE_FF, D_MODEL), MOE_FF ** -0.5)
    return p


def reference(x, c,
              l0_norm_mix, l0_norm_ffn, l0_ada_w, l0_ada_b,
              l0_gm_w_in, l0_gm_v_norm, l0_gm_w_s, l0_gm_b_s, l0_gm_w_out,
              l0_router_w, l0_router_b, l0_expert_w_in, l0_expert_w_out,
              l1_norm_mix, l1_norm_ffn, l1_ada_w, l1_ada_b,
              l1_dn_w_in, l1_dn_conv_w, l1_dn_a_log, l1_dn_dt_bias, l1_dn_o_norm, l1_dn_w_out,
              l1_router_w, l1_router_b, l1_expert_w_in, l1_expert_w_out,
              l2_norm_mix, l2_norm_ffn, l2_ada_w, l2_ada_b,
              l2_swa_w_in, l2_swa_q_norm, l2_swa_k_norm, l2_swa_sinks, l2_swa_w_out,
              l2_router_w, l2_router_b, l2_expert_w_in, l2_expert_w_out,
              l3_norm_mix, l3_norm_ffn, l3_ada_w, l3_ada_b,
              l3_gm_w_in, l3_gm_v_norm, l3_gm_w_s, l3_gm_b_s, l3_gm_w_out,
              l3_router_w, l3_router_b, l3_expert_w_in, l3_expert_w_out):
    layers = [
        ((l0_norm_mix, l0_norm_ffn, l0_ada_w, l0_ada_b),
         (l0_gm_w_in, l0_gm_v_norm, l0_gm_w_s, l0_gm_b_s, l0_gm_w_out),
         (l0_router_w, l0_router_b, l0_expert_w_in, l0_expert_w_out)),
        ((l1_norm_mix, l1_norm_ffn, l1_ada_w, l1_ada_b),
         (l1_dn_w_in, l1_dn_conv_w, l1_dn_a_log, l1_dn_dt_bias, l1_dn_o_norm, l1_dn_w_out),
         (l1_router_w, l1_router_b, l1_expert_w_in, l1_expert_w_out)),
        ((l2_norm_mix, l2_norm_ffn, l2_ada_w, l2_ada_b),
         (l2_swa_w_in, l2_swa_q_norm, l2_swa_k_norm, l2_swa_sinks, l2_swa_w_out),
         (l2_router_w, l2_router_b, l2_expert_w_in, l2_expert_w_out)),
        ((l3_norm_mix, l3_norm_ffn, l3_ada_w, l3_ada_b),
         (l3_gm_w_in, l3_gm_v_norm, l3_gm_w_s, l3_gm_b_s, l3_gm_w_out),
         (l3_router_w, l3_router_b, l3_expert_w_in, l3_expert_w_out)),
    ]
    mixers = (gmlp_mixer, deltanet_mixer, swa_mixer)
    c_act = jax.nn.silu(c)
    for i in range(DEPTH):
        (norm_mix, norm_ffn, ada_w, ada_b), mixer_params, moe_params = layers[i]
        mod = (c_act @ ada_w + ada_b)[:, None, :]
        shift_m, scale_m, gate_m, shift_f, scale_f, gate_f = jnp.split(mod, 6, axis=-1)
        h = rms_norm(x, norm_mix) * (1 + scale_m) + shift_m
        x = x + gate_m * mixers[i % N_MIXERS](h, *mixer_params)
        h = rms_norm(x, norm_ffn) * (1 + scale_f) + shift_f
        x = x + gate_f * hierarchical_moe(h, *moe_params)
    return x
```

```python
import functools
import math

import jax
import jax.numpy as jnp
from jax import lax
from jax.experimental import pallas as pl
from jax.experimental.pallas import tpu as pltpu

F32 = jnp.float32
BF16 = jnp.bfloat16
HIGHEST = lax.Precision.HIGHEST

D_MODEL = 1024
NORM_EPS = 1e-6

GM_CHUNK = 128
GM_GROUPS = 8
GM_GROUP_DIM = D_MODEL // GM_GROUPS

MOE_GROUPS = 4
MOE_EXPERTS_PER_GROUP = 8
MOE_EXPERTS = MOE_GROUPS * MOE_EXPERTS_PER_GROUP
MOE_FF = D_MODEL // 2

VMEM_LIMIT_BYTES = 56 * 1024 * 1024

TOKEN_TILE = 512
ROUTER_TILE = 512
MOVE_TILE = 256
EXPERT_ROWS = 256


def _params(*semantics):
    return pltpu.CompilerParams(dimension_semantics=semantics, vmem_limit_bytes=VMEM_LIMIT_BYTES,
                                disable_bounds_checks=True)


def _rms(xf):
    return xf * lax.rsqrt(jnp.mean(xf * xf, axis=-1, keepdims=True) + NORM_EPS)


def _modulated_norm(x, gain, shift, scale):
    return _rms(x) * gain * (1.0 + scale) + shift


def _gelu_tanh(x):
    return 0.5 * x * (1.0 + jnp.tanh(math.sqrt(2.0 / math.pi) * (x + 0.044715 * (x * x * x))))


def _silu(x):
    return x * (1.0 / (1.0 + jnp.exp(-x)))


def _adaln_kernel(c_ref, w_ref, b_ref, o_ref):
    o_ref[...] = jnp.dot(_silu(c_ref[...]), w_ref[...], precision=HIGHEST,
                         preferred_element_type=F32) + b_ref[...]


def adaln(c_pad, ada_w, ada_b):
    rows, d = c_pad.shape
    n = ada_w.shape[1]
    tn = d
    return pl.pallas_call(
        _adaln_kernel,
        grid=(n // tn,),
        in_specs=[pl.BlockSpec((rows, d), lambda j: (0, 0)),
                  pl.BlockSpec((d, tn), lambda j: (0, j)),
                  pl.BlockSpec((1, tn), lambda j: (0, j))],
        out_specs=pl.BlockSpec((rows, tn), lambda j: (0, j)),
        out_shape=jax.ShapeDtypeStruct((rows, n), F32),
        compiler_params=_params("arbitrary"),
        name="adaln",
    )(c_pad, ada_w, ada_b.reshape(1, n))


def _gmlp_kernel(x_ref, mod_ref, gain_ref, win_ref, vn_ref, ws_ref, bs_ref, wout_ref, o_ref,
                 u_ref, v_ref, g_ref):
    width = D_MODEL
    x = x_ref[...]
    mod = mod_ref[0]
    h = _modulated_norm(x, gain_ref[...], mod[0:1], mod[1:2])
    z = _gelu_tanh(jnp.dot(h.astype(BF16), win_ref[...], preferred_element_type=F32))
    u_ref[...] = z[:, :width]
    v_ref[...] = (_rms(z[:, width:]) * vn_ref[...]).astype(BF16)
    for c in range(x.shape[0] // GM_CHUNK):
        rows = slice(c * GM_CHUNK, (c + 1) * GM_CHUNK)
        for g in range(GM_GROUPS):
            cols = slice(g * GM_GROUP_DIM, (g + 1) * GM_GROUP_DIM)
            sv = jnp.dot(ws_ref[g], v_ref[rows, cols], preferred_element_type=F32) + bs_ref[:, g:g + 1]
            g_ref[rows, cols] = (u_ref[rows, cols] * sv).astype(BF16)
    y = jnp.dot(g_ref[...], wout_ref[...], preferred_element_type=F32)
    o_ref[...] = x + mod[2:3] * y


def gmlp_layer(x2, mod, gain, w_in, v_norm, w_s, b_s, w_out, tiles_per_batch):
    t, d = x2.shape
    tm = TOKEN_TILE
    ws_causal = jnp.where(jnp.tril(jnp.ones((GM_CHUNK, GM_CHUNK), dtype=bool)), w_s, 0).astype(BF16)
    const2 = lambda i: (0, 0)
    return pl.pallas_call(
        _gmlp_kernel,
        grid=(t // tm,),
        in_specs=[pl.BlockSpec((tm, d), lambda i: (i, 0)),
                  pl.BlockSpec((1, 6, d), lambda i: (i // tiles_per_batch, 0, 0)),
                  pl.BlockSpec((1, d), const2),
                  pl.BlockSpec((d, 2 * d), const2),
                  pl.BlockSpec((1, d), const2),
                  pl.BlockSpec((GM_GROUPS, GM_CHUNK, GM_CHUNK), lambda i: (0, 0, 0)),
                  pl.BlockSpec((GM_CHUNK, GM_GROUPS), const2),
                  pl.BlockSpec((d, d), const2)],
        out_specs=pl.BlockSpec((tm, d), lambda i: (i, 0)),
        out_shape=jax.ShapeDtypeStruct((t, d), F32),
        scratch_shapes=[pltpu.VMEM((tm, d), F32), pltpu.VMEM((tm, d), BF16), pltpu.VMEM((tm, d), BF16)],
        compiler_params=_params("arbitrary"),
        name="gmlp_mixer",
    )(x2, mod, gain.reshape(1, d), w_in.astype(BF16), v_norm.reshape(1, d), ws_causal, b_s.T,
      w_out.astype(BF16))


def _router_kernel(x_ref, mod_ref, gain_ref, rw_ref, rb_ref, eid_ref, gate_ref, rank_ref, cnt_ref, base_ref):
    i = pl.program_id(0)
    tm = x_ref.shape[0]
    ne = MOE_EXPERTS
    npg = MOE_EXPERTS_PER_GROUP

    @pl.when(i == 0)
    def _():
        base_ref[...] = jnp.zeros_like(base_ref)

    mod = mod_ref[0]
    h = _modulated_norm(x_ref[...], gain_ref[...], mod[3:4], mod[4:5])
    lt = lax.dot_general(rw_ref[...], h, (((1,), (1,)), ((), ())), precision=HIGHEST,
                         preferred_element_type=F32) + rb_ref[...]
    lg = [lt[ne + g:ne + g + 1, :] for g in range(MOE_GROUPS)]
    gmax = functools.reduce(jnp.maximum, lg)
    gsum = functools.reduce(lambda a, b: a + b, [jnp.exp(l - gmax) for l in lg])
    pg_top = 1.0 / gsum
    g_sel = jnp.full(gmax.shape, MOE_GROUPS - 1, jnp.int32)
    for g in range(MOE_GROUPS - 2, -1, -1):
        g_sel = jnp.where(lg[g] == gmax, g, g_sel)
    sel = lt[(MOE_GROUPS - 1) * npg:MOE_GROUPS * npg, :]
    for g in range(MOE_GROUPS - 2, -1, -1):
        sel = jnp.where(g_sel == g, lt[g * npg:(g + 1) * npg, :], sel)
    row = lax.broadcasted_iota(jnp.int32, sel.shape, 0)
    m1 = jnp.max(sel, axis=0, keepdims=True)
    i1 = jnp.min(jnp.where(sel == m1, row, npg), axis=0, keepdims=True)
    rest = jnp.where(row == i1, -jnp.inf, sel)
    m2 = jnp.max(rest, axis=0, keepdims=True)
    i2 = jnp.min(jnp.where(rest == m2, row, npg), axis=0, keepdims=True)
    e2 = jnp.exp(m2 - m1)
    inv = pg_top / (1.0 + e2)
    eid = jnp.concatenate([g_sel * npg + i1, g_sel * npg + i2], axis=0)
    eid_ref[...] = eid
    gate_ref[...] = jnp.concatenate([inv, inv * e2], axis=0)

    erow = lax.broadcasted_iota(jnp.int32, (ne, tm), 0)
    before = (lax.broadcasted_iota(jnp.int32, (tm, tm), 0)
              < lax.broadcasted_iota(jnp.int32, (tm, tm), 1)).astype(BF16)
    base = base_ref[...]
    ranks = []
    for k in range(2):
        hit = erow == eid[k:k + 1, :]
        prefix = jnp.dot(hit.astype(BF16), before, preferred_element_type=F32) + base
        ranks.append(jnp.sum(jnp.where(hit, prefix, 0.0), axis=0, keepdims=True))
        base = base + jnp.sum(hit.astype(F32), axis=1, keepdims=True)
    rank_ref[...] = jnp.concatenate(ranks, axis=0).astype(jnp.int32)
    base_ref[...] = base
    cnt_ref[...] = jnp.broadcast_to(base, cnt_ref.shape).astype(jnp.int32)


def moe_router(x2, mod, gain, router_w, router_b, tiles_per_batch_of):
    t, d = x2.shape
    tm = ROUTER_TILE
    ne = MOE_EXPERTS
    nr = ne + 8
    rw = jnp.concatenate([router_w[:, MOE_GROUPS:], router_w[:, :MOE_GROUPS],
                          jnp.zeros((d, nr - ne - MOE_GROUPS), F32)], axis=1).T
    rb = jnp.concatenate([router_b[MOE_GROUPS:], router_b[:MOE_GROUPS],
                          jnp.zeros((nr - ne - MOE_GROUPS,), F32)]).reshape(nr, 1)
    tpb = tiles_per_batch_of(tm)
    const2 = lambda i: (0, 0)
    slot_spec = pl.BlockSpec((2, tm), lambda i: (0, i))
    return pl.pallas_call(
        _router_kernel,
        grid=(t // tm,),
        in_specs=[pl.BlockSpec((tm, d), lambda i: (i, 0)),
                  pl.BlockSpec((1, 6, d), lambda i: (i // tpb, 0, 0)),
                  pl.BlockSpec((1, d), const2),
                  pl.BlockSpec((nr, d), const2),
                  pl.BlockSpec((nr, 1), const2)],
        out_specs=[slot_spec, slot_spec, slot_spec, pl.BlockSpec((ne, 128), const2)],
        out_shape=[jax.ShapeDtypeStruct((2, t), jnp.int32), jax.ShapeDtypeStruct((2, t), F32),
                   jax.ShapeDtypeStruct((2, t), jnp.int32), jax.ShapeDtypeStruct((ne, 128), jnp.int32)],
        scratch_shapes=[pltpu.VMEM((ne, 1), F32)],
        compiler_params=_params("arbitrary"),
        name="moe_router",
    )(x2, mod, gain.reshape(1, d), rw, rb)


def _dispatch_kernel(dest_ref, x_ref, mod_ref, gain_ref, xs_ref, h_ref, sem):
    tm = x_ref.shape[0]
    mod = mod_ref[0]
    h_ref[...] = _modulated_norm(x_ref[...], gain_ref[...], mod[3:4], mod[4:5])

    def issue(r, carry):
        for k in range(2):
            pltpu.make_async_copy(h_ref.at[pl.ds(r, 1)], xs_ref.at[pl.ds(dest_ref[k, r], 1)], sem).start()
        return carry

    lax.fori_loop(0, tm, issue, 0, unroll=8)
    for k in range(2):
        pltpu.make_async_copy(h_ref, xs_ref.at[pl.ds(0, tm)], sem).wait()


def moe_dispatch(x2, mod, gain, dest, tiles_per_batch_of):
    t, d = x2.shape
    tm = MOVE_TILE
    tpb = tiles_per_batch_of(tm)
    const2 = lambda i: (0, 0)
    return pl.pallas_call(
        _dispatch_kernel,
        grid=(t // tm,),
        in_specs=[pl.BlockSpec((2, tm), lambda i: (0, i), memory_space=pltpu.SMEM),
                  pl.BlockSpec((tm, d), lambda i: (i, 0)),
                  pl.BlockSpec((1, 6, d), lambda i: (i // tpb, 0, 0)),
                  pl.BlockSpec((1, d), const2)],
        out_specs=pl.BlockSpec(memory_space=pl.ANY),
        out_shape=jax.ShapeDtypeStruct((2 * t, d), F32),
        scratch_shapes=[pltpu.VMEM((tm, d), F32), pltpu.SemaphoreType.DMA(())],
        compiler_params=_params("arbitrary"),
        name="moe_dispatch",
    )(dest, x2, mod, gain.reshape(1, d))


def _expert_kernel(blk_ref, exp_ref, start_ref, nvalid_ref, xs_ref, win_ref, wout_ref, ys_ref,
                   winb_ref, woutb_ref):
    i = pl.program_id(0)
    bm = xs_ref.shape[0]
    e = exp_ref[i]
    blk = blk_ref[i]
    prev = jnp.maximum(i - 1, 0)
    first = i == 0
    lo = start_ref[e]
    hi = start_ref[e + 1]
    row0 = blk * bm
    live = i < nvalid_ref[0]

    @pl.when(jnp.logical_or(first, exp_ref[prev] != e))
    def _():
        winb_ref[...] = win_ref[0].astype(BF16)
        woutb_ref[...] = wout_ref[0].astype(BF16)

    new_block = jnp.logical_or(first, blk_ref[prev] != blk)

    @pl.when(new_block)
    def _():
        ys_ref[...] = jnp.zeros_like(ys_ref)

    @pl.when(live)
    def _():
        a_gl = jnp.dot(xs_ref[...].astype(BF16), winb_ref[...], preferred_element_type=F32)
        mid = (_silu(a_gl[:, :MOE_FF]) * a_gl[:, MOE_FF:]).astype(BF16)
        y = jnp.dot(mid, woutb_ref[...], preferred_element_type=F32)
        rows = row0 + lax.broadcasted_iota(jnp.int32, (bm, 1), 0)
        mine = jnp.logical_and(rows >= lo, rows < hi)
        ys_ref[...] += jnp.where(mine, y, 0.0)


def moe_experts(xs, w_in, w_out, item_block, item_expert, seg_start, n_valid):
    n, d = xs.shape
    bm = EXPERT_ROWS
    n_items = item_block.shape[0]
    grid_spec = pltpu.PrefetchScalarGridSpec(
        num_scalar_prefetch=4,
        grid=(n_items,),
        in_specs=[pl.BlockSpec((bm, d), lambda i, b, e, s, v: (b[i], 0)),
                  pl.BlockSpec((1, d, 2 * MOE_FF), lambda i, b, e, s, v: (e[i], 0, 0)),
                  pl.BlockSpec((1, MOE_FF, d), lambda i, b, e, s, v: (e[i], 0, 0))],
        out_specs=pl.BlockSpec((bm, d), lambda i, b, e, s, v: (b[i], 0)),
        scratch_shapes=[pltpu.VMEM((d, 2 * MOE_FF), BF16), pltpu.VMEM((MOE_FF, d), BF16)],
    )
    return pl.pallas_call(
        _expert_kernel,
        grid_spec=grid_spec,
        out_shape=jax.ShapeDtypeStruct((n, d), F32),
        compiler_params=_params("arbitrary"),
        name="moe_experts",
    )(item_block, item_expert, seg_start, n_valid, xs, w_in, w_out)


def _combine_kernel(dest_ref, x_ref, mod_ref, gate_ref, ys_ref, o_ref, y_ref, sem):
    tm = x_ref.shape[0]

    def issue(r, carry):
        for k in range(2):
            pltpu.make_async_copy(ys_ref.at[pl.ds(dest_ref[k, r], 1)], y_ref.at[k, pl.ds(r, 1)], sem).start()
        return carry

    lax.fori_loop(0, tm, issue, 0, unroll=8)
    for k in range(2):
        pltpu.make_async_copy(ys_ref.at[pl.ds(0, tm)], y_ref.at[k], sem).wait()
    gates = gate_ref[...]
    moe = gates[:, 0:1] * y_ref[0] + gates[:, 1:2] * y_ref[1]
    o_ref[...] = x_ref[...] + mod_ref[0][5:6] * moe


def moe_combine(x2, mod, gates_t, dest, ys, tiles_per_batch_of):
    t, d = x2.shape
    tm = MOVE_TILE
    tpb = tiles_per_batch_of(tm)
    return pl.pallas_call(
        _combine_kernel,
        grid=(t // tm,),
        in_specs=[pl.BlockSpec((2, tm), lambda i: (0, i), memory_space=pltpu.SMEM),
                  pl.BlockSpec((tm, d), lambda i: (i, 0)),
                  pl.BlockSpec((1, 6, d), lambda i: (i // tpb, 0, 0)),
                  pl.BlockSpec((tm, 2), lambda i: (i, 0)),
                  pl.BlockSpec(memory_space=pl.ANY)],
        out_specs=pl.BlockSpec((tm, d), lambda i: (i, 0)),
        out_shape=jax.ShapeDtypeStruct((t, d), F32),
        scratch_shapes=[pltpu.VMEM((2, tm, d), F32), pltpu.SemaphoreType.DMA(())],
        compiler_params=_params("arbitrary"),
        name="moe_combine",
    )(dest, x2, mod, gates_t, ys)


def moe_layer(x2, mod, gain, router_w, router_b, w_in, w_out, tiles_per_batch_of):
    t, d = x2.shape
    bm = EXPERT_ROWS
    n_assign = 2 * t
    eid, gate, rank, cnt = moe_router(x2, mod, gain, router_w, router_b, tiles_per_batch_of)
    counts = cnt[:, 0]
    seg_end = jnp.cumsum(counts)
    seg_start = jnp.concatenate([jnp.zeros((1,), jnp.int32), seg_end]).astype(jnp.int32)
    dest = (seg_start[eid] + rank).astype(jnp.int32)
    n_blocks = n_assign // bm
    n_items = n_blocks + MOE_EXPERTS - 1
    first_blk = seg_start[:-1] // bm
    last_blk = jnp.where(counts > 0, (seg_end - 1) // bm, first_blk - 1)
    per_expert = jnp.maximum(last_blk - first_blk + 1, 0)
    item_end = jnp.cumsum(per_expert)
    item_ids = jnp.arange(n_items, dtype=jnp.int32)
    item_expert = jnp.minimum(jnp.searchsorted(item_end, item_ids, side="right"), MOE_EXPERTS - 1).astype(jnp.int32)
    item_block = first_blk[item_expert] + item_ids - (item_end - per_expert)[item_expert]
    n_valid = item_end[-1]
    last_item = jnp.minimum(item_ids, n_valid - 1)
    item_block = item_block[last_item].astype(jnp.int32)
    item_expert = item_expert[last_item].astype(jnp.int32)
    xs = moe_dispatch(x2, mod, gain, dest, tiles_per_batch_of)
    ys = moe_experts(xs, w_in, w_out, item_block, item_expert, seg_start,
                     n_valid.reshape(1).astype(jnp.int32))
    return moe_combine(x2, mod, gate.T, dest, ys, tiles_per_batch_of)


SWA_HEAD_DIM = 64
SWA_Q_HEADS = D_MODEL // SWA_HEAD_DIM
SWA_KV_HEADS = 4
SWA_BLOCK = 128
SWA_Q_WIDTH = SWA_Q_HEADS * SWA_HEAD_DIM
SWA_KV_WIDTH = SWA_KV_HEADS * SWA_HEAD_DIM
LANES = 128
HEADS_PER_VREG = LANES // SWA_HEAD_DIM


def _swa_kernel(sink_ref, x_ref, mod_ref, gain_ref, win_ref, qg_ref, kg_ref, ones_ref, wout_ref, o_ref,
                kx_ref, vx_ref, att_ref):
    t = pl.program_id(1)
    tm = x_ref.shape[0]
    blk = SWA_BLOCK
    hd = SWA_HEAD_DIM

    @pl.when(t == 0)
    def _():
        kx_ref[0:blk, :] = jnp.zeros((blk, kx_ref.shape[1]), BF16)
        vx_ref[0:blk, :] = jnp.zeros((blk, vx_ref.shape[1]), BF16)

    x = x_ref[...]
    mod = mod_ref[0]
    h = _modulated_norm(x, gain_ref[...], mod[0:1], mod[1:2]).astype(BF16)
    proj = jnp.dot(h, win_ref[...], preferred_element_type=F32)

    def head_rms(a):
        ss = jnp.dot((a * a).astype(BF16), ones_ref[...], preferred_element_type=F32)
        return a * lax.rsqrt(ss * (1.0 / hd) + NORM_EPS)

    lane = lax.broadcasted_iota(jnp.int32, (tm, LANES), 1)
    low = lane < hd
    for c in range(SWA_Q_WIDTH // LANES):
        cols = slice(c * LANES, (c + 1) * LANES)
        att_ref[:, cols] = (head_rms(proj[:, cols]) * qg_ref[:, cols]).astype(BF16)
    for c in range(SWA_KV_WIDTH // LANES):
        cols = slice(c * LANES, (c + 1) * LANES)
        kc = head_rms(proj[:, SWA_Q_WIDTH + c * LANES:SWA_Q_WIDTH + (c + 1) * LANES]) * kg_ref[:, cols]
        vc = proj[:, SWA_Q_WIDTH + SWA_KV_WIDTH + c * LANES:SWA_Q_WIDTH + SWA_KV_WIDTH + (c + 1) * LANES]
        for ref, a in ((kx_ref, kc), (vx_ref, vc)):
            even_lo = jnp.where(low, a, 0.0)
            odd_hi = jnp.where(low, 0.0, a)
            j0 = HEADS_PER_VREG * c
            ref[blk:blk + tm, (2 * j0) * LANES:(2 * j0 + 1) * LANES] = even_lo.astype(BF16)
            ref[blk:blk + tm, (2 * j0 + 1) * LANES:(2 * j0 + 2) * LANES] = pltpu.roll(even_lo, hd, 1).astype(BF16)
            ref[blk:blk + tm, (2 * j0 + 2) * LANES:(2 * j0 + 3) * LANES] = pltpu.roll(odd_hi, hd, 1).astype(BF16)
            ref[blk:blk + tm, (2 * j0 + 3) * LANES:(2 * j0 + 4) * LANES] = odd_hi.astype(BF16)

    qi = lax.broadcasted_iota(jnp.int32, (blk, 2 * blk), 0)
    kj = lax.broadcasted_iota(jnp.int32, (blk, 2 * blk), 1)
    lane_q = lax.broadcasted_iota(jnp.int32, (blk, LANES), 1) < hd
    pairs_per_kv = SWA_Q_HEADS // SWA_KV_HEADS // HEADS_PER_VREG
    for i in range(tm // blk):
        rows = slice(i * blk, (i + 1) * blk)
        keys = slice(i * blk, (i + 2) * blk)
        floor = jnp.where(t == 0, blk - 1, qi) if i == 0 else qi
        mask = jnp.logical_and(kj > floor, kj <= qi + blk)
        for p in range(SWA_Q_HEADS // HEADS_PER_VREG):
            j = p // pairs_per_kv
            qp = att_ref[rows, p * LANES:(p + 1) * LANES]
            out = None
            inv = []
            for half in range(HEADS_PER_VREG):
                col = (2 * j + half) * LANES
                sc = lax.dot_general(qp, kx_ref[keys, col:col + LANES], (((1,), (1,)), ((), ())),
                                     preferred_element_type=F32)
                sc = jnp.where(mask, sc, -jnp.inf)
                sink = sink_ref[HEADS_PER_VREG * p + half]
                m = jnp.maximum(jnp.max(sc, axis=-1, keepdims=True), sink)
                pr = jnp.exp(sc - m)
                inv.append(1.0 / (jnp.sum(pr, axis=-1, keepdims=True) + jnp.exp(sink - m)))
                pv = jnp.dot(pr.astype(BF16), vx_ref[keys, col:col + LANES], preferred_element_type=F32)
                out = pv if out is None else out + pv
            att_ref[rows, p * LANES:(p + 1) * LANES] = (out * jnp.where(lane_q, inv[0], inv[1])).astype(BF16)
    kx_ref[0:blk, :] = kx_ref[tm:tm + blk, :]
    vx_ref[0:blk, :] = vx_ref[tm:tm + blk, :]
    y = jnp.dot(att_ref[...], wout_ref[...], preferred_element_type=F32)
    o_ref[...] = x + mod[2:3] * y


def swa_layer(x2, mod, gain, w_in, q_norm, k_norm, sinks, w_out, b, s):
    t, d = x2.shape
    tm = TOKEN_TILE
    tpb = s // tm
    hd = SWA_HEAD_DIM
    qg = (jnp.tile(q_norm, SWA_Q_HEADS) * hd ** -0.5).reshape(1, SWA_Q_WIDTH)
    kg = jnp.tile(k_norm, SWA_KV_HEADS).reshape(1, SWA_KV_WIDTH)
    ids = jnp.arange(LANES) // hd
    ones = (ids[:, None] == ids[None, :]).astype(BF16)
    proj_w = SWA_Q_WIDTH + 2 * SWA_KV_WIDTH
    ext_w = 2 * SWA_KV_HEADS * LANES
    const2 = lambda bi, ti: (0, 0)
    tile = lambda bi, ti: (bi * tpb + ti, 0)
    return pl.pallas_call(
        _swa_kernel,
        grid=(b, tpb),
        in_specs=[pl.BlockSpec(memory_space=pltpu.SMEM),
                  pl.BlockSpec((tm, d), tile),
                  pl.BlockSpec((1, 6, d), lambda bi, ti: (bi, 0, 0)),
                  pl.BlockSpec((1, d), const2),
                  pl.BlockSpec((d, proj_w), const2),
                  pl.BlockSpec((1, SWA_Q_WIDTH), const2),
                  pl.BlockSpec((1, SWA_KV_WIDTH), const2),
                  pl.BlockSpec((LANES, LANES), const2),
                  pl.BlockSpec((SWA_Q_WIDTH, d), const2)],
        out_specs=pl.BlockSpec((tm, d), tile),
        out_shape=jax.ShapeDtypeStruct((t, d), F32),
        scratch_shapes=[pltpu.VMEM((tm + SWA_BLOCK, ext_w), BF16), pltpu.VMEM((tm + SWA_BLOCK, ext_w), BF16),
                        pltpu.VMEM((tm, SWA_Q_WIDTH), BF16)],
        compiler_params=_params("arbitrary", "arbitrary"),
        name="swa_mixer",
    )(sinks, x2, mod, gain.reshape(1, d), w_in.astype(BF16), qg, kg, ones, w_out.astype(BF16))


DN_QK_HEADS = 4
DN_V_HEADS = 8
DN_HEAD_DIM = D_MODEL // DN_V_HEADS
DN_CONV = 4
DN_CHUNK = 64
DN_KEY_WIDTH = DN_QK_HEADS * DN_HEAD_DIM
DN_VAL_WIDTH = DN_V_HEADS * DN_HEAD_DIM
DN_CONV_WIDTH = 2 * DN_KEY_WIDTH + DN_VAL_WIDTH
DN_TILE = 256
SUBLANES = 8


def _split_bf16(a):
    hi = a.astype(BF16)
    return hi, (a - hi.astype(F32)).astype(BF16)


def _mm3(a, b):
    ah, al = _split_bf16(a)
    bh, bl = _split_bf16(b)
    dot = functools.partial(jnp.dot, preferred_element_type=F32)
    return dot(ah, bh) + (dot(ah, bl) + dot(al, bh))


def _dn_proj_kernel(x_ref, mod_ref, gain_ref, w_ref, wba_ref, conv_ref, rate_ref, dtb_ref,
                    q_ref, k_ref, v_ref, z_ref, bg_ref, ext_ref):
    t = pl.program_id(1)
    tm = x_ref.shape[0]
    pad = SUBLANES

    @pl.when(t == 0)
    def _():
        ext_ref[0:pad, :] = jnp.zeros((pad, ext_ref.shape[1]), F32)

    mod = mod_ref[0]
    h = _modulated_norm(x_ref[...], gain_ref[...], mod[0:1], mod[1:2])
    hh, hl = _split_bf16(h)
    proj = jnp.dot(hh, w_ref[...], preferred_element_type=F32)
    z_ref[...] = proj[:, DN_CONV_WIDTH:].astype(BF16)

    ext_ref[pad:pad + tm, :] = proj[:, :DN_CONV_WIDTH]
    acc = None
    for j in range(DN_CONV):
        start = pad - (DN_CONV - 1) + j
        term = conv_ref[j:j + 1, :] * ext_ref[start:start + tm, :]
        acc = term if acc is None else acc + term
    ext_ref[0:pad, :] = ext_ref[tm:tm + pad, :]
    qkv = _silu(acc)
    for hq in range(2 * DN_QK_HEADS):
        cols = slice(hq * DN_HEAD_DIM, (hq + 1) * DN_HEAD_DIM)
        a = qkv[:, cols]
        a = a * lax.rsqrt(jnp.sum(a * a, axis=-1, keepdims=True) + NORM_EPS)
        if hq < DN_QK_HEADS:
            q_ref[:, cols] = (a * DN_HEAD_DIM ** -0.5).astype(BF16)
        else:
            k_ref[:, hq * DN_HEAD_DIM - DN_KEY_WIDTH:(hq + 1) * DN_HEAD_DIM - DN_KEY_WIDTH] = a.astype(BF16)
    v_ref[...] = qkv[:, 2 * DN_KEY_WIDTH:].astype(BF16)

    ba = jnp.dot(hh, wba_ref[...], preferred_element_type=F32) + jnp.dot(hl, wba_ref[...], preferred_element_type=F32)
    ba = ba + pltpu.roll(ba, LANES - 2 * DN_V_HEADS, 1)
    lane = lax.broadcasted_iota(jnp.int32, ba.shape, 1)
    beta = 1.0 / (1.0 + jnp.exp(-ba))
    sp = ba + dtb_ref[...]
    g = -rate_ref[...] * (jnp.maximum(sp, 0.0) + jnp.log1p(jnp.exp(-jnp.abs(sp))))
    g = jnp.where(jnp.logical_and(lane >= DN_V_HEADS, lane < 2 * DN_V_HEADS), g, 0.0)
    r = lax.broadcasted_iota(jnp.int32, (tm, tm), 0)
    c = lax.broadcasted_iota(jnp.int32, (tm, tm), 1)
    tri = jnp.logical_and(r // DN_CHUNK == c // DN_CHUNK, c <= r).astype(F32)
    gc = jnp.dot(tri, g, precision=HIGHEST, preferred_element_type=F32)
    bg_ref[...] = jnp.where(lane < DN_V_HEADS, beta, gc)


def _unit_lower_inverse(a):
    n = a.shape[0]
    r = lax.broadcasted_iota(jnp.int32, (n, n), 0)
    c = lax.broadcasted_iota(jnp.int32, (n, n), 1)
    eye = (r == c).astype(F32)
    size = SUBLANES
    same = (r // size) == (c // size)
    d = jnp.where(same, a, 0.0)
    d2 = _mm3(d, d)
    d4 = _mm3(d2, d2)
    x = _mm3(_mm3(eye - d, eye + d2), eye + d4)
    while size < n:
        wider = (r // (2 * size)) == (c // (2 * size))
        off = jnp.where(jnp.logical_and(wider, jnp.logical_not(same)), a, 0.0)
        x = x - _mm3(_mm3(x, off), x)
        same = wider
        size *= 2
    return x


def _dn_rule_kernel(x_ref, mod_ref, q_ref, k_ref, v_ref, z_ref, bg_ref, gct_ref, og_ref, wout_ref, o_ref,
                    state_ref, att_ref):
    t = pl.program_id(1)
    tm = x_ref.shape[0]
    ck = DN_CHUNK
    hd = DN_HEAD_DIM
    rep = DN_V_HEADS // DN_QK_HEADS

    @pl.when(t == 0)
    def _():
        state_ref[...] = jnp.zeros_like(state_ref)

    r = lax.broadcasted_iota(jnp.int32, (ck, ck), 0)
    c = lax.broadcasted_iota(jnp.int32, (ck, ck), 1)
    lower = c <= r
    strict = c < r
    dot = functools.partial(jnp.dot, preferred_element_type=F32)
    for hv in range(DN_V_HEADS):
        qcols = slice((hv // rep) * hd, (hv // rep + 1) * hd)
        vcols = slice(hv * hd, (hv + 1) * hd)
        state = state_ref[hv]
        for ci in range(tm // ck):
            rows = slice(ci * ck, (ci + 1) * ck)
            q = q_ref[rows, qcols]
            k = k_ref[rows, qcols]
            kf = k.astype(F32)
            beta = bg_ref[rows, hv:hv + 1]
            gcc = bg_ref[rows, DN_V_HEADS + hv:DN_V_HEADS + hv + 1]
            gcr = gct_ref[hv:hv + 1, rows]
            g_end = gcc[ck - 1:ck, :]
            decay = jnp.where(lower, jnp.exp(jnp.where(lower, gcc - gcr, 0.0)), 0.0)
            egc = jnp.exp(gcc)
            kb = kf * beta
            both = lax.dot_general(jnp.concatenate([kb.astype(BF16), q], axis=0), k,
                                   (((1,), (1,)), ((), ())), preferred_element_type=F32)
            both = both * jnp.concatenate([decay, decay], axis=0)
            tinv = _unit_lower_inverse(jnp.where(strict, both[:ck], 0.0))
            qk = both[ck:]
            rhs = jnp.concatenate([v_ref[rows, vcols].astype(F32) * beta, kb * egc], axis=1)
            uw = dot(tinv.astype(BF16), rhs.astype(BF16))
            u = uw[:, :hd]
            w = uw[:, hd:]
            q_dec = q.astype(F32) * egc
            ws_qs = dot(jnp.concatenate([w, q_dec], axis=0).astype(BF16), state.astype(BF16))
            v_new = u - ws_qs[:ck]
            v_new_b = v_new.astype(BF16)
            o = ws_qs[ck:] + dot(qk.astype(BF16), v_new_b)
            k_dec = (kf * jnp.exp(g_end - gcc)).astype(BF16)
            state = state * jnp.exp(g_end) + lax.dot_general(k_dec, v_new_b, (((0,), (0,)), ((), ())),
                                                             preferred_element_type=F32)
            zf = z_ref[rows, vcols].astype(F32)
            att_ref[rows, vcols] = (_rms(o) * og_ref[...] * _silu(zf)).astype(BF16)
        state_ref[hv] = state
    y = dot(att_ref[...], wout_ref[...])
    o_ref[...] = x_ref[...] + mod_ref[0][2:3] * y


def deltanet_layer(x2, mod, gain, w_in, conv_w, a_log, dt_bias, o_norm, w_out, b, s):
    t, d = x2.shape
    nh = DN_V_HEADS
    main_w = DN_CONV_WIDTH + DN_VAL_WIDTH
    w_main = w_in[:, :main_w].astype(BF16)
    w_ba = w_in[:, main_w:]
    w_ba_hi = w_ba.astype(BF16)
    w_ba_lo = (w_ba - w_ba_hi.astype(F32)).astype(BF16)
    w_ba2 = jnp.concatenate([w_ba_hi, w_ba_lo, jnp.zeros((d, LANES - 4 * nh), BF16)], axis=1)
    lanes_pad = lambda v: jnp.concatenate([jnp.zeros((nh,), F32), v.astype(F32),
                                           jnp.zeros((LANES - 2 * nh,), F32)]).reshape(1, LANES)
    rate = lanes_pad(jnp.exp(a_log.astype(F32)))
    dtb = lanes_pad(dt_bias)

    tm = TOKEN_TILE
    tpb = s // tm
    const2 = lambda bi, ti: (0, 0)
    tile = lambda bi, ti: (bi * tpb + ti, 0)
    q, k, v, z, bg = pl.pallas_call(
        _dn_proj_kernel,
        grid=(b, tpb),
        in_specs=[pl.BlockSpec((tm, d), tile),
                  pl.BlockSpec((1, 6, d), lambda bi, ti: (bi, 0, 0)),
                  pl.BlockSpec((1, d), const2),
                  pl.BlockSpec((d, main_w), const2),
                  pl.BlockSpec((d, LANES), const2),
                  pl.BlockSpec((DN_CONV, DN_CONV_WIDTH), const2),
                  pl.BlockSpec((1, LANES), const2),
                  pl.BlockSpec((1, LANES), const2)],
        out_specs=[pl.BlockSpec((tm, DN_KEY_WIDTH), tile), pl.BlockSpec((tm, DN_KEY_WIDTH), tile),
                   pl.BlockSpec((tm, DN_VAL_WIDTH), tile), pl.BlockSpec((tm, DN_VAL_WIDTH), tile),
                   pl.BlockSpec((tm, LANES), tile)],
        out_shape=[jax.ShapeDtypeStruct((t, DN_KEY_WIDTH), BF16), jax.ShapeDtypeStruct((t, DN_KEY_WIDTH), BF16),
                   jax.ShapeDtypeStruct((t, DN_VAL_WIDTH), BF16), jax.ShapeDtypeStruct((t, DN_VAL_WIDTH), BF16),
                   jax.ShapeDtypeStruct((t, LANES), F32)],
        scratch_shapes=[pltpu.VMEM((tm + SUBLANES, DN_CONV_WIDTH), F32)],
        compiler_params=_params("arbitrary", "arbitrary"),
        name="deltanet_proj",
    )(x2, mod, gain.reshape(1, d), w_main, w_ba2, conv_w, rate, dtb)

    gct = bg[:, nh:2 * nh].T
    tm = DN_TILE
    tpb = s // tm
    tile = lambda bi, ti: (bi * tpb + ti, 0)
    return pl.pallas_call(
        _dn_rule_kernel,
        grid=(b, tpb),
        in_specs=[pl.BlockSpec((tm, d), tile),
                  pl.BlockSpec((1, 6, d), lambda bi, ti: (bi, 0, 0)),
                  pl.BlockSpec((tm, DN_KEY_WIDTH), tile),
                  pl.BlockSpec((tm, DN_KEY_WIDTH), tile),
                  pl.BlockSpec((tm, DN_VAL_WIDTH), tile),
                  pl.BlockSpec((tm, DN_VAL_WIDTH), tile),
                  pl.BlockSpec((tm, LANES), tile),
                  pl.BlockSpec((nh, tm), lambda bi, ti: (0, bi * tpb + ti)),
                  pl.BlockSpec((1, DN_HEAD_DIM), const2),
                  pl.BlockSpec((DN_VAL_WIDTH, d), const2)],
        out_specs=pl.BlockSpec((tm, d), tile),
        out_shape=jax.ShapeDtypeStruct((t, d), F32),
        scratch_shapes=[pltpu.VMEM((nh, DN_HEAD_DIM, DN_HEAD_DIM), F32), pltpu.VMEM((tm, DN_VAL_WIDTH), BF16)],
        compiler_params=_params("arbitrary", "arbitrary"),
        name="deltanet_rule",
    )(x2, mod, q, k, v, z, bg, gct, o_norm.reshape(1, DN_HEAD_DIM), w_out.astype(BF16))


def kernel(x, c, l0_norm_mix, l0_norm_ffn, l0_ada_w, l0_ada_b, l0_gm_w_in, l0_gm_v_norm, l0_gm_w_s, l0_gm_b_s, l0_gm_w_out, l0_router_w, l0_router_b, l0_expert_w_in, l0_expert_w_out, l1_norm_mix, l1_norm_ffn, l1_ada_w, l1_ada_b, l1_dn_w_in, l1_dn_conv_w, l1_dn_a_log, l1_dn_dt_bias, l1_dn_o_norm, l1_dn_w_out, l1_router_w, l1_router_b, l1_expert_w_in, l1_expert_w_out, l2_norm_mix, l2_norm_ffn, l2_ada_w, l2_ada_b, l2_swa_w_in, l2_swa_q_norm, l2_swa_k_norm, l2_swa_sinks, l2_swa_w_out, l2_router_w, l2_router_b, l2_expert_w_in, l2_expert_w_out, l3_norm_mix, l3_norm_ffn, l3_ada_w, l3_ada_b, l3_gm_w_in, l3_gm_v_norm, l3_gm_w_s, l3_gm_b_s, l3_gm_w_out, l3_router_w, l3_router_b, l3_expert_w_in, l3_expert_w_out):
    b, s, d = x.shape
    x2 = x.reshape(b * s, d)
    c_pad = jnp.zeros((8, d), F32).at[:b].set(c)
    tiles_per_batch_of = lambda tm: s // tm

    def modulation(ada_w, ada_b):
        return adaln(c_pad, ada_w, ada_b)[:b].reshape(b, 6, d)

    mod = modulation(l0_ada_w, l0_ada_b)
    x2 = gmlp_layer(x2, mod, l0_norm_mix, l0_gm_w_in, l0_gm_v_norm, l0_gm_w_s, l0_gm_b_s, l0_gm_w_out,
                    tiles_per_batch_of(TOKEN_TILE))
    x2 = moe_layer(x2, mod, l0_norm_ffn, l0_router_w, l0_router_b, l0_expert_w_in, l0_expert_w_out,
                   tiles_per_batch_of)

    mod = modulation(l1_ada_w, l1_ada_b)
    x2 = deltanet_layer(x2, mod, l1_norm_mix, l1_dn_w_in, l1_dn_conv_w, l1_dn_a_log, l1_dn_dt_bias,
                        l1_dn_o_norm, l1_dn_w_out, b, s)
    x2 = moe_layer(x2, mod, l1_norm_ffn, l1_router_w, l1_router_b, l1_expert_w_in, l1_expert_w_out,
                   tiles_per_batch_of)

    mod = modulation(l2_ada_w, l2_ada_b)
    x2 = swa_layer(x2, mod, l2_norm_mix, l2_swa_w_in, l2_swa_q_norm, l2_swa_k_norm, l2_swa_sinks,
                   l2_swa_w_out, b, s)
    x2 = moe_layer(x2, mod, l2_norm_ffn, l2_router_w, l2_router_b, l2_expert_w_in, l2_expert_w_out,
                   tiles_per_batch_of)

    mod = modulation(l3_ada_w, l3_ada_b)
    x2 = gmlp_layer(x2, mod, l3_norm_mix, l3_gm_w_in, l3_gm_v_norm, l3_gm_w_s, l3_gm_b_s, l3_gm_w_out,
                    tiles_per_batch_of(TOKEN_TILE))
    x2 = moe_layer(x2, mod, l3_norm_ffn, l3_router_w, l3_router_b, l3_expert_w_in, l3_expert_w_out,
                   tiles_per_batch_of)
    return x2.reshape(b, s, d)
```

```python
import functools
import math

import jax
import jax.numpy as jnp
from jax import lax
from jax.experimental import pallas as pl
from jax.experimental.pallas import tpu as pltpu

F32 = jnp.float32
BF16 = jnp.bfloat16
HIGHEST = lax.Precision.HIGHEST

D_MODEL = 1024
NORM_EPS = 1e-6

GM_CHUNK = 128
GM_GROUPS = 8
GM_GROUP_DIM = D_MODEL // GM_GROUPS

MOE_GROUPS = 4
MOE_EXPERTS_PER_GROUP = 8
MOE_EXPERTS = MOE_GROUPS * MOE_EXPERTS_PER_GROUP
MOE_FF = D_MODEL // 2

VMEM_LIMIT_BYTES = 56 * 1024 * 1024

TOKEN_TILE = 512
ROUTER_TILE = 512
MOVE_TILE = 256
EXPERT_ROWS = 256


def _params(*semantics):
    return pltpu.CompilerParams(dimension_semantics=semantics, vmem_limit_bytes=VMEM_LIMIT_BYTES,
                                disable_bounds_checks=True)


def _rms(xf):
    return xf * lax.rsqrt(jnp.mean(xf * xf, axis=-1, keepdims=True) + NORM_EPS)


def _modulated_norm(x, gain, shift, scale):
    return _rms(x) * gain * (1.0 + scale) + shift


def _gelu_tanh(x):
    return 0.5 * x * (1.0 + jnp.tanh(math.sqrt(2.0 / math.pi) * (x + 0.044715 * (x * x * x))))


def _silu(x):
    return x * (1.0 / (1.0 + jnp.exp(-x)))


def _adaln_kernel(c_ref, w_ref, b_ref, o_ref):
    o_ref[...] = jnp.dot(_silu(c_ref[...]), w_ref[...], precision=HIGHEST,
                         preferred_element_type=F32) + b_ref[...]


def adaln(c_pad, ada_w, ada_b):
    rows, d = c_pad.shape
    n = ada_w.shape[1]
    tn = d
    return pl.pallas_call(
        _adaln_kernel,
        grid=(n // tn,),
        in_specs=[pl.BlockSpec((rows, d), lambda j: (0, 0)),
                  pl.BlockSpec((d, tn), lambda j: (0, j)),
                  pl.BlockSpec((1, tn), lambda j: (0, j))],
        out_specs=pl.BlockSpec((rows, tn), lambda j: (0, j)),
        out_shape=jax.ShapeDtypeStruct((rows, n), F32),
        compiler_params=_params("arbitrary"),
        name="adaln",
    )(c_pad, ada_w, ada_b.reshape(1, n))


def _gmlp_kernel(x_ref, mod_ref, gain_ref, win_ref, vn_ref, ws_ref, bs_ref, wout_ref, o_ref,
                 u_ref, v_ref, g_ref):
    width = D_MODEL
    x = x_ref[...]
    mod = mod_ref[0]
    h = _modulated_norm(x, gain_ref[...], mod[0:1], mod[1:2])
    z = _gelu_tanh(jnp.dot(h.astype(BF16), win_ref[...], preferred_element_type=F32))
    u_ref[...] = z[:, :width]
    v_ref[...] = (_rms(z[:, width:]) * vn_ref[...]).astype(BF16)
    for c in range(x.shape[0] // GM_CHUNK):
        rows = slice(c * GM_CHUNK, (c + 1) * GM_CHUNK)
        for g in range(GM_GROUPS):
            cols = slice(g * GM_GROUP_DIM, (g + 1) * GM_GROUP_DIM)
            sv = jnp.dot(ws_ref[g], v_ref[rows, cols], preferred_element_type=F32) + bs_ref[:, g:g + 1]
            g_ref[rows, cols] = (u_ref[rows, cols] * sv).astype(BF16)
    y = jnp.dot(g_ref[...], wout_ref[...], preferred_element_type=F32)
    o_ref[...] = x + mod[2:3] * y


def gmlp_layer(x2, mod, gain, w_in, v_norm, w_s, b_s, w_out, tiles_per_batch):
    t, d = x2.shape
    tm = TOKEN_TILE
    ws_causal = jnp.where(jnp.tril(jnp.ones((GM_CHUNK, GM_CHUNK), dtype=bool)), w_s, 0).astype(BF16)
    const2 = lambda i: (0, 0)
    return pl.pallas_call(
        _gmlp_kernel,
        grid=(t // tm,),
        in_specs=[pl.BlockSpec((tm, d), lambda i: (i, 0)),
                  pl.BlockSpec((1, 6, d), lambda i: (i // tiles_per_batch, 0, 0)),
                  pl.BlockSpec((1, d), const2),
                  pl.BlockSpec((d, 2 * d), const2),
                  pl.BlockSpec((1, d), const2),
                  pl.BlockSpec((GM_GROUPS, GM_CHUNK, GM_CHUNK), lambda i: (0, 0, 0)),
                  pl.BlockSpec((GM_CHUNK, GM_GROUPS), const2),
                  pl.BlockSpec((d, d), const2)],
        out_specs=pl.BlockSpec((tm, d), lambda i: (i, 0)),
        out_shape=jax.ShapeDtypeStruct((t, d), F32),
        scratch_shapes=[pltpu.VMEM((tm, d), F32), pltpu.VMEM((tm, d), BF16), pltpu.VMEM((tm, d), BF16)],
        compiler_params=_params("arbitrary"),
        name="gmlp_mixer",
    )(x2, mod, gain.reshape(1, d), w_in.astype(BF16), v_norm.reshape(1, d), ws_causal, b_s.T,
      w_out.astype(BF16))


def _router_kernel(x_ref, mod_ref, gain_ref, rw_ref, rb_ref, eid_ref, gate_ref, rank_ref, cnt_ref, base_ref):
    i = pl.program_id(0)
    tm = x_ref.shape[0]
    ne = MOE_EXPERTS
    npg = MOE_EXPERTS_PER_GROUP

    @pl.when(i == 0)
    def _():
        base_ref[...] = jnp.zeros_like(base_ref)

    mod = mod_ref[0]
    h = _modulated_norm(x_ref[...], gain_ref[...], mod[3:4], mod[4:5])
    lt = lax.dot_general(rw_ref[...], h, (((1,), (1,)), ((), ())), precision=HIGHEST,
                         preferred_element_type=F32) + rb_ref[...]
    lg = [lt[ne + g:ne + g + 1, :] for g in range(MOE_GROUPS)]
    gmax = functools.reduce(jnp.maximum, lg)
    gsum = functools.reduce(lambda a, b: a + b, [jnp.exp(l - gmax) for l in lg])
    pg_top = 1.0 / gsum
    g_sel = jnp.full(gmax.shape, MOE_GROUPS - 1, jnp.int32)
    for g in range(MOE_GROUPS - 2, -1, -1):
        g_sel = jnp.where(lg[g] == gmax, g, g_sel)
    sel = lt[(MOE_GROUPS - 1) * npg:MOE_GROUPS * npg, :]
    for g in range(MOE_GROUPS - 2, -1, -1):
        sel = jnp.where(g_sel == g, lt[g * npg:(g + 1) * npg, :], sel)
    row = lax.broadcasted_iota(jnp.int32, sel.shape, 0)
    m1 = jnp.max(sel, axis=0, keepdims=True)
    i1 = jnp.min(jnp.where(sel == m1, row, npg), axis=0, keepdims=True)
    rest = jnp.where(row == i1, -jnp.inf, sel)
    m2 = jnp.max(rest, axis=0, keepdims=True)
    i2 = jnp.min(jnp.where(rest == m2, row, npg), axis=0, keepdims=True)
    e2 = jnp.exp(m2 - m1)
    inv = pg_top / (1.0 + e2)
    eid = jnp.concatenate([g_sel * npg + i1, g_sel * npg + i2], axis=0)
    eid_ref[...] = eid
    gate_ref[...] = jnp.concatenate([inv, inv * e2], axis=0)

    erow = lax.broadcasted_iota(jnp.int32, (ne, tm), 0)
    before = (lax.broadcasted_iota(jnp.int32, (tm, tm), 0)
              < lax.broadcasted_iota(jnp.int32, (tm, tm), 1)).astype(BF16)
    base = base_ref[...]
    ranks = []
    for k in range(2):
        hit = erow == eid[k:k + 1, :]
        prefix = jnp.dot(hit.astype(BF16), before, preferred_element_type=F32) + base
        ranks.append(jnp.sum(jnp.where(hit, prefix, 0.0), axis=0, keepdims=True))
        base = base + jnp.sum(hit.astype(F32), axis=1, keepdims=True)
    rank_ref[...] = jnp.concatenate(ranks, axis=0).astype(jnp.int32)
    base_ref[...] = base
    cnt_ref[...] = jnp.broadcast_to(base, cnt_ref.shape).astype(jnp.int32)


def moe_router(x2, mod, gain, router_w, router_b, tiles_per_batch_of):
    t, d = x2.shape
    tm = ROUTER_TILE
    ne = MOE_EXPERTS
    nr = ne + 8
    rw = jnp.concatenate([router_w[:, MOE_GROUPS:], router_w[:, :MOE_GROUPS],
                          jnp.zeros((d, nr - ne - MOE_GROUPS), F32)], axis=1).T
    rb = jnp.concatenate([router_b[MOE_GROUPS:], router_b[:MOE_GROUPS],
                          jnp.zeros((nr - ne - MOE_GROUPS,), F32)]).reshape(nr, 1)
    tpb = tiles_per_batch_of(tm)
    const2 = lambda i: (0, 0)
    slot_spec = pl.BlockSpec((2, tm), lambda i: (0, i))
    return pl.pallas_call(
        _router_kernel,
        grid=(t // tm,),
        in_specs=[pl.BlockSpec((tm, d), lambda i: (i, 0)),
                  pl.BlockSpec((1, 6, d), lambda i: (i // tpb, 0, 0)),
                  pl.BlockSpec((1, d), const2),
                  pl.BlockSpec((nr, d), const2),
                  pl.BlockSpec((nr, 1), const2)],
        out_specs=[slot_spec, slot_spec, slot_spec, pl.BlockSpec((ne, 128), const2)],
        out_shape=[jax.ShapeDtypeStruct((2, t), jnp.int32), jax.ShapeDtypeStruct((2, t), F32),
                   jax.ShapeDtypeStruct((2, t), jnp.int32), jax.ShapeDtypeStruct((ne, 128), jnp.int32)],
        scratch_shapes=[pltpu.VMEM((ne, 1), F32)],
        compiler_params=_params("arbitrary"),
        name="moe_router",
    )(x2, mod, gain.reshape(1, d), rw, rb)


def _dispatch_kernel(dest_ref, x_ref, mod_ref, gain_ref, xs_ref, h_ref, sem):
    tm = x_ref.shape[0]
    mod = mod_ref[0]
    h_ref[...] = _modulated_norm(x_ref[...], gain_ref[...], mod[3:4], mod[4:5])

    def issue(r, carry):
        for k in range(2):
            pltpu.make_async_copy(h_ref.at[pl.ds(r, 1)], xs_ref.at[pl.ds(dest_ref[k, r], 1)], sem).start()
        return carry

    lax.fori_loop(0, tm, issue, 0, unroll=8)
    for k in range(2):
        pltpu.make_async_copy(h_ref, xs_ref.at[pl.ds(0, tm)], sem).wait()


def moe_dispatch(x2, mod, gain, dest, tiles_per_batch_of):
    t, d = x2.shape
    tm = MOVE_TILE
    tpb = tiles_per_batch_of(tm)
    const2 = lambda i: (0, 0)
    return pl.pallas_call(
        _dispatch_kernel,
        grid=(t // tm,),
        in_specs=[pl.BlockSpec((2, tm), lambda i: (0, i), memory_space=pltpu.SMEM),
                  pl.BlockSpec((tm, d), lambda i: (i, 0)),
                  pl.BlockSpec((1, 6, d), lambda i: (i // tpb, 0, 0)),
                  pl.BlockSpec((1, d), const2)],
        out_specs=pl.BlockSpec(memory_space=pl.ANY),
        out_shape=jax.ShapeDtypeStruct((2 * t, d), F32),
        scratch_shapes=[pltpu.VMEM((tm, d), F32), pltpu.SemaphoreType.DMA(())],
        compiler_params=_params("arbitrary"),
        name="moe_dispatch",
    )(dest, x2, mod, gain.reshape(1, d))


def _expert_kernel(blk_ref, exp_ref, start_ref, nvalid_ref, xs_ref, win_ref, wout_ref, ys_ref,
                   winb_ref, woutb_ref):
    i = pl.program_id(0)
    bm = xs_ref.shape[0]
    e = exp_ref[i]
    blk = blk_ref[i]
    prev = jnp.maximum(i - 1, 0)
    first = i == 0
    lo = start_ref[e]
    hi = start_ref[e + 1]
    row0 = blk * bm
    live = i < nvalid_ref[0]

    @pl.when(jnp.logical_or(first, exp_ref[prev] != e))
    def _():
        winb_ref[...] = win_ref[0].astype(BF16)
        woutb_ref[...] = wout_ref[0].astype(BF16)

    new_block = jnp.logical_or(first, blk_ref[prev] != blk)

    @pl.when(new_block)
    def _():
        ys_ref[...] = jnp.zeros_like(ys_ref)

    @pl.when(live)
    def _():
        a_gl = jnp.dot(xs_ref[...].astype(BF16), winb_ref[...], preferred_element_type=F32)
        mid = (_silu(a_gl[:, :MOE_FF]) * a_gl[:, MOE_FF:]).astype(BF16)
        y = jnp.dot(mid, woutb_ref[...], preferred_element_type=F32)
        rows = row0 + lax.broadcasted_iota(jnp.int32, (bm, 1), 0)
        mine = jnp.logical_and(rows >= lo, rows < hi)
        ys_ref[...] += jnp.where(mine, y, 0.0)


def moe_experts(xs, w_in, w_out, item_block, item_expert, seg_start, n_valid):
    n, d = xs.shape
    bm = EXPERT_ROWS
    n_items = item_block.shape[0]
    grid_spec = pltpu.PrefetchScalarGridSpec(
        num_scalar_prefetch=4,
        grid=(n_items,),
        in_specs=[pl.BlockSpec((bm, d), lambda i, b, e, s, v: (b[i], 0)),
                  pl.BlockSpec((1, d, 2 * MOE_FF), lambda i, b, e, s, v: (e[i], 0, 0)),
                  pl.BlockSpec((1, MOE_FF, d), lambda i, b, e, s, v: (e[i], 0, 0))],
        out_specs=pl.BlockSpec((bm, d), lambda i, b, e, s, v: (b[i], 0)),
        scratch_shapes=[pltpu.VMEM((d, 2 * MOE_FF), BF16), pltpu.VMEM((MOE_FF, d), BF16)],
    )
    return pl.pallas_call(
        _expert_kernel,
        grid_spec=grid_spec,
        out_shape=jax.ShapeDtypeStruct((n, d), F32),
        compiler_params=_params("arbitrary"),
        name="moe_experts",
    )(item_block, item_expert, seg_start, n_valid, xs, w_in, w_out)


def _combine_kernel(dest_ref, x_ref, mod_ref, gate_ref, ys_ref, o_ref, y_ref, sem):
    tm = x_ref.shape[0]

    def issue(r, carry):
        for k in range(2):
            pltpu.make_async_copy(ys_ref.at[pl.ds(dest_ref[k, r], 1)], y_ref.at[k, pl.ds(r, 1)], sem).start()
        return carry

    lax.fori_loop(0, tm, issue, 0, unroll=8)
    for k in range(2):
        pltpu.make_async_copy(ys_ref.at[pl.ds(0, tm)], y_ref.at[k], sem).wait()
    gates = gate_ref[...]
    moe = gates[:, 0:1] * y_ref[0] + gates[:, 1:2] * y_ref[1]
    o_ref[...] = x_ref[...] + mod_ref[0][5:6] * moe


def moe_combine(x2, mod, gates_t, dest, ys, tiles_per_batch_of):
    t, d = x2.shape
    tm = MOVE_TILE
    tpb = tiles_per_batch_of(tm)
    return pl.pallas_call(
        _combine_kernel,
        grid=(t // tm,),
        in_specs=[pl.BlockSpec((2, tm), lambda i: (0, i), memory_space=pltpu.SMEM),
                  pl.BlockSpec((tm, d), lambda i: (i, 0)),
                  pl.BlockSpec((1, 6, d), lambda i: (i // tpb, 0, 0)),
                  pl.BlockSpec((tm, 2), lambda i: (i, 0)),
                  pl.BlockSpec(memory_space=pl.ANY)],
        out_specs=pl.BlockSpec((tm, d), lambda i: (i, 0)),
        out_shape=jax.ShapeDtypeStruct((t, d), F32),
        scratch_shapes=[pltpu.VMEM((2, tm, d), F32), pltpu.SemaphoreType.DMA(())],
        compiler_params=_params("arbitrary"),
        name="moe_combine",
    )(dest, x2, mod, gates_t, ys)


def moe_layer(x2, mod, gain, router_w, router_b, w_in, w_out, tiles_per_batch_of):
    t, d = x2.shape
    bm = EXPERT_ROWS
    n_assign = 2 * t
    eid, gate, rank, cnt = moe_router(x2, mod, gain, router_w, router_b, tiles_per_batch_of)
    ne = MOE_EXPERTS
    experts = jnp.arange(ne, dtype=jnp.int32)
    upto = experts[None, :] <= experts[:, None]

    def running_total(v):
        return jnp.sum(jnp.where(upto, v[None, :], 0), axis=1).astype(jnp.int32)

    def lookup(table, idx):
        hit = idx[None] == experts.reshape((ne,) + (1,) * idx.ndim)
        return jnp.sum(jnp.where(hit, table.reshape((ne,) + (1,) * idx.ndim), 0), axis=0).astype(jnp.int32)

    counts = cnt[:, 0]
    seg_end = running_total(counts)
    seg_start = jnp.concatenate([jnp.zeros((1,), jnp.int32), seg_end])
    dest = lookup(seg_start[:ne], eid) + rank
    n_blocks = n_assign // bm
    n_items = n_blocks + ne - 1
    first_blk = seg_start[:ne] // bm
    last_blk = jnp.where(counts > 0, (seg_end - 1) // bm, first_blk - 1)
    per_expert = jnp.maximum(last_blk - first_blk + 1, 0)
    item_end = running_total(per_expert)
    n_valid = item_end[ne - 1]
    item_ids = jnp.arange(n_items, dtype=jnp.int32)
    item_ids_c = jnp.minimum(item_ids, n_valid - 1)
    item_expert = jnp.minimum(jnp.sum(item_ids_c[:, None] >= item_end[None, :], axis=1), ne - 1).astype(jnp.int32)
    item_block = lookup(first_blk - (item_end - per_expert), item_expert) + item_ids_c
    xs = moe_dispatch(x2, mod, gain, dest, tiles_per_batch_of)
    ys = moe_experts(xs, w_in, w_out, item_block, item_expert, seg_start,
                     n_valid.reshape(1).astype(jnp.int32))
    return moe_combine(x2, mod, gate.T, dest, ys, tiles_per_batch_of)


SWA_HEAD_DIM = 64
SWA_Q_HEADS = D_MODEL // SWA_HEAD_DIM
SWA_KV_HEADS = 4
SWA_BLOCK = 128
SWA_Q_WIDTH = SWA_Q_HEADS * SWA_HEAD_DIM
SWA_KV_WIDTH = SWA_KV_HEADS * SWA_HEAD_DIM
LANES = 128
HEADS_PER_VREG = LANES // SWA_HEAD_DIM


def _swa_kernel(sink_ref, x_ref, mod_ref, gain_ref, win_ref, qg_ref, kg_ref, ones_ref, wout_ref, o_ref,
                kx_ref, vx_ref, att_ref):
    t = pl.program_id(1)
    tm = x_ref.shape[0]
    blk = SWA_BLOCK
    hd = SWA_HEAD_DIM

    @pl.when(t == 0)
    def _():
        kx_ref[0:blk, :] = jnp.zeros((blk, kx_ref.shape[1]), BF16)
        vx_ref[0:blk, :] = jnp.zeros((blk, vx_ref.shape[1]), BF16)

    x = x_ref[...]
    mod = mod_ref[0]
    h = _modulated_norm(x, gain_ref[...], mod[0:1], mod[1:2]).astype(BF16)
    proj = jnp.dot(h, win_ref[...], preferred_element_type=F32)

    def head_rms(a):
        ss = jnp.dot((a * a).astype(BF16), ones_ref[...], preferred_element_type=F32)
        return a * lax.rsqrt(ss * (1.0 / hd) + NORM_EPS)

    lane = lax.broadcasted_iota(jnp.int32, (tm, LANES), 1)
    low = lane < hd
    for c in range(SWA_Q_WIDTH // LANES):
        cols = slice(c * LANES, (c + 1) * LANES)
        att_ref[:, cols] = (head_rms(proj[:, cols]) * qg_ref[:, cols]).astype(BF16)
    for c in range(SWA_KV_WIDTH // LANES):
        cols = slice(c * LANES, (c + 1) * LANES)
        kc = head_rms(proj[:, SWA_Q_WIDTH + c * LANES:SWA_Q_WIDTH + (c + 1) * LANES]) * kg_ref[:, cols]
        vc = proj[:, SWA_Q_WIDTH + SWA_KV_WIDTH + c * LANES:SWA_Q_WIDTH + SWA_KV_WIDTH + (c + 1) * LANES]
        for ref, a in ((kx_ref, kc), (vx_ref, vc)):
            even_lo = jnp.where(low, a, 0.0)
            odd_hi = jnp.where(low, 0.0, a)
            j0 = HEADS_PER_VREG * c
            ref[blk:blk + tm, (2 * j0) * LANES:(2 * j0 + 1) * LANES] = even_lo.astype(BF16)
            ref[blk:blk + tm, (2 * j0 + 1) * LANES:(2 * j0 + 2) * LANES] = pltpu.roll(even_lo, hd, 1).astype(BF16)
            ref[blk:blk + tm, (2 * j0 + 2) * LANES:(2 * j0 + 3) * LANES] = pltpu.roll(odd_hi, hd, 1).astype(BF16)
            ref[blk:blk + tm, (2 * j0 + 3) * LANES:(2 * j0 + 4) * LANES] = odd_hi.astype(BF16)

    qi = lax.broadcasted_iota(jnp.int32, (blk, 2 * blk), 0)
    kj = lax.broadcasted_iota(jnp.int32, (blk, 2 * blk), 1)
    lane_q = lax.broadcasted_iota(jnp.int32, (blk, LANES), 1) < hd
    pairs_per_kv = SWA_Q_HEADS // SWA_KV_HEADS // HEADS_PER_VREG
    for i in range(tm // blk):
        rows = slice(i * blk, (i + 1) * blk)
        keys = slice(i * blk, (i + 2) * blk)
        floor = jnp.where(t == 0, blk - 1, qi) if i == 0 else qi
        mask = jnp.logical_and(kj > floor, kj <= qi + blk)
        for p in range(SWA_Q_HEADS // HEADS_PER_VREG):
            j = p // pairs_per_kv
            qp = att_ref[rows, p * LANES:(p + 1) * LANES]
            out = None
            inv = []
            for half in range(HEADS_PER_VREG):
                col = (2 * j + half) * LANES
                sc = lax.dot_general(qp, kx_ref[keys, col:col + LANES], (((1,), (1,)), ((), ())),
                                     preferred_element_type=F32)
                sc = jnp.where(mask, sc, -jnp.inf)
                sink = sink_ref[HEADS_PER_VREG * p + half]
                m = jnp.maximum(jnp.max(sc, axis=-1, keepdims=True), sink)
                pr = jnp.exp(sc - m)
                inv.append(1.0 / (jnp.sum(pr, axis=-1, keepdims=True) + jnp.exp(sink - m)))
                pv = jnp.dot(pr.astype(BF16), vx_ref[keys, col:col + LANES], preferred_element_type=F32)
                out = pv if out is None else out + pv
            att_ref[rows, p * LANES:(p + 1) * LANES] = (out * jnp.where(lane_q, inv[0], inv[1])).astype(BF16)
    kx_ref[0:blk, :] = kx_ref[tm:tm + blk, :]
    vx_ref[0:blk, :] = vx_ref[tm:tm + blk, :]
    y = jnp.dot(att_ref[...], wout_ref[...], preferred_element_type=F32)
    o_ref[...] = x + mod[2:3] * y


def swa_layer(x2, mod, gain, w_in, q_norm, k_norm, sinks, w_out, b, s):
    t, d = x2.shape
    tm = TOKEN_TILE
    tpb = s // tm
    hd = SWA_HEAD_DIM
    qg = (jnp.tile(q_norm, SWA_Q_HEADS) * hd ** -0.5).reshape(1, SWA_Q_WIDTH)
    kg = jnp.tile(k_norm, SWA_KV_HEADS).reshape(1, SWA_KV_WIDTH)
    ids = jnp.arange(LANES) // hd
    ones = (ids[:, None] == ids[None, :]).astype(BF16)
    proj_w = SWA_Q_WIDTH + 2 * SWA_KV_WIDTH
    ext_w = 2 * SWA_KV_HEADS * LANES
    const2 = lambda bi, ti: (0, 0)
    tile = lambda bi, ti: (bi * tpb + ti, 0)
    return pl.pallas_call(
        _swa_kernel,
        grid=(b, tpb),
        in_specs=[pl.BlockSpec(memory_space=pltpu.SMEM),
                  pl.BlockSpec((tm, d), tile),
                  pl.BlockSpec((1, 6, d), lambda bi, ti: (bi, 0, 0)),
                  pl.BlockSpec((1, d), const2),
                  pl.BlockSpec((d, proj_w), const2),
                  pl.BlockSpec((1, SWA_Q_WIDTH), const2),
                  pl.BlockSpec((1, SWA_KV_WIDTH), const2),
                  pl.BlockSpec((LANES, LANES), const2),
                  pl.BlockSpec((SWA_Q_WIDTH, d), const2)],
        out_specs=pl.BlockSpec((tm, d), tile),
        out_shape=jax.ShapeDtypeStruct((t, d), F32),
        scratch_shapes=[pltpu.VMEM((tm + SWA_BLOCK, ext_w), BF16), pltpu.VMEM((tm + SWA_BLOCK, ext_w), BF16),
                        pltpu.VMEM((tm, SWA_Q_WIDTH), BF16)],
        compiler_params=_params("arbitrary", "arbitrary"),
        name="swa_mixer",
    )(sinks, x2, mod, gain.reshape(1, d), w_in.astype(BF16), qg, kg, ones, w_out.astype(BF16))


DN_QK_HEADS = 4
DN_V_HEADS = 8
DN_HEAD_DIM = D_MODEL // DN_V_HEADS
DN_CONV = 4
DN_CHUNK = 64
DN_KEY_WIDTH = DN_QK_HEADS * DN_HEAD_DIM
DN_VAL_WIDTH = DN_V_HEADS * DN_HEAD_DIM
DN_CONV_WIDTH = 2 * DN_KEY_WIDTH + DN_VAL_WIDTH
DN_PREP_TILE = 128
DN_SCAN_TILE = 128
SUBLANES = 8


def _split_bf16(a):
    hi = a.astype(BF16)
    return hi, (a - hi.astype(F32)).astype(BF16)


def _mm3(a, b):
    ah, al = _split_bf16(a)
    bh, bl = _split_bf16(b)
    dot = functools.partial(jnp.dot, preferred_element_type=F32)
    return dot(ah, bh) + (dot(ah, bl) + dot(al, bh))


def _dn_proj_kernel(x_ref, mod_ref, gain_ref, w_ref, wba_ref, conv_ref, rate_ref, dtb_ref,
                    q_ref, k_ref, v_ref, z_ref, bg_ref, ext_ref):
    t = pl.program_id(1)
    tm = x_ref.shape[0]
    pad = SUBLANES

    @pl.when(t == 0)
    def _():
        ext_ref[0:pad, :] = jnp.zeros((pad, ext_ref.shape[1]), F32)

    mod = mod_ref[0]
    h = _modulated_norm(x_ref[...], gain_ref[...], mod[0:1], mod[1:2])
    hh, hl = _split_bf16(h)
    proj = jnp.dot(hh, w_ref[...], preferred_element_type=F32)
    z_ref[...] = proj[:, DN_CONV_WIDTH:].astype(BF16)

    ext_ref[pad:pad + tm, :] = proj[:, :DN_CONV_WIDTH]
    acc = None
    for j in range(DN_CONV):
        start = pad - (DN_CONV - 1) + j
        term = conv_ref[j:j + 1, :] * ext_ref[start:start + tm, :]
        acc = term if acc is None else acc + term
    ext_ref[0:pad, :] = ext_ref[tm:tm + pad, :]
    qkv = _silu(acc)
    for hq in range(2 * DN_QK_HEADS):
        cols = slice(hq * DN_HEAD_DIM, (hq + 1) * DN_HEAD_DIM)
        a = qkv[:, cols]
        a = a * lax.rsqrt(jnp.sum(a * a, axis=-1, keepdims=True) + NORM_EPS)
        if hq < DN_QK_HEADS:
            q_ref[:, cols] = (a * DN_HEAD_DIM ** -0.5).astype(BF16)
        else:
            k_ref[:, hq * DN_HEAD_DIM - DN_KEY_WIDTH:(hq + 1) * DN_HEAD_DIM - DN_KEY_WIDTH] = a.astype(BF16)
    v_ref[...] = qkv[:, 2 * DN_KEY_WIDTH:].astype(BF16)

    ba = jnp.dot(hh, wba_ref[...], preferred_element_type=F32) + jnp.dot(hl, wba_ref[...], preferred_element_type=F32)
    ba = ba + pltpu.roll(ba, LANES - 2 * DN_V_HEADS, 1)
    lane = lax.broadcasted_iota(jnp.int32, ba.shape, 1)
    beta = 1.0 / (1.0 + jnp.exp(-ba))
    sp = ba + dtb_ref[...]
    g = -rate_ref[...] * (jnp.maximum(sp, 0.0) + jnp.log1p(jnp.exp(-jnp.abs(sp))))
    g = jnp.where(jnp.logical_and(lane >= DN_V_HEADS, lane < 2 * DN_V_HEADS), g, 0.0)
    r = lax.broadcasted_iota(jnp.int32, (tm, tm), 0)
    c = lax.broadcasted_iota(jnp.int32, (tm, tm), 1)
    tri = jnp.logical_and(r // DN_CHUNK == c // DN_CHUNK, c <= r).astype(F32)
    gc = jnp.dot(tri, g, precision=HIGHEST, preferred_element_type=F32)
    bg_ref[...] = jnp.where(lane < DN_V_HEADS, beta, gc)


def _unit_lower_inverses(mats):
    n = mats[0].shape[0]
    r = lax.broadcasted_iota(jnp.int32, (n, n), 0)
    c = lax.broadcasted_iota(jnp.int32, (n, n), 1)
    eye = (r == c).astype(F32)
    size = SUBLANES
    same = (r // size) == (c // size)
    d = [jnp.where(same, a, 0.0) for a in mats]
    d2 = [_mm3(v, v) for v in d]
    d4 = [_mm3(v, v) for v in d2]
    x = [_mm3(eye - v, eye + v2) for v, v2 in zip(d, d2)]
    x = [_mm3(v, eye + v4) for v, v4 in zip(x, d4)]
    while size < n:
        wider = (r // (2 * size)) == (c // (2 * size))
        ring = jnp.logical_and(wider, jnp.logical_not(same))
        xl = [_mm3(v, jnp.where(ring, a, 0.0)) for v, a in zip(x, mats)]
        x = [v - _mm3(vl, v) for v, vl in zip(x, xl)]
        same = wider
        size *= 2
    return x


def _dn_prep_kernel(q_ref, k_ref, v_ref, bg_ref, gct_ref, u_ref, w_ref, qk_ref):
    tm = q_ref.shape[0]
    ck = DN_CHUNK
    hd = DN_HEAD_DIM
    rep = DN_V_HEADS // DN_QK_HEADS
    r = lax.broadcasted_iota(jnp.int32, (ck, ck), 0)
    c = lax.broadcasted_iota(jnp.int32, (ck, ck), 1)
    lower = c <= r
    strict = c < r
    dot = functools.partial(jnp.dot, preferred_element_type=F32)
    chunks = [slice(ci * ck, (ci + 1) * ck) for ci in range(tm // ck)]
    gram = {}
    for rows in chunks:
        for hq in range(DN_QK_HEADS):
            cols = slice(hq * hd, (hq + 1) * hd)
            k = k_ref[rows, cols]
            gram[(hq, rows.start)] = lax.dot_general(jnp.concatenate([k, q_ref[rows, cols]], axis=0), k,
                                                     (((1,), (1,)), ((), ())), preferred_element_type=F32)
    problems = [(hv, rows) for rows in chunks for hv in range(DN_V_HEADS)]
    mats, rhs = [], []
    for hv, rows in problems:
        beta = bg_ref[rows, hv:hv + 1]
        gcc = bg_ref[rows, DN_V_HEADS + hv:DN_V_HEADS + hv + 1]
        gcr = gct_ref[hv:hv + 1, rows]
        decay = jnp.where(lower, jnp.exp(jnp.where(lower, gcc - gcr, 0.0)), 0.0)
        g = gram[(hv // rep, rows.start)]
        mats.append(jnp.where(strict, g[:ck] * beta * decay, 0.0))
        qk_ref[rows, hv * ck:(hv + 1) * ck] = (g[ck:] * decay).astype(BF16)
        kf = k_ref[rows, (hv // rep) * hd:(hv // rep + 1) * hd].astype(F32)
        vf = v_ref[rows, hv * hd:(hv + 1) * hd].astype(F32)
        rhs.append(jnp.concatenate([vf * beta, kf * (beta * jnp.exp(gcc))], axis=1).astype(BF16))
    inverses = _unit_lower_inverses(mats)
    for (hv, rows), tinv, b in zip(problems, inverses, rhs):
        uw = dot(tinv.astype(BF16), b)
        u_ref[rows, hv * hd:(hv + 1) * hd] = uw[:, :hd].astype(BF16)
        w_ref[rows, hv * hd:(hv + 1) * hd] = uw[:, hd:].astype(BF16)


def _dn_scan_kernel(x_ref, mod_ref, u_ref, w_ref, qk_ref, q_ref, k_ref, z_ref, bg_ref, og_ref, wout_ref, o_ref,
                    state_ref, att_ref):
    t = pl.program_id(0)
    nb, tm, _ = x_ref.shape
    ck = DN_CHUNK
    hd = DN_HEAD_DIM
    rep = DN_V_HEADS // DN_QK_HEADS

    @pl.when(t == 0)
    def _():
        state_ref[...] = jnp.zeros_like(state_ref)

    dot = functools.partial(jnp.dot, preferred_element_type=F32)
    chains = [(bi, hv) for bi in range(nb) for hv in range(DN_V_HEADS)]
    for ci in range(tm // ck):
        rows = slice(ci * ck, (ci + 1) * ck)
        gcc, g_end, lhs = [], [], []
        for bi, hv in chains:
            gc = bg_ref[bi, rows, DN_V_HEADS + hv:DN_V_HEADS + hv + 1]
            gcc.append(gc)
            g_end.append(gc[ck - 1:ck, :])
            q_dec = q_ref[bi, rows, (hv // rep) * hd:(hv // rep + 1) * hd].astype(F32) * jnp.exp(gc)
            lhs.append(jnp.concatenate([w_ref[bi, rows, hv * hd:(hv + 1) * hd], q_dec.astype(BF16)], axis=0))
        ws_qs = [dot(a, state_ref[n].astype(BF16)) for n, a in enumerate(lhs)]
        v_new = [(u_ref[bi, rows, hv * hd:(hv + 1) * hd].astype(F32) - m[:ck]).astype(BF16)
                 for (bi, hv), m in zip(chains, ws_qs)]
        outs = [m[ck:] + dot(qk_ref[bi, rows, hv * ck:(hv + 1) * ck], vn)
                for (bi, hv), m, vn in zip(chains, ws_qs, v_new)]
        for n, (bi, hv) in enumerate(chains):
            kf = k_ref[bi, rows, (hv // rep) * hd:(hv // rep + 1) * hd].astype(F32)
            k_dec = (kf * jnp.exp(g_end[n] - gcc[n])).astype(BF16)
            state_ref[n] = state_ref[n] * jnp.exp(g_end[n]) + lax.dot_general(
                k_dec, v_new[n], (((0,), (0,)), ((), ())), preferred_element_type=F32)
        for (bi, hv), o in zip(chains, outs):
            zf = z_ref[bi, rows, hv * hd:(hv + 1) * hd].astype(F32)
            att_ref[bi * tm + ci * ck:bi * tm + (ci + 1) * ck, hv * hd:(hv + 1) * hd] = (
                _rms(o) * og_ref[...] * _silu(zf)).astype(BF16)
    y = dot(att_ref[...], wout_ref[...])
    for bi in range(nb):
        o_ref[bi] = x_ref[bi] + mod_ref[bi][2:3] * y[bi * tm:(bi + 1) * tm]


def deltanet_layer(x2, mod, gain, w_in, conv_w, a_log, dt_bias, o_norm, w_out, b, s):
    t, d = x2.shape
    nh = DN_V_HEADS
    main_w = DN_CONV_WIDTH + DN_VAL_WIDTH
    w_main = w_in[:, :main_w].astype(BF16)
    w_ba = w_in[:, main_w:]
    w_ba_hi = w_ba.astype(BF16)
    w_ba_lo = (w_ba - w_ba_hi.astype(F32)).astype(BF16)
    w_ba2 = jnp.concatenate([w_ba_hi, w_ba_lo, jnp.zeros((d, LANES - 4 * nh), BF16)], axis=1)
    lanes_pad = lambda v: jnp.concatenate([jnp.zeros((nh,), F32), v.astype(F32),
                                           jnp.zeros((LANES - 2 * nh,), F32)]).reshape(1, LANES)
    rate = lanes_pad(jnp.exp(a_log.astype(F32)))
    dtb = lanes_pad(dt_bias)

    tm = TOKEN_TILE
    tpb = s // tm
    const2 = lambda bi, ti: (0, 0)
    tile = lambda bi, ti: (bi * tpb + ti, 0)
    q, k, v, z, bg = pl.pallas_call(
        _dn_proj_kernel,
        grid=(b, tpb),
        in_specs=[pl.BlockSpec((tm, d), tile),
                  pl.BlockSpec((1, 6, d), lambda bi, ti: (bi, 0, 0)),
                  pl.BlockSpec((1, d), const2),
                  pl.BlockSpec((d, main_w), const2),
                  pl.BlockSpec((d, LANES), const2),
                  pl.BlockSpec((DN_CONV, DN_CONV_WIDTH), const2),
                  pl.BlockSpec((1, LANES), const2),
                  pl.BlockSpec((1, LANES), const2)],
        out_specs=[pl.BlockSpec((tm, DN_KEY_WIDTH), tile), pl.BlockSpec((tm, DN_KEY_WIDTH), tile),
                   pl.BlockSpec((tm, DN_VAL_WIDTH), tile), pl.BlockSpec((tm, DN_VAL_WIDTH), tile),
                   pl.BlockSpec((tm, LANES), tile)],
        out_shape=[jax.ShapeDtypeStruct((t, DN_KEY_WIDTH), BF16), jax.ShapeDtypeStruct((t, DN_KEY_WIDTH), BF16),
                   jax.ShapeDtypeStruct((t, DN_VAL_WIDTH), BF16), jax.ShapeDtypeStruct((t, DN_VAL_WIDTH), BF16),
                   jax.ShapeDtypeStruct((t, LANES), F32)],
        scratch_shapes=[pltpu.VMEM((tm + SUBLANES, DN_CONV_WIDTH), F32)],
        compiler_params=_params("arbitrary", "arbitrary"),
        name="deltanet_proj",
    )(x2, mod, gain.reshape(1, d), w_main, w_ba2, conv_w, rate, dtb)

    gct = bg[:, nh:2 * nh].T
    tp = DN_PREP_TILE
    rows_of = lambda width: pl.BlockSpec((tp, width), lambda i: (i, 0))
    qk_w = nh * DN_CHUNK
    u, w, qk = pl.pallas_call(
        _dn_prep_kernel,
        grid=(t // tp,),
        in_specs=[rows_of(DN_KEY_WIDTH), rows_of(DN_KEY_WIDTH), rows_of(DN_VAL_WIDTH), rows_of(LANES),
                  pl.BlockSpec((nh, tp), lambda i: (0, i))],
        out_specs=[rows_of(DN_VAL_WIDTH), rows_of(DN_VAL_WIDTH), rows_of(qk_w)],
        out_shape=[jax.ShapeDtypeStruct((t, DN_VAL_WIDTH), BF16), jax.ShapeDtypeStruct((t, DN_VAL_WIDTH), BF16),
                   jax.ShapeDtypeStruct((t, qk_w), BF16)],
        compiler_params=_params("arbitrary"),
        name="deltanet_prep",
    )(q, k, v, bg, gct)

    ts = DN_SCAN_TILE
    seq = lambda a: a.reshape(b, s, a.shape[-1])
    both = lambda width: pl.BlockSpec((b, ts, width), lambda i: (0, i, 0))
    out = pl.pallas_call(
        _dn_scan_kernel,
        grid=(s // ts,),
        in_specs=[both(d),
                  pl.BlockSpec((b, 6, d), lambda i: (0, 0, 0)),
                  both(DN_VAL_WIDTH), both(DN_VAL_WIDTH), both(qk_w), both(DN_KEY_WIDTH), both(DN_KEY_WIDTH),
                  both(DN_VAL_WIDTH), both(LANES),
                  pl.BlockSpec((1, DN_HEAD_DIM), lambda i: (0, 0)),
                  pl.BlockSpec((DN_VAL_WIDTH, d), lambda i: (0, 0))],
        out_specs=both(d),
        out_shape=jax.ShapeDtypeStruct((b, s, d), F32),
        scratch_shapes=[pltpu.VMEM((b * nh, DN_HEAD_DIM, DN_HEAD_DIM), F32),
                        pltpu.VMEM((b * ts, DN_VAL_WIDTH), BF16)],
        compiler_params=_params("arbitrary"),
        name="deltanet_scan",
    )(seq(x2), mod, seq(u), seq(w), seq(qk), seq(q), seq(k), seq(z), seq(bg), o_norm.reshape(1, DN_HEAD_DIM),
      w_out.astype(BF16))
    return out.reshape(t, d)


def kernel(x, c, l0_norm_mix, l0_norm_ffn, l0_ada_w, l0_ada_b, l0_gm_w_in, l0_gm_v_norm, l0_gm_w_s, l0_gm_b_s, l0_gm_w_out, l0_router_w, l0_router_b, l0_expert_w_in, l0_expert_w_out, l1_norm_mix, l1_norm_ffn, l1_ada_w, l1_ada_b, l1_dn_w_in, l1_dn_conv_w, l1_dn_a_log, l1_dn_dt_bias, l1_dn_o_norm, l1_dn_w_out, l1_router_w, l1_router_b, l1_expert_w_in, l1_expert_w_out, l2_norm_mix, l2_norm_ffn, l2_ada_w, l2_ada_b, l2_swa_w_in, l2_swa_q_norm, l2_swa_k_norm, l2_swa_sinks, l2_swa_w_out, l2_router_w, l2_router_b, l2_expert_w_in, l2_expert_w_out, l3_norm_mix, l3_norm_ffn, l3_ada_w, l3_ada_b, l3_gm_w_in, l3_gm_v_norm, l3_gm_w_s, l3_gm_b_s, l3_gm_w_out, l3_router_w, l3_router_b, l3_expert_w_in, l3_expert_w_out):
    b, s, d = x.shape
    x2 = x.reshape(b * s, d)
    c_pad = jnp.zeros((8, d), F32).at[:b].set(c)
    tiles_per_batch_of = lambda tm: s // tm

    def modulation(ada_w, ada_b):
        return adaln(c_pad, ada_w, ada_b)[:b].reshape(b, 6, d)

    mod = modulation(l0_ada_w, l0_ada_b)
    x2 = gmlp_layer(x2, mod, l0_norm_mix, l0_gm_w_in, l0_gm_v_norm, l0_gm_w_s, l0_gm_b_s, l0_gm_w_out,
                    tiles_per_batch_of(TOKEN_TILE))
    x2 = moe_layer(x2, mod, l0_norm_ffn, l0_router_w, l0_router_b, l0_expert_w_in, l0_expert_w_out,
                   tiles_per_batch_of)

    mod = modulation(l1_ada_w, l1_ada_b)
    x2 = deltanet_layer(x2, mod, l1_norm_mix, l1_dn_w_in, l1_dn_conv_w, l1_dn_a_log, l1_dn_dt_bias,
                        l1_dn_o_norm, l1_dn_w_out, b, s)
    x2 = moe_layer(x2, mod, l1_norm_ffn, l1_router_w, l1_router_b, l1_expert_w_in, l1_expert_w_out,
                   tiles_per_batch_of)

    mod = modulation(l2_ada_w, l2_ada_b)
    x2 = swa_layer(x2, mod, l2_norm_mix, l2_swa_w_in, l2_swa_q_norm, l2_swa_k_norm, l2_swa_sinks,
                   l2_swa_w_out, b, s)
    x2 = moe_layer(x2, mod, l2_norm_ffn, l2_router_w, l2_router_b, l2_expert_w_in, l2_expert_w_out,
                   tiles_per_batch_of)

    mod = modulation(l3_ada_w, l3_ada_b)
    x2 = gmlp_layer(x2, mod, l3_norm_mix, l3_gm_w_in, l3_gm_v_norm, l3_gm_w_s, l3_gm_b_s, l3_gm_w_out,
                    tiles_per_batch_of(TOKEN_TILE))
    x2 = moe_layer(x2, mod, l3_norm_ffn, l3_router_w, l3_router_b, l3_expert_w_in, l3_expert_w_out,
                   tiles_per_batch_of)
    return x2.reshape(b, s, d)
```

```python
import functools
import math

import jax
import jax.numpy as jnp
from jax import lax
from jax.experimental import pallas as pl
from jax.experimental.pallas import tpu as pltpu

F32 = jnp.float32
BF16 = jnp.bfloat16
HIGHEST = lax.Precision.HIGHEST

D_MODEL = 1024
NORM_EPS = 1e-6

GM_CHUNK = 128
GM_GROUPS = 8
GM_GROUP_DIM = D_MODEL // GM_GROUPS

MOE_GROUPS = 4
MOE_EXPERTS_PER_GROUP = 8
MOE_EXPERTS = MOE_GROUPS * MOE_EXPERTS_PER_GROUP
MOE_FF = D_MODEL // 2

VMEM_LIMIT_BYTES = 56 * 1024 * 1024

TOKEN_TILE = 512
ROUTER_TILE = 512
MOVE_TILE = 512
EXPERT_ROWS = 512


def _params(*semantics):
    return pltpu.CompilerParams(dimension_semantics=semantics, vmem_limit_bytes=VMEM_LIMIT_BYTES,
                                disable_bounds_checks=True)


def _rms(xf):
    return xf * lax.rsqrt(jnp.mean(xf * xf, axis=-1, keepdims=True) + NORM_EPS)


def _modulated_norm(x, gain, shift, scale):
    return _rms(x) * gain * (1.0 + scale) + shift


def _gelu_tanh(x):
    return 0.5 * x * (1.0 + jnp.tanh(math.sqrt(2.0 / math.pi) * (x + 0.044715 * (x * x * x))))


def _silu(x):
    return x * (1.0 / (1.0 + jnp.exp(-x)))


def _adaln_kernel(c_ref, w_ref, b_ref, o_ref):
    o_ref[...] = jnp.dot(_silu(c_ref[...]), w_ref[...], precision=HIGHEST,
                         preferred_element_type=F32) + b_ref[...]


def adaln(c_pad, ada_w, ada_b):
    rows, d = c_pad.shape
    n = ada_w.shape[1]
    tn = d
    return pl.pallas_call(
        _adaln_kernel,
        grid=(n // tn,),
        in_specs=[pl.BlockSpec((rows, d), lambda j: (0, 0)),
                  pl.BlockSpec((d, tn), lambda j: (0, j)),
                  pl.BlockSpec((1, tn), lambda j: (0, j))],
        out_specs=pl.BlockSpec((rows, tn), lambda j: (0, j)),
        out_shape=jax.ShapeDtypeStruct((rows, n), F32),
        compiler_params=_params("arbitrary"),
        name="adaln",
    )(c_pad, ada_w, ada_b.reshape(1, n))


def _gmlp_kernel(x_ref, mod_ref, gain_ref, win_ref, vn_ref, ws_ref, bs_ref, wout_ref, o_ref,
                 u_ref, v_ref, g_ref):
    width = D_MODEL
    x = x_ref[...]
    mod = mod_ref[0]
    h = _modulated_norm(x, gain_ref[...], mod[0:1], mod[1:2])
    z = _gelu_tanh(jnp.dot(h.astype(BF16), win_ref[...], preferred_element_type=F32))
    u_ref[...] = z[:, :width]
    v_ref[...] = (_rms(z[:, width:]) * vn_ref[...]).astype(BF16)
    for c in range(x.shape[0] // GM_CHUNK):
        rows = slice(c * GM_CHUNK, (c + 1) * GM_CHUNK)
        for g in range(GM_GROUPS):
            cols = slice(g * GM_GROUP_DIM, (g + 1) * GM_GROUP_DIM)
            sv = jnp.dot(ws_ref[g], v_ref[rows, cols], preferred_element_type=F32) + bs_ref[:, g:g + 1]
            g_ref[rows, cols] = (u_ref[rows, cols] * sv).astype(BF16)
    y = jnp.dot(g_ref[...], wout_ref[...], preferred_element_type=F32)
    o_ref[...] = x + mod[2:3] * y


def gmlp_layer(x2, mod, gain, w_in, v_norm, w_s, b_s, w_out, tiles_per_batch):
    t, d = x2.shape
    tm = TOKEN_TILE
    ws_causal = jnp.where(jnp.tril(jnp.ones((GM_CHUNK, GM_CHUNK), dtype=bool)), w_s, 0).astype(BF16)
    const2 = lambda i: (0, 0)
    return pl.pallas_call(
        _gmlp_kernel,
        grid=(t // tm,),
        in_specs=[pl.BlockSpec((tm, d), lambda i: (i, 0)),
                  pl.BlockSpec((1, 6, d), lambda i: (i // tiles_per_batch, 0, 0)),
                  pl.BlockSpec((1, d), const2),
                  pl.BlockSpec((d, 2 * d), const2),
                  pl.BlockSpec((1, d), const2),
                  pl.BlockSpec((GM_GROUPS, GM_CHUNK, GM_CHUNK), lambda i: (0, 0, 0)),
                  pl.BlockSpec((GM_CHUNK, GM_GROUPS), const2),
                  pl.BlockSpec((d, d), const2)],
        out_specs=pl.BlockSpec((tm, d), lambda i: (i, 0)),
        out_shape=jax.ShapeDtypeStruct((t, d), F32),
        scratch_shapes=[pltpu.VMEM((tm, d), F32), pltpu.VMEM((tm, d), BF16), pltpu.VMEM((tm, d), BF16)],
        compiler_params=_params("arbitrary"),
        name="gmlp_mixer",
    )(x2, mod, gain.reshape(1, d), w_in.astype(BF16), v_norm.reshape(1, d), ws_causal, b_s.T,
      w_out.astype(BF16))


def _router_kernel(x_ref, mod_ref, gain_ref, rw_ref, rb_ref, eid_ref, gate_ref, rank_ref, cnt_ref, base_ref):
    i = pl.program_id(0)
    tm = x_ref.shape[0]
    ne = MOE_EXPERTS
    npg = MOE_EXPERTS_PER_GROUP

    @pl.when(i == 0)
    def _():
        base_ref[...] = jnp.zeros_like(base_ref)

    mod = mod_ref[0]
    h = _modulated_norm(x_ref[...], gain_ref[...], mod[3:4], mod[4:5])
    lt = lax.dot_general(rw_ref[...], h, (((1,), (1,)), ((), ())), precision=HIGHEST,
                         preferred_element_type=F32) + rb_ref[...]
    lg = [lt[ne + g:ne + g + 1, :] for g in range(MOE_GROUPS)]
    gmax = functools.reduce(jnp.maximum, lg)
    gsum = functools.reduce(lambda a, b: a + b, [jnp.exp(l - gmax) for l in lg])
    pg_top = 1.0 / gsum
    g_sel = jnp.full(gmax.shape, MOE_GROUPS - 1, jnp.int32)
    for g in range(MOE_GROUPS - 2, -1, -1):
        g_sel = jnp.where(lg[g] == gmax, g, g_sel)
    sel = lt[(MOE_GROUPS - 1) * npg:MOE_GROUPS * npg, :]
    for g in range(MOE_GROUPS - 2, -1, -1):
        sel = jnp.where(g_sel == g, lt[g * npg:(g + 1) * npg, :], sel)
    row = lax.broadcasted_iota(jnp.int32, sel.shape, 0)
    m1 = jnp.max(sel, axis=0, keepdims=True)
    i1 = jnp.min(jnp.where(sel == m1, row, npg), axis=0, keepdims=True)
    rest = jnp.where(row == i1, -jnp.inf, sel)
    m2 = jnp.max(rest, axis=0, keepdims=True)
    i2 = jnp.min(jnp.where(rest == m2, row, npg), axis=0, keepdims=True)
    e2 = jnp.exp(m2 - m1)
    inv = pg_top / (1.0 + e2)
    eid = jnp.concatenate([g_sel * npg + i1, g_sel * npg + i2], axis=0)
    eid_ref[...] = eid
    gate_ref[...] = jnp.concatenate([inv, inv * e2], axis=0)

    erow = lax.broadcasted_iota(jnp.int32, (ne, tm), 0)
    before = (lax.broadcasted_iota(jnp.int32, (tm, tm), 0)
              < lax.broadcasted_iota(jnp.int32, (tm, tm), 1)).astype(BF16)
    base = base_ref[...]
    ranks = []
    for k in range(2):
        hit = erow == eid[k:k + 1, :]
        prefix = jnp.dot(hit.astype(BF16), before, preferred_element_type=F32) + base
        ranks.append(jnp.sum(jnp.where(hit, prefix, 0.0), axis=0, keepdims=True))
        base = base + jnp.sum(hit.astype(F32), axis=1, keepdims=True)
    rank_ref[...] = jnp.concatenate(ranks, axis=0).astype(jnp.int32)
    base_ref[...] = base
    cnt_ref[...] = jnp.broadcast_to(base, cnt_ref.shape).astype(jnp.int32)


def moe_router(x2, mod, gain, router_w, router_b, tiles_per_batch_of):
    t, d = x2.shape
    tm = ROUTER_TILE
    ne = MOE_EXPERTS
    nr = ne + 8
    rw = jnp.concatenate([router_w[:, MOE_GROUPS:], router_w[:, :MOE_GROUPS],
                          jnp.zeros((d, nr - ne - MOE_GROUPS), F32)], axis=1).T
    rb = jnp.concatenate([router_b[MOE_GROUPS:], router_b[:MOE_GROUPS],
                          jnp.zeros((nr - ne - MOE_GROUPS,), F32)]).reshape(nr, 1)
    tpb = tiles_per_batch_of(tm)
    const2 = lambda i: (0, 0)
    slot_spec = pl.BlockSpec((2, tm), lambda i: (0, i))
    return pl.pallas_call(
        _router_kernel,
        grid=(t // tm,),
        in_specs=[pl.BlockSpec((tm, d), lambda i: (i, 0)),
                  pl.BlockSpec((1, 6, d), lambda i: (i // tpb, 0, 0)),
                  pl.BlockSpec((1, d), const2),
                  pl.BlockSpec((nr, d), const2),
                  pl.BlockSpec((nr, 1), const2)],
        out_specs=[slot_spec, slot_spec, slot_spec, pl.BlockSpec((ne, 128), const2)],
        out_shape=[jax.ShapeDtypeStruct((2, t), jnp.int32), jax.ShapeDtypeStruct((2, t), F32),
                   jax.ShapeDtypeStruct((2, t), jnp.int32), jax.ShapeDtypeStruct((ne, 128), jnp.int32)],
        scratch_shapes=[pltpu.VMEM((ne, 1), F32)],
        compiler_params=_params("arbitrary"),
        name="moe_router",
    )(x2, mod, gain.reshape(1, d), rw, rb)


def _dispatch_kernel(dest_ref, x_ref, mod_ref, gain_ref, xs_ref, h_ref, sem):
    tm = x_ref.shape[0]
    mod = mod_ref[0]
    h_ref[...] = _modulated_norm(x_ref[...], gain_ref[...], mod[3:4], mod[4:5])

    def issue(r, carry):
        for k in range(2):
            pltpu.make_async_copy(h_ref.at[pl.ds(r, 1)], xs_ref.at[pl.ds(dest_ref[k, r], 1)], sem).start(priority=k)
        return carry

    lax.fori_loop(0, tm, issue, 0, unroll=8)
    for k in range(2):
        pltpu.make_async_copy(h_ref, xs_ref.at[pl.ds(0, tm)], sem).wait()


def moe_dispatch(x2, mod, gain, dest, tiles_per_batch_of):
    t, d = x2.shape
    tm = MOVE_TILE
    tpb = tiles_per_batch_of(tm)
    const2 = lambda i: (0, 0)
    return pl.pallas_call(
        _dispatch_kernel,
        grid=(t // tm,),
        in_specs=[pl.BlockSpec((2, tm), lambda i: (0, i), memory_space=pltpu.SMEM),
                  pl.BlockSpec((tm, d), lambda i: (i, 0)),
                  pl.BlockSpec((1, 6, d), lambda i: (i // tpb, 0, 0)),
                  pl.BlockSpec((1, d), const2)],
        out_specs=pl.BlockSpec(memory_space=pl.ANY),
        out_shape=jax.ShapeDtypeStruct((2 * t, d), F32),
        scratch_shapes=[pltpu.VMEM((tm, d), F32), pltpu.SemaphoreType.DMA(())],
        compiler_params=_params("arbitrary"),
        name="moe_dispatch",
    )(dest, x2, mod, gain.reshape(1, d))


def _expert_kernel(blk_ref, exp_ref, start_ref, nvalid_ref, xs_ref, win_ref, wout_ref, ys_ref,
                   winb_ref, woutb_ref):
    i = pl.program_id(0)
    bm = xs_ref.shape[0]
    e = exp_ref[i]
    blk = blk_ref[i]
    prev = jnp.maximum(i - 1, 0)
    first = i == 0
    lo = start_ref[e]
    hi = start_ref[e + 1]
    row0 = blk * bm
    live = i < nvalid_ref[0]

    @pl.when(jnp.logical_or(first, exp_ref[prev] != e))
    def _():
        winb_ref[...] = win_ref[0].astype(BF16)
        woutb_ref[...] = wout_ref[0].astype(BF16)

    new_block = jnp.logical_or(first, blk_ref[prev] != blk)

    @pl.when(new_block)
    def _():
        ys_ref[...] = jnp.zeros_like(ys_ref)

    @pl.when(live)
    def _():
        a_gl = jnp.dot(xs_ref[...].astype(BF16), winb_ref[...], preferred_element_type=F32)
        mid = (_silu(a_gl[:, :MOE_FF]) * a_gl[:, MOE_FF:]).astype(BF16)
        y = jnp.dot(mid, woutb_ref[...], preferred_element_type=F32)
        rows = row0 + lax.broadcasted_iota(jnp.int32, (bm, 1), 0)
        mine = jnp.logical_and(rows >= lo, rows < hi)
        ys_ref[...] += jnp.where(mine, y, 0.0)


def moe_experts(xs, w_in, w_out, item_block, item_expert, seg_start, n_valid):
    n, d = xs.shape
    bm = EXPERT_ROWS
    n_items = item_block.shape[0]
    grid_spec = pltpu.PrefetchScalarGridSpec(
        num_scalar_prefetch=4,
        grid=(n_items,),
        in_specs=[pl.BlockSpec((bm, d), lambda i, b, e, s, v: (b[i], 0)),
                  pl.BlockSpec((1, d, 2 * MOE_FF), lambda i, b, e, s, v: (e[i], 0, 0)),
                  pl.BlockSpec((1, MOE_FF, d), lambda i, b, e, s, v: (e[i], 0, 0))],
        out_specs=pl.BlockSpec((bm, d), lambda i, b, e, s, v: (b[i], 0)),
        scratch_shapes=[pltpu.VMEM((d, 2 * MOE_FF), BF16), pltpu.VMEM((MOE_FF, d), BF16)],
    )
    return pl.pallas_call(
        _expert_kernel,
        grid_spec=grid_spec,
        out_shape=jax.ShapeDtypeStruct((n, d), F32),
        compiler_params=_params("arbitrary"),
        name="moe_experts",
    )(item_block, item_expert, seg_start, n_valid, xs, w_in, w_out)


def _combine_kernel(dest_ref, x_ref, mod_ref, gate_ref, ys_ref, o_ref, y_ref, sem):
    tm = x_ref.shape[0]

    def issue(r, carry):
        for k in range(2):
            pltpu.make_async_copy(ys_ref.at[pl.ds(dest_ref[k, r], 1)], y_ref.at[k, pl.ds(r, 1)], sem).start(priority=k)
        return carry

    lax.fori_loop(0, tm, issue, 0, unroll=8)
    for k in range(2):
        pltpu.make_async_copy(ys_ref.at[pl.ds(0, tm)], y_ref.at[k], sem).wait()
    gates = gate_ref[...]
    moe = gates[:, 0:1] * y_ref[0] + gates[:, 1:2] * y_ref[1]
    o_ref[...] = x_ref[...] + mod_ref[0][5:6] * moe


def moe_combine(x2, mod, gates_t, dest, ys, tiles_per_batch_of):
    t, d = x2.shape
    tm = MOVE_TILE
    tpb = tiles_per_batch_of(tm)
    return pl.pallas_call(
        _combine_kernel,
        grid=(t // tm,),
        in_specs=[pl.BlockSpec((2, tm), lambda i: (0, i), memory_space=pltpu.SMEM),
                  pl.BlockSpec((tm, d), lambda i: (i, 0)),
                  pl.BlockSpec((1, 6, d), lambda i: (i // tpb, 0, 0)),
                  pl.BlockSpec((tm, 2), lambda i: (i, 0)),
                  pl.BlockSpec(memory_space=pl.ANY)],
        out_specs=pl.BlockSpec((tm, d), lambda i: (i, 0)),
        out_shape=jax.ShapeDtypeStruct((t, d), F32),
        scratch_shapes=[pltpu.VMEM((2, tm, d), F32), pltpu.SemaphoreType.DMA(())],
        compiler_params=_params("arbitrary"),
        name="moe_combine",
    )(dest, x2, mod, gates_t, ys)


def moe_layer(x2, mod, gain, router_w, router_b, w_in, w_out, tiles_per_batch_of):
    t, d = x2.shape
    bm = EXPERT_ROWS
    n_assign = 2 * t
    eid, gate, rank, cnt = moe_router(x2, mod, gain, router_w, router_b, tiles_per_batch_of)
    ne = MOE_EXPERTS
    experts = jnp.arange(ne, dtype=jnp.int32)
    upto = experts[None, :] <= experts[:, None]

    def running_total(v):
        return jnp.sum(jnp.where(upto, v[None, :], 0), axis=1).astype(jnp.int32)

    def lookup(table, idx):
        hit = idx[None] == experts.reshape((ne,) + (1,) * idx.ndim)
        return jnp.sum(jnp.where(hit, table.reshape((ne,) + (1,) * idx.ndim), 0), axis=0).astype(jnp.int32)

    counts = cnt[:, 0]
    seg_end = running_total(counts)
    seg_start = jnp.concatenate([jnp.zeros((1,), jnp.int32), seg_end])
    dest = lookup(seg_start[:ne], eid) + rank
    n_blocks = n_assign // bm
    n_items = n_blocks + ne - 1
    first_blk = seg_start[:ne] // bm
    last_blk = jnp.where(counts > 0, (seg_end - 1) // bm, first_blk - 1)
    per_expert = jnp.maximum(last_blk - first_blk + 1, 0)
    item_end = running_total(per_expert)
    n_valid = item_end[ne - 1]
    item_ids = jnp.arange(n_items, dtype=jnp.int32)
    item_ids_c = jnp.maximum(jnp.minimum(item_ids, n_valid - 1), 0)
    item_expert = jnp.minimum(jnp.sum(item_ids_c[:, None] >= item_end[None, :], axis=1), ne - 1).astype(jnp.int32)
    item_block = lookup(first_blk - (item_end - per_expert), item_expert) + item_ids_c
    xs = moe_dispatch(x2, mod, gain, dest, tiles_per_batch_of)
    ys = moe_experts(xs, w_in, w_out, item_block, item_expert, seg_start,
                     n_valid.reshape(1).astype(jnp.int32))
    return moe_combine(x2, mod, gate.T, dest, ys, tiles_per_batch_of)


SWA_HEAD_DIM = 64
SWA_Q_HEADS = D_MODEL // SWA_HEAD_DIM
SWA_KV_HEADS = 4
SWA_BLOCK = 128
SWA_Q_WIDTH = SWA_Q_HEADS * SWA_HEAD_DIM
SWA_KV_WIDTH = SWA_KV_HEADS * SWA_HEAD_DIM
LANES = 128
HEADS_PER_VREG = LANES // SWA_HEAD_DIM


def _swa_kernel(sink_ref, x_ref, mod_ref, gain_ref, win_ref, qg_ref, kg_ref, ones_ref, wout_ref, o_ref,
                kx_ref, vx_ref, att_ref):
    t = pl.program_id(1)
    tm = x_ref.shape[0]
    blk = SWA_BLOCK
    hd = SWA_HEAD_DIM

    @pl.when(t == 0)
    def _():
        kx_ref[0:blk, :] = jnp.zeros((blk, kx_ref.shape[1]), BF16)
        vx_ref[0:blk, :] = jnp.zeros((blk, vx_ref.shape[1]), BF16)

    x = x_ref[...]
    mod = mod_ref[0]
    h = _modulated_norm(x, gain_ref[...], mod[0:1], mod[1:2]).astype(BF16)
    proj = jnp.dot(h, win_ref[...], preferred_element_type=F32)

    def head_rms(a):
        ss = jnp.dot((a * a).astype(BF16), ones_ref[...], preferred_element_type=F32)
        return a * lax.rsqrt(ss * (1.0 / hd) + NORM_EPS)

    lane = lax.broadcasted_iota(jnp.int32, (tm, LANES), 1)
    low = lane < hd
    for c in range(SWA_Q_WIDTH // LANES):
        cols = slice(c * LANES, (c + 1) * LANES)
        att_ref[:, cols] = (head_rms(proj[:, cols]) * qg_ref[:, cols]).astype(BF16)
    for c in range(SWA_KV_WIDTH // LANES):
        cols = slice(c * LANES, (c + 1) * LANES)
        kc = head_rms(proj[:, SWA_Q_WIDTH + c * LANES:SWA_Q_WIDTH + (c + 1) * LANES]) * kg_ref[:, cols]
        vc = proj[:, SWA_Q_WIDTH + SWA_KV_WIDTH + c * LANES:SWA_Q_WIDTH + SWA_KV_WIDTH + (c + 1) * LANES]
        for ref, a in ((kx_ref, kc), (vx_ref, vc)):
            even_lo = jnp.where(low, a, 0.0)
            odd_hi = jnp.where(low, 0.0, a)
            j0 = HEADS_PER_VREG * c
            ref[blk:blk + tm, (2 * j0) * LANES:(2 * j0 + 1) * LANES] = even_lo.astype(BF16)
            ref[blk:blk + tm, (2 * j0 + 1) * LANES:(2 * j0 + 2) * LANES] = pltpu.roll(even_lo, hd, 1).astype(BF16)
            ref[blk:blk + tm, (2 * j0 + 2) * LANES:(2 * j0 + 3) * LANES] = pltpu.roll(odd_hi, hd, 1).astype(BF16)
            ref[blk:blk + tm, (2 * j0 + 3) * LANES:(2 * j0 + 4) * LANES] = odd_hi.astype(BF16)

    qi = lax.broadcasted_iota(jnp.int32, (blk, 2 * blk), 0)
    kj = lax.broadcasted_iota(jnp.int32, (blk, 2 * blk), 1)
    lane_q = lax.broadcasted_iota(jnp.int32, (blk, LANES), 1) < hd
    pairs_per_kv = SWA_Q_HEADS // SWA_KV_HEADS // HEADS_PER_VREG
    for i in range(tm // blk):
        rows = slice(i * blk, (i + 1) * blk)
        keys = slice(i * blk, (i + 2) * blk)
        floor = jnp.where(t == 0, blk - 1, qi) if i == 0 else qi
        mask = jnp.logical_and(kj > floor, kj <= qi + blk)
        for p in range(SWA_Q_HEADS // HEADS_PER_VREG):
            j = p // pairs_per_kv
            qp = att_ref[rows, p * LANES:(p + 1) * LANES]
            out = None
            inv = []
            for half in range(HEADS_PER_VREG):
                col = (2 * j + half) * LANES
                sc = lax.dot_general(qp, kx_ref[keys, col:col + LANES], (((1,), (1,)), ((), ())),
                                     preferred_element_type=F32)
                sc = jnp.where(mask, sc, -jnp.inf)
                sink = sink_ref[HEADS_PER_VREG * p + half]
                m = jnp.maximum(jnp.max(sc, axis=-1, keepdims=True), sink)
                pr = jnp.exp(sc - m)
                inv.append(1.0 / (jnp.sum(pr, axis=-1, keepdims=True) + jnp.exp(sink - m)))
                pv = jnp.dot(pr.astype(BF16), vx_ref[keys, col:col + LANES], preferred_element_type=F32)
                out = pv if out is None else out + pv
            att_ref[rows, p * LANES:(p + 1) * LANES] = (out * jnp.where(lane_q, inv[0], inv[1])).astype(BF16)
    kx_ref[0:blk, :] = kx_ref[tm:tm + blk, :]
    vx_ref[0:blk, :] = vx_ref[tm:tm + blk, :]
    y = jnp.dot(att_ref[...], wout_ref[...], preferred_element_type=F32)
    o_ref[...] = x + mod[2:3] * y


def swa_layer(x2, mod, gain, w_in, q_norm, k_norm, sinks, w_out, b, s):
    t, d = x2.shape
    tm = TOKEN_TILE
    tpb = s // tm
    hd = SWA_HEAD_DIM
    qg = (jnp.tile(q_norm, SWA_Q_HEADS) * hd ** -0.5).reshape(1, SWA_Q_WIDTH)
    kg = jnp.tile(k_norm, SWA_KV_HEADS).reshape(1, SWA_KV_WIDTH)
    ids = jnp.arange(LANES) // hd
    ones = (ids[:, None] == ids[None, :]).astype(BF16)
    proj_w = SWA_Q_WIDTH + 2 * SWA_KV_WIDTH
    ext_w = 2 * SWA_KV_HEADS * LANES
    const2 = lambda bi, ti: (0, 0)
    tile = lambda bi, ti: (bi * tpb + ti, 0)
    return pl.pallas_call(
        _swa_kernel,
        grid=(b, tpb),
        in_specs=[pl.BlockSpec(memory_space=pltpu.SMEM),
                  pl.BlockSpec((tm, d), tile),
                  pl.BlockSpec((1, 6, d), lambda bi, ti: (bi, 0, 0)),
                  pl.BlockSpec((1, d), const2),
                  pl.BlockSpec((d, proj_w), const2),
                  pl.BlockSpec((1, SWA_Q_WIDTH), const2),
                  pl.BlockSpec((1, SWA_KV_WIDTH), const2),
                  pl.BlockSpec((LANES, LANES), const2),
                  pl.BlockSpec((SWA_Q_WIDTH, d), const2)],
        out_specs=pl.BlockSpec((tm, d), tile),
        out_shape=jax.ShapeDtypeStruct((t, d), F32),
        scratch_shapes=[pltpu.VMEM((tm + SWA_BLOCK, ext_w), BF16), pltpu.VMEM((tm + SWA_BLOCK, ext_w), BF16),
                        pltpu.VMEM((tm, SWA_Q_WIDTH), BF16)],
        compiler_params=_params("arbitrary", "arbitrary"),
        name="swa_mixer",
    )(sinks, x2, mod, gain.reshape(1, d), w_in.astype(BF16), qg, kg, ones, w_out.astype(BF16))


DN_QK_HEADS = 4
DN_V_HEADS = 8
DN_HEAD_DIM = D_MODEL // DN_V_HEADS
DN_CONV = 4
DN_CHUNK = 64
DN_KEY_WIDTH = DN_QK_HEADS * DN_HEAD_DIM
DN_VAL_WIDTH = DN_V_HEADS * DN_HEAD_DIM
DN_CONV_WIDTH = 2 * DN_KEY_WIDTH + DN_VAL_WIDTH
DN_PREP_TILE = 128
DN_SCAN_TILE = 128
SUBLANES = 8


def _split_bf16(a):
    hi = a.astype(BF16)
    return hi, (a - hi.astype(F32)).astype(BF16)


def _mm(a, b):
    return jnp.dot(a.astype(BF16), b.astype(BF16), preferred_element_type=F32)


def _dn_proj_kernel(x_ref, mod_ref, gain_ref, w_ref, wba_ref, conv_ref, rate_ref, dtb_ref,
                    q_ref, k_ref, v_ref, z_ref, bg_ref, ext_ref):
    t = pl.program_id(1)
    tm = x_ref.shape[0]
    pad = SUBLANES

    @pl.when(t == 0)
    def _():
        ext_ref[0:pad, :] = jnp.zeros((pad, ext_ref.shape[1]), F32)

    mod = mod_ref[0]
    h = _modulated_norm(x_ref[...], gain_ref[...], mod[0:1], mod[1:2])
    hh, hl = _split_bf16(h)
    proj = jnp.dot(hh, w_ref[...], preferred_element_type=F32)
    z_ref[...] = proj[:, DN_CONV_WIDTH:].astype(BF16)

    ext_ref[pad:pad + tm, :] = proj[:, :DN_CONV_WIDTH]
    acc = None
    for j in range(DN_CONV):
        start = pad - (DN_CONV - 1) + j
        term = conv_ref[j:j + 1, :] * ext_ref[start:start + tm, :]
        acc = term if acc is None else acc + term
    ext_ref[0:pad, :] = ext_ref[tm:tm + pad, :]
    qkv = _silu(acc)
    for hq in range(2 * DN_QK_HEADS):
        cols = slice(hq * DN_HEAD_DIM, (hq + 1) * DN_HEAD_DIM)
        a = qkv[:, cols]
        a = a * lax.rsqrt(jnp.sum(a * a, axis=-1, keepdims=True) + NORM_EPS)
        if hq < DN_QK_HEADS:
            q_ref[:, cols] = (a * DN_HEAD_DIM ** -0.5).astype(BF16)
        else:
            k_ref[:, hq * DN_HEAD_DIM - DN_KEY_WIDTH:(hq + 1) * DN_HEAD_DIM - DN_KEY_WIDTH] = a.astype(BF16)
    v_ref[...] = qkv[:, 2 * DN_KEY_WIDTH:].astype(BF16)

    ba = jnp.dot(hh, wba_ref[...], preferred_element_type=F32) + jnp.dot(hl, wba_ref[...], preferred_element_type=F32)
    ba = ba + pltpu.roll(ba, LANES - 2 * DN_V_HEADS, 1)
    lane = lax.broadcasted_iota(jnp.int32, ba.shape, 1)
    beta = 1.0 / (1.0 + jnp.exp(-ba))
    sp = ba + dtb_ref[...]
    g = -rate_ref[...] * (jnp.maximum(sp, 0.0) + jnp.log1p(jnp.exp(-jnp.abs(sp))))
    g = jnp.where(jnp.logical_and(lane >= DN_V_HEADS, lane < 2 * DN_V_HEADS), g, 0.0)
    r = lax.broadcasted_iota(jnp.int32, (tm, tm), 0)
    c = lax.broadcasted_iota(jnp.int32, (tm, tm), 1)
    tri = jnp.logical_and(r // DN_CHUNK == c // DN_CHUNK, c <= r).astype(F32)
    gc = jnp.dot(tri, g, precision=HIGHEST, preferred_element_type=F32)
    bg_ref[...] = jnp.where(lane < DN_V_HEADS, beta, gc)


def _unit_lower_inverses(mats):
    n = mats[0].shape[0]
    r = lax.broadcasted_iota(jnp.int32, (n, n), 0)
    c = lax.broadcasted_iota(jnp.int32, (n, n), 1)
    eye = (r == c).astype(F32)
    size = SUBLANES
    same = (r // size) == (c // size)
    d = [jnp.where(same, a, 0.0) for a in mats]
    d2 = [_mm(v, v) for v in d]
    d4 = [_mm(v, v) for v in d2]
    x = [_mm(eye - v, eye + v2) for v, v2 in zip(d, d2)]
    x = [_mm(v, eye + v4) for v, v4 in zip(x, d4)]
    while size < n:
        wider = (r // (2 * size)) == (c // (2 * size))
        ring = jnp.logical_and(wider, jnp.logical_not(same))
        xl = [_mm(v, jnp.where(ring, a, 0.0)) for v, a in zip(x, mats)]
        x = [v - _mm(vl, v) for v, vl in zip(x, xl)]
        same = wider
        size *= 2
    return x


def _dn_prep_kernel(q_ref, k_ref, v_ref, bg_ref, gct_ref, u_ref, w_ref, qk_ref):
    tm = q_ref.shape[0]
    ck = DN_CHUNK
    hd = DN_HEAD_DIM
    rep = DN_V_HEADS // DN_QK_HEADS
    r = lax.broadcasted_iota(jnp.int32, (ck, ck), 0)
    c = lax.broadcasted_iota(jnp.int32, (ck, ck), 1)
    lower = c <= r
    strict = c < r
    dot = functools.partial(jnp.dot, preferred_element_type=F32)
    chunks = [slice(ci * ck, (ci + 1) * ck) for ci in range(tm // ck)]
    gram = {}
    for rows in chunks:
        for hq in range(DN_QK_HEADS):
            cols = slice(hq * hd, (hq + 1) * hd)
            k = k_ref[rows, cols]
            gram[(hq, rows.start)] = lax.dot_general(jnp.concatenate([k, q_ref[rows, cols]], axis=0), k,
                                                     (((1,), (1,)), ((), ())), preferred_element_type=F32)
    problems = [(hv, rows) for rows in chunks for hv in range(DN_V_HEADS)]
    mats, rhs = [], []
    for hv, rows in problems:
        beta = bg_ref[rows, hv:hv + 1]
        gcc = bg_ref[rows, DN_V_HEADS + hv:DN_V_HEADS + hv + 1]
        gcr = gct_ref[hv:hv + 1, rows]
        decay = jnp.where(lower, jnp.exp(jnp.where(lower, gcc - gcr, 0.0)), 0.0)
        g = gram[(hv // rep, rows.start)]
        mats.append(jnp.where(strict, g[:ck] * beta * decay, 0.0))
        qk_ref[rows, hv * ck:(hv + 1) * ck] = (g[ck:] * decay).astype(BF16)
        kf = k_ref[rows, (hv // rep) * hd:(hv // rep + 1) * hd].astype(F32)
        vf = v_ref[rows, hv * hd:(hv + 1) * hd].astype(F32)
        rhs.append(jnp.concatenate([vf * beta, kf * (beta * jnp.exp(gcc))], axis=1).astype(BF16))
    inverses = _unit_lower_inverses(mats)
    for (hv, rows), tinv, b in zip(problems, inverses, rhs):
        uw = dot(tinv.astype(BF16), b)
        u_ref[rows, hv * hd:(hv + 1) * hd] = uw[:, :hd].astype(BF16)
        w_ref[rows, hv * hd:(hv + 1) * hd] = uw[:, hd:].astype(BF16)


def _dn_scan_kernel(x_ref, mod_ref, u_ref, w_ref, qk_ref, q_ref, k_ref, z_ref, bg_ref, og_ref, wout_ref, o_ref,
                    state_ref, att_ref):
    t = pl.program_id(0)
    nb, tm, _ = x_ref.shape
    ck = DN_CHUNK
    hd = DN_HEAD_DIM
    rep = DN_V_HEADS // DN_QK_HEADS

    @pl.when(t == 0)
    def _():
        state_ref[...] = jnp.zeros_like(state_ref)

    dot = functools.partial(jnp.dot, preferred_element_type=F32)
    chains = [(bi, hv) for bi in range(nb) for hv in range(DN_V_HEADS)]
    for ci in range(tm // ck):
        rows = slice(ci * ck, (ci + 1) * ck)
        gcc, g_end, lhs = [], [], []
        for bi, hv in chains:
            gc = bg_ref[bi, rows, DN_V_HEADS + hv:DN_V_HEADS + hv + 1]
            gcc.append(gc)
            g_end.append(gc[ck - 1:ck, :])
            q_dec = q_ref[bi, rows, (hv // rep) * hd:(hv // rep + 1) * hd].astype(F32) * jnp.exp(gc)
            lhs.append(jnp.concatenate([w_ref[bi, rows, hv * hd:(hv + 1) * hd], q_dec.astype(BF16)], axis=0))
        ws_qs = [dot(a, state_ref[n].astype(BF16)) for n, a in enumerate(lhs)]
        v_new = [(u_ref[bi, rows, hv * hd:(hv + 1) * hd].astype(F32) - m[:ck]).astype(BF16)
                 for (bi, hv), m in zip(chains, ws_qs)]
        outs = [m[ck:] + dot(qk_ref[bi, rows, hv * ck:(hv + 1) * ck], vn)
                for (bi, hv), m, vn in zip(chains, ws_qs, v_new)]
        for n, (bi, hv) in enumerate(chains):
            kf = k_ref[bi, rows, (hv // rep) * hd:(hv // rep + 1) * hd].astype(F32)
            k_dec = (kf * jnp.exp(g_end[n] - gcc[n])).astype(BF16)
            state_ref[n] = state_ref[n] * jnp.exp(g_end[n]) + lax.dot_general(
                k_dec, v_new[n], (((0,), (0,)), ((), ())), preferred_element_type=F32)
        for (bi, hv), o in zip(chains, outs):
            zf = z_ref[bi, rows, hv * hd:(hv + 1) * hd].astype(F32)
            att_ref[bi * tm + ci * ck:bi * tm + (ci + 1) * ck, hv * hd:(hv + 1) * hd] = (
                _rms(o) * og_ref[...] * _silu(zf)).astype(BF16)
    y = dot(att_ref[...], wout_ref[...])
    for bi in range(nb):
        o_ref[bi] = x_ref[bi] + mod_ref[bi][2:3] * y[bi * tm:(bi + 1) * tm]


def deltanet_layer(x2, mod, gain, w_in, conv_w, a_log, dt_bias, o_norm, w_out, b, s):
    t, d = x2.shape
    nh = DN_V_HEADS
    main_w = DN_CONV_WIDTH + DN_VAL_WIDTH
    w_main = w_in[:, :main_w].astype(BF16)
    w_ba = w_in[:, main_w:]
    w_ba_hi = w_ba.astype(BF16)
    w_ba_lo = (w_ba - w_ba_hi.astype(F32)).astype(BF16)
    w_ba2 = jnp.concatenate([w_ba_hi, w_ba_lo, jnp.zeros((d, LANES - 4 * nh), BF16)], axis=1)
    lanes_pad = lambda v: jnp.concatenate([jnp.zeros((nh,), F32), v.astype(F32),
                                           jnp.zeros((LANES - 2 * nh,), F32)]).reshape(1, LANES)
    rate = lanes_pad(jnp.exp(a_log.astype(F32)))
    dtb = lanes_pad(dt_bias)

    tm = TOKEN_TILE
    tpb = s // tm
    const2 = lambda bi, ti: (0, 0)
    tile = lambda bi, ti: (bi * tpb + ti, 0)
    q, k, v, z, bg = pl.pallas_call(
        _dn_proj_kernel,
        grid=(b, tpb),
        in_specs=[pl.BlockSpec((tm, d), tile),
                  pl.BlockSpec((1, 6, d), lambda bi, ti: (bi, 0, 0)),
                  pl.BlockSpec((1, d), const2),
                  pl.BlockSpec((d, main_w), const2),
                  pl.BlockSpec((d, LANES), const2),
                  pl.BlockSpec((DN_CONV, DN_CONV_WIDTH), const2),
                  pl.BlockSpec((1, LANES), const2),
                  pl.BlockSpec((1, LANES), const2)],
        out_specs=[pl.BlockSpec((tm, DN_KEY_WIDTH), tile), pl.BlockSpec((tm, DN_KEY_WIDTH), tile),
                   pl.BlockSpec((tm, DN_VAL_WIDTH), tile), pl.BlockSpec((tm, DN_VAL_WIDTH), tile),
                   pl.BlockSpec((tm, LANES), tile)],
        out_shape=[jax.ShapeDtypeStruct((t, DN_KEY_WIDTH), BF16), jax.ShapeDtypeStruct((t, DN_KEY_WIDTH), BF16),
                   jax.ShapeDtypeStruct((t, DN_VAL_WIDTH), BF16), jax.ShapeDtypeStruct((t, DN_VAL_WIDTH), BF16),
                   jax.ShapeDtypeStruct((t, LANES), F32)],
        scratch_shapes=[pltpu.VMEM((tm + SUBLANES, DN_CONV_WIDTH), F32)],
        compiler_params=_params("arbitrary", "arbitrary"),
        name="deltanet_proj",
    )(x2, mod, gain.reshape(1, d), w_main, w_ba2, conv_w, rate, dtb)

    gct = bg[:, nh:2 * nh].T
    tp = DN_PREP_TILE
    rows_of = lambda width: pl.BlockSpec((tp, width), lambda i: (i, 0))
    qk_w = nh * DN_CHUNK
    u, w, qk = pl.pallas_call(
        _dn_prep_kernel,
        grid=(t // tp,),
        in_specs=[rows_of(DN_KEY_WIDTH), rows_of(DN_KEY_WIDTH), rows_of(DN_VAL_WIDTH), rows_of(LANES),
                  pl.BlockSpec((nh, tp), lambda i: (0, i))],
        out_specs=[rows_of(DN_VAL_WIDTH), rows_of(DN_VAL_WIDTH), rows_of(qk_w)],
        out_shape=[jax.ShapeDtypeStruct((t, DN_VAL_WIDTH), BF16), jax.ShapeDtypeStruct((t, DN_VAL_WIDTH), BF16),
                   jax.ShapeDtypeStruct((t, qk_w), BF16)],
        compiler_params=_params("arbitrary"),
        name="deltanet_prep",
    )(q, k, v, bg, gct)

    ts = DN_SCAN_TILE
    seq = lambda a: a.reshape(b, s, a.shape[-1])
    both = lambda width: pl.BlockSpec((b, ts, width), lambda i: (0, i, 0))
    out = pl.pallas_call(
        _dn_scan_kernel,
        grid=(s // ts,),
        in_specs=[both(d),
                  pl.BlockSpec((b, 6, d), lambda i: (0, 0, 0)),
                  both(DN_VAL_WIDTH), both(DN_VAL_WIDTH), both(qk_w), both(DN_KEY_WIDTH), both(DN_KEY_WIDTH),
                  both(DN_VAL_WIDTH), both(LANES),
                  pl.BlockSpec((1, DN_HEAD_DIM), lambda i: (0, 0)),
                  pl.BlockSpec((DN_VAL_WIDTH, d), lambda i: (0, 0))],
        out_specs=both(d),
        out_shape=jax.ShapeDtypeStruct((b, s, d), F32),
        scratch_shapes=[pltpu.VMEM((b * nh, DN_HEAD_DIM, DN_HEAD_DIM), F32),
                        pltpu.VMEM((b * ts, DN_VAL_WIDTH), BF16)],
        compiler_params=_params("arbitrary"),
        name="deltanet_scan",
    )(seq(x2), mod, seq(u), seq(w), seq(qk), seq(q), seq(k), seq(z), seq(bg), o_norm.reshape(1, DN_HEAD_DIM),
      w_out.astype(BF16))
    return out.reshape(t, d)


def kernel(x, c, l0_norm_mix, l0_norm_ffn, l0_ada_w, l0_ada_b, l0_gm_w_in, l0_gm_v_norm, l0_gm_w_s, l0_gm_b_s, l0_gm_w_out, l0_router_w, l0_router_b, l0_expert_w_in, l0_expert_w_out, l1_norm_mix, l1_norm_ffn, l1_ada_w, l1_ada_b, l1_dn_w_in, l1_dn_conv_w, l1_dn_a_log, l1_dn_dt_bias, l1_dn_o_norm, l1_dn_w_out, l1_router_w, l1_router_b, l1_expert_w_in, l1_expert_w_out, l2_norm_mix, l2_norm_ffn, l2_ada_w, l2_ada_b, l2_swa_w_in, l2_swa_q_norm, l2_swa_k_norm, l2_swa_sinks, l2_swa_w_out, l2_router_w, l2_router_b, l2_expert_w_in, l2_expert_w_out, l3_norm_mix, l3_norm_ffn, l3_ada_w, l3_ada_b, l3_gm_w_in, l3_gm_v_norm, l3_gm_w_s, l3_gm_b_s, l3_gm_w_out, l3_router_w, l3_router_b, l3_expert_w_in, l3_expert_w_out):
    b, s, d = x.shape
    x2 = x.reshape(b * s, d)
    c_pad = jnp.zeros((8, d), F32).at[:b].set(c)
    tiles_per_batch_of = lambda tm: s // tm

    def modulation(ada_w, ada_b):
        return adaln(c_pad, ada_w, ada_b)[:b].reshape(b, 6, d)

    mod = modulation(l0_ada_w, l0_ada_b)
    x2 = gmlp_layer(x2, mod, l0_norm_mix, l0_gm_w_in, l0_gm_v_norm, l0_gm_w_s, l0_gm_b_s, l0_gm_w_out,
                    tiles_per_batch_of(TOKEN_TILE))
    x2 = moe_layer(x2, mod, l0_norm_ffn, l0_router_w, l0_router_b, l0_expert_w_in, l0_expert_w_out,
                   tiles_per_batch_of)

    mod = modulation(l1_ada_w, l1_ada_b)
    x2 = deltanet_layer(x2, mod, l1_norm_mix, l1_dn_w_in, l1_dn_conv_w, l1_dn_a_log, l1_dn_dt_bias,
                        l1_dn_o_norm, l1_dn_w_out, b, s)
    x2 = moe_layer(x2, mod, l1_norm_ffn, l1_router_w, l1_router_b, l1_expert_w_in, l1_expert_w_out,
                   tiles_per_batch_of)

    mod = modulation(l2_ada_w, l2_ada_b)
    x2 = swa_layer(x2, mod, l2_norm_mix, l2_swa_w_in, l2_swa_q_norm, l2_swa_k_norm, l2_swa_sinks,
                   l2_swa_w_out, b, s)
    x2 = moe_layer(x2, mod, l2_norm_ffn, l2_router_w, l2_router_b, l2_expert_w_in, l2_expert_w_out,
                   tiles_per_batch_of)

    mod = modulation(l3_ada_w, l3_ada_b)
    x2 = gmlp_layer(x2, mod, l3_norm_mix, l3_gm_w_in, l3_gm_v_norm, l3_gm_w_s, l3_gm_b_s, l3_gm_w_out,
                    tiles_per_batch_of(TOKEN_TILE))
    x2 = moe_layer(x2, mod, l3_norm_ffn, l3_router_w, l3_router_b, l3_expert_w_in, l3_expert_w_out,
                   tiles_per_batch_of)
    return x2.reshape(b, s, d)
```

```python
import functools
import math

import jax
import jax.numpy as jnp
from jax import lax
from jax.experimental import pallas as pl
from jax.experimental.pallas import tpu as pltpu

F32 = jnp.float32
BF16 = jnp.bfloat16
HIGHEST = lax.Precision.HIGHEST

D_MODEL = 1024
NORM_EPS = 1e-6

GM_CHUNK = 128
GM_GROUPS = 8
GM_GROUP_DIM = D_MODEL // GM_GROUPS

MOE_GROUPS = 4
MOE_EXPERTS_PER_GROUP = 8
MOE_EXPERTS = MOE_GROUPS * MOE_EXPERTS_PER_GROUP
MOE_FF = D_MODEL // 2

VMEM_LIMIT_BYTES = 56 * 1024 * 1024

TOKEN_TILE = 512
ROUTER_TILE = 512
MOVE_TILE = 512
EXPERT_ROWS = 512
EXPERT_SUB_ROWS = 128
LANES = 128
SUBLANES = 8


def _params(*semantics):
    return pltpu.CompilerParams(dimension_semantics=semantics, vmem_limit_bytes=VMEM_LIMIT_BYTES,
                                disable_bounds_checks=True)


def _rms(xf):
    return xf * lax.rsqrt(jnp.mean(xf * xf, axis=-1, keepdims=True) + NORM_EPS)


def _modulated_norm(x, gain, shift, scale):
    return _rms(x) * gain * (1.0 + scale) + shift


def _gelu_tanh(x):
    return 0.5 * x * (1.0 + jnp.tanh(math.sqrt(2.0 / math.pi) * (x + 0.044715 * (x * x * x))))


def _silu(x):
    return x * (1.0 / (1.0 + jnp.exp(-x)))


def _adaln_kernel(c_ref, w_ref, b_ref, o_ref):
    o_ref[...] = jnp.dot(_silu(c_ref[...]), w_ref[...], precision=HIGHEST,
                         preferred_element_type=F32) + b_ref[...]


def adaln(c_pad, ada_w, ada_b):
    rows, d = c_pad.shape
    n = ada_w.shape[1]
    tn = d
    return pl.pallas_call(
        _adaln_kernel,
        grid=(n // tn,),
        in_specs=[pl.BlockSpec((rows, d), lambda j: (0, 0)),
                  pl.BlockSpec((d, tn), lambda j: (0, j)),
                  pl.BlockSpec((1, tn), lambda j: (0, j))],
        out_specs=pl.BlockSpec((rows, tn), lambda j: (0, j)),
        out_shape=jax.ShapeDtypeStruct((rows, n), F32),
        compiler_params=_params("arbitrary"),
        name="adaln",
    )(c_pad, ada_w, ada_b.reshape(1, n))


def _gmlp_kernel(x_ref, mod_ref, gain_ref, win_ref, vn_ref, ws_ref, bs_ref, wout_ref, o_ref,
                 u_ref, v_ref, g_ref):
    width = D_MODEL
    x = x_ref[...]
    mod = mod_ref[0]
    h = _modulated_norm(x, gain_ref[...], mod[0:1], mod[1:2])
    z = _gelu_tanh(jnp.dot(h.astype(BF16), win_ref[...], preferred_element_type=F32))
    u_ref[...] = z[:, :width]
    v_ref[...] = (_rms(z[:, width:]) * vn_ref[...]).astype(BF16)
    cells = [(slice(c * GM_CHUNK, (c + 1) * GM_CHUNK), g, slice(g * GM_GROUP_DIM, (g + 1) * GM_GROUP_DIM))
             for c in range(x.shape[0] // GM_CHUNK) for g in range(GM_GROUPS)]
    mixed = [jnp.dot(ws_ref[g], v_ref[rows, cols], preferred_element_type=F32) for rows, g, cols in cells]
    for (rows, g, cols), sv in zip(cells, mixed):
        g_ref[rows, cols] = (u_ref[rows, cols] * (sv + bs_ref[:, g:g + 1])).astype(BF16)
    y = jnp.dot(g_ref[...], wout_ref[...], preferred_element_type=F32)
    o_ref[...] = x + mod[2:3] * y


def gmlp_layer(x2, mod, gain, w_in, v_norm, w_s, b_s, w_out, tiles_per_batch):
    t, d = x2.shape
    tm = TOKEN_TILE
    ws_causal = jnp.where(jnp.tril(jnp.ones((GM_CHUNK, GM_CHUNK), dtype=bool)), w_s, 0).astype(BF16)
    const2 = lambda i: (0, 0)
    return pl.pallas_call(
        _gmlp_kernel,
        grid=(t // tm,),
        in_specs=[pl.BlockSpec((tm, d), lambda i: (i, 0)),
                  pl.BlockSpec((1, 6, d), lambda i: (i // tiles_per_batch, 0, 0)),
                  pl.BlockSpec((1, d), const2),
                  pl.BlockSpec((d, 2 * d), const2),
                  pl.BlockSpec((1, d), const2),
                  pl.BlockSpec((GM_GROUPS, GM_CHUNK, GM_CHUNK), lambda i: (0, 0, 0)),
                  pl.BlockSpec((GM_CHUNK, GM_GROUPS), const2),
                  pl.BlockSpec((d, d), const2)],
        out_specs=pl.BlockSpec((tm, d), lambda i: (i, 0)),
        out_shape=jax.ShapeDtypeStruct((t, d), F32),
        scratch_shapes=[pltpu.VMEM((tm, d), F32), pltpu.VMEM((tm, d), BF16), pltpu.VMEM((tm, d), BF16)],
        compiler_params=_params("arbitrary"),
        name="gmlp_mixer",
    )(x2, mod, gain.reshape(1, d), w_in.astype(BF16), v_norm.reshape(1, d), ws_causal, b_s.T,
      w_out.astype(BF16))


def _router_kernel(x_ref, mod_ref, gain_ref, rw_ref, rb_ref, eid_ref, gate_ref, rank_ref, cnt_ref, base_ref):
    i = pl.program_id(0)
    tm = x_ref.shape[0]
    ne = MOE_EXPERTS
    npg = MOE_EXPERTS_PER_GROUP

    @pl.when(i == 0)
    def _():
        base_ref[...] = jnp.zeros_like(base_ref)

    mod = mod_ref[0]
    h = _modulated_norm(x_ref[...], gain_ref[...], mod[3:4], mod[4:5])
    lt = lax.dot_general(rw_ref[...], h, (((1,), (1,)), ((), ())), precision=HIGHEST,
                         preferred_element_type=F32) + rb_ref[...]
    lg = [lt[ne + g:ne + g + 1, :] for g in range(MOE_GROUPS)]
    gmax = functools.reduce(jnp.maximum, lg)
    gsum = functools.reduce(lambda a, b: a + b, [jnp.exp(l - gmax) for l in lg])
    pg_top = 1.0 / gsum
    g_sel = jnp.full(gmax.shape, MOE_GROUPS - 1, jnp.int32)
    for g in range(MOE_GROUPS - 2, -1, -1):
        g_sel = jnp.where(lg[g] == gmax, g, g_sel)
    sel = lt[(MOE_GROUPS - 1) * npg:MOE_GROUPS * npg, :]
    for g in range(MOE_GROUPS - 2, -1, -1):
        sel = jnp.where(g_sel == g, lt[g * npg:(g + 1) * npg, :], sel)
    row = lax.broadcasted_iota(jnp.int32, sel.shape, 0)
    m1 = jnp.max(sel, axis=0, keepdims=True)
    i1 = jnp.min(jnp.where(sel == m1, row, npg), axis=0, keepdims=True)
    rest = jnp.where(row == i1, -jnp.inf, sel)
    m2 = jnp.max(rest, axis=0, keepdims=True)
    i2 = jnp.min(jnp.where(rest == m2, row, npg), axis=0, keepdims=True)
    e2 = jnp.exp(m2 - m1)
    inv = pg_top / (1.0 + e2)
    eid = jnp.concatenate([g_sel * npg + i1, g_sel * npg + i2], axis=0)
    eid_ref[...] = eid
    gate_ref[...] = jnp.concatenate([inv, inv * e2], axis=0)

    erow = lax.broadcasted_iota(jnp.int32, (ne, tm), 0)
    before = (lax.broadcasted_iota(jnp.int32, (tm, tm), 0)
              < lax.broadcasted_iota(jnp.int32, (tm, tm), 1)).astype(BF16)
    base = base_ref[...]
    ranks = []
    for k in range(2):
        hit = erow == eid[k:k + 1, :]
        prefix = jnp.dot(hit.astype(BF16), before, preferred_element_type=F32) + base
        ranks.append(jnp.sum(jnp.where(hit, prefix, 0.0), axis=0, keepdims=True))
        base = base + jnp.sum(hit.astype(F32), axis=1, keepdims=True)
    rank_ref[...] = jnp.concatenate(ranks, axis=0).astype(jnp.int32)
    base_ref[...] = base
    cnt_ref[...] = jnp.broadcast_to(base, cnt_ref.shape).astype(jnp.int32)


def moe_router(x2, mod, gain, router_w, router_b, tiles_per_batch_of):
    t, d = x2.shape
    tm = ROUTER_TILE
    ne = MOE_EXPERTS
    nr = ne + 8
    rw = jnp.concatenate([router_w[:, MOE_GROUPS:], router_w[:, :MOE_GROUPS],
                          jnp.zeros((d, nr - ne - MOE_GROUPS), F32)], axis=1).T
    rb = jnp.concatenate([router_b[MOE_GROUPS:], router_b[:MOE_GROUPS],
                          jnp.zeros((nr - ne - MOE_GROUPS,), F32)]).reshape(nr, 1)
    tpb = tiles_per_batch_of(tm)
    const2 = lambda i: (0, 0)
    slot_spec = pl.BlockSpec((2, tm), lambda i: (0, i))
    return pl.pallas_call(
        _router_kernel,
        grid=(t // tm,),
        in_specs=[pl.BlockSpec((tm, d), lambda i: (i, 0)),
                  pl.BlockSpec((1, 6, d), lambda i: (i // tpb, 0, 0)),
                  pl.BlockSpec((1, d), const2),
                  pl.BlockSpec((nr, d), const2),
                  pl.BlockSpec((nr, 1), const2)],
        out_specs=[slot_spec, slot_spec, slot_spec, pl.BlockSpec((ne, 128), const2)],
        out_shape=[jax.ShapeDtypeStruct((2, t), jnp.int32), jax.ShapeDtypeStruct((2, t), F32),
                   jax.ShapeDtypeStruct((2, t), jnp.int32), jax.ShapeDtypeStruct((ne, 128), jnp.int32)],
        scratch_shapes=[pltpu.VMEM((ne, 1), F32)],
        compiler_params=_params("arbitrary"),
        name="moe_router",
    )(x2, mod, gain.reshape(1, d), rw, rb)


def _dispatch_kernel(d0_ref, d1_ref, x_ref, mod_ref, gain_ref, xs_ref, h_ref, sem):
    tm, d = x_ref.shape
    mod = mod_ref[0]
    h_ref[...] = _modulated_norm(x_ref[...], gain_ref[...], mod[3:4], mod[4:5]).reshape(tm // SUBLANES, SUBLANES, d)

    def issue(g, carry):
        for u in range(SUBLANES):
            for k, dest_ref in enumerate((d0_ref, d1_ref)):
                pltpu.make_async_copy(h_ref.at[g, pl.ds(u, 1)],
                                      xs_ref.at[pl.ds(dest_ref[g * SUBLANES + u], 1)], sem).start(priority=k)
        return carry

    lax.fori_loop(0, tm // SUBLANES, issue, 0)
    rows = xs_ref.at[pl.ds(0, tm)]
    for k in range(2):
        pltpu.make_async_copy(rows, rows, sem).wait()


def moe_dispatch(x2, mod, gain, dest, tiles_per_batch_of):
    t, d = x2.shape
    tm = MOVE_TILE
    tpb = tiles_per_batch_of(tm)
    const2 = lambda i: (0, 0)
    slot_spec = pl.BlockSpec((tm,), lambda i: (i,), memory_space=pltpu.SMEM)
    return pl.pallas_call(
        _dispatch_kernel,
        grid=(t // tm,),
        in_specs=[slot_spec, slot_spec,
                  pl.BlockSpec((tm, d), lambda i: (i, 0)),
                  pl.BlockSpec((1, 6, d), lambda i: (i // tpb, 0, 0)),
                  pl.BlockSpec((1, d), const2)],
        out_specs=pl.BlockSpec(memory_space=pl.ANY),
        out_shape=jax.ShapeDtypeStruct((2 * t, d), F32),
        scratch_shapes=[pltpu.VMEM((tm // SUBLANES, SUBLANES, d), F32), pltpu.SemaphoreType.DMA(())],
        compiler_params=_params("arbitrary"),
        name="moe_dispatch",
    )(dest[0], dest[1], x2, mod, gain.reshape(1, d))


def _expert_kernel(blk_ref, exp_ref, start_ref, nvalid_ref, xs_ref, win_ref, wout_ref, ys_ref,
                   winb_ref, woutb_ref):
    i = pl.program_id(0)
    bm = xs_ref.shape[0]
    e = exp_ref[i]
    blk = blk_ref[i]
    prev = jnp.maximum(i - 1, 0)
    first = i == 0
    lo = start_ref[e]
    hi = start_ref[e + 1]
    row0 = blk * bm
    live = i < nvalid_ref[0]

    @pl.when(jnp.logical_or(first, exp_ref[prev] != e))
    def _():
        winb_ref[...] = win_ref[0].astype(BF16)
        woutb_ref[...] = wout_ref[0].astype(BF16)

    new_block = jnp.logical_or(first, blk_ref[prev] != blk)
    whole = jnp.logical_and(lo <= row0, hi >= row0 + bm)

    def ffn(x):
        a_gl = jnp.dot(x.astype(BF16), winb_ref[...], preferred_element_type=F32)
        mid = (_silu(a_gl[:, :MOE_FF]) * a_gl[:, MOE_FF:]).astype(BF16)
        return jnp.dot(mid, woutb_ref[...], preferred_element_type=F32)

    @pl.when(jnp.logical_and(live, whole))
    def _():
        ys_ref[...] = ffn(xs_ref[...])

    @pl.when(jnp.logical_and(new_block, jnp.logical_not(whole)))
    def _():
        ys_ref[...] = jnp.zeros_like(ys_ref)

    for sub in range(bm // EXPERT_SUB_ROWS):
        sub0 = row0 + sub * EXPERT_SUB_ROWS
        touched = jnp.logical_and(hi > sub0, lo < sub0 + EXPERT_SUB_ROWS)

        @pl.when(jnp.logical_and(jnp.logical_and(live, jnp.logical_not(whole)), touched))
        def _():
            sl = pl.ds(sub * EXPERT_SUB_ROWS, EXPERT_SUB_ROWS)
            rows = sub0 + lax.broadcasted_iota(jnp.int32, (EXPERT_SUB_ROWS, 1), 0)
            mine = jnp.logical_and(rows >= lo, rows < hi)
            ys_ref[sl, :] += jnp.where(mine, ffn(xs_ref[sl, :]), 0.0)


def moe_experts(xs, w_in, w_out, item_block, item_expert, seg_start, n_valid):
    n, d = xs.shape
    bm = EXPERT_ROWS
    n_items = item_block.shape[0]
    grid_spec = pltpu.PrefetchScalarGridSpec(
        num_scalar_prefetch=4,
        grid=(n_items,),
        in_specs=[pl.BlockSpec((bm, d), lambda i, b, e, s, v: (b[i], 0)),
                  pl.BlockSpec((1, d, 2 * MOE_FF), lambda i, b, e, s, v: (e[i], 0, 0)),
                  pl.BlockSpec((1, MOE_FF, d), lambda i, b, e, s, v: (e[i], 0, 0))],
        out_specs=pl.BlockSpec((bm, d), lambda i, b, e, s, v: (b[i], 0)),
        scratch_shapes=[pltpu.VMEM((d, 2 * MOE_FF), BF16), pltpu.VMEM((MOE_FF, d), BF16)],
    )
    return pl.pallas_call(
        _expert_kernel,
        grid_spec=grid_spec,
        out_shape=jax.ShapeDtypeStruct((n, d), F32),
        compiler_params=_params("arbitrary"),
        name="moe_experts",
    )(item_block, item_expert, seg_start, n_valid, xs, w_in, w_out)


def _combine_kernel(d0_ref, d1_ref, x_ref, mod_ref, gate_ref, ys_ref, o_ref, y_ref, sem):
    tm, d = x_ref.shape

    def issue(g, carry):
        for u in range(SUBLANES):
            for k, dest_ref in enumerate((d0_ref, d1_ref)):
                pltpu.make_async_copy(ys_ref.at[pl.ds(dest_ref[g * SUBLANES + u], 1)],
                                      y_ref.at[k, g, pl.ds(u, 1)], sem).start(priority=k)
        return carry

    lax.fori_loop(0, tm // SUBLANES, issue, 0)
    rows = ys_ref.at[pl.ds(0, tm)]
    for k in range(2):
        pltpu.make_async_copy(rows, rows, sem).wait()
    gates = gate_ref[...]
    moe = gates[:, 0:1] * y_ref[0].reshape(tm, d) + gates[:, 1:2] * y_ref[1].reshape(tm, d)
    o_ref[...] = x_ref[...] + mod_ref[0][5:6] * moe


def moe_combine(x2, mod, gates_t, dest, ys, tiles_per_batch_of):
    t, d = x2.shape
    tm = MOVE_TILE
    tpb = tiles_per_batch_of(tm)
    slot_spec = pl.BlockSpec((tm,), lambda i: (i,), memory_space=pltpu.SMEM)
    return pl.pallas_call(
        _combine_kernel,
        grid=(t // tm,),
        in_specs=[slot_spec, slot_spec,
                  pl.BlockSpec((tm, d), lambda i: (i, 0)),
                  pl.BlockSpec((1, 6, d), lambda i: (i // tpb, 0, 0)),
                  pl.BlockSpec((tm, 2), lambda i: (i, 0)),
                  pl.BlockSpec(memory_space=pl.ANY)],
        out_specs=pl.BlockSpec((tm, d), lambda i: (i, 0)),
        out_shape=jax.ShapeDtypeStruct((t, d), F32),
        scratch_shapes=[pltpu.VMEM((2, tm // SUBLANES, SUBLANES, d), F32), pltpu.SemaphoreType.DMA(())],
        compiler_params=_params("arbitrary"),
        name="moe_combine",
    )(dest[0], dest[1], x2, mod, gates_t, ys)


def moe_layer(x2, mod, gain, router_w, router_b, w_in, w_out, tiles_per_batch_of):
    t, d = x2.shape
    bm = EXPERT_ROWS
    n_assign = 2 * t
    eid, gate, rank, cnt = moe_router(x2, mod, gain, router_w, router_b, tiles_per_batch_of)
    ne = MOE_EXPERTS
    experts = jnp.arange(ne, dtype=jnp.int32)
    upto = experts[None, :] <= experts[:, None]

    def running_total(v):
        return jnp.sum(jnp.where(upto, v[None, :], 0), axis=1).astype(jnp.int32)

    def lookup(table, idx):
        hit = idx[None] == experts.reshape((ne,) + (1,) * idx.ndim)
        return jnp.sum(jnp.where(hit, table.reshape((ne,) + (1,) * idx.ndim), 0), axis=0).astype(jnp.int32)

    counts = cnt[:, 0]
    seg_end = running_total(counts)
    seg_start = jnp.concatenate([jnp.zeros((1,), jnp.int32), seg_end])
    dest = lookup(seg_start[:ne], eid) + rank
    n_blocks = n_assign // bm
    n_items = n_blocks + ne - 1
    first_blk = seg_start[:ne] // bm
    last_blk = jnp.where(counts > 0, (seg_end - 1) // bm, first_blk - 1)
    per_expert = jnp.maximum(last_blk - first_blk + 1, 0)
    item_end = running_total(per_expert)
    n_valid = item_end[ne - 1]
    item_ids = jnp.arange(n_items, dtype=jnp.int32)
    item_ids_c = jnp.maximum(jnp.minimum(item_ids, n_valid - 1), 0)
    item_expert = jnp.minimum(jnp.sum(item_ids_c[:, None] >= item_end[None, :], axis=1), ne - 1).astype(jnp.int32)
    item_block = lookup(first_blk - (item_end - per_expert), item_expert) + item_ids_c
    xs = moe_dispatch(x2, mod, gain, dest, tiles_per_batch_of)
    ys = moe_experts(xs, w_in, w_out, item_block, item_expert, seg_start,
                     n_valid.reshape(1).astype(jnp.int32))
    return moe_combine(x2, mod, gate.T, dest, ys, tiles_per_batch_of)


SWA_HEAD_DIM = 64
SWA_Q_HEADS = D_MODEL // SWA_HEAD_DIM
SWA_KV_HEADS = 4
SWA_BLOCK = 128
SWA_Q_WIDTH = SWA_Q_HEADS * SWA_HEAD_DIM
SWA_KV_WIDTH = SWA_KV_HEADS * SWA_HEAD_DIM
HEADS_PER_VREG = LANES // SWA_HEAD_DIM


def _swa_kernel(sink_ref, x_ref, mod_ref, gain_ref, win_ref, qg_ref, kg_ref, ones_ref, wout_ref, o_ref,
                kx_ref, vx_ref, att_ref):
    t = pl.program_id(1)
    tm = x_ref.shape[0]
    blk = SWA_BLOCK
    hd = SWA_HEAD_DIM

    @pl.when(t == 0)
    def _():
        kx_ref[0:blk, :] = jnp.zeros((blk, kx_ref.shape[1]), BF16)
        vx_ref[0:blk, :] = jnp.zeros((blk, vx_ref.shape[1]), BF16)

    x = x_ref[...]
    mod = mod_ref[0]
    h = _modulated_norm(x, gain_ref[...], mod[0:1], mod[1:2]).astype(BF16)
    proj = jnp.dot(h, win_ref[...], preferred_element_type=F32)

    def head_rms(a):
        ss = jnp.dot((a * a).astype(BF16), ones_ref[...], preferred_element_type=F32)
        return a * lax.rsqrt(ss * (1.0 / hd) + NORM_EPS)

    lane = lax.broadcasted_iota(jnp.int32, (tm, LANES), 1)
    low = lane < hd
    for c in range(SWA_Q_WIDTH // LANES):
        cols = slice(c * LANES, (c + 1) * LANES)
        att_ref[:, cols] = (head_rms(proj[:, cols]) * qg_ref[:, cols]).astype(BF16)
    for c in range(SWA_KV_WIDTH // LANES):
        cols = slice(c * LANES, (c + 1) * LANES)
        kc = head_rms(proj[:, SWA_Q_WIDTH + c * LANES:SWA_Q_WIDTH + (c + 1) * LANES]) * kg_ref[:, cols]
        vc = proj[:, SWA_Q_WIDTH + SWA_KV_WIDTH + c * LANES:SWA_Q_WIDTH + SWA_KV_WIDTH + (c + 1) * LANES]
        for ref, a in ((kx_ref, kc), (vx_ref, vc)):
            even_lo = jnp.where(low, a, 0.0)
            odd_hi = jnp.where(low, 0.0, a)
            j0 = HEADS_PER_VREG * c
            ref[blk:blk + tm, (2 * j0) * LANES:(2 * j0 + 1) * LANES] = even_lo.astype(BF16)
            ref[blk:blk + tm, (2 * j0 + 1) * LANES:(2 * j0 + 2) * LANES] = pltpu.roll(even_lo, hd, 1).astype(BF16)
            ref[blk:blk + tm, (2 * j0 + 2) * LANES:(2 * j0 + 3) * LANES] = pltpu.roll(odd_hi, hd, 1).astype(BF16)
            ref[blk:blk + tm, (2 * j0 + 3) * LANES:(2 * j0 + 4) * LANES] = odd_hi.astype(BF16)

    qi = lax.broadcasted_iota(jnp.int32, (blk, 2 * blk), 0)
    kj = lax.broadcasted_iota(jnp.int32, (blk, 2 * blk), 1)
    lane_q = lax.broadcasted_iota(jnp.int32, (blk, LANES), 1) < hd
    pairs_per_kv = SWA_Q_HEADS // SWA_KV_HEADS // HEADS_PER_VREG
    for i in range(tm // blk):
        rows = slice(i * blk, (i + 1) * blk)
        keys = slice(i * blk, (i + 2) * blk)
        floor = jnp.where(t == 0, blk - 1, qi) if i == 0 else qi
        mask = jnp.logical_and(kj > floor, kj <= qi + blk)
        heads = [(p, half) for p in range(SWA_Q_HEADS // HEADS_PER_VREG) for half in range(HEADS_PER_VREG)]
        col_of = lambda p, half: (2 * (p // pairs_per_kv) + half) * LANES
        scores = [lax.dot_general(att_ref[rows, p * LANES:(p + 1) * LANES],
                                  kx_ref[keys, col_of(p, half):col_of(p, half) + LANES],
                                  (((1,), (1,)), ((), ())), preferred_element_type=F32) for p, half in heads]
        probs, inv = [], []
        for (p, half), sc in zip(heads, scores):
            sc = jnp.where(mask, sc, -jnp.inf)
            sink = sink_ref[HEADS_PER_VREG * p + half]
            m = jnp.maximum(jnp.max(sc, axis=-1, keepdims=True), sink)
            pr = jnp.exp(sc - m)
            inv.append(1.0 / (jnp.sum(pr, axis=-1, keepdims=True) + jnp.exp(sink - m)))
            probs.append(pr.astype(BF16))
        pvs = [jnp.dot(pr, vx_ref[keys, col_of(p, half):col_of(p, half) + LANES], preferred_element_type=F32)
               for (p, half), pr in zip(heads, probs)]
        for p in range(SWA_Q_HEADS // HEADS_PER_VREG):
            out = (pvs[2 * p] + pvs[2 * p + 1]) * jnp.where(lane_q, inv[2 * p], inv[2 * p + 1])
            att_ref[rows, p * LANES:(p + 1) * LANES] = out.astype(BF16)
    kx_ref[0:blk, :] = kx_ref[tm:tm + blk, :]
    vx_ref[0:blk, :] = vx_ref[tm:tm + blk, :]
    y = jnp.dot(att_ref[...], wout_ref[...], preferred_element_type=F32)
    o_ref[...] = x + mod[2:3] * y


def swa_layer(x2, mod, gain, w_in, q_norm, k_norm, sinks, w_out, b, s):
    t, d = x2.shape
    tm = TOKEN_TILE
    tpb = s // tm
    hd = SWA_HEAD_DIM
    qg = (jnp.tile(q_norm, SWA_Q_HEADS) * hd ** -0.5).reshape(1, SWA_Q_WIDTH)
    kg = jnp.tile(k_norm, SWA_KV_HEADS).reshape(1, SWA_KV_WIDTH)
    ids = jnp.arange(LANES) // hd
    ones = (ids[:, None] == ids[None, :]).astype(BF16)
    proj_w = SWA_Q_WIDTH + 2 * SWA_KV_WIDTH
    ext_w = 2 * SWA_KV_HEADS * LANES
    const2 = lambda bi, ti: (0, 0)
    tile = lambda bi, ti: (bi * tpb + ti, 0)
    return pl.pallas_call(
        _swa_kernel,
        grid=(b, tpb),
        in_specs=[pl.BlockSpec(memory_space=pltpu.SMEM),
                  pl.BlockSpec((tm, d), tile),
                  pl.BlockSpec((1, 6, d), lambda bi, ti: (bi, 0, 0)),
                  pl.BlockSpec((1, d), const2),
                  pl.BlockSpec((d, proj_w), const2),
                  pl.BlockSpec((1, SWA_Q_WIDTH), const2),
                  pl.BlockSpec((1, SWA_KV_WIDTH), const2),
                  pl.BlockSpec((LANES, LANES), const2),
                  pl.BlockSpec((SWA_Q_WIDTH, d), const2)],
        out_specs=pl.BlockSpec((tm, d), tile),
        out_shape=jax.ShapeDtypeStruct((t, d), F32),
        scratch_shapes=[pltpu.VMEM((tm + SWA_BLOCK, ext_w), BF16), pltpu.VMEM((tm + SWA_BLOCK, ext_w), BF16),
                        pltpu.VMEM((tm, SWA_Q_WIDTH), BF16)],
        compiler_params=_params("arbitrary", "arbitrary"),
        name="swa_mixer",
    )(sinks, x2, mod, gain.reshape(1, d), w_in.astype(BF16), qg, kg, ones, w_out.astype(BF16))


DN_QK_HEADS = 4
DN_V_HEADS = 8
DN_HEAD_DIM = D_MODEL // DN_V_HEADS
DN_CONV = 4
DN_CHUNK = 64
DN_KEY_WIDTH = DN_QK_HEADS * DN_HEAD_DIM
DN_VAL_WIDTH = DN_V_HEADS * DN_HEAD_DIM
DN_CONV_WIDTH = 2 * DN_KEY_WIDTH + DN_VAL_WIDTH
DN_PREP_TILE = 128
DN_SCAN_TILE = 128


def _split_bf16(a):
    hi = a.astype(BF16)
    return hi, (a - hi.astype(F32)).astype(BF16)


def _mm(a, b):
    return jnp.dot(a.astype(BF16), b.astype(BF16), preferred_element_type=F32)


def _dn_proj_kernel(x_ref, mod_ref, gain_ref, w_ref, wba_ref, conv_ref, rate_ref, dtb_ref,
                    q_ref, k_ref, v_ref, z_ref, bg_ref, ext_ref):
    t = pl.program_id(1)
    tm = x_ref.shape[0]
    pad = SUBLANES

    @pl.when(t == 0)
    def _():
        ext_ref[0:pad, :] = jnp.zeros((pad, ext_ref.shape[1]), F32)

    mod = mod_ref[0]
    h = _modulated_norm(x_ref[...], gain_ref[...], mod[0:1], mod[1:2])
    hh, hl = _split_bf16(h)
    proj = jnp.dot(hh, w_ref[...], preferred_element_type=F32)
    z_ref[...] = proj[:, DN_CONV_WIDTH:].astype(BF16)

    ext_ref[pad:pad + tm, :] = proj[:, :DN_CONV_WIDTH]
    acc = None
    for j in range(DN_CONV):
        start = pad - (DN_CONV - 1) + j
        term = conv_ref[j:j + 1, :] * ext_ref[start:start + tm, :]
        acc = term if acc is None else acc + term
    ext_ref[0:pad, :] = ext_ref[tm:tm + pad, :]
    qkv = _silu(acc)
    for hq in range(2 * DN_QK_HEADS):
        cols = slice(hq * DN_HEAD_DIM, (hq + 1) * DN_HEAD_DIM)
        a = qkv[:, cols]
        a = a * lax.rsqrt(jnp.sum(a * a, axis=-1, keepdims=True) + NORM_EPS)
        if hq < DN_QK_HEADS:
            q_ref[:, cols] = (a * DN_HEAD_DIM ** -0.5).astype(BF16)
        else:
            k_ref[:, hq * DN_HEAD_DIM - DN_KEY_WIDTH:(hq + 1) * DN_HEAD_DIM - DN_KEY_WIDTH] = a.astype(BF16)
    v_ref[...] = qkv[:, 2 * DN_KEY_WIDTH:].astype(BF16)

    ba = jnp.dot(hh, wba_ref[...], preferred_element_type=F32) + jnp.dot(hl, wba_ref[...], preferred_element_type=F32)
    ba = ba + pltpu.roll(ba, LANES - 2 * DN_V_HEADS, 1)
    lane = lax.broadcasted_iota(jnp.int32, ba.shape, 1)
    beta = 1.0 / (1.0 + jnp.exp(-ba))
    sp = ba + dtb_ref[...]
    g = -rate_ref[...] * (jnp.maximum(sp, 0.0) + jnp.log1p(jnp.exp(-jnp.abs(sp))))
    g = jnp.where(jnp.logical_and(lane >= DN_V_HEADS, lane < 2 * DN_V_HEADS), g, 0.0)
    r = lax.broadcasted_iota(jnp.int32, (tm, tm), 0)
    c = lax.broadcasted_iota(jnp.int32, (tm, tm), 1)
    tri = jnp.logical_and(r // DN_CHUNK == c // DN_CHUNK, c <= r).astype(F32)
    gc = jnp.dot(tri, g, precision=HIGHEST, preferred_element_type=F32)
    bg_ref[...] = jnp.where(lane < DN_V_HEADS, beta, gc)


def _unit_lower_inverses(mats):
    n = mats[0].shape[0]
    r = lax.broadcasted_iota(jnp.int32, (n, n), 0)
    c = lax.broadcasted_iota(jnp.int32, (n, n), 1)
    eye = (r == c).astype(F32)
    size = SUBLANES
    same = (r // size) == (c // size)
    d = [jnp.where(same, a, 0.0) for a in mats]
    d2 = [_mm(v, v) for v in d]
    d4 = [_mm(v, v) for v in d2]
    x = [_mm(eye - v, eye + v2) for v, v2 in zip(d, d2)]
    x = [_mm(v, eye + v4) for v, v4 in zip(x, d4)]
    while size < n:
        wider = (r // (2 * size)) == (c // (2 * size))
        ring = jnp.logical_and(wider, jnp.logical_not(same))
        xl = [_mm(v, jnp.where(ring, a, 0.0)) for v, a in zip(x, mats)]
        x = [v - _mm(vl, v) for v, vl in zip(x, xl)]
        same = wider
        size *= 2
    return x


def _dn_prep_kernel(q_ref, k_ref, v_ref, bg_ref, gct_ref, u_ref, w_ref, qk_ref):
    tm = q_ref.shape[0]
    ck = DN_CHUNK
    hd = DN_HEAD_DIM
    rep = DN_V_HEADS // DN_QK_HEADS
    r = lax.broadcasted_iota(jnp.int32, (ck, ck), 0)
    c = lax.broadcasted_iota(jnp.int32, (ck, ck), 1)
    lower = c <= r
    strict = c < r
    dot = functools.partial(jnp.dot, preferred_element_type=F32)
    chunks = [slice(ci * ck, (ci + 1) * ck) for ci in range(tm // ck)]
    gram = {}
    for rows in chunks:
        for hq in range(DN_QK_HEADS):
            cols = slice(hq * hd, (hq + 1) * hd)
            k = k_ref[rows, cols]
            gram[(hq, rows.start)] = lax.dot_general(jnp.concatenate([k, q_ref[rows, cols]], axis=0), k,
                                                     (((1,), (1,)), ((), ())), preferred_element_type=F32)
    problems = [(hv, rows) for rows in chunks for hv in range(DN_V_HEADS)]
    mats, rhs = [], []
    for hv, rows in problems:
        beta = bg_ref[rows, hv:hv + 1]
        gcc = bg_ref[rows, DN_V_HEADS + hv:DN_V_HEADS + hv + 1]
        gcr = gct_ref[hv:hv + 1, rows]
        decay = jnp.where(lower, jnp.exp(jnp.where(lower, gcc - gcr, 0.0)), 0.0)
        g = gram[(hv // rep, rows.start)]
        mats.append(jnp.where(strict, g[:ck] * beta * decay, 0.0))
        qk_ref[rows, hv * ck:(hv + 1) * ck] = (g[ck:] * decay).astype(BF16)
        kf = k_ref[rows, (hv // rep) * hd:(hv // rep + 1) * hd].astype(F32)
        vf = v_ref[rows, hv * hd:(hv + 1) * hd].astype(F32)
        rhs.append(jnp.concatenate([vf * beta, kf * (beta * jnp.exp(gcc))], axis=1).astype(BF16))
    inverses = _unit_lower_inverses(mats)
    for (hv, rows), tinv, b in zip(problems, inverses, rhs):
        uw = dot(tinv.astype(BF16), b)
        u_ref[rows, hv * hd:(hv + 1) * hd] = uw[:, :hd].astype(BF16)
        w_ref[rows, hv * hd:(hv + 1) * hd] = uw[:, hd:].astype(BF16)


def _dn_scan_kernel(x_ref, mod_ref, u_ref, w_ref, qk_ref, q_ref, k_ref, z_ref, bg_ref, og_ref, wout_ref, o_ref,
                    state_ref, att_ref):
    t = pl.program_id(0)
    nb, tm, _ = x_ref.shape
    ck = DN_CHUNK
    hd = DN_HEAD_DIM
    rep = DN_V_HEADS // DN_QK_HEADS

    @pl.when(t == 0)
    def _():
        state_ref[...] = jnp.zeros_like(state_ref)

    dot = functools.partial(jnp.dot, preferred_element_type=F32)
    chains = [(bi, hv) for bi in range(nb) for hv in range(DN_V_HEADS)]
    for ci in range(tm // ck):
        rows = slice(ci * ck, (ci + 1) * ck)
        gcc, g_end, lhs = [], [], []
        for bi, hv in chains:
            gc = bg_ref[bi, rows, DN_V_HEADS + hv:DN_V_HEADS + hv + 1]
            gcc.append(gc)
            g_end.append(gc[ck - 1:ck, :])
            q_dec = q_ref[bi, rows, (hv // rep) * hd:(hv // rep + 1) * hd].astype(F32) * jnp.exp(gc)
            lhs.append(jnp.concatenate([w_ref[bi, rows, hv * hd:(hv + 1) * hd], q_dec.astype(BF16)], axis=0))
        ws_qs = [dot(a, state_ref[n].astype(BF16)) for n, a in enumerate(lhs)]
        v_new = [(u_ref[bi, rows, hv * hd:(hv + 1) * hd].astype(F32) - m[:ck]).astype(BF16)
                 for (bi, hv), m in zip(chains, ws_qs)]
        outs = [m[ck:] + dot(qk_ref[bi, rows, hv * ck:(hv + 1) * ck], vn)
                for (bi, hv), m, vn in zip(chains, ws_qs, v_new)]
        for n, (bi, hv) in enumerate(chains):
            kf = k_ref[bi, rows, (hv // rep) * hd:(hv // rep + 1) * hd].astype(F32)
            k_dec = (kf * jnp.exp(g_end[n] - gcc[n])).astype(BF16)
            state_ref[n] = state_ref[n] * jnp.exp(g_end[n]) + lax.dot_general(
                k_dec, v_new[n], (((0,), (0,)), ((), ())), preferred_element_type=F32)
        for (bi, hv), o in zip(chains, outs):
            zf = z_ref[bi, rows, hv * hd:(hv + 1) * hd].astype(F32)
            att_ref[bi * tm + ci * ck:bi * tm + (ci + 1) * ck, hv * hd:(hv + 1) * hd] = (
                _rms(o) * og_ref[...] * _silu(zf)).astype(BF16)
    y = dot(att_ref[...], wout_ref[...])
    for bi in range(nb):
        o_ref[bi] = x_ref[bi] + mod_ref[bi][2:3] * y[bi * tm:(bi + 1) * tm]


def deltanet_layer(x2, mod, gain, w_in, conv_w, a_log, dt_bias, o_norm, w_out, b, s):
    t, d = x2.shape
    nh = DN_V_HEADS
    main_w = DN_CONV_WIDTH + DN_VAL_WIDTH
    w_main = w_in[:, :main_w].astype(BF16)
    w_ba = w_in[:, main_w:]
    w_ba_hi = w_ba.astype(BF16)
    w_ba_lo = (w_ba - w_ba_hi.astype(F32)).astype(BF16)
    w_ba2 = jnp.concatenate([w_ba_hi, w_ba_lo, jnp.zeros((d, LANES - 4 * nh), BF16)], axis=1)
    lanes_pad = lambda v: jnp.concatenate([jnp.zeros((nh,), F32), v.astype(F32),
                                           jnp.zeros((LANES - 2 * nh,), F32)]).reshape(1, LANES)
    rate = lanes_pad(jnp.exp(a_log.astype(F32)))
    dtb = lanes_pad(dt_bias)

    tm = TOKEN_TILE
    tpb = s // tm
    const2 = lambda bi, ti: (0, 0)
    tile = lambda bi, ti: (bi * tpb + ti, 0)
    q, k, v, z, bg = pl.pallas_call(
        _dn_proj_kernel,
        grid=(b, tpb),
        in_specs=[pl.BlockSpec((tm, d), tile),
                  pl.BlockSpec((1, 6, d), lambda bi, ti: (bi, 0, 0)),
                  pl.BlockSpec((1, d), const2),
                  pl.BlockSpec((d, main_w), const2),
                  pl.BlockSpec((d, LANES), const2),
                  pl.BlockSpec((DN_CONV, DN_CONV_WIDTH), const2),
                  pl.BlockSpec((1, LANES), const2),
                  pl.BlockSpec((1, LANES), const2)],
        out_specs=[pl.BlockSpec((tm, DN_KEY_WIDTH), tile), pl.BlockSpec((tm, DN_KEY_WIDTH), tile),
                   pl.BlockSpec((tm, DN_VAL_WIDTH), tile), pl.BlockSpec((tm, DN_VAL_WIDTH), tile),
                   pl.BlockSpec((tm, LANES), tile)],
        out_shape=[jax.ShapeDtypeStruct((t, DN_KEY_WIDTH), BF16), jax.ShapeDtypeStruct((t, DN_KEY_WIDTH), BF16),
                   jax.ShapeDtypeStruct((t, DN_VAL_WIDTH), BF16), jax.ShapeDtypeStruct((t, DN_VAL_WIDTH), BF16),
                   jax.ShapeDtypeStruct((t, LANES), F32)],
        scratch_shapes=[pltpu.VMEM((tm + SUBLANES, DN_CONV_WIDTH), F32)],
        compiler_params=_params("arbitrary", "arbitrary"),
        name="deltanet_proj",
    )(x2, mod, gain.reshape(1, d), w_main, w_ba2, conv_w, rate, dtb)

    gct = bg[:, nh:2 * nh].T
    tp = DN_PREP_TILE
    rows_of = lambda width: pl.BlockSpec((tp, width), lambda i: (i, 0))
    qk_w = nh * DN_CHUNK
    u, w, qk = pl.pallas_call(
        _dn_prep_kernel,
        grid=(t // tp,),
        in_specs=[rows_of(DN_KEY_WIDTH), rows_of(DN_KEY_WIDTH), rows_of(DN_VAL_WIDTH), rows_of(LANES),
                  pl.BlockSpec((nh, tp), lambda i: (0, i))],
        out_specs=[rows_of(DN_VAL_WIDTH), rows_of(DN_VAL_WIDTH), rows_of(qk_w)],
        out_shape=[jax.ShapeDtypeStruct((t, DN_VAL_WIDTH), BF16), jax.ShapeDtypeStruct((t, DN_VAL_WIDTH), BF16),
                   jax.ShapeDtypeStruct((t, qk_w), BF16)],
        compiler_params=_params("arbitrary"),
        name="deltanet_prep",
    )(q, k, v, bg, gct)

    ts = DN_SCAN_TILE
    seq = lambda a: a.reshape(b, s, a.shape[-1])
    both = lambda width: pl.BlockSpec((b, ts, width), lambda i: (0, i, 0))
    out = pl.pallas_call(
        _dn_scan_kernel,
        grid=(s // ts,),
        in_specs=[both(d),
                  pl.BlockSpec((b, 6, d), lambda i: (0, 0, 0)),
                  both(DN_VAL_WIDTH), both(DN_VAL_WIDTH), both(qk_w), both(DN_KEY_WIDTH), both(DN_KEY_WIDTH),
                  both(DN_VAL_WIDTH), both(LANES),
                  pl.BlockSpec((1, DN_HEAD_DIM), lambda i: (0, 0)),
                  pl.BlockSpec((DN_VAL_WIDTH, d), lambda i: (0, 0))],
        out_specs=both(d),
        out_shape=jax.ShapeDtypeStruct((b, s, d), F32),
        scratch_shapes=[pltpu.VMEM((b * nh, DN_HEAD_DIM, DN_HEAD_DIM), F32),
                        pltpu.VMEM((b * ts, DN_VAL_WIDTH), BF16)],
        compiler_params=_params("arbitrary"),
        name="deltanet_scan",
    )(seq(x2), mod, seq(u), seq(w), seq(qk), seq(q), seq(k), seq(z), seq(bg), o_norm.reshape(1, DN_HEAD_DIM),
      w_out.astype(BF16))
    return out.reshape(t, d)


def kernel(x, c, l0_norm_mix, l0_norm_ffn, l0_ada_w, l0_ada_b, l0_gm_w_in, l0_gm_v_norm, l0_gm_w_s, l0_gm_b_s, l0_gm_w_out, l0_router_w, l0_router_b, l0_expert_w_in, l0_expert_w_out, l1_norm_mix, l1_norm_ffn, l1_ada_w, l1_ada_b, l1_dn_w_in, l1_dn_conv_w, l1_dn_a_log, l1_dn_dt_bias, l1_dn_o_norm, l1_dn_w_out, l1_router_w, l1_router_b, l1_expert_w_in, l1_expert_w_out, l2_norm_mix, l2_norm_ffn, l2_ada_w, l2_ada_b, l2_swa_w_in, l2_swa_q_norm, l2_swa_k_norm, l2_swa_sinks, l2_swa_w_out, l2_router_w, l2_router_b, l2_expert_w_in, l2_expert_w_out, l3_norm_mix, l3_norm_ffn, l3_ada_w, l3_ada_b, l3_gm_w_in, l3_gm_v_norm, l3_gm_w_s, l3_gm_b_s, l3_gm_w_out, l3_router_w, l3_router_b, l3_expert_w_in, l3_expert_w_out):
    b, s, d = x.shape
    x2 = x.reshape(b * s, d)
    c_pad = jnp.zeros((8, d), F32).at[:b].set(c)
    tiles_per_batch_of = lambda tm: s // tm

    def modulation(ada_w, ada_b):
        return adaln(c_pad, ada_w, ada_b)[:b].reshape(b, 6, d)

    mod = modulation(l0_ada_w, l0_ada_b)
    x2 = gmlp_layer(x2, mod, l0_norm_mix, l0_gm_w_in, l0_gm_v_norm, l0_gm_w_s, l0_gm_b_s, l0_gm_w_out,
                    tiles_per_batch_of(TOKEN_TILE))
    x2 = moe_layer(x2, mod, l0_norm_ffn, l0_router_w, l0_router_b, l0_expert_w_in, l0_expert_w_out,
                   tiles_per_batch_of)

    mod = modulation(l1_ada_w, l1_ada_b)
    x2 = deltanet_layer(x2, mod, l1_norm_mix, l1_dn_w_in, l1_dn_conv_w, l1_dn_a_log, l1_dn_dt_bias,
                        l1_dn_o_norm, l1_dn_w_out, b, s)
    x2 = moe_layer(x2, mod, l1_norm_ffn, l1_router_w, l1_router_b, l1_expert_w_in, l1_expert_w_out,
                   tiles_per_batch_of)

    mod = modulation(l2_ada_w, l2_ada_b)
    x2 = swa_layer(x2, mod, l2_norm_mix, l2_swa_w_in, l2_swa_q_norm, l2_swa_k_norm, l2_swa_sinks,
                   l2_swa_w_out, b, s)
    x2 = moe_layer(x2, mod, l2_norm_ffn, l2_router_w, l2_router_b, l2_expert_w_in, l2_expert_w_out,
                   tiles_per_batch_of)

    mod = modulation(l3_ada_w, l3_ada_b)
    x2 = gmlp_layer(x2, mod, l3_norm_mix, l3_gm_w_in, l3_gm_v_norm, l3_gm_w_s, l3_gm_b_s, l3_gm_w_out,
                    tiles_per_batch_of(TOKEN_TILE))
    x2 = moe_layer(x2, mod, l3_norm_ffn, l3_router_w, l3_router_b, l3_expert_w_in, l3_expert_w_out,
                   tiles_per_batch_of)
    return x2.reshape(b, s, d)
```

```python
import functools
import math

import jax
import jax.numpy as jnp
from jax import lax
from jax.experimental import pallas as pl
from jax.experimental.pallas import tpu as pltpu

F32 = jnp.float32
BF16 = jnp.bfloat16
HIGHEST = lax.Precision.HIGHEST

D_MODEL = 1024
NORM_EPS = 1e-6

GM_CHUNK = 128
GM_GROUPS = 8
GM_GROUP_DIM = D_MODEL // GM_GROUPS

MOE_GROUPS = 4
MOE_EXPERTS_PER_GROUP = 8
MOE_EXPERTS = MOE_GROUPS * MOE_EXPERTS_PER_GROUP
MOE_FF = D_MODEL // 2

VMEM_LIMIT_BYTES = 56 * 1024 * 1024

TOKEN_TILE = 512
MOVE_TILE = 512
EXPERT_ROWS = 512
EXPERT_SUB_ROWS = 128
LANES = 128
SUBLANES = 8


def _params(*semantics):
    return pltpu.CompilerParams(dimension_semantics=semantics, vmem_limit_bytes=VMEM_LIMIT_BYTES,
                                disable_bounds_checks=True)


def _rms(xf):
    return xf * lax.rsqrt(jnp.mean(xf * xf, axis=-1, keepdims=True) + NORM_EPS)


def _modulated_norm(x, gain, shift, scale):
    return _rms(x) * gain * (1.0 + scale) + shift


def _gelu_tanh(x):
    return 0.5 * x * (1.0 + jnp.tanh(math.sqrt(2.0 / math.pi) * (x + 0.044715 * (x * x * x))))


def _silu(x):
    return x * (1.0 / (1.0 + jnp.exp(-x)))


def _adaln_kernel(c_ref, w_ref, b_ref, o_ref):
    o_ref[...] = jnp.dot(_silu(c_ref[...]), w_ref[...], precision=HIGHEST,
                         preferred_element_type=F32) + b_ref[...]


def adaln(c_pad, ada_w, ada_b):
    rows, d = c_pad.shape
    n = ada_w.shape[1]
    tn = d
    return pl.pallas_call(
        _adaln_kernel,
        grid=(n // tn,),
        in_specs=[pl.BlockSpec((rows, d), lambda j: (0, 0)),
                  pl.BlockSpec((d, tn), lambda j: (0, j)),
                  pl.BlockSpec((1, tn), lambda j: (0, j))],
        out_specs=pl.BlockSpec((rows, tn), lambda j: (0, j)),
        out_shape=jax.ShapeDtypeStruct((rows, n), F32),
        compiler_params=_params("arbitrary"),
        name="adaln",
    )(c_pad, ada_w, ada_b.reshape(1, n))


def _gmlp_kernel(x_ref, mod_ref, gain_ref, win_ref, vn_ref, ws_ref, bs_ref, wout_ref, fgain_ref, rw_ref, rb_ref, before_ref,
                 o_ref, eid_ref, gate_ref, rank_ref, cnt_ref, u_ref, v_ref, g_ref, base_ref):
    width = D_MODEL
    x = x_ref[...]
    mod = mod_ref[0]
    h = _modulated_norm(x, gain_ref[...], mod[0:1], mod[1:2])
    z = _gelu_tanh(jnp.dot(h.astype(BF16), win_ref[...], preferred_element_type=F32))
    u_ref[...] = z[:, :width]
    v_ref[...] = (_rms(z[:, width:]) * vn_ref[...]).astype(BF16)
    cells = [(slice(c * GM_CHUNK, (c + 1) * GM_CHUNK), g, slice(g * GM_GROUP_DIM, (g + 1) * GM_GROUP_DIM))
             for c in range(x.shape[0] // GM_CHUNK) for g in range(GM_GROUPS)]
    mixed = [jnp.dot(ws_ref[g], v_ref[rows, cols], preferred_element_type=F32) for rows, g, cols in cells]
    for (rows, g, cols), sv in zip(cells, mixed):
        g_ref[rows, cols] = (u_ref[rows, cols] * (sv + bs_ref[:, g:g + 1])).astype(BF16)
    y = jnp.dot(g_ref[...], wout_ref[...], preferred_element_type=F32)
    x_new = x + mod[2:3] * y
    o_ref[...] = x_new
    eid_ref[...], gate_ref[...], rank_ref[...] = _route(
        x_new, mod[3:4], mod[4:5], pl.program_id(0) == 0, fgain_ref, rw_ref, rb_ref, before_ref, cnt_ref, base_ref)


def gmlp_layer(x2, mod, gain, w_in, v_norm, w_s, b_s, w_out, router, tiles_per_batch):
    t, d = x2.shape
    tm = TOKEN_TILE
    ws_causal = jnp.where(jnp.tril(jnp.ones((GM_CHUNK, GM_CHUNK), dtype=bool)), w_s, 0).astype(BF16)
    const2 = lambda i: (0, 0)
    r_args, r_in = _router_operands(*router, tm)
    r_out, r_shapes, r_scratch = _router_results(t, pl.BlockSpec((2, tm), lambda i: (0, i)), (2, t))
    out = pl.pallas_call(
        _gmlp_kernel,
        grid=(t // tm,),
        in_specs=[pl.BlockSpec((tm, d), lambda i: (i, 0)),
                  pl.BlockSpec((1, 6, d), lambda i: (i // tiles_per_batch, 0, 0)),
                  pl.BlockSpec((1, d), const2),
                  pl.BlockSpec((d, 2 * d), const2),
                  pl.BlockSpec((1, d), const2),
                  pl.BlockSpec((GM_GROUPS, GM_CHUNK, GM_CHUNK), lambda i: (0, 0, 0)),
                  pl.BlockSpec((GM_CHUNK, GM_GROUPS), const2),
                  pl.BlockSpec((d, d), const2)] + r_in,
        out_specs=[pl.BlockSpec((tm, d), lambda i: (i, 0))] + r_out,
        out_shape=[jax.ShapeDtypeStruct((t, d), F32)] + r_shapes,
        scratch_shapes=[pltpu.VMEM((tm, d), F32), pltpu.VMEM((tm, d), BF16), pltpu.VMEM((tm, d), BF16), r_scratch],
        compiler_params=_params("arbitrary"),
        name="gmlp_mixer",
    )(x2, mod, gain.reshape(1, d), w_in.astype(BF16), v_norm.reshape(1, d), ws_causal, b_s.T,
      w_out.astype(BF16), *r_args)
    return out[0], tuple(out[1:])


def _route(x, shift, scale, first, gain_ref, rw_ref, rb_ref, before_ref, cnt_ref, base_ref):
    tm = x.shape[0]
    ne = MOE_EXPERTS
    npg = MOE_EXPERTS_PER_GROUP

    @pl.when(first)
    def _():
        base_ref[...] = jnp.zeros_like(base_ref)

    hh, hl = _split_bf16(_modulated_norm(x, gain_ref[...], shift, scale))
    nt = (((1,), (1,)), ((), ()))
    nr = ROUTER_ROWS
    by_hi = lax.dot_general(rw_ref[...], hh, nt, preferred_element_type=F32)
    by_lo = lax.dot_general(rw_ref[0:nr, :], hl, nt, preferred_element_type=F32)
    lt = (by_hi[0:nr] + rb_ref[...]) + (by_hi[nr:] + by_lo)
    lg = [lt[ne + g:ne + g + 1, :] for g in range(MOE_GROUPS)]
    gmax = functools.reduce(jnp.maximum, lg)
    gsum = functools.reduce(lambda a, b: a + b, [jnp.exp(l - gmax) for l in lg])
    pg_top = 1.0 / gsum
    g_sel = jnp.full(gmax.shape, MOE_GROUPS - 1, jnp.int32)
    for g in range(MOE_GROUPS - 2, -1, -1):
        g_sel = jnp.where(lg[g] == gmax, g, g_sel)
    sel = lt[(MOE_GROUPS - 1) * npg:MOE_GROUPS * npg, :]
    for g in range(MOE_GROUPS - 2, -1, -1):
        sel = jnp.where(g_sel == g, lt[g * npg:(g + 1) * npg, :], sel)
    row = lax.broadcasted_iota(jnp.int32, sel.shape, 0)
    m1 = jnp.max(sel, axis=0, keepdims=True)
    i1 = jnp.min(jnp.where(sel == m1, row, npg), axis=0, keepdims=True)
    rest = jnp.where(row == i1, -jnp.inf, sel)
    m2 = jnp.max(rest, axis=0, keepdims=True)
    i2 = jnp.min(jnp.where(rest == m2, row, npg), axis=0, keepdims=True)
    e2 = jnp.exp(m2 - m1)
    inv = pg_top / (1.0 + e2)
    eid = jnp.concatenate([g_sel * npg + i1, g_sel * npg + i2], axis=0)
    gates = jnp.concatenate([inv, inv * e2], axis=0)

    erow = lax.broadcasted_iota(jnp.int32, (ne, tm), 0)
    hits = [erow == eid[k:k + 1, :] for k in range(2)]
    prefix = jnp.dot(jnp.concatenate([jnp.where(hit, 1.0, 0.0).astype(BF16) for hit in hits], axis=0),
                     before_ref[...], preferred_element_type=F32)
    base = base_ref[...]
    ranks = []
    for k, hit in enumerate(hits):
        ranks.append(jnp.sum(jnp.where(hit, prefix[k * ne:(k + 1) * ne] + base, 0.0), axis=0, keepdims=True))
        base = base + jnp.sum(jnp.where(hit, 1.0, 0.0), axis=1, keepdims=True)
    base_ref[...] = base
    cnt_ref[...] = jnp.broadcast_to(base, cnt_ref.shape).astype(jnp.int32)
    return eid, gates, jnp.concatenate(ranks, axis=0).astype(jnp.int32)


ROUTER_ROWS = MOE_EXPERTS + 2 * SUBLANES


def _router_operands(gain, router_w, router_b, tm):
    d = router_w.shape[0]
    pad = ROUTER_ROWS - MOE_EXPERTS - MOE_GROUPS
    rw = jnp.concatenate([router_w[:, MOE_GROUPS:], router_w[:, :MOE_GROUPS], jnp.zeros((d, pad), F32)], axis=1).T
    rw_hi = rw.astype(BF16)
    rw2 = jnp.concatenate([rw_hi, (rw - rw_hi.astype(F32)).astype(BF16)], axis=0)
    rb = jnp.concatenate([router_b[MOE_GROUPS:], router_b[:MOE_GROUPS], jnp.zeros((pad,), F32)])
    pos = jnp.arange(tm)
    before = (pos[:, None] < pos[None, :]).astype(BF16)
    const = lambda *_: (0, 0)
    specs = [pl.BlockSpec((1, d), const), pl.BlockSpec((2 * ROUTER_ROWS, d), const),
             pl.BlockSpec((ROUTER_ROWS, 1), const), pl.BlockSpec((tm, tm), const)]
    return [gain.reshape(1, d), rw2, rb.reshape(ROUTER_ROWS, 1), before], specs


def _router_results(t, slot_spec, slot_shape):
    specs = [slot_spec, slot_spec, slot_spec, pl.BlockSpec((MOE_EXPERTS, LANES), lambda *_: (0, 0))]
    shapes = [jax.ShapeDtypeStruct(slot_shape, jnp.int32), jax.ShapeDtypeStruct(slot_shape, F32),
              jax.ShapeDtypeStruct(slot_shape, jnp.int32), jax.ShapeDtypeStruct((MOE_EXPERTS, LANES), jnp.int32)]
    return specs, shapes, pltpu.VMEM((MOE_EXPERTS, 1), F32)


def _wait_rows(hbm_ref, n_rows, sem, times):
    rows = hbm_ref.at[pl.ds(0, n_rows)]
    for _ in range(times):
        pltpu.make_async_copy(rows, rows, sem).wait()


def _dispatch_kernel(d0_ref, d1_ref, x_ref, mod_ref, gain_ref, xs_ref, h_ref, sem):
    i = pl.program_id(0)
    last = pl.num_programs(0) - 1
    tm, d = x_ref.shape
    mod = mod_ref[0]
    h = _modulated_norm(x_ref[...], gain_ref[...], mod[3:4], mod[4:5])

    def step(buf):
        h_ref[buf] = h.reshape(tm // SUBLANES, SUBLANES, d)

        def issue(g, carry):
            for u in range(SUBLANES):
                for k, dest_ref in enumerate((d0_ref, d1_ref)):
                    pltpu.make_async_copy(h_ref.at[buf, g, pl.ds(u, 1)],
                                          xs_ref.at[pl.ds(dest_ref[g * SUBLANES + u], 1)],
                                          sem.at[buf]).start(priority=k)
            return carry

        lax.fori_loop(0, tm // SUBLANES, issue, 0)

        @pl.when(i > 0)
        def _():
            _wait_rows(xs_ref, tm, sem.at[1 - buf], 2)

        @pl.when(i == last)
        def _():
            _wait_rows(xs_ref, tm, sem.at[buf], 2)

    for buf in range(2):
        pl.when(i % 2 == buf)(functools.partial(step, buf))


def moe_dispatch(x2, mod, gain, dest, tiles_per_batch_of):
    t, d = x2.shape
    tm = MOVE_TILE
    tpb = tiles_per_batch_of(tm)
    const2 = lambda i: (0, 0)
    slot_spec = pl.BlockSpec((tm,), lambda i: (i,), memory_space=pltpu.SMEM)
    return pl.pallas_call(
        _dispatch_kernel,
        grid=(t // tm,),
        in_specs=[slot_spec, slot_spec,
                  pl.BlockSpec((tm, d), lambda i: (i, 0)),
                  pl.BlockSpec((1, 6, d), lambda i: (i // tpb, 0, 0)),
                  pl.BlockSpec((1, d), const2)],
        out_specs=pl.BlockSpec(memory_space=pl.ANY),
        out_shape=jax.ShapeDtypeStruct((2 * t, d), F32),
        scratch_shapes=[pltpu.VMEM((2, tm // SUBLANES, SUBLANES, d), F32), pltpu.SemaphoreType.DMA((2,))],
        compiler_params=_params("arbitrary"),
        name="moe_dispatch",
    )(dest[0], dest[1], x2, mod, gain.reshape(1, d))


def _expert_kernel(blk_ref, exp_ref, start_ref, nvalid_ref, xs_ref, win_ref, wout_ref, ys_ref,
                   winb_ref, woutb_ref):
    i = pl.program_id(0)
    bm = xs_ref.shape[0]
    e = exp_ref[i]
    blk = blk_ref[i]
    prev = jnp.maximum(i - 1, 0)
    first = i == 0
    lo = start_ref[e]
    hi = start_ref[e + 1]
    row0 = blk * bm
    live = i < nvalid_ref[0]

    @pl.when(jnp.logical_or(first, exp_ref[prev] != e))
    def _():
        winb_ref[...] = win_ref[0].astype(BF16)
        woutb_ref[...] = wout_ref[0].astype(BF16)

    new_block = jnp.logical_or(first, blk_ref[prev] != blk)
    whole = jnp.logical_and(lo <= row0, hi >= row0 + bm)

    def ffn(x):
        a_gl = jnp.dot(x.astype(BF16), winb_ref[...], preferred_element_type=F32)
        mid = (_silu(a_gl[:, :MOE_FF]) * a_gl[:, MOE_FF:]).astype(BF16)
        return jnp.dot(mid, woutb_ref[...], preferred_element_type=F32)

    @pl.when(jnp.logical_and(live, whole))
    def _():
        ys_ref[...] = ffn(xs_ref[...])

    @pl.when(jnp.logical_and(new_block, jnp.logical_not(whole)))
    def _():
        ys_ref[...] = jnp.zeros_like(ys_ref)

    for sub in range(bm // EXPERT_SUB_ROWS):
        sub0 = row0 + sub * EXPERT_SUB_ROWS
        touched = jnp.logical_and(hi > sub0, lo < sub0 + EXPERT_SUB_ROWS)

        @pl.when(jnp.logical_and(jnp.logical_and(live, jnp.logical_not(whole)), touched))
        def _():
            sl = pl.ds(sub * EXPERT_SUB_ROWS, EXPERT_SUB_ROWS)
            rows = sub0 + lax.broadcasted_iota(jnp.int32, (EXPERT_SUB_ROWS, 1), 0)
            mine = jnp.logical_and(rows >= lo, rows < hi)
            ys_ref[sl, :] += jnp.where(mine, ffn(xs_ref[sl, :]), 0.0)


def moe_experts(xs, w_in, w_out, item_block, item_expert, seg_start, n_valid):
    n, d = xs.shape
    bm = EXPERT_ROWS
    n_items = item_block.shape[0]
    grid_spec = pltpu.PrefetchScalarGridSpec(
        num_scalar_prefetch=4,
        grid=(n_items,),
        in_specs=[pl.BlockSpec((bm, d), lambda i, b, e, s, v: (b[i], 0)),
                  pl.BlockSpec((1, d, 2 * MOE_FF), lambda i, b, e, s, v: (e[i], 0, 0)),
                  pl.BlockSpec((1, MOE_FF, d), lambda i, b, e, s, v: (e[i], 0, 0))],
        out_specs=pl.BlockSpec((bm, d), lambda i, b, e, s, v: (b[i], 0)),
        scratch_shapes=[pltpu.VMEM((d, 2 * MOE_FF), BF16), pltpu.VMEM((MOE_FF, d), BF16)],
    )
    return pl.pallas_call(
        _expert_kernel,
        grid_spec=grid_spec,
        out_shape=jax.ShapeDtypeStruct((n, d), F32),
        compiler_params=_params("arbitrary"),
        name="moe_experts",
    )(item_block, item_expert, seg_start, n_valid, xs, w_in, w_out)


def _combine_kernel(d0_ref, d1_ref, n0_ref, n1_ref, x_ref, mod_ref, gate_ref, ys_ref, o_ref, y_ref, sem):
    i = pl.program_id(0)
    last = pl.num_programs(0) - 1
    tm, d = x_ref.shape

    def gather(dest_refs, buf):
        def issue(g, carry):
            for u in range(SUBLANES):
                for k, dest_ref in enumerate(dest_refs):
                    pltpu.make_async_copy(ys_ref.at[pl.ds(dest_ref[g * SUBLANES + u], 1)],
                                          y_ref.at[buf, k, g, pl.ds(u, 1)], sem.at[buf]).start(priority=k)
            return carry

        lax.fori_loop(0, tm // SUBLANES, issue, 0)

    def step(buf):
        @pl.when(i == 0)
        def _():
            gather((d0_ref, d1_ref), buf)

        @pl.when(i < last)
        def _():
            gather((n0_ref, n1_ref), 1 - buf)

        _wait_rows(ys_ref, tm, sem.at[buf], 2)
        gates = gate_ref[...]
        moe = gates[:, 0:1] * y_ref[buf, 0].reshape(tm, d) + gates[:, 1:2] * y_ref[buf, 1].reshape(tm, d)
        o_ref[...] = x_ref[...] + mod_ref[0][5:6] * moe

    for buf in range(2):
        pl.when(i % 2 == buf)(functools.partial(step, buf))


def moe_combine(x2, mod, gates_t, dest, ys, tiles_per_batch_of):
    t, d = x2.shape
    tm = MOVE_TILE
    tpb = tiles_per_batch_of(tm)
    n_tiles = t // tm
    slot_spec = pl.BlockSpec((tm,), lambda i: (i,), memory_space=pltpu.SMEM)
    next_spec = pl.BlockSpec((tm,), lambda i: (jnp.minimum(i + 1, n_tiles - 1),), memory_space=pltpu.SMEM)
    return pl.pallas_call(
        _combine_kernel,
        grid=(n_tiles,),
        in_specs=[slot_spec, slot_spec, next_spec, next_spec,
                  pl.BlockSpec((tm, d), lambda i: (i, 0)),
                  pl.BlockSpec((1, 6, d), lambda i: (i // tpb, 0, 0)),
                  pl.BlockSpec((tm, 2), lambda i: (i, 0)),
                  pl.BlockSpec(memory_space=pl.ANY)],
        out_specs=pl.BlockSpec((tm, d), lambda i: (i, 0)),
        out_shape=jax.ShapeDtypeStruct((t, d), F32),
        scratch_shapes=[pltpu.VMEM((2, 2, tm // SUBLANES, SUBLANES, d), F32), pltpu.SemaphoreType.DMA((2,))],
        compiler_params=_params("arbitrary"),
        name="moe_combine",
    )(dest[0], dest[1], dest[0], dest[1], x2, mod, gates_t, ys)


def moe_layer(x2, mod, gain, routing, w_in, w_out, tiles_per_batch_of):
    t, d = x2.shape
    bm = EXPERT_ROWS
    n_assign = 2 * t
    eid, gate, rank, cnt = routing
    ne = MOE_EXPERTS
    experts = jnp.arange(ne, dtype=jnp.int32)
    upto = experts[None, :] <= experts[:, None]

    def running_total(v):
        return jnp.sum(jnp.where(upto, v[None, :], 0), axis=1).astype(jnp.int32)

    def lookup(table, idx):
        hit = idx[None] == experts.reshape((ne,) + (1,) * idx.ndim)
        return jnp.sum(jnp.where(hit, table.reshape((ne,) + (1,) * idx.ndim), 0), axis=0).astype(jnp.int32)

    counts = cnt[:, 0]
    seg_end = running_total(counts)
    seg_start = jnp.concatenate([jnp.zeros((1,), jnp.int32), seg_end])
    dest = lookup(seg_start[:ne], eid) + rank
    n_blocks = n_assign // bm
    n_items = n_blocks + ne - 1
    first_blk = seg_start[:ne] // bm
    last_blk = jnp.where(counts > 0, (seg_end - 1) // bm, first_blk - 1)
    per_expert = jnp.maximum(last_blk - first_blk + 1, 0)
    item_end = running_total(per_expert)
    n_valid = item_end[ne - 1]
    item_ids = jnp.arange(n_items, dtype=jnp.int32)
    item_ids_c = jnp.maximum(jnp.minimum(item_ids, n_valid - 1), 0)
    item_expert = jnp.minimum(jnp.sum(item_ids_c[:, None] >= item_end[None, :], axis=1), ne - 1).astype(jnp.int32)
    item_block = lookup(first_blk - (item_end - per_expert), item_expert) + item_ids_c
    xs = moe_dispatch(x2, mod, gain, dest, tiles_per_batch_of)
    ys = moe_experts(xs, w_in, w_out, item_block, item_expert, seg_start,
                     n_valid.reshape(1).astype(jnp.int32))
    return moe_combine(x2, mod, gate.T, dest, ys, tiles_per_batch_of)


SWA_HEAD_DIM = 64
SWA_Q_HEADS = D_MODEL // SWA_HEAD_DIM
SWA_KV_HEADS = 4
SWA_BLOCK = 128
SWA_Q_WIDTH = SWA_Q_HEADS * SWA_HEAD_DIM
SWA_KV_WIDTH = SWA_KV_HEADS * SWA_HEAD_DIM
HEADS_PER_VREG = LANES // SWA_HEAD_DIM


def _swa_kernel(sink_ref, x_ref, mod_ref, gain_ref, win_ref, qg_ref, kg_ref, ones_ref, wout_ref,
                fgain_ref, rw_ref, rb_ref, before_ref, o_ref, eid_ref, gate_ref, rank_ref, cnt_ref,
                kx_ref, vx_ref, att_ref, base_ref):
    t = pl.program_id(1)
    tm = x_ref.shape[0]
    blk = SWA_BLOCK
    hd = SWA_HEAD_DIM

    @pl.when(t == 0)
    def _():
        kx_ref[0:blk, :] = jnp.zeros((blk, kx_ref.shape[1]), BF16)
        vx_ref[0:blk, :] = jnp.zeros((blk, vx_ref.shape[1]), BF16)

    x = x_ref[...]
    mod = mod_ref[0]
    h = _modulated_norm(x, gain_ref[...], mod[0:1], mod[1:2]).astype(BF16)
    proj = jnp.dot(h, win_ref[...], preferred_element_type=F32)

    def head_rms(a):
        ss = jnp.dot((a * a).astype(BF16), ones_ref[...], preferred_element_type=F32)
        return a * lax.rsqrt(ss * (1.0 / hd) + NORM_EPS)

    lane = lax.broadcasted_iota(jnp.int32, (tm, LANES), 1)
    low = lane < hd
    for c in range(SWA_Q_WIDTH // LANES):
        cols = slice(c * LANES, (c + 1) * LANES)
        att_ref[:, cols] = (head_rms(proj[:, cols]) * qg_ref[:, cols]).astype(BF16)
    for c in range(SWA_KV_WIDTH // LANES):
        cols = slice(c * LANES, (c + 1) * LANES)
        kc = head_rms(proj[:, SWA_Q_WIDTH + c * LANES:SWA_Q_WIDTH + (c + 1) * LANES]) * kg_ref[:, cols]
        vc = proj[:, SWA_Q_WIDTH + SWA_KV_WIDTH + c * LANES:SWA_Q_WIDTH + SWA_KV_WIDTH + (c + 1) * LANES]
        for ref, a in ((kx_ref, kc), (vx_ref, vc)):
            even_lo = jnp.where(low, a, 0.0)
            odd_hi = jnp.where(low, 0.0, a)
            j0 = HEADS_PER_VREG * c
            ref[blk:blk + tm, (2 * j0) * LANES:(2 * j0 + 1) * LANES] = even_lo.astype(BF16)
            ref[blk:blk + tm, (2 * j0 + 1) * LANES:(2 * j0 + 2) * LANES] = pltpu.roll(even_lo, hd, 1).astype(BF16)
            ref[blk:blk + tm, (2 * j0 + 2) * LANES:(2 * j0 + 3) * LANES] = pltpu.roll(odd_hi, hd, 1).astype(BF16)
            ref[blk:blk + tm, (2 * j0 + 3) * LANES:(2 * j0 + 4) * LANES] = odd_hi.astype(BF16)

    qi = lax.broadcasted_iota(jnp.int32, (blk, 2 * blk), 0)
    kj = lax.broadcasted_iota(jnp.int32, (blk, 2 * blk), 1)
    lane_q = lax.broadcasted_iota(jnp.int32, (blk, LANES), 1) < hd
    pairs_per_kv = SWA_Q_HEADS // SWA_KV_HEADS // HEADS_PER_VREG
    for i in range(tm // blk):
        rows = slice(i * blk, (i + 1) * blk)
        keys = slice(i * blk, (i + 2) * blk)
        floor = jnp.where(t == 0, blk - 1, qi) if i == 0 else qi
        mask = jnp.logical_and(kj > floor, kj <= qi + blk)
        heads = [(p, half) for p in range(SWA_Q_HEADS // HEADS_PER_VREG) for half in range(HEADS_PER_VREG)]
        col_of = lambda p, half: (2 * (p // pairs_per_kv) + half) * LANES
        scores = [lax.dot_general(att_ref[rows, p * LANES:(p + 1) * LANES],
                                  kx_ref[keys, col_of(p, half):col_of(p, half) + LANES],
                                  (((1,), (1,)), ((), ())), preferred_element_type=F32) for p, half in heads]
        probs, inv = [], []
        for (p, half), sc in zip(heads, scores):
            sc = jnp.where(mask, sc, -jnp.inf)
            sink = sink_ref[HEADS_PER_VREG * p + half]
            m = jnp.maximum(jnp.max(sc, axis=-1, keepdims=True), sink)
            pr = jnp.exp(sc - m)
            inv.append(1.0 / (jnp.sum(pr, axis=-1, keepdims=True) + jnp.exp(sink - m)))
            probs.append(pr.astype(BF16))
        pvs = [jnp.dot(pr, vx_ref[keys, col_of(p, half):col_of(p, half) + LANES], preferred_element_type=F32)
               for (p, half), pr in zip(heads, probs)]
        for p in range(SWA_Q_HEADS // HEADS_PER_VREG):
            out = (pvs[2 * p] + pvs[2 * p + 1]) * jnp.where(lane_q, inv[2 * p], inv[2 * p + 1])
            att_ref[rows, p * LANES:(p + 1) * LANES] = out.astype(BF16)
    kx_ref[0:blk, :] = kx_ref[tm:tm + blk, :]
    vx_ref[0:blk, :] = vx_ref[tm:tm + blk, :]
    y = jnp.dot(att_ref[...], wout_ref[...], preferred_element_type=F32)
    x_new = x + mod[2:3] * y
    o_ref[...] = x_new
    first = jnp.logical_and(pl.program_id(0) == 0, t == 0)
    eid_ref[...], gate_ref[...], rank_ref[...] = _route(
        x_new, mod[3:4], mod[4:5], first, fgain_ref, rw_ref, rb_ref, before_ref, cnt_ref, base_ref)


def swa_layer(x2, mod, gain, w_in, q_norm, k_norm, sinks, w_out, router, b, s):
    t, d = x2.shape
    tm = TOKEN_TILE
    tpb = s // tm
    hd = SWA_HEAD_DIM
    qg = (jnp.tile(q_norm, SWA_Q_HEADS) * hd ** -0.5).reshape(1, SWA_Q_WIDTH)
    kg = jnp.tile(k_norm, SWA_KV_HEADS).reshape(1, SWA_KV_WIDTH)
    ids = jnp.arange(LANES) // hd
    ones = (ids[:, None] == ids[None, :]).astype(BF16)
    proj_w = SWA_Q_WIDTH + 2 * SWA_KV_WIDTH
    ext_w = 2 * SWA_KV_HEADS * LANES
    const2 = lambda bi, ti: (0, 0)
    tile = lambda bi, ti: (bi * tpb + ti, 0)
    r_args, r_in = _router_operands(*router, tm)
    r_out, r_shapes, r_scratch = _router_results(t, pl.BlockSpec((2, tm), lambda bi, ti: (0, bi * tpb + ti)), (2, t))
    out = pl.pallas_call(
        _swa_kernel,
        grid=(b, tpb),
        in_specs=[pl.BlockSpec(memory_space=pltpu.SMEM),
                  pl.BlockSpec((tm, d), tile),
                  pl.BlockSpec((1, 6, d), lambda bi, ti: (bi, 0, 0)),
                  pl.BlockSpec((1, d), const2),
                  pl.BlockSpec((d, proj_w), const2),
                  pl.BlockSpec((1, SWA_Q_WIDTH), const2),
                  pl.BlockSpec((1, SWA_KV_WIDTH), const2),
                  pl.BlockSpec((LANES, LANES), const2),
                  pl.BlockSpec((SWA_Q_WIDTH, d), const2)] + r_in,
        out_specs=[pl.BlockSpec((tm, d), tile)] + r_out,
        out_shape=[jax.ShapeDtypeStruct((t, d), F32)] + r_shapes,
        scratch_shapes=[pltpu.VMEM((tm + SWA_BLOCK, ext_w), BF16), pltpu.VMEM((tm + SWA_BLOCK, ext_w), BF16),
                        pltpu.VMEM((tm, SWA_Q_WIDTH), BF16), r_scratch],
        compiler_params=_params("arbitrary", "arbitrary"),
        name="swa_mixer",
    )(sinks, x2, mod, gain.reshape(1, d), w_in.astype(BF16), qg, kg, ones, w_out.astype(BF16), *r_args)
    return out[0], tuple(out[1:])


DN_QK_HEADS = 4
DN_V_HEADS = 8
DN_HEAD_DIM = D_MODEL // DN_V_HEADS
DN_CONV = 4
DN_CHUNK = 64
DN_KEY_WIDTH = DN_QK_HEADS * DN_HEAD_DIM
DN_VAL_WIDTH = DN_V_HEADS * DN_HEAD_DIM
DN_CONV_WIDTH = 2 * DN_KEY_WIDTH + DN_VAL_WIDTH
DN_PREP_TILE = 128
DN_SCAN_TILE = 256


def _split_bf16(a):
    hi = a.astype(BF16)
    return hi, (a - hi.astype(F32)).astype(BF16)


def _mm(a, b):
    return jnp.dot(a.astype(BF16), b.astype(BF16), preferred_element_type=F32)


def _dn_proj_kernel(x_ref, mod_ref, gain_ref, w_ref, wba_ref, conv_ref, rate_ref, dtb_ref,
                    q_ref, k_ref, v_ref, z_ref, bg_ref, ext_ref):
    t = pl.program_id(1)
    tm = x_ref.shape[0]
    pad = SUBLANES

    @pl.when(t == 0)
    def _():
        ext_ref[0:pad, :] = jnp.zeros((pad, ext_ref.shape[1]), F32)

    mod = mod_ref[0]
    h = _modulated_norm(x_ref[...], gain_ref[...], mod[0:1], mod[1:2])
    hh, hl = _split_bf16(h)
    proj = jnp.dot(hh, w_ref[...], preferred_element_type=F32)
    z_ref[...] = proj[:, DN_CONV_WIDTH:].astype(BF16)

    ext_ref[pad:pad + tm, :] = proj[:, :DN_CONV_WIDTH]
    acc = None
    for j in range(DN_CONV):
        start = pad - (DN_CONV - 1) + j
        term = conv_ref[j:j + 1, :] * ext_ref[start:start + tm, :]
        acc = term if acc is None else acc + term
    ext_ref[0:pad, :] = ext_ref[tm:tm + pad, :]
    qkv = _silu(acc)
    for hq in range(2 * DN_QK_HEADS):
        cols = slice(hq * DN_HEAD_DIM, (hq + 1) * DN_HEAD_DIM)
        a = qkv[:, cols]
        a = a * lax.rsqrt(jnp.sum(a * a, axis=-1, keepdims=True) + NORM_EPS)
        if hq < DN_QK_HEADS:
            q_ref[:, cols] = (a * DN_HEAD_DIM ** -0.5).astype(BF16)
        else:
            k_ref[:, hq * DN_HEAD_DIM - DN_KEY_WIDTH:(hq + 1) * DN_HEAD_DIM - DN_KEY_WIDTH] = a.astype(BF16)
    v_ref[...] = qkv[:, 2 * DN_KEY_WIDTH:].astype(BF16)

    ba = jnp.dot(hh, wba_ref[...], preferred_element_type=F32) + jnp.dot(hl, wba_ref[...], preferred_element_type=F32)
    ba = ba + pltpu.roll(ba, LANES - 2 * DN_V_HEADS, 1)
    lane = lax.broadcasted_iota(jnp.int32, ba.shape, 1)
    beta = 1.0 / (1.0 + jnp.exp(-ba))
    sp = ba + dtb_ref[...]
    g = -rate_ref[...] * (jnp.maximum(sp, 0.0) + jnp.log1p(jnp.exp(-jnp.abs(sp))))
    g = jnp.where(jnp.logical_and(lane >= DN_V_HEADS, lane < 2 * DN_V_HEADS), g, 0.0)
    r = lax.broadcasted_iota(jnp.int32, (tm, tm), 0)
    c = lax.broadcasted_iota(jnp.int32, (tm, tm), 1)
    tri = jnp.logical_and(r // DN_CHUNK == c // DN_CHUNK, c <= r).astype(F32)
    gc = jnp.dot(tri, g, precision=HIGHEST, preferred_element_type=F32)
    bg_ref[...] = jnp.where(lane < DN_V_HEADS, beta, gc)


def _unit_lower_inverses(mats):
    n = mats[0].shape[0]
    r = lax.broadcasted_iota(jnp.int32, (n, n), 0)
    c = lax.broadcasted_iota(jnp.int32, (n, n), 1)
    eye = (r == c).astype(F32)
    size = SUBLANES
    same = (r // size) == (c // size)
    d = [jnp.where(same, a, 0.0) for a in mats]
    d2 = [_mm(v, v) for v in d]
    d4 = [_mm(v, v) for v in d2]
    x = [_mm(eye - v, eye + v2) for v, v2 in zip(d, d2)]
    x = [_mm(v, eye + v4) for v, v4 in zip(x, d4)]
    while size < n:
        wider = (r // (2 * size)) == (c // (2 * size))
        ring = jnp.logical_and(wider, jnp.logical_not(same))
        xl = [_mm(v, jnp.where(ring, a, 0.0)) for v, a in zip(x, mats)]
        x = [v - _mm(vl, v) for v, vl in zip(x, xl)]
        same = wider
        size *= 2
    return x


def _dn_prep_kernel(q_ref, k_ref, v_ref, bg_ref, gct_ref, u_ref, w_ref, qk_ref):
    tm = q_ref.shape[0]
    ck = DN_CHUNK
    hd = DN_HEAD_DIM
    rep = DN_V_HEADS // DN_QK_HEADS
    r = lax.broadcasted_iota(jnp.int32, (ck, ck), 0)
    c = lax.broadcasted_iota(jnp.int32, (ck, ck), 1)
    lower = c <= r
    strict = c < r
    dot = functools.partial(jnp.dot, preferred_element_type=F32)
    chunks = [slice(ci * ck, (ci + 1) * ck) for ci in range(tm // ck)]
    gram = {}
    for rows in chunks:
        for hq in range(DN_QK_HEADS):
            cols = slice(hq * hd, (hq + 1) * hd)
            k = k_ref[rows, cols]
            gram[(hq, rows.start)] = lax.dot_general(jnp.concatenate([k, q_ref[rows, cols]], axis=0), k,
                                                     (((1,), (1,)), ((), ())), preferred_element_type=F32)
    problems = [(hv, rows) for rows in chunks for hv in range(DN_V_HEADS)]
    mats, rhs = [], []
    for hv, rows in problems:
        beta = bg_ref[rows, hv:hv + 1]
        gcc = bg_ref[rows, DN_V_HEADS + hv:DN_V_HEADS + hv + 1]
        gcr = gct_ref[hv:hv + 1, rows]
        decay = jnp.where(lower, jnp.exp(jnp.where(lower, gcc - gcr, 0.0)), 0.0)
        g = gram[(hv // rep, rows.start)]
        mats.append(jnp.where(strict, g[:ck] * beta * decay, 0.0))
        qk_ref[rows, hv * ck:(hv + 1) * ck] = (g[ck:] * decay).astype(BF16)
        kf = k_ref[rows, (hv // rep) * hd:(hv // rep + 1) * hd].astype(F32)
        vf = v_ref[rows, hv * hd:(hv + 1) * hd].astype(F32)
        rhs.append(jnp.concatenate([vf * beta, kf * (beta * jnp.exp(gcc))], axis=1).astype(BF16))
    inverses = _unit_lower_inverses(mats)
    for (hv, rows), tinv, b in zip(problems, inverses, rhs):
        uw = dot(tinv.astype(BF16), b)
        u_ref[rows, hv * hd:(hv + 1) * hd] = uw[:, :hd].astype(BF16)
        w_ref[rows, hv * hd:(hv + 1) * hd] = uw[:, hd:].astype(BF16)


def _dn_scan_kernel(x_ref, mod_ref, u_ref, w_ref, qk_ref, q_ref, k_ref, z_ref, bg_ref, og_ref, wout_ref,
                    fgain_ref, rw_ref, rb_ref, before_ref, o_ref, eid_ref, gate_ref, rank_ref, cnt_ref,
                    state_ref, att_ref, base_ref):
    t = pl.program_id(0)
    nb, tm, _ = x_ref.shape
    ck = DN_CHUNK
    hd = DN_HEAD_DIM
    rep = DN_V_HEADS // DN_QK_HEADS

    @pl.when(t == 0)
    def _():
        state_ref[...] = jnp.zeros_like(state_ref)

    dot = functools.partial(jnp.dot, preferred_element_type=F32)
    chains = [(bi, hv) for bi in range(nb) for hv in range(DN_V_HEADS)]
    for ci in range(tm // ck):
        rows = slice(ci * ck, (ci + 1) * ck)
        gcc, g_end, lhs = [], [], []
        for bi, hv in chains:
            gc = bg_ref[bi, rows, DN_V_HEADS + hv:DN_V_HEADS + hv + 1]
            gcc.append(gc)
            g_end.append(gc[ck - 1:ck, :])
            q_dec = q_ref[bi, rows, (hv // rep) * hd:(hv // rep + 1) * hd].astype(F32) * jnp.exp(gc)
            lhs.append(jnp.concatenate([w_ref[bi, rows, hv * hd:(hv + 1) * hd], q_dec.astype(BF16)], axis=0))
        ws_qs = [dot(a, state_ref[n].astype(BF16)) for n, a in enumerate(lhs)]
        v_new = [(u_ref[bi, rows, hv * hd:(hv + 1) * hd].astype(F32) - m[:ck]).astype(BF16)
                 for (bi, hv), m in zip(chains, ws_qs)]
        outs = [m[ck:] + dot(qk_ref[bi, rows, hv * ck:(hv + 1) * ck], vn)
                for (bi, hv), m, vn in zip(chains, ws_qs, v_new)]
        for n, (bi, hv) in enumerate(chains):
            kf = k_ref[bi, rows, (hv // rep) * hd:(hv // rep + 1) * hd].astype(F32)
            k_dec = (kf * jnp.exp(g_end[n] - gcc[n])).astype(BF16)
            state_ref[n] = state_ref[n] * jnp.exp(g_end[n]) + lax.dot_general(
                k_dec, v_new[n], (((0,), (0,)), ((), ())), preferred_element_type=F32)
        for (bi, hv), o in zip(chains, outs):
            zf = z_ref[bi, rows, hv * hd:(hv + 1) * hd].astype(F32)
            att_ref[bi * tm + ci * ck:bi * tm + (ci + 1) * ck, hv * hd:(hv + 1) * hd] = (
                _rms(o) * og_ref[...] * _silu(zf)).astype(BF16)
    y = dot(att_ref[...], wout_ref[...])
    for bi in range(nb):
        mod = mod_ref[bi]
        x_new = x_ref[bi] + mod[2:3] * y[bi * tm:(bi + 1) * tm]
        o_ref[bi] = x_new
        eid_ref[:, bi, :], gate_ref[:, bi, :], rank_ref[:, bi, :] = _route(
            x_new, mod[3:4], mod[4:5], jnp.logical_and(t == 0, bi == 0), fgain_ref, rw_ref, rb_ref, before_ref, cnt_ref, base_ref)


def deltanet_layer(x2, mod, gain, w_in, conv_w, a_log, dt_bias, o_norm, w_out, router, b, s):
    t, d = x2.shape
    nh = DN_V_HEADS
    main_w = DN_CONV_WIDTH + DN_VAL_WIDTH
    w_main = w_in[:, :main_w].astype(BF16)
    w_ba = w_in[:, main_w:]
    w_ba_hi = w_ba.astype(BF16)
    w_ba_lo = (w_ba - w_ba_hi.astype(F32)).astype(BF16)
    w_ba2 = jnp.concatenate([w_ba_hi, w_ba_lo, jnp.zeros((d, LANES - 4 * nh), BF16)], axis=1)
    lanes_pad = lambda v: jnp.concatenate([jnp.zeros((nh,), F32), v.astype(F32),
                                           jnp.zeros((LANES - 2 * nh,), F32)]).reshape(1, LANES)
    rate = lanes_pad(jnp.exp(a_log.astype(F32)))
    dtb = lanes_pad(dt_bias)

    tm = TOKEN_TILE
    tpb = s // tm
    const2 = lambda bi, ti: (0, 0)
    tile = lambda bi, ti: (bi * tpb + ti, 0)
    q, k, v, z, bg = pl.pallas_call(
        _dn_proj_kernel,
        grid=(b, tpb),
        in_specs=[pl.BlockSpec((tm, d), tile),
                  pl.BlockSpec((1, 6, d), lambda bi, ti: (bi, 0, 0)),
                  pl.BlockSpec((1, d), const2),
                  pl.BlockSpec((d, main_w), const2),
                  pl.BlockSpec((d, LANES), const2),
                  pl.BlockSpec((DN_CONV, DN_CONV_WIDTH), const2),
                  pl.BlockSpec((1, LANES), const2),
                  pl.BlockSpec((1, LANES), const2)],
        out_specs=[pl.BlockSpec((tm, DN_KEY_WIDTH), tile), pl.BlockSpec((tm, DN_KEY_WIDTH), tile),
                   pl.BlockSpec((tm, DN_VAL_WIDTH), tile), pl.BlockSpec((tm, DN_VAL_WIDTH), tile),
                   pl.BlockSpec((tm, LANES), tile)],
        out_shape=[jax.ShapeDtypeStruct((t, DN_KEY_WIDTH), BF16), jax.ShapeDtypeStruct((t, DN_KEY_WIDTH), BF16),
                   jax.ShapeDtypeStruct((t, DN_VAL_WIDTH), BF16), jax.ShapeDtypeStruct((t, DN_VAL_WIDTH), BF16),
                   jax.ShapeDtypeStruct((t, LANES), F32)],
        scratch_shapes=[pltpu.VMEM((tm + SUBLANES, DN_CONV_WIDTH), F32)],
        compiler_params=_params("arbitrary", "arbitrary"),
        name="deltanet_proj",
    )(x2, mod, gain.reshape(1, d), w_main, w_ba2, conv_w, rate, dtb)

    gct = bg[:, nh:2 * nh].T
    tp = DN_PREP_TILE
    rows_of = lambda width: pl.BlockSpec((tp, width), lambda i: (i, 0))
    qk_w = nh * DN_CHUNK
    u, w, qk = pl.pallas_call(
        _dn_prep_kernel,
        grid=(t // tp,),
        in_specs=[rows_of(DN_KEY_WIDTH), rows_of(DN_KEY_WIDTH), rows_of(DN_VAL_WIDTH), rows_of(LANES),
                  pl.BlockSpec((nh, tp), lambda i: (0, i))],
        out_specs=[rows_of(DN_VAL_WIDTH), rows_of(DN_VAL_WIDTH), rows_of(qk_w)],
        out_shape=[jax.ShapeDtypeStruct((t, DN_VAL_WIDTH), BF16), jax.ShapeDtypeStruct((t, DN_VAL_WIDTH), BF16),
                   jax.ShapeDtypeStruct((t, qk_w), BF16)],
        compiler_params=_params("arbitrary"),
        name="deltanet_prep",
    )(q, k, v, bg, gct)

    ts = DN_SCAN_TILE
    seq = lambda a: a.reshape(b, s, a.shape[-1])
    both = lambda width: pl.BlockSpec((b, ts, width), lambda i: (0, i, 0))
    r_args, r_in = _router_operands(*router, ts)
    r_out, r_shapes, r_scratch = _router_results(t, pl.BlockSpec((2, b, ts), lambda i: (0, 0, i)), (2, b, s))
    out = pl.pallas_call(
        _dn_scan_kernel,
        grid=(s // ts,),
        in_specs=[both(d),
                  pl.BlockSpec((b, 6, d), lambda i: (0, 0, 0)),
                  both(DN_VAL_WIDTH), both(DN_VAL_WIDTH), both(qk_w), both(DN_KEY_WIDTH), both(DN_KEY_WIDTH),
                  both(DN_VAL_WIDTH), both(LANES),
                  pl.BlockSpec((1, DN_HEAD_DIM), lambda i: (0, 0)),
                  pl.BlockSpec((DN_VAL_WIDTH, d), lambda i: (0, 0))] + r_in,
        out_specs=[both(d)] + r_out,
        out_shape=[jax.ShapeDtypeStruct((b, s, d), F32)] + r_shapes,
        scratch_shapes=[pltpu.VMEM((b * nh, DN_HEAD_DIM, DN_HEAD_DIM), F32),
                        pltpu.VMEM((b * ts, DN_VAL_WIDTH), BF16), r_scratch],
        compiler_params=_params("arbitrary"),
        name="deltanet_scan",
    )(seq(x2), mod, seq(u), seq(w), seq(qk), seq(q), seq(k), seq(z), seq(bg), o_norm.reshape(1, DN_HEAD_DIM),
      w_out.astype(BF16), *r_args)
    eid, gate, rank, cnt = out[1:]
    return out[0].reshape(t, d), (eid.reshape(2, t), gate.reshape(2, t), rank.reshape(2, t), cnt)


def kernel(x, c, l0_norm_mix, l0_norm_ffn, l0_ada_w, l0_ada_b, l0_gm_w_in, l0_gm_v_norm, l0_gm_w_s, l0_gm_b_s, l0_gm_w_out, l0_router_w, l0_router_b, l0_expert_w_in, l0_expert_w_out, l1_norm_mix, l1_norm_ffn, l1_ada_w, l1_ada_b, l1_dn_w_in, l1_dn_conv_w, l1_dn_a_log, l1_dn_dt_bias, l1_dn_o_norm, l1_dn_w_out, l1_router_w, l1_router_b, l1_expert_w_in, l1_expert_w_out, l2_norm_mix, l2_norm_ffn, l2_ada_w, l2_ada_b, l2_swa_w_in, l2_swa_q_norm, l2_swa_k_norm, l2_swa_sinks, l2_swa_w_out, l2_router_w, l2_router_b, l2_expert_w_in, l2_expert_w_out, l3_norm_mix, l3_norm_ffn, l3_ada_w, l3_ada_b, l3_gm_w_in, l3_gm_v_norm, l3_gm_w_s, l3_gm_b_s, l3_gm_w_out, l3_router_w, l3_router_b, l3_expert_w_in, l3_expert_w_out):
    b, s, d = x.shape
    x2 = x.reshape(b * s, d)
    c_pad = jnp.zeros((8, d), F32).at[:b].set(c)
    tiles_per_batch_of = lambda tm: s // tm

    def modulation(ada_w, ada_b):
        return adaln(c_pad, ada_w, ada_b)[:b].reshape(b, 6, d)

    mod = modulation(l0_ada_w, l0_ada_b)
    x2, routing = gmlp_layer(x2, mod, l0_norm_mix, l0_gm_w_in, l0_gm_v_norm, l0_gm_w_s, l0_gm_b_s, l0_gm_w_out,
                             (l0_norm_ffn, l0_router_w, l0_router_b), tiles_per_batch_of(TOKEN_TILE))
    x2 = moe_layer(x2, mod, l0_norm_ffn, routing, l0_expert_w_in, l0_expert_w_out, tiles_per_batch_of)

    mod = modulation(l1_ada_w, l1_ada_b)
    x2, routing = deltanet_layer(x2, mod, l1_norm_mix, l1_dn_w_in, l1_dn_conv_w, l1_dn_a_log, l1_dn_dt_bias,
                                 l1_dn_o_norm, l1_dn_w_out, (l1_norm_ffn, l1_router_w, l1_router_b), b, s)
    x2 = moe_layer(x2, mod, l1_norm_ffn, routing, l1_expert_w_in, l1_expert_w_out, tiles_per_batch_of)

    mod = modulation(l2_ada_w, l2_ada_b)
    x2, routing = swa_layer(x2, mod, l2_norm_mix, l2_swa_w_in, l2_swa_q_norm, l2_swa_k_norm, l2_swa_sinks,
                            l2_swa_w_out, (l2_norm_ffn, l2_router_w, l2_router_b), b, s)
    x2 = moe_layer(x2, mod, l2_norm_ffn, routing, l2_expert_w_in, l2_expert_w_out, tiles_per_batch_of)

    mod = modulation(l3_ada_w, l3_ada_b)
    x2, routing = gmlp_layer(x2, mod, l3_norm_mix, l3_gm_w_in, l3_gm_v_norm, l3_gm_w_s, l3_gm_b_s, l3_gm_w_out,
                             (l3_norm_ffn, l3_router_w, l3_router_b), tiles_per_batch_of(TOKEN_TILE))
    x2 = moe_layer(x2, mod, l3_norm_ffn, routing, l3_expert_w_in, l3_expert_w_out, tiles_per_batch_of)
    return x2.reshape(b, s, d)
```

```python
import functools
import math

import jax
import jax.numpy as jnp
from jax import lax
from jax.experimental import pallas as pl
from jax.experimental.pallas import tpu as pltpu

F32 = jnp.float32
BF16 = jnp.bfloat16
HIGHEST = lax.Precision.HIGHEST

D_MODEL = 1024
NORM_EPS = 1e-6

GM_CHUNK = 128
GM_GROUPS = 8
GM_GROUP_DIM = D_MODEL // GM_GROUPS

MOE_GROUPS = 4
MOE_EXPERTS_PER_GROUP = 8
MOE_EXPERTS = MOE_GROUPS * MOE_EXPERTS_PER_GROUP
MOE_FF = D_MODEL // 2

VMEM_LIMIT_BYTES = 56 * 1024 * 1024

TOKEN_TILE = 512
MOVE_TILE = 512
EXPERT_ROWS = 512
EXPERT_SUB_ROWS = 128
LANES = 128
SUBLANES = 8


def _params(*semantics):
    return pltpu.CompilerParams(dimension_semantics=semantics, vmem_limit_bytes=VMEM_LIMIT_BYTES,
                                disable_bounds_checks=True)


def _rms(xf):
    return xf * lax.rsqrt(jnp.mean(xf * xf, axis=-1, keepdims=True) + NORM_EPS)


def _modulated_norm(x, gain, shift, scale):
    return _rms(x) * gain * (1.0 + scale) + shift


def _gelu_tanh(x):
    return 0.5 * x * (1.0 + jnp.tanh(math.sqrt(2.0 / math.pi) * (x + 0.044715 * (x * x * x))))


def _silu(x):
    return x * (1.0 / (1.0 + jnp.exp(-x)))


def _adaln_kernel(c_ref, w_ref, b_ref, o_ref):
    o_ref[...] = jnp.dot(_silu(c_ref[...]), w_ref[...], precision=HIGHEST,
                         preferred_element_type=F32) + b_ref[...]


def adaln(c_pad, ada_w, ada_b):
    rows, d = c_pad.shape
    n = ada_w.shape[1]
    tn = d
    return pl.pallas_call(
        _adaln_kernel,
        grid=(n // tn,),
        in_specs=[pl.BlockSpec((rows, d), lambda j: (0, 0)),
                  pl.BlockSpec((d, tn), lambda j: (0, j)),
                  pl.BlockSpec((1, tn), lambda j: (0, j))],
        out_specs=pl.BlockSpec((rows, tn), lambda j: (0, j)),
        out_shape=jax.ShapeDtypeStruct((rows, n), F32),
        compiler_params=_params("arbitrary"),
        name="adaln",
    )(c_pad, ada_w, ada_b.reshape(1, n))


def _gmlp_kernel(x_ref, mod_ref, gain_ref, win_ref, vn_ref, ws_ref, bs_ref, wout_ref, fgain_ref, rw_ref, rb_ref, before_ref,
                 o_ref, eid_ref, gate_ref, rank_ref, cnt_ref, u_ref, v_ref, g_ref, base_ref):
    width = D_MODEL
    x = x_ref[...]
    mod = mod_ref[0]
    h = _modulated_norm(x, gain_ref[...], mod[0:1], mod[1:2])
    z = _gelu_tanh(jnp.dot(h.astype(BF16), win_ref[...], preferred_element_type=F32))
    u_ref[...] = z[:, :width]
    v_ref[...] = (_rms(z[:, width:]) * vn_ref[...]).astype(BF16)
    cells = [(slice(c * GM_CHUNK, (c + 1) * GM_CHUNK), g, slice(g * GM_GROUP_DIM, (g + 1) * GM_GROUP_DIM))
             for c in range(x.shape[0] // GM_CHUNK) for g in range(GM_GROUPS)]
    mixed = [jnp.dot(ws_ref[g], v_ref[rows, cols], preferred_element_type=F32) for rows, g, cols in cells]
    for (rows, g, cols), sv in zip(cells, mixed):
        g_ref[rows, cols] = (u_ref[rows, cols] * (sv + bs_ref[:, g:g + 1])).astype(BF16)
    y = jnp.dot(g_ref[...], wout_ref[...], preferred_element_type=F32)
    x_new = x + mod[2:3] * y
    o_ref[...] = x_new
    eid_ref[...], gate_ref[...], rank_ref[...] = _route(
        x_new, mod[3:4], mod[4:5], pl.program_id(0) == 0, fgain_ref, rw_ref, rb_ref, before_ref, cnt_ref, base_ref)


def gmlp_layer(x2, mod, gain, w_in, v_norm, w_s, b_s, w_out, router, tiles_per_batch):
    t, d = x2.shape
    tm = TOKEN_TILE
    ws_causal = jnp.where(jnp.tril(jnp.ones((GM_CHUNK, GM_CHUNK), dtype=bool)), w_s, 0).astype(BF16)
    const2 = lambda i: (0, 0)
    r_args, r_in = _router_operands(*router, tm)
    r_out, r_shapes, r_scratch = _router_results(t, pl.BlockSpec((2, tm), lambda i: (0, i)), (2, t))
    out = pl.pallas_call(
        _gmlp_kernel,
        grid=(t // tm,),
        in_specs=[pl.BlockSpec((tm, d), lambda i: (i, 0)),
                  pl.BlockSpec((1, 6, d), lambda i: (i // tiles_per_batch, 0, 0)),
                  pl.BlockSpec((1, d), const2),
                  pl.BlockSpec((d, 2 * d), const2),
                  pl.BlockSpec((1, d), const2),
                  pl.BlockSpec((GM_GROUPS, GM_CHUNK, GM_CHUNK), lambda i: (0, 0, 0)),
                  pl.BlockSpec((GM_CHUNK, GM_GROUPS), const2),
                  pl.BlockSpec((d, d), const2)] + r_in,
        out_specs=[pl.BlockSpec((tm, d), lambda i: (i, 0))] + r_out,
        out_shape=[jax.ShapeDtypeStruct((t, d), F32)] + r_shapes,
        scratch_shapes=[pltpu.VMEM((tm, d), F32), pltpu.VMEM((tm, d), BF16), pltpu.VMEM((tm, d), BF16), r_scratch],
        compiler_params=_params("arbitrary"),
        name="gmlp_mixer",
    )(x2, mod, gain.reshape(1, d), w_in.astype(BF16), v_norm.reshape(1, d), ws_causal, b_s.T,
      w_out.astype(BF16), *r_args)
    return out[0], tuple(out[1:])


def _route(x, shift, scale, first, gain_ref, rw_ref, rb_ref, before_ref, cnt_ref, base_ref):
    tm = x.shape[0]
    ne = MOE_EXPERTS
    npg = MOE_EXPERTS_PER_GROUP

    @pl.when(first)
    def _():
        base_ref[...] = jnp.zeros_like(base_ref)

    hh, hl = _split_bf16(_modulated_norm(x, gain_ref[...], shift, scale))
    nt = (((1,), (1,)), ((), ()))
    nr = ROUTER_ROWS
    by_hi = lax.dot_general(rw_ref[...], hh, nt, preferred_element_type=F32)
    by_lo = lax.dot_general(rw_ref[0:nr, :], hl, nt, preferred_element_type=F32)
    lt = (by_hi[0:nr] + rb_ref[...]) + (by_hi[nr:] + by_lo)
    lg = [lt[ne + g:ne + g + 1, :] for g in range(MOE_GROUPS)]
    gmax = functools.reduce(jnp.maximum, lg)
    gsum = functools.reduce(lambda a, b: a + b, [jnp.exp(l - gmax) for l in lg])
    pg_top = 1.0 / gsum
    g_sel = jnp.full(gmax.shape, MOE_GROUPS - 1, jnp.int32)
    for g in range(MOE_GROUPS - 2, -1, -1):
        g_sel = jnp.where(lg[g] == gmax, g, g_sel)
    sel = lt[(MOE_GROUPS - 1) * npg:MOE_GROUPS * npg, :]
    for g in range(MOE_GROUPS - 2, -1, -1):
        sel = jnp.where(g_sel == g, lt[g * npg:(g + 1) * npg, :], sel)
    row = lax.broadcasted_iota(jnp.int32, sel.shape, 0)
    m1 = jnp.max(sel, axis=0, keepdims=True)
    i1 = jnp.min(jnp.where(sel == m1, row, npg), axis=0, keepdims=True)
    rest = jnp.where(row == i1, -jnp.inf, sel)
    m2 = jnp.max(rest, axis=0, keepdims=True)
    i2 = jnp.min(jnp.where(rest == m2, row, npg), axis=0, keepdims=True)
    e2 = jnp.exp(m2 - m1)
    inv = pg_top / (1.0 + e2)
    eid = jnp.concatenate([g_sel * npg + i1, g_sel * npg + i2], axis=0)
    gates = jnp.concatenate([inv, inv * e2], axis=0)

    erow = lax.broadcasted_iota(jnp.int32, (ne, tm), 0)
    hits = [erow == eid[k:k + 1, :] for k in range(2)]
    prefix = jnp.dot(jnp.concatenate([jnp.where(hit, 1.0, 0.0).astype(BF16) for hit in hits], axis=0),
                     before_ref[...], preferred_element_type=F32)
    base = base_ref[...]
    ranks = []
    for k, hit in enumerate(hits):
        ranks.append(jnp.sum(jnp.where(hit, prefix[k * ne:(k + 1) * ne] + base, 0.0), axis=0, keepdims=True))
        base = base + jnp.sum(jnp.where(hit, 1.0, 0.0), axis=1, keepdims=True)
    base_ref[...] = base
    cnt_ref[...] = jnp.broadcast_to(base, cnt_ref.shape).astype(jnp.int32)
    return eid, gates, jnp.concatenate(ranks, axis=0).astype(jnp.int32)


ROUTER_ROWS = MOE_EXPERTS + 2 * SUBLANES


def _router_operands(gain, router_w, router_b, tm):
    d = router_w.shape[0]
    pad = ROUTER_ROWS - MOE_EXPERTS - MOE_GROUPS
    rw = jnp.concatenate([router_w[:, MOE_GROUPS:], router_w[:, :MOE_GROUPS], jnp.zeros((d, pad), F32)], axis=1).T
    rw_hi = rw.astype(BF16)
    rw2 = jnp.concatenate([rw_hi, (rw - rw_hi.astype(F32)).astype(BF16)], axis=0)
    rb = jnp.concatenate([router_b[MOE_GROUPS:], router_b[:MOE_GROUPS], jnp.zeros((pad,), F32)])
    pos = jnp.arange(tm)
    before = (pos[:, None] < pos[None, :]).astype(BF16)
    const = lambda *_: (0, 0)
    specs = [pl.BlockSpec((1, d), const), pl.BlockSpec((2 * ROUTER_ROWS, d), const),
             pl.BlockSpec((ROUTER_ROWS, 1), const), pl.BlockSpec((tm, tm), const)]
    return [gain.reshape(1, d), rw2, rb.reshape(ROUTER_ROWS, 1), before], specs


def _router_results(t, slot_spec, slot_shape):
    specs = [slot_spec, slot_spec, slot_spec, pl.BlockSpec((MOE_EXPERTS, LANES), lambda *_: (0, 0))]
    shapes = [jax.ShapeDtypeStruct(slot_shape, jnp.int32), jax.ShapeDtypeStruct(slot_shape, F32),
              jax.ShapeDtypeStruct(slot_shape, jnp.int32), jax.ShapeDtypeStruct((MOE_EXPERTS, LANES), jnp.int32)]
    return specs, shapes, pltpu.VMEM((MOE_EXPERTS, 1), F32)


def _wait_rows(hbm_ref, n_rows, sem, times):
    rows = hbm_ref.at[pl.ds(0, n_rows)]
    for _ in range(times):
        pltpu.make_async_copy(rows, rows, sem).wait()


def _tiled_shape(n, d):
    return (n // SUBLANES, d // LANES, SUBLANES, LANES)


def _store_tiled(ref, a):
    n, d = a.shape
    for j in range(d // LANES):
        ref[:, j, :, :] = a[:, j * LANES:(j + 1) * LANES].reshape(n // SUBLANES, SUBLANES, LANES)


def _load_tiled(ref):
    g, nj, _, _ = ref.shape
    return jnp.concatenate([ref[:, j, :, :].reshape(g * SUBLANES, LANES) for j in range(nj)], axis=1)


def _dispatch_kernel(d0_ref, d1_ref, x_ref, mod_ref, gain_ref, xs_ref, h_ref, sem):
    i = pl.program_id(0)
    last = pl.num_programs(0) - 1
    tm, d = x_ref.shape
    mod = mod_ref[0]
    h = _modulated_norm(x_ref[...], gain_ref[...], mod[3:4], mod[4:5])

    def step(buf):
        _store_tiled(h_ref.at[buf], h)

        def issue(g, carry):
            for u in range(SUBLANES):
                for k, dest_ref in enumerate((d0_ref, d1_ref)):
                    pltpu.make_async_copy(h_ref.at[buf, g, :, u, :], xs_ref.at[dest_ref[g * SUBLANES + u]],
                                          sem.at[buf]).start(priority=k)
            return carry

        lax.fori_loop(0, tm // SUBLANES, issue, 0)

        @pl.when(i > 0)
        def _():
            _wait_rows(xs_ref, tm, sem.at[1 - buf], 2)

        @pl.when(i == last)
        def _():
            _wait_rows(xs_ref, tm, sem.at[buf], 2)

    for buf in range(2):
        pl.when(i % 2 == buf)(functools.partial(step, buf))


def moe_dispatch(x2, mod, gain, dest, tiles_per_batch_of):
    t, d = x2.shape
    tm = MOVE_TILE
    tpb = tiles_per_batch_of(tm)
    const2 = lambda i: (0, 0)
    slot_spec = pl.BlockSpec((tm,), lambda i: (i,), memory_space=pltpu.SMEM)
    return pl.pallas_call(
        _dispatch_kernel,
        grid=(t // tm,),
        in_specs=[slot_spec, slot_spec,
                  pl.BlockSpec((tm, d), lambda i: (i, 0)),
                  pl.BlockSpec((1, 6, d), lambda i: (i // tpb, 0, 0)),
                  pl.BlockSpec((1, d), const2)],
        out_specs=pl.BlockSpec(memory_space=pl.ANY),
        out_shape=jax.ShapeDtypeStruct((2 * t, d // LANES, LANES), F32),
        scratch_shapes=[pltpu.VMEM((2,) + _tiled_shape(tm, d), F32), pltpu.SemaphoreType.DMA((2,))],
        compiler_params=_params("arbitrary"),
        name="moe_dispatch",
    )(dest[0], dest[1], x2, mod, gain.reshape(1, d))


def _expert_kernel(blk_ref, exp_ref, start_ref, nvalid_ref, xs_ref, win_ref, wout_ref, ys_ref,
                   winb_ref, woutb_ref, xbuf_ref, ybuf_ref, in_sem, out_sem):
    i = pl.program_id(0)
    n_items = pl.num_programs(0)
    bm = EXPERT_ROWS
    groups = bm // SUBLANES
    n_blocks = xs_ref.shape[0] // groups
    e = exp_ref[i]
    blk = blk_ref[i]
    prev = jnp.maximum(i - 1, 0)
    nxt = jnp.minimum(i + 1, n_items - 1)
    first = i == 0
    final = i == n_items - 1
    lo = start_ref[e]
    hi = start_ref[e + 1]
    row0 = blk * bm
    live = i < nvalid_ref[0]
    slot = blk % 2

    def block_copies(hbm_ref, buf_ref, b, s, sem, to_hbm):
        copies = []
        for u in range(SUBLANES):
            hbm = hbm_ref.at[pl.ds(b * groups, groups), u]
            vmem = buf_ref.at[s, :, :, u, :]
            copies.append(pltpu.make_async_copy(vmem, hbm, sem.at[s]) if to_hbm
                          else pltpu.make_async_copy(hbm, vmem, sem.at[s]))
        return copies

    def fetch(b, s):
        for c in block_copies(xs_ref, xbuf_ref, b, s, in_sem, False):
            c.start()

    @pl.when(jnp.logical_or(first, exp_ref[prev] != e))
    def _():
        winb_ref[...] = win_ref[0].astype(BF16)
        woutb_ref[...] = wout_ref[0].astype(BF16)

    new_block = jnp.logical_or(first, blk_ref[prev] != blk)
    last_of_block = jnp.logical_or(final, blk_ref[nxt] != blk)
    whole = jnp.logical_and(lo <= row0, hi >= row0 + bm)

    @pl.when(first)
    def _():
        fetch(0, 0)

    @pl.when(new_block)
    def _():
        for c in block_copies(xs_ref, xbuf_ref, blk, slot, in_sem, False):
            c.wait()

        @pl.when(blk + 1 < n_blocks)
        def _():
            fetch(blk + 1, 1 - slot)

        @pl.when(blk >= 2)
        def _():
            for c in block_copies(ys_ref, ybuf_ref, blk - 2, slot, out_sem, True):
                c.wait()

    def ffn(x):
        a_gl = jnp.dot(x.astype(BF16), winb_ref[...], preferred_element_type=F32)
        mid = (_silu(a_gl[:, :MOE_FF]) * a_gl[:, MOE_FF:]).astype(BF16)
        return jnp.dot(mid, woutb_ref[...], preferred_element_type=F32)

    @pl.when(jnp.logical_and(live, whole))
    def _():
        _store_tiled(ybuf_ref.at[slot], ffn(_load_tiled(xbuf_ref.at[slot])))

    @pl.when(jnp.logical_and(new_block, jnp.logical_not(whole)))
    def _():
        ybuf_ref[slot] = jnp.zeros(ybuf_ref.shape[1:], F32)

    sub_groups = EXPERT_SUB_ROWS // SUBLANES
    for sub in range(bm // EXPERT_SUB_ROWS):
        sub0 = row0 + sub * EXPERT_SUB_ROWS
        touched = jnp.logical_and(hi > sub0, lo < sub0 + EXPERT_SUB_ROWS)

        @pl.when(jnp.logical_and(jnp.logical_and(live, jnp.logical_not(whole)), touched))
        def _():
            gs = pl.ds(sub * sub_groups, sub_groups)
            rows = sub0 + lax.broadcasted_iota(jnp.int32, (EXPERT_SUB_ROWS, 1), 0)
            mine = jnp.logical_and(rows >= lo, rows < hi)
            y = jnp.where(mine, ffn(_load_tiled(xbuf_ref.at[slot, gs])), 0.0)
            _store_tiled(ybuf_ref.at[slot, gs], _load_tiled(ybuf_ref.at[slot, gs]) + y)

    @pl.when(last_of_block)
    def _():
        for c in block_copies(ys_ref, ybuf_ref, blk, slot, out_sem, True):
            c.start()

    @pl.when(final)
    def _():
        for c in block_copies(ys_ref, ybuf_ref, blk, slot, out_sem, True):
            c.wait()

        @pl.when(blk >= 1)
        def _():
            for c in block_copies(ys_ref, ybuf_ref, blk - 1, 1 - slot, out_sem, True):
                c.wait()


def moe_experts(xs, w_in, w_out, item_block, item_expert, seg_start, n_valid):
    n, nj, _ = xs.shape
    d = nj * LANES
    bm = EXPERT_ROWS
    n_items = item_block.shape[0]
    by_group = (n // SUBLANES, SUBLANES, nj, LANES)
    grid_spec = pltpu.PrefetchScalarGridSpec(
        num_scalar_prefetch=4,
        grid=(n_items,),
        in_specs=[pl.BlockSpec(memory_space=pl.ANY),
                  pl.BlockSpec((1, d, 2 * MOE_FF), lambda i, b, e, s, v: (e[i], 0, 0)),
                  pl.BlockSpec((1, MOE_FF, d), lambda i, b, e, s, v: (e[i], 0, 0))],
        out_specs=pl.BlockSpec(memory_space=pl.ANY),
        scratch_shapes=[pltpu.VMEM((d, 2 * MOE_FF), BF16), pltpu.VMEM((MOE_FF, d), BF16),
                        pltpu.VMEM((2,) + _tiled_shape(bm, d), F32), pltpu.VMEM((2,) + _tiled_shape(bm, d), F32),
                        pltpu.SemaphoreType.DMA((2,)), pltpu.SemaphoreType.DMA((2,))],
    )
    ys = pl.pallas_call(
        _expert_kernel,
        grid_spec=grid_spec,
        out_shape=jax.ShapeDtypeStruct(by_group, F32),
        compiler_params=_params("arbitrary"),
        name="moe_experts",
    )(item_block, item_expert, seg_start, n_valid, xs.reshape(by_group), w_in, w_out)
    return ys.reshape(n, nj, LANES)


def _combine_kernel(d0_ref, d1_ref, n0_ref, n1_ref, x_ref, mod_ref, gate_ref, ys_ref, o_ref, y_ref, sem):
    i = pl.program_id(0)
    last = pl.num_programs(0) - 1
    tm, d = x_ref.shape

    def gather(dest_refs, buf):
        def issue(g, carry):
            for u in range(SUBLANES):
                for k, dest_ref in enumerate(dest_refs):
                    pltpu.make_async_copy(ys_ref.at[dest_ref[g * SUBLANES + u]],
                                          y_ref.at[buf, k, g, :, u, :], sem.at[buf]).start(priority=k)
            return carry

        lax.fori_loop(0, tm // SUBLANES, issue, 0)

    def step(buf):
        @pl.when(i == 0)
        def _():
            gather((d0_ref, d1_ref), buf)

        @pl.when(i < last)
        def _():
            gather((n0_ref, n1_ref), 1 - buf)

        _wait_rows(ys_ref, tm, sem.at[buf], 2)
        gates = gate_ref[...]
        moe = gates[:, 0:1] * _load_tiled(y_ref.at[buf, 0]) + gates[:, 1:2] * _load_tiled(y_ref.at[buf, 1])
        o_ref[...] = x_ref[...] + mod_ref[0][5:6] * moe

    for buf in range(2):
        pl.when(i % 2 == buf)(functools.partial(step, buf))


def moe_combine(x2, mod, gates_t, dest, ys, tiles_per_batch_of):
    t, d = x2.shape
    tm = MOVE_TILE
    tpb = tiles_per_batch_of(tm)
    n_tiles = t // tm
    slot_spec = pl.BlockSpec((tm,), lambda i: (i,), memory_space=pltpu.SMEM)
    next_spec = pl.BlockSpec((tm,), lambda i: (jnp.minimum(i + 1, n_tiles - 1),), memory_space=pltpu.SMEM)
    return pl.pallas_call(
        _combine_kernel,
        grid=(n_tiles,),
        in_specs=[slot_spec, slot_spec, next_spec, next_spec,
                  pl.BlockSpec((tm, d), lambda i: (i, 0)),
                  pl.BlockSpec((1, 6, d), lambda i: (i // tpb, 0, 0)),
                  pl.BlockSpec((tm, 2), lambda i: (i, 0)),
                  pl.BlockSpec(memory_space=pl.ANY)],
        out_specs=pl.BlockSpec((tm, d), lambda i: (i, 0)),
        out_shape=jax.ShapeDtypeStruct((t, d), F32),
        scratch_shapes=[pltpu.VMEM((2, 2) + _tiled_shape(tm, d), F32), pltpu.SemaphoreType.DMA((2,))],
        compiler_params=_params("arbitrary"),
        name="moe_combine",
    )(dest[0], dest[1], dest[0], dest[1], x2, mod, gates_t, ys)


def moe_layer(x2, mod, gain, routing, w_in, w_out, tiles_per_batch_of):
    t, d = x2.shape
    bm = EXPERT_ROWS
    n_assign = 2 * t
    eid, gate, rank, cnt = routing
    ne = MOE_EXPERTS
    experts = jnp.arange(ne, dtype=jnp.int32)
    upto = experts[None, :] <= experts[:, None]

    def running_total(v):
        return jnp.sum(jnp.where(upto, v[None, :], 0), axis=1).astype(jnp.int32)

    def lookup(table, idx):
        hit = idx[None] == experts.reshape((ne,) + (1,) * idx.ndim)
        return jnp.sum(jnp.where(hit, table.reshape((ne,) + (1,) * idx.ndim), 0), axis=0).astype(jnp.int32)

    counts = cnt[:, 0]
    seg_end = running_total(counts)
    seg_start = jnp.concatenate([jnp.zeros((1,), jnp.int32), seg_end])
    dest = lookup(seg_start[:ne], eid) + rank
    n_blocks = n_assign // bm
    n_items = n_blocks + ne - 1
    first_blk = seg_start[:ne] // bm
    last_blk = jnp.where(counts > 0, (seg_end - 1) // bm, first_blk - 1)
    per_expert = jnp.maximum(last_blk - first_blk + 1, 0)
    item_end = running_total(per_expert)
    n_valid = item_end[ne - 1]
    item_ids = jnp.arange(n_items, dtype=jnp.int32)
    item_ids_c = jnp.maximum(jnp.minimum(item_ids, n_valid - 1), 0)
    item_expert = jnp.minimum(jnp.sum(item_ids_c[:, None] >= item_end[None, :], axis=1), ne - 1).astype(jnp.int32)
    item_block = lookup(first_blk - (item_end - per_expert), item_expert) + item_ids_c
    xs = moe_dispatch(x2, mod, gain, dest, tiles_per_batch_of)
    ys = moe_experts(xs, w_in, w_out, item_block, item_expert, seg_start,
                     n_valid.reshape(1).astype(jnp.int32))
    return moe_combine(x2, mod, gate.T, dest, ys, tiles_per_batch_of)


SWA_HEAD_DIM = 64
SWA_Q_HEADS = D_MODEL // SWA_HEAD_DIM
SWA_KV_HEADS = 4
SWA_BLOCK = 128
SWA_Q_WIDTH = SWA_Q_HEADS * SWA_HEAD_DIM
SWA_KV_WIDTH = SWA_KV_HEADS * SWA_HEAD_DIM
HEADS_PER_VREG = LANES // SWA_HEAD_DIM


def _swa_kernel(sink_ref, x_ref, mod_ref, gain_ref, win_ref, qg_ref, kg_ref, ones_ref, wout_ref,
                fgain_ref, rw_ref, rb_ref, before_ref, o_ref, eid_ref, gate_ref, rank_ref, cnt_ref,
                kx_ref, vx_ref, att_ref, base_ref):
    t = pl.program_id(1)
    tm = x_ref.shape[0]
    blk = SWA_BLOCK
    hd = SWA_HEAD_DIM

    @pl.when(t == 0)
    def _():
        kx_ref[0:blk, :] = jnp.zeros((blk, kx_ref.shape[1]), BF16)
        vx_ref[0:blk, :] = jnp.zeros((blk, vx_ref.shape[1]), BF16)

    x = x_ref[...]
    mod = mod_ref[0]
    h = _modulated_norm(x, gain_ref[...], mod[0:1], mod[1:2]).astype(BF16)
    proj = jnp.dot(h, win_ref[...], preferred_element_type=F32)

    def head_rms(a):
        ss = jnp.dot((a * a).astype(BF16), ones_ref[...], preferred_element_type=F32)
        return a * lax.rsqrt(ss * (1.0 / hd) + NORM_EPS)

    lane = lax.broadcasted_iota(jnp.int32, (tm, LANES), 1)
    low = lane < hd
    for c in range(SWA_Q_WIDTH // LANES):
        cols = slice(c * LANES, (c + 1) * LANES)
        att_ref[:, cols] = (head_rms(proj[:, cols]) * qg_ref[:, cols]).astype(BF16)
    for c in range(SWA_KV_WIDTH // LANES):
        cols = slice(c * LANES, (c + 1) * LANES)
        kc = head_rms(proj[:, SWA_Q_WIDTH + c * LANES:SWA_Q_WIDTH + (c + 1) * LANES]) * kg_ref[:, cols]
        vc = proj[:, SWA_Q_WIDTH + SWA_KV_WIDTH + c * LANES:SWA_Q_WIDTH + SWA_KV_WIDTH + (c + 1) * LANES]
        for ref, a in ((kx_ref, kc), (vx_ref, vc)):
            even_lo = jnp.where(low, a, 0.0)
            odd_hi = jnp.where(low, 0.0, a)
            j0 = HEADS_PER_VREG * c
            ref[blk:blk + tm, (2 * j0) * LANES:(2 * j0 + 1) * LANES] = even_lo.astype(BF16)
            ref[blk:blk + tm, (2 * j0 + 1) * LANES:(2 * j0 + 2) * LANES] = pltpu.roll(even_lo, hd, 1).astype(BF16)
            ref[blk:blk + tm, (2 * j0 + 2) * LANES:(2 * j0 + 3) * LANES] = pltpu.roll(odd_hi, hd, 1).astype(BF16)
            ref[blk:blk + tm, (2 * j0 + 3) * LANES:(2 * j0 + 4) * LANES] = odd_hi.astype(BF16)

    qi = lax.broadcasted_iota(jnp.int32, (blk, 2 * blk), 0)
    kj = lax.broadcasted_iota(jnp.int32, (blk, 2 * blk), 1)
    lane_q = lax.broadcasted_iota(jnp.int32, (blk, LANES), 1) < hd
    pairs_per_kv = SWA_Q_HEADS // SWA_KV_HEADS // HEADS_PER_VREG
    for i in range(tm // blk):
        rows = slice(i * blk, (i + 1) * blk)
        keys = slice(i * blk, (i + 2) * blk)
        floor = jnp.where(t == 0, blk - 1, qi) if i == 0 else qi
        mask = jnp.logical_and(kj > floor, kj <= qi + blk)
        heads = [(p, half) for p in range(SWA_Q_HEADS // HEADS_PER_VREG) for half in range(HEADS_PER_VREG)]
        col_of = lambda p, half: (2 * (p // pairs_per_kv) + half) * LANES
        scores = [lax.dot_general(att_ref[rows, p * LANES:(p + 1) * LANES],
                                  kx_ref[keys, col_of(p, half):col_of(p, half) + LANES],
                                  (((1,), (1,)), ((), ())), preferred_element_type=F32) for p, half in heads]
        probs, inv = [], []
        for (p, half), sc in zip(heads, scores):
            sc = jnp.where(mask, sc, -jnp.inf)
            sink = sink_ref[HEADS_PER_VREG * p + half]
            m = jnp.maximum(jnp.max(sc, axis=-1, keepdims=True), sink)
            pr = jnp.exp(sc - m)
            inv.append(1.0 / (jnp.sum(pr, axis=-1, keepdims=True) + jnp.exp(sink - m)))
            probs.append(pr.astype(BF16))
        pvs = [jnp.dot(pr, vx_ref[keys, col_of(p, half):col_of(p, half) + LANES], preferred_element_type=F32)
               for (p, half), pr in zip(heads, probs)]
        for p in range(SWA_Q_HEADS // HEADS_PER_VREG):
            out = (pvs[2 * p] + pvs[2 * p + 1]) * jnp.where(lane_q, inv[2 * p], inv[2 * p + 1])
            att_ref[rows, p * LANES:(p + 1) * LANES] = out.astype(BF16)
    kx_ref[0:blk, :] = kx_ref[tm:tm + blk, :]
    vx_ref[0:blk, :] = vx_ref[tm:tm + blk, :]
    y = jnp.dot(att_ref[...], wout_ref[...], preferred_element_type=F32)
    x_new = x + mod[2:3] * y
    o_ref[...] = x_new
    first = jnp.logical_and(pl.program_id(0) == 0, t == 0)
    eid_ref[...], gate_ref[...], rank_ref[...] = _route(
        x_new, mod[3:4], mod[4:5], first, fgain_ref, rw_ref, rb_ref, before_ref, cnt_ref, base_ref)


def swa_layer(x2, mod, gain, w_in, q_norm, k_norm, sinks, w_out, router, b, s):
    t, d = x2.shape
    tm = TOKEN_TILE
    tpb = s // tm
    hd = SWA_HEAD_DIM
    qg = (jnp.tile(q_norm, SWA_Q_HEADS) * hd ** -0.5).reshape(1, SWA_Q_WIDTH)
    kg = jnp.tile(k_norm, SWA_KV_HEADS).reshape(1, SWA_KV_WIDTH)
    ids = jnp.arange(LANES) // hd
    ones = (ids[:, None] == ids[None, :]).astype(BF16)
    proj_w = SWA_Q_WIDTH + 2 * SWA_KV_WIDTH
    ext_w = 2 * SWA_KV_HEADS * LANES
    const2 = lambda bi, ti: (0, 0)
    tile = lambda bi, ti: (bi * tpb + ti, 0)
    r_args, r_in = _router_operands(*router, tm)
    r_out, r_shapes, r_scratch = _router_results(t, pl.BlockSpec((2, tm), lambda bi, ti: (0, bi * tpb + ti)), (2, t))
    out = pl.pallas_call(
        _swa_kernel,
        grid=(b, tpb),
        in_specs=[pl.BlockSpec(memory_space=pltpu.SMEM),
                  pl.BlockSpec((tm, d), tile),
                  pl.BlockSpec((1, 6, d), lambda bi, ti: (bi, 0, 0)),
                  pl.BlockSpec((1, d), const2),
                  pl.BlockSpec((d, proj_w), const2),
                  pl.BlockSpec((1, SWA_Q_WIDTH), const2),
                  pl.BlockSpec((1, SWA_KV_WIDTH), const2),
                  pl.BlockSpec((LANES, LANES), const2),
                  pl.BlockSpec((SWA_Q_WIDTH, d), const2)] + r_in,
        out_specs=[pl.BlockSpec((tm, d), tile)] + r_out,
        out_shape=[jax.ShapeDtypeStruct((t, d), F32)] + r_shapes,
        scratch_shapes=[pltpu.VMEM((tm + SWA_BLOCK, ext_w), BF16), pltpu.VMEM((tm + SWA_BLOCK, ext_w), BF16),
                        pltpu.VMEM((tm, SWA_Q_WIDTH), BF16), r_scratch],
        compiler_params=_params("arbitrary", "arbitrary"),
        name="swa_mixer",
    )(sinks, x2, mod, gain.reshape(1, d), w_in.astype(BF16), qg, kg, ones, w_out.astype(BF16), *r_args)
    return out[0], tuple(out[1:])


DN_QK_HEADS = 4
DN_V_HEADS = 8
DN_HEAD_DIM = D_MODEL // DN_V_HEADS
DN_CONV = 4
DN_CHUNK = 64
DN_KEY_WIDTH = DN_QK_HEADS * DN_HEAD_DIM
DN_VAL_WIDTH = DN_V_HEADS * DN_HEAD_DIM
DN_CONV_WIDTH = 2 * DN_KEY_WIDTH + DN_VAL_WIDTH
DN_PREP_TILE = 128
DN_SCAN_TILE = 256


def _split_bf16(a):
    hi = a.astype(BF16)
    return hi, (a - hi.astype(F32)).astype(BF16)


def _mm(a, b):
    return jnp.dot(a.astype(BF16), b.astype(BF16), preferred_element_type=F32)


def _dn_proj_kernel(x_ref, mod_ref, gain_ref, w_ref, wba_ref, conv_ref, rate_ref, dtb_ref,
                    q_ref, k_ref, v_ref, z_ref, bg_ref, ext_ref):
    t = pl.program_id(1)
    tm = x_ref.shape[0]
    pad = SUBLANES

    @pl.when(t == 0)
    def _():
        ext_ref[0:pad, :] = jnp.zeros((pad, ext_ref.shape[1]), F32)

    mod = mod_ref[0]
    h = _modulated_norm(x_ref[...], gain_ref[...], mod[0:1], mod[1:2])
    hh, hl = _split_bf16(h)
    proj = jnp.dot(hh, w_ref[...], preferred_element_type=F32)
    z_ref[...] = proj[:, DN_CONV_WIDTH:].astype(BF16)

    ext_ref[pad:pad + tm, :] = proj[:, :DN_CONV_WIDTH]
    acc = None
    for j in range(DN_CONV):
        start = pad - (DN_CONV - 1) + j
        term = conv_ref[j:j + 1, :] * ext_ref[start:start + tm, :]
        acc = term if acc is None else acc + term
    ext_ref[0:pad, :] = ext_ref[tm:tm + pad, :]
    qkv = _silu(acc)
    for hq in range(2 * DN_QK_HEADS):
        cols = slice(hq * DN_HEAD_DIM, (hq + 1) * DN_HEAD_DIM)
        a = qkv[:, cols]
        a = a * lax.rsqrt(jnp.sum(a * a, axis=-1, keepdims=True) + NORM_EPS)
        if hq < DN_QK_HEADS:
            q_ref[:, cols] = (a * DN_HEAD_DIM ** -0.5).astype(BF16)
        else:
            k_ref[:, hq * DN_HEAD_DIM - DN_KEY_WIDTH:(hq + 1) * DN_HEAD_DIM - DN_KEY_WIDTH] = a.astype(BF16)
    v_ref[...] = qkv[:, 2 * DN_KEY_WIDTH:].astype(BF16)

    ba = jnp.dot(hh, wba_ref[...], preferred_element_type=F32) + jnp.dot(hl, wba_ref[...], preferred_element_type=F32)
    ba = ba + pltpu.roll(ba, LANES - 2 * DN_V_HEADS, 1)
    lane = lax.broadcasted_iota(jnp.int32, ba.shape, 1)
    beta = 1.0 / (1.0 + jnp.exp(-ba))
    sp = ba + dtb_ref[...]
    g = -rate_ref[...] * (jnp.maximum(sp, 0.0) + jnp.log1p(jnp.exp(-jnp.abs(sp))))
    g = jnp.where(jnp.logical_and(lane >= DN_V_HEADS, lane < 2 * DN_V_HEADS), g, 0.0)
    r = lax.broadcasted_iota(jnp.int32, (tm, tm), 0)
    c = lax.broadcasted_iota(jnp.int32, (tm, tm), 1)
    tri = jnp.logical_and(r // DN_CHUNK == c // DN_CHUNK, c <= r).astype(F32)
    gc = jnp.dot(tri, g, precision=HIGHEST, preferred_element_type=F32)
    bg_ref[...] = jnp.where(lane < DN_V_HEADS, beta, gc)


def _unit_lower_inverses(mats):
    n = mats[0].shape[0]
    r = lax.broadcasted_iota(jnp.int32, (n, n), 0)
    c = lax.broadcasted_iota(jnp.int32, (n, n), 1)
    eye = (r == c).astype(F32)
    size = SUBLANES
    same = (r // size) == (c // size)
    d = [jnp.where(same, a, 0.0) for a in mats]
    d2 = [_mm(v, v) for v in d]
    d4 = [_mm(v, v) for v in d2]
    x = [_mm(eye - v, eye + v2) for v, v2 in zip(d, d2)]
    x = [_mm(v, eye + v4) for v, v4 in zip(x, d4)]
    while size < n:
        wider = (r // (2 * size)) == (c // (2 * size))
        ring = jnp.logical_and(wider, jnp.logical_not(same))
        xl = [_mm(v, jnp.where(ring, a, 0.0)) for v, a in zip(x, mats)]
        x = [v - _mm(vl, v) for v, vl in zip(x, xl)]
        same = wider
        size *= 2
    return x


def _dn_prep_kernel(q_ref, k_ref, v_ref, bg_ref, gct_ref, u_ref, w_ref, qk_ref):
    tm = q_ref.shape[0]
    ck = DN_CHUNK
    hd = DN_HEAD_DIM
    rep = DN_V_HEADS // DN_QK_HEADS
    r = lax.broadcasted_iota(jnp.int32, (ck, ck), 0)
    c = lax.broadcasted_iota(jnp.int32, (ck, ck), 1)
    lower = c <= r
    strict = c < r
    dot = functools.partial(jnp.dot, preferred_element_type=F32)
    chunks = [slice(ci * ck, (ci + 1) * ck) for ci in range(tm // ck)]
    gram = {}
    for rows in chunks:
        for hq in range(DN_QK_HEADS):
            cols = slice(hq * hd, (hq + 1) * hd)
            k = k_ref[rows, cols]
            gram[(hq, rows.start)] = lax.dot_general(jnp.concatenate([k, q_ref[rows, cols]], axis=0), k,
                                                     (((1,), (1,)), ((), ())), preferred_element_type=F32)
    problems = [(hv, rows) for rows in chunks for hv in range(DN_V_HEADS)]
    mats, rhs = [], []
    for hv, rows in problems:
        beta = bg_ref[rows, hv:hv + 1]
        gcc = bg_ref[rows, DN_V_HEADS + hv:DN_V_HEADS + hv + 1]
        gcr = gct_ref[hv:hv + 1, rows]
        decay = jnp.where(lower, jnp.exp(jnp.where(lower, gcc - gcr, 0.0)), 0.0)
        g = gram[(hv // rep, rows.start)]
        mats.append(jnp.where(strict, g[:ck] * beta * decay, 0.0))
        qk_ref[rows, hv * ck:(hv + 1) * ck] = (g[ck:] * decay).astype(BF16)
        kf = k_ref[rows, (hv // rep) * hd:(hv // rep + 1) * hd].astype(F32)
        vf = v_ref[rows, hv * hd:(hv + 1) * hd].astype(F32)
        rhs.append(jnp.concatenate([vf * beta, kf * (beta * jnp.exp(gcc))], axis=1).astype(BF16))
    inverses = _unit_lower_inverses(mats)
    for (hv, rows), tinv, b in zip(problems, inverses, rhs):
        uw = dot(tinv.astype(BF16), b)
        u_ref[rows, hv * hd:(hv + 1) * hd] = uw[:, :hd].astype(BF16)
        w_ref[rows, hv * hd:(hv + 1) * hd] = uw[:, hd:].astype(BF16)


def _dn_scan_kernel(x_ref, mod_ref, u_ref, w_ref, qk_ref, q_ref, k_ref, z_ref, bg_ref, og_ref, wout_ref,
                    fgain_ref, rw_ref, rb_ref, before_ref, o_ref, eid_ref, gate_ref, rank_ref, cnt_ref,
                    state_ref, att_ref, base_ref):
    t = pl.program_id(0)
    nb, tm, _ = x_ref.shape
    ck = DN_CHUNK
    hd = DN_HEAD_DIM
    rep = DN_V_HEADS // DN_QK_HEADS

    @pl.when(t == 0)
    def _():
        state_ref[...] = jnp.zeros_like(state_ref)

    dot = functools.partial(jnp.dot, preferred_element_type=F32)
    chains = [(bi, hv) for bi in range(nb) for hv in range(DN_V_HEADS)]
    for ci in range(tm // ck):
        rows = slice(ci * ck, (ci + 1) * ck)
        gcc, g_end, lhs = [], [], []
        for bi, hv in chains:
            gc = bg_ref[bi, rows, DN_V_HEADS + hv:DN_V_HEADS + hv + 1]
            gcc.append(gc)
            g_end.append(gc[ck - 1:ck, :])
            q_dec = q_ref[bi, rows, (hv // rep) * hd:(hv // rep + 1) * hd].astype(F32) * jnp.exp(gc)
            lhs.append(jnp.concatenate([w_ref[bi, rows, hv * hd:(hv + 1) * hd], q_dec.astype(BF16)], axis=0))
        ws_qs = [dot(a, state_ref[n].astype(BF16)) for n, a in enumerate(lhs)]
        v_new = [(u_ref[bi, rows, hv * hd:(hv + 1) * hd].astype(F32) - m[:ck]).astype(BF16)
                 for (bi, hv), m in zip(chains, ws_qs)]
        outs = [m[ck:] + dot(qk_ref[bi, rows, hv * ck:(hv + 1) * ck], vn)
                for (bi, hv), m, vn in zip(chains, ws_qs, v_new)]
        for n, (bi, hv) in enumerate(chains):
            kf = k_ref[bi, rows, (hv // rep) * hd:(hv // rep + 1) * hd].astype(F32)
            k_dec = (kf * jnp.exp(g_end[n] - gcc[n])).astype(BF16)
            state_ref[n] = state_ref[n] * jnp.exp(g_end[n]) + lax.dot_general(
                k_dec, v_new[n], (((0,), (0,)), ((), ())), preferred_element_type=F32)
        for (bi, hv), o in zip(chains, outs):
            zf = z_ref[bi, rows, hv * hd:(hv + 1) * hd].astype(F32)
            att_ref[bi * tm + ci * ck:bi * tm + (ci + 1) * ck, hv * hd:(hv + 1) * hd] = (
                _rms(o) * og_ref[...] * _silu(zf)).astype(BF16)
    y = dot(att_ref[...], wout_ref[...])
    for bi in range(nb):
        mod = mod_ref[bi]
        x_new = x_ref[bi] + mod[2:3] * y[bi * tm:(bi + 1) * tm]
        o_ref[bi] = x_new
        eid_ref[:, bi, :], gate_ref[:, bi, :], rank_ref[:, bi, :] = _route(
            x_new, mod[3:4], mod[4:5], jnp.logical_and(t == 0, bi == 0), fgain_ref, rw_ref, rb_ref, before_ref, cnt_ref, base_ref)


def deltanet_layer(x2, mod, gain, w_in, conv_w, a_log, dt_bias, o_norm, w_out, router, b, s):
    t, d = x2.shape
    nh = DN_V_HEADS
    main_w = DN_CONV_WIDTH + DN_VAL_WIDTH
    w_main = w_in[:, :main_w].astype(BF16)
    w_ba = w_in[:, main_w:]
    w_ba_hi = w_ba.astype(BF16)
    w_ba_lo = (w_ba - w_ba_hi.astype(F32)).astype(BF16)
    w_ba2 = jnp.concatenate([w_ba_hi, w_ba_lo, jnp.zeros((d, LANES - 4 * nh), BF16)], axis=1)
    lanes_pad = lambda v: jnp.concatenate([jnp.zeros((nh,), F32), v.astype(F32),
                                           jnp.zeros((LANES - 2 * nh,), F32)]).reshape(1, LANES)
    rate = lanes_pad(jnp.exp(a_log.astype(F32)))
    dtb = lanes_pad(dt_bias)

    tm = TOKEN_TILE
    tpb = s // tm
    const2 = lambda bi, ti: (0, 0)
    tile = lambda bi, ti: (bi * tpb + ti, 0)
    q, k, v, z, bg = pl.pallas_call(
        _dn_proj_kernel,
        grid=(b, tpb),
        in_specs=[pl.BlockSpec((tm, d), tile),
                  pl.BlockSpec((1, 6, d), lambda bi, ti: (bi, 0, 0)),
                  pl.BlockSpec((1, d), const2),
                  pl.BlockSpec((d, main_w), const2),
                  pl.BlockSpec((d, LANES), const2),
                  pl.BlockSpec((DN_CONV, DN_CONV_WIDTH), const2),
                  pl.BlockSpec((1, LANES), const2),
                  pl.BlockSpec((1, LANES), const2)],
        out_specs=[pl.BlockSpec((tm, DN_KEY_WIDTH), tile), pl.BlockSpec((tm, DN_KEY_WIDTH), tile),
                   pl.BlockSpec((tm, DN_VAL_WIDTH), tile), pl.BlockSpec((tm, DN_VAL_WIDTH), tile),
                   pl.BlockSpec((tm, LANES), tile)],
        out_shape=[jax.ShapeDtypeStruct((t, DN_KEY_WIDTH), BF16), jax.ShapeDtypeStruct((t, DN_KEY_WIDTH), BF16),
                   jax.ShapeDtypeStruct((t, DN_VAL_WIDTH), BF16), jax.ShapeDtypeStruct((t, DN_VAL_WIDTH), BF16),
                   jax.ShapeDtypeStruct((t, LANES), F32)],
        scratch_shapes=[pltpu.VMEM((tm + SUBLANES, DN_CONV_WIDTH), F32)],
        compiler_params=_params("arbitrary", "arbitrary"),
        name="deltanet_proj",
    )(x2, mod, gain.reshape(1, d), w_main, w_ba2, conv_w, rate, dtb)

    gct = bg[:, nh:2 * nh].T
    tp = DN_PREP_TILE
    rows_of = lambda width: pl.BlockSpec((tp, width), lambda i: (i, 0))
    qk_w = nh * DN_CHUNK
    u, w, qk = pl.pallas_call(
        _dn_prep_kernel,
        grid=(t // tp,),
        in_specs=[rows_of(DN_KEY_WIDTH), rows_of(DN_KEY_WIDTH), rows_of(DN_VAL_WIDTH), rows_of(LANES),
                  pl.BlockSpec((nh, tp), lambda i: (0, i))],
        out_specs=[rows_of(DN_VAL_WIDTH), rows_of(DN_VAL_WIDTH), rows_of(qk_w)],
        out_shape=[jax.ShapeDtypeStruct((t, DN_VAL_WIDTH), BF16), jax.ShapeDtypeStruct((t, DN_VAL_WIDTH), BF16),
                   jax.ShapeDtypeStruct((t, qk_w), BF16)],
        compiler_params=_params("arbitrary"),
        name="deltanet_prep",
    )(q, k, v, bg, gct)

    ts = DN_SCAN_TILE
    seq = lambda a: a.reshape(b, s, a.shape[-1])
    both = lambda width: pl.BlockSpec((b, ts, width), lambda i: (0, i, 0))
    r_args, r_in = _router_operands(*router, ts)
    r_out, r_shapes, r_scratch = _router_results(t, pl.BlockSpec((2, b, ts), lambda i: (0, 0, i)), (2, b, s))
    out = pl.pallas_call(
        _dn_scan_kernel,
        grid=(s // ts,),
        in_specs=[both(d),
                  pl.BlockSpec((b, 6, d), lambda i: (0, 0, 0)),
                  both(DN_VAL_WIDTH), both(DN_VAL_WIDTH), both(qk_w), both(DN_KEY_WIDTH), both(DN_KEY_WIDTH),
                  both(DN_VAL_WIDTH), both(LANES),
                  pl.BlockSpec((1, DN_HEAD_DIM), lambda i: (0, 0)),
                  pl.BlockSpec((DN_VAL_WIDTH, d), lambda i: (0, 0))] + r_in,
        out_specs=[both(d)] + r_out,
        out_shape=[jax.ShapeDtypeStruct((b, s, d), F32)] + r_shapes,
        scratch_shapes=[pltpu.VMEM((b * nh, DN_HEAD_DIM, DN_HEAD_DIM), F32),
                        pltpu.VMEM((b * ts, DN_VAL_WIDTH), BF16), r_scratch],
        compiler_params=_params("arbitrary"),
        name="deltanet_scan",
    )(seq(x2), mod, seq(u), seq(w), seq(qk), seq(q), seq(k), seq(z), seq(bg), o_norm.reshape(1, DN_HEAD_DIM),
      w_out.astype(BF16), *r_args)
    eid, gate, rank, cnt = out[1:]
    return out[0].reshape(t, d), (eid.reshape(2, t), gate.reshape(2, t), rank.reshape(2, t), cnt)


def kernel(x, c, l0_norm_mix, l0_norm_ffn, l0_ada_w, l0_ada_b, l0_gm_w_in, l0_gm_v_norm, l0_gm_w_s, l0_gm_b_s, l0_gm_w_out, l0_router_w, l0_router_b, l0_expert_w_in, l0_expert_w_out, l1_norm_mix, l1_norm_ffn, l1_ada_w, l1_ada_b, l1_dn_w_in, l1_dn_conv_w, l1_dn_a_log, l1_dn_dt_bias, l1_dn_o_norm, l1_dn_w_out, l1_router_w, l1_router_b, l1_expert_w_in, l1_expert_w_out, l2_norm_mix, l2_norm_ffn, l2_ada_w, l2_ada_b, l2_swa_w_in, l2_swa_q_norm, l2_swa_k_norm, l2_swa_sinks, l2_swa_w_out, l2_router_w, l2_router_b, l2_expert_w_in, l2_expert_w_out, l3_norm_mix, l3_norm_ffn, l3_ada_w, l3_ada_b, l3_gm_w_in, l3_gm_v_norm, l3_gm_w_s, l3_gm_b_s, l3_gm_w_out, l3_router_w, l3_router_b, l3_expert_w_in, l3_expert_w_out):
    b, s, d = x.shape
    x2 = x.reshape(b * s, d)
    c_pad = jnp.zeros((8, d), F32).at[:b].set(c)
    tiles_per_batch_of = lambda tm: s // tm

    def modulation(ada_w, ada_b):
        return adaln(c_pad, ada_w, ada_b)[:b].reshape(b, 6, d)

    mod = modulation(l0_ada_w, l0_ada_b)
    x2, routing = gmlp_layer(x2, mod, l0_norm_mix, l0_gm_w_in, l0_gm_v_norm, l0_gm_w_s, l0_gm_b_s, l0_gm_w_out,
                             (l0_norm_ffn, l0_router_w, l0_router_b), tiles_per_batch_of(TOKEN_TILE))
    x2 = moe_layer(x2, mod, l0_norm_ffn, routing, l0_expert_w_in, l0_expert_w_out, tiles_per_batch_of)

    mod = modulation(l1_ada_w, l1_ada_b)
    x2, routing = deltanet_layer(x2, mod, l1_norm_mix, l1_dn_w_in, l1_dn_conv_w, l1_dn_a_log, l1_dn_dt_bias,
                                 l1_dn_o_norm, l1_dn_w_out, (l1_norm_ffn, l1_router_w, l1_router_b), b, s)
    x2 = moe_layer(x2, mod, l1_norm_ffn, routing, l1_expert_w_in, l1_expert_w_out, tiles_per_batch_of)

    mod = modulation(l2_ada_w, l2_ada_b)
    x2, routing = swa_layer(x2, mod, l2_norm_mix, l2_swa_w_in, l2_swa_q_norm, l2_swa_k_norm, l2_swa_sinks,
                            l2_swa_w_out, (l2_norm_ffn, l2_router_w, l2_router_b), b, s)
    x2 = moe_layer(x2, mod, l2_norm_ffn, routing, l2_expert_w_in, l2_expert_w_out, tiles_per_batch_of)

    mod = modulation(l3_ada_w, l3_ada_b)
    x2, routing = gmlp_layer(x2, mod, l3_norm_mix, l3_gm_w_in, l3_gm_v_norm, l3_gm_w_s, l3_gm_b_s, l3_gm_w_out,
                             (l3_norm_ffn, l3_router_w, l3_router_b), tiles_per_batch_of(TOKEN_TILE))
    x2 = moe_layer(x2, mod, l3_norm_ffn, routing, l3_expert_w_in, l3_expert_w_out, tiles_per_batch_of)
    return x2.reshape(b, s, d)
```

```python
import functools
import math

import jax
import jax.numpy as jnp
from jax import lax
from jax.experimental import pallas as pl
from jax.experimental.pallas import tpu as pltpu

F32 = jnp.float32
BF16 = jnp.bfloat16
HIGHEST = lax.Precision.HIGHEST

D_MODEL = 1024
NORM_EPS = 1e-6

GM_CHUNK = 128
GM_GROUPS = 8
GM_GROUP_DIM = D_MODEL // GM_GROUPS

MOE_GROUPS = 4
MOE_EXPERTS_PER_GROUP = 8
MOE_EXPERTS = MOE_GROUPS * MOE_EXPERTS_PER_GROUP
MOE_FF = D_MODEL // 2

VMEM_LIMIT_BYTES = 56 * 1024 * 1024

TOKEN_TILE = 512
MOVE_TILE = 512
EXPERT_ROWS = 512
EXPERT_SUB_ROWS = 128
LANES = 128
SUBLANES = 8


def _params(*semantics):
    return pltpu.CompilerParams(dimension_semantics=semantics, vmem_limit_bytes=VMEM_LIMIT_BYTES,
                                disable_bounds_checks=True)


def _rms(xf):
    return xf * lax.rsqrt(jnp.mean(xf * xf, axis=-1, keepdims=True) + NORM_EPS)


def _modulated_norm(x, gain, shift, scale):
    return _rms(x) * gain * (1.0 + scale) + shift


def _gelu_tanh(x):
    return 0.5 * x * (1.0 + jnp.tanh(math.sqrt(2.0 / math.pi) * (x + 0.044715 * (x * x * x))))


def _silu(x):
    return x * (1.0 / (1.0 + jnp.exp(-x)))


def _adaln_kernel(c_ref, w_ref, b_ref, o_ref):
    o_ref[...] = jnp.dot(_silu(c_ref[...]), w_ref[...], precision=HIGHEST,
                         preferred_element_type=F32) + b_ref[...]


def adaln(c_pad, ada_w, ada_b):
    rows, d = c_pad.shape
    n = ada_w.shape[1]
    tn = d
    return pl.pallas_call(
        _adaln_kernel,
        grid=(n // tn,),
        in_specs=[pl.BlockSpec((rows, d), lambda j: (0, 0)),
                  pl.BlockSpec((d, tn), lambda j: (0, j)),
                  pl.BlockSpec((1, tn), lambda j: (0, j))],
        out_specs=pl.BlockSpec((rows, tn), lambda j: (0, j)),
        out_shape=jax.ShapeDtypeStruct((rows, n), F32),
        compiler_params=_params("arbitrary"),
        name="adaln",
    )(c_pad, ada_w, ada_b.reshape(1, n))


def _gmlp_kernel(x_ref, mod_ref, gain_ref, win_ref, vn_ref, ws_ref, bs_ref, wout_ref, fgain_ref, rw_ref, rb_ref, before_ref,
                 o_ref, eid_ref, gate_ref, rank_ref, cnt_ref, u_ref, v_ref, g_ref, base_ref):
    width = D_MODEL
    x = x_ref[...]
    mod = mod_ref[0]
    h = _modulated_norm(x, gain_ref[...], mod[0:1], mod[1:2])
    z = _gelu_tanh(jnp.dot(h.astype(BF16), win_ref[...], preferred_element_type=F32))
    u_ref[...] = z[:, :width]
    v_ref[...] = (_rms(z[:, width:]) * vn_ref[...]).astype(BF16)
    cells = [(slice(c * GM_CHUNK, (c + 1) * GM_CHUNK), g, slice(g * GM_GROUP_DIM, (g + 1) * GM_GROUP_DIM))
             for c in range(x.shape[0] // GM_CHUNK) for g in range(GM_GROUPS)]
    mixed = [jnp.dot(ws_ref[g], v_ref[rows, cols], preferred_element_type=F32) for rows, g, cols in cells]
    for (rows, g, cols), sv in zip(cells, mixed):
        g_ref[rows, cols] = (u_ref[rows, cols] * (sv + bs_ref[:, g:g + 1])).astype(BF16)
    y = jnp.dot(g_ref[...], wout_ref[...], preferred_element_type=F32)
    x_new = x + mod[2:3] * y
    o_ref[...] = x_new
    eid_ref[...], gate_ref[...], rank_ref[...] = _route(
        x_new, mod[3:4], mod[4:5], pl.program_id(0) == 0, fgain_ref, rw_ref, rb_ref, before_ref, cnt_ref, base_ref)


def gmlp_layer(x2, mod, gain, w_in, v_norm, w_s, b_s, w_out, router, tiles_per_batch, pending=None):
    t, d = x2.shape
    tm = TOKEN_TILE
    ws_causal = jnp.where(jnp.tril(jnp.ones((GM_CHUNK, GM_CHUNK), dtype=bool)), w_s, 0).astype(BF16)
    const2 = lambda i: (0, 0)
    r_args, r_in = _router_operands(*router, tm)
    r_out, r_shapes, r_scratch = _router_results(t, pl.BlockSpec((2, tm), lambda i: (0, i)), (2, t))
    body = _gmlp_kernel
    args = [x2, mod, gain.reshape(1, d), w_in.astype(BF16), v_norm.reshape(1, d), ws_causal, b_s.T,
            w_out.astype(BF16), *r_args]
    in_specs = [pl.BlockSpec((tm, d), lambda i: (i, 0)),
                pl.BlockSpec((1, 6, d), lambda i: (i // tiles_per_batch, 0, 0)),
                pl.BlockSpec((1, d), const2),
                pl.BlockSpec((d, 2 * d), const2),
                pl.BlockSpec((1, d), const2),
                pl.BlockSpec((GM_GROUPS, GM_CHUNK, GM_CHUNK), lambda i: (0, 0, 0)),
                pl.BlockSpec((GM_CHUNK, GM_GROUPS), const2),
                pl.BlockSpec((d, d), const2)] + r_in
    scratch = [pltpu.VMEM((tm, d), F32), pltpu.VMEM((tm, d), BF16), pltpu.VMEM((tm, d), BF16), r_scratch]
    if pending is not None:
        body = _with_pending_moe(body, 0, 1, len(scratch), False)
        p_args, p_specs, p_scratch = _pending_operands(pending, tm, d, lambda i: i, lambda i: i // tiles_per_batch,
                                                       t // tm)
        args, in_specs, scratch = p_args + args, p_specs + in_specs, scratch + p_scratch
    out = pl.pallas_call(
        body,
        grid=(t // tm,),
        in_specs=in_specs,
        out_specs=[pl.BlockSpec((tm, d), lambda i: (i, 0))] + r_out,
        out_shape=[jax.ShapeDtypeStruct((t, d), F32)] + r_shapes,
        scratch_shapes=scratch,
        compiler_params=_params("arbitrary"),
        name="gmlp_mixer",
    )(*args)
    return out[0], tuple(out[1:])


def _route(x, shift, scale, first, gain_ref, rw_ref, rb_ref, before_ref, cnt_ref, base_ref):
    tm = x.shape[0]
    ne = MOE_EXPERTS
    npg = MOE_EXPERTS_PER_GROUP

    @pl.when(first)
    def _():
        base_ref[...] = jnp.zeros_like(base_ref)

    hh, hl = _split_bf16(_modulated_norm(x, gain_ref[...], shift, scale))
    nt = (((1,), (1,)), ((), ()))
    nr = ROUTER_ROWS
    by_hi = lax.dot_general(rw_ref[...], hh, nt, preferred_element_type=F32)
    by_lo = lax.dot_general(rw_ref[0:nr, :], hl, nt, preferred_element_type=F32)
    lt = (by_hi[0:nr] + rb_ref[...]) + (by_hi[nr:] + by_lo)
    lg = [lt[ne + g:ne + g + 1, :] for g in range(MOE_GROUPS)]
    gmax = functools.reduce(jnp.maximum, lg)
    gsum = functools.reduce(lambda a, b: a + b, [jnp.exp(l - gmax) for l in lg])
    pg_top = 1.0 / gsum
    g_sel = jnp.full(gmax.shape, MOE_GROUPS - 1, jnp.int32)
    for g in range(MOE_GROUPS - 2, -1, -1):
        g_sel = jnp.where(lg[g] == gmax, g, g_sel)
    sel = lt[(MOE_GROUPS - 1) * npg:MOE_GROUPS * npg, :]
    for g in range(MOE_GROUPS - 2, -1, -1):
        sel = jnp.where(g_sel == g, lt[g * npg:(g + 1) * npg, :], sel)
    row = lax.broadcasted_iota(jnp.int32, sel.shape, 0)
    m1 = jnp.max(sel, axis=0, keepdims=True)
    i1 = jnp.min(jnp.where(sel == m1, row, npg), axis=0, keepdims=True)
    rest = jnp.where(row == i1, -jnp.inf, sel)
    m2 = jnp.max(rest, axis=0, keepdims=True)
    i2 = jnp.min(jnp.where(rest == m2, row, npg), axis=0, keepdims=True)
    e2 = jnp.exp(m2 - m1)
    inv = pg_top / (1.0 + e2)
    eid = jnp.concatenate([g_sel * npg + i1, g_sel * npg + i2], axis=0)
    gates = jnp.concatenate([inv, inv * e2], axis=0)

    erow = lax.broadcasted_iota(jnp.int32, (ne, tm), 0)
    hits = [erow == eid[k:k + 1, :] for k in range(2)]
    prefix = jnp.dot(jnp.concatenate([jnp.where(hit, 1.0, 0.0).astype(BF16) for hit in hits], axis=0),
                     before_ref[...], preferred_element_type=F32)
    base = base_ref[...]
    ranks = []
    for k, hit in enumerate(hits):
        ranks.append(jnp.sum(jnp.where(hit, prefix[k * ne:(k + 1) * ne] + base, 0.0), axis=0, keepdims=True))
        base = base + jnp.sum(jnp.where(hit, 1.0, 0.0), axis=1, keepdims=True)
    base_ref[...] = base
    cnt_ref[...] = jnp.broadcast_to(base, cnt_ref.shape).astype(jnp.int32)
    return eid, gates, jnp.concatenate(ranks, axis=0).astype(jnp.int32)


ROUTER_ROWS = MOE_EXPERTS + 2 * SUBLANES


def _router_operands(gain, router_w, router_b, tm):
    d = router_w.shape[0]
    pad = ROUTER_ROWS - MOE_EXPERTS - MOE_GROUPS
    rw = jnp.concatenate([router_w[:, MOE_GROUPS:], router_w[:, :MOE_GROUPS], jnp.zeros((d, pad), F32)], axis=1).T
    rw_hi = rw.astype(BF16)
    rw2 = jnp.concatenate([rw_hi, (rw - rw_hi.astype(F32)).astype(BF16)], axis=0)
    rb = jnp.concatenate([router_b[MOE_GROUPS:], router_b[:MOE_GROUPS], jnp.zeros((pad,), F32)])
    pos = jnp.arange(tm)
    before = (pos[:, None] < pos[None, :]).astype(BF16)
    const = lambda *_: (0, 0)
    specs = [pl.BlockSpec((1, d), const), pl.BlockSpec((2 * ROUTER_ROWS, d), const),
             pl.BlockSpec((ROUTER_ROWS, 1), const), pl.BlockSpec((tm, tm), const)]
    return [gain.reshape(1, d), rw2, rb.reshape(ROUTER_ROWS, 1), before], specs


def _router_results(t, slot_spec, slot_shape):
    specs = [slot_spec, slot_spec, slot_spec, pl.BlockSpec((MOE_EXPERTS, LANES), lambda *_: (0, 0))]
    shapes = [jax.ShapeDtypeStruct(slot_shape, jnp.int32), jax.ShapeDtypeStruct(slot_shape, F32),
              jax.ShapeDtypeStruct(slot_shape, jnp.int32), jax.ShapeDtypeStruct((MOE_EXPERTS, LANES), jnp.int32)]
    return specs, shapes, pltpu.VMEM((MOE_EXPERTS, 1), F32)


def _wait_rows(hbm_ref, n_rows, sem, times):
    rows = hbm_ref.at[pl.ds(0, n_rows)]
    for _ in range(times):
        pltpu.make_async_copy(rows, rows, sem).wait()


def _tiled_shape(n, d):
    return (n // SUBLANES, d // LANES, SUBLANES, LANES)


def _store_tiled(ref, a):
    n, d = a.shape
    for j in range(d // LANES):
        ref[:, j, :, :] = a[:, j * LANES:(j + 1) * LANES].reshape(n // SUBLANES, SUBLANES, LANES)


def _load_tiled(ref):
    g, nj, _, _ = ref.shape
    return jnp.concatenate([ref[:, j, :, :].reshape(g * SUBLANES, LANES) for j in range(nj)], axis=1)


def _dispatch_kernel(d0_ref, d1_ref, x_ref, mod_ref, gain_ref, xs_ref, h_ref, sem):
    i = pl.program_id(0)
    last = pl.num_programs(0) - 1
    tm, d = x_ref.shape
    mod = mod_ref[0]
    h = _modulated_norm(x_ref[...], gain_ref[...], mod[3:4], mod[4:5])

    def step(buf):
        _store_tiled(h_ref.at[buf], h)

        def issue(g, carry):
            for u in range(SUBLANES):
                for k, dest_ref in enumerate((d0_ref, d1_ref)):
                    pltpu.make_async_copy(h_ref.at[buf, g, :, u, :], xs_ref.at[dest_ref[g * SUBLANES + u]],
                                          sem.at[buf]).start(priority=k)
            return carry

        lax.fori_loop(0, tm // SUBLANES, issue, 0)

        @pl.when(i > 0)
        def _():
            _wait_rows(xs_ref, tm, sem.at[1 - buf], 2)

        @pl.when(i == last)
        def _():
            _wait_rows(xs_ref, tm, sem.at[buf], 2)

    for buf in range(2):
        pl.when(i % 2 == buf)(functools.partial(step, buf))


def moe_dispatch(x2, mod, gain, dest, tiles_per_batch_of):
    t, d = x2.shape
    tm = MOVE_TILE
    tpb = tiles_per_batch_of(tm)
    const2 = lambda i: (0, 0)
    slot_spec = pl.BlockSpec((tm,), lambda i: (i,), memory_space=pltpu.SMEM)
    return pl.pallas_call(
        _dispatch_kernel,
        grid=(t // tm,),
        in_specs=[slot_spec, slot_spec,
                  pl.BlockSpec((tm, d), lambda i: (i, 0)),
                  pl.BlockSpec((1, 6, d), lambda i: (i // tpb, 0, 0)),
                  pl.BlockSpec((1, d), const2)],
        out_specs=pl.BlockSpec(memory_space=pl.ANY),
        out_shape=jax.ShapeDtypeStruct((2 * t, d // LANES, LANES), F32),
        scratch_shapes=[pltpu.VMEM((2,) + _tiled_shape(tm, d), F32), pltpu.SemaphoreType.DMA((2,))],
        compiler_params=_params("arbitrary"),
        name="moe_dispatch",
    )(dest[0], dest[1], x2, mod, gain.reshape(1, d))


def _expert_kernel(blk_ref, exp_ref, start_ref, nvalid_ref, xs_ref, win_ref, wout_ref, ys_ref,
                   winb_ref, woutb_ref, xbuf_ref, ybuf_ref, in_sem, out_sem):
    i = pl.program_id(0)
    n_items = pl.num_programs(0)
    bm = EXPERT_ROWS
    groups = bm // SUBLANES
    n_blocks = xs_ref.shape[0] // groups
    e = exp_ref[i]
    blk = blk_ref[i]
    prev = jnp.maximum(i - 1, 0)
    nxt = jnp.minimum(i + 1, n_items - 1)
    first = i == 0
    final = i == n_items - 1
    lo = start_ref[e]
    hi = start_ref[e + 1]
    row0 = blk * bm
    live = i < nvalid_ref[0]
    slot = blk % 2

    def block_copies(hbm_ref, buf_ref, b, s, sem, to_hbm):
        copies = []
        for u in range(SUBLANES):
            hbm = hbm_ref.at[pl.ds(b * groups, groups), u]
            vmem = buf_ref.at[s, :, :, u, :]
            copies.append(pltpu.make_async_copy(vmem, hbm, sem.at[s]) if to_hbm
                          else pltpu.make_async_copy(hbm, vmem, sem.at[s]))
        return copies

    def fetch(b, s):
        for c in block_copies(xs_ref, xbuf_ref, b, s, in_sem, False):
            c.start()

    @pl.when(jnp.logical_or(first, exp_ref[prev] != e))
    def _():
        winb_ref[...] = win_ref[0].astype(BF16)
        woutb_ref[...] = wout_ref[0].astype(BF16)

    new_block = jnp.logical_or(first, blk_ref[prev] != blk)
    last_of_block = jnp.logical_or(final, blk_ref[nxt] != blk)
    whole = jnp.logical_and(lo <= row0, hi >= row0 + bm)

    @pl.when(first)
    def _():
        fetch(0, 0)

    @pl.when(new_block)
    def _():
        for c in block_copies(xs_ref, xbuf_ref, blk, slot, in_sem, False):
            c.wait()

        @pl.when(blk + 1 < n_blocks)
        def _():
            fetch(blk + 1, 1 - slot)

        @pl.when(blk >= 2)
        def _():
            for c in block_copies(ys_ref, ybuf_ref, blk - 2, slot, out_sem, True):
                c.wait()

    def ffn(x):
        a_gl = jnp.dot(x.astype(BF16), winb_ref[...], preferred_element_type=F32)
        mid = (_silu(a_gl[:, :MOE_FF]) * a_gl[:, MOE_FF:]).astype(BF16)
        return jnp.dot(mid, woutb_ref[...], preferred_element_type=F32)

    @pl.when(jnp.logical_and(live, whole))
    def _():
        _store_tiled(ybuf_ref.at[slot], ffn(_load_tiled(xbuf_ref.at[slot])))

    @pl.when(jnp.logical_and(new_block, jnp.logical_not(whole)))
    def _():
        ybuf_ref[slot] = jnp.zeros(ybuf_ref.shape[1:], F32)

    sub_groups = EXPERT_SUB_ROWS // SUBLANES
    for sub in range(bm // EXPERT_SUB_ROWS):
        sub0 = row0 + sub * EXPERT_SUB_ROWS
        touched = jnp.logical_and(hi > sub0, lo < sub0 + EXPERT_SUB_ROWS)

        @pl.when(jnp.logical_and(jnp.logical_and(live, jnp.logical_not(whole)), touched))
        def _():
            gs = pl.ds(sub * sub_groups, sub_groups)
            rows = sub0 + lax.broadcasted_iota(jnp.int32, (EXPERT_SUB_ROWS, 1), 0)
            mine = jnp.logical_and(rows >= lo, rows < hi)
            y = jnp.where(mine, ffn(_load_tiled(xbuf_ref.at[slot, gs])), 0.0)
            _store_tiled(ybuf_ref.at[slot, gs], _load_tiled(ybuf_ref.at[slot, gs]) + y)

    @pl.when(last_of_block)
    def _():
        for c in block_copies(ys_ref, ybuf_ref, blk, slot, out_sem, True):
            c.start()

    @pl.when(final)
    def _():
        for c in block_copies(ys_ref, ybuf_ref, blk, slot, out_sem, True):
            c.wait()

        @pl.when(blk >= 1)
        def _():
            for c in block_copies(ys_ref, ybuf_ref, blk - 1, 1 - slot, out_sem, True):
                c.wait()


def moe_experts(xs, w_in, w_out, item_block, item_expert, seg_start, n_valid):
    n, nj, _ = xs.shape
    d = nj * LANES
    bm = EXPERT_ROWS
    n_items = item_block.shape[0]
    by_group = (n // SUBLANES, SUBLANES, nj, LANES)
    grid_spec = pltpu.PrefetchScalarGridSpec(
        num_scalar_prefetch=4,
        grid=(n_items,),
        in_specs=[pl.BlockSpec(memory_space=pl.ANY),
                  pl.BlockSpec((1, d, 2 * MOE_FF), lambda i, b, e, s, v: (e[i], 0, 0)),
                  pl.BlockSpec((1, MOE_FF, d), lambda i, b, e, s, v: (e[i], 0, 0))],
        out_specs=pl.BlockSpec(memory_space=pl.ANY),
        scratch_shapes=[pltpu.VMEM((d, 2 * MOE_FF), BF16), pltpu.VMEM((MOE_FF, d), BF16),
                        pltpu.VMEM((2,) + _tiled_shape(bm, d), F32), pltpu.VMEM((2,) + _tiled_shape(bm, d), F32),
                        pltpu.SemaphoreType.DMA((2,)), pltpu.SemaphoreType.DMA((2,))],
    )
    ys = pl.pallas_call(
        _expert_kernel,
        grid_spec=grid_spec,
        out_shape=jax.ShapeDtypeStruct(by_group, F32),
        compiler_params=_params("arbitrary"),
        name="moe_experts",
    )(item_block, item_expert, seg_start, n_valid, xs.reshape(by_group), w_in, w_out)
    return ys.reshape(n, nj, LANES)


def _moe_residual(step, last, dests, next_dests, x, gates, gate_f, ys_ref, y_ref, sem, out_ref):
    tm = x.shape[0]

    def gather(dest_refs, buf):
        def issue(g, carry):
            for u in range(SUBLANES):
                for k, dest_ref in enumerate(dest_refs):
                    pltpu.make_async_copy(ys_ref.at[dest_ref[g * SUBLANES + u]],
                                          y_ref.at[buf, k, g, :, u, :], sem.at[buf]).start(priority=k)
            return carry

        lax.fori_loop(0, tm // SUBLANES, issue, 0)

    def run(buf):
        @pl.when(step == 0)
        def _():
            gather(dests, buf)

        @pl.when(step < last)
        def _():
            gather(next_dests, 1 - buf)

        _wait_rows(ys_ref, tm, sem.at[buf], 2)
        moe = gates[:, 0:1] * _load_tiled(y_ref.at[buf, 0]) + gates[:, 1:2] * _load_tiled(y_ref.at[buf, 1])
        out_ref[...] = x + gate_f * moe

    for buf in range(2):
        pl.when(step % 2 == buf)(functools.partial(run, buf))


def _combine_kernel(d0_ref, d1_ref, n0_ref, n1_ref, x_ref, mod_ref, gate_ref, ys_ref, o_ref, y_ref, sem):
    _moe_residual(pl.program_id(0), pl.num_programs(0) - 1, (d0_ref, d1_ref), (n0_ref, n1_ref), x_ref[...],
                  gate_ref[...], mod_ref[0][5:6], ys_ref, y_ref, sem, o_ref)


N_PENDING_INPUTS = 7


def _with_pending_moe(inner, x_pos, n_grid, n_inner_scratch, emit_x):
    def kern(*refs):
        d0_ref, d1_ref, n0_ref, n1_ref, pgate_ref, pmod_ref, ys_ref = refs[:N_PENDING_INPUTS]
        y_ref, sem, xc_ref = refs[-3:]
        inner_refs = list(refs[N_PENDING_INPUTS:-3])
        xout_ref = inner_refs.pop(len(inner_refs) - n_inner_scratch - 1) if emit_x else None
        step, total = pl.program_id(0), pl.num_programs(0)
        for axis in range(1, n_grid):
            step = step * pl.num_programs(axis) + pl.program_id(axis)
            total = total * pl.num_programs(axis)
        _moe_residual(step, total - 1, (d0_ref, d1_ref), (n0_ref, n1_ref), inner_refs[x_pos][...],
                      pgate_ref[...], pmod_ref[0][5:6], ys_ref, y_ref, sem, xc_ref)
        if emit_x:
            xout_ref[...] = xc_ref[...]
        inner_refs[x_pos] = xc_ref
        inner(*inner_refs)
    return kern


def _pending_operands(pending, tm, d, flat_of, batch_of, n_tiles):
    dest, gates_t, ys, pmod = pending
    cur = pl.BlockSpec((tm,), lambda *g: (flat_of(*g),), memory_space=pltpu.SMEM)
    nxt = pl.BlockSpec((tm,), lambda *g: (jnp.minimum(flat_of(*g) + 1, n_tiles - 1),), memory_space=pltpu.SMEM)
    specs = [cur, cur, nxt, nxt,
             pl.BlockSpec((tm, 2), lambda *g: (flat_of(*g), 0)),
             pl.BlockSpec((1, 6, d), lambda *g: (batch_of(*g), 0, 0)),
             pl.BlockSpec(memory_space=pl.ANY)]
    args = [dest[0], dest[1], dest[0], dest[1], gates_t, pmod, ys]
    scratch = [pltpu.VMEM((2, 2) + _tiled_shape(tm, d), F32), pltpu.SemaphoreType.DMA((2,)), pltpu.VMEM((tm, d), F32)]
    return args, specs, scratch


def moe_combine(x2, mod, gates_t, dest, ys, tiles_per_batch_of):
    t, d = x2.shape
    tm = MOVE_TILE
    tpb = tiles_per_batch_of(tm)
    n_tiles = t // tm
    slot_spec = pl.BlockSpec((tm,), lambda i: (i,), memory_space=pltpu.SMEM)
    next_spec = pl.BlockSpec((tm,), lambda i: (jnp.minimum(i + 1, n_tiles - 1),), memory_space=pltpu.SMEM)
    return pl.pallas_call(
        _combine_kernel,
        grid=(n_tiles,),
        in_specs=[slot_spec, slot_spec, next_spec, next_spec,
                  pl.BlockSpec((tm, d), lambda i: (i, 0)),
                  pl.BlockSpec((1, 6, d), lambda i: (i // tpb, 0, 0)),
                  pl.BlockSpec((tm, 2), lambda i: (i, 0)),
                  pl.BlockSpec(memory_space=pl.ANY)],
        out_specs=pl.BlockSpec((tm, d), lambda i: (i, 0)),
        out_shape=jax.ShapeDtypeStruct((t, d), F32),
        scratch_shapes=[pltpu.VMEM((2, 2) + _tiled_shape(tm, d), F32), pltpu.SemaphoreType.DMA((2,))],
        compiler_params=_params("arbitrary"),
        name="moe_combine",
    )(dest[0], dest[1], dest[0], dest[1], x2, mod, gates_t, ys)


def moe_layer(x2, mod, gain, routing, w_in, w_out, tiles_per_batch_of):
    t, d = x2.shape
    bm = EXPERT_ROWS
    n_assign = 2 * t
    eid, gate, rank, cnt = routing
    ne = MOE_EXPERTS
    experts = jnp.arange(ne, dtype=jnp.int32)
    upto = experts[None, :] <= experts[:, None]

    def running_total(v):
        return jnp.sum(jnp.where(upto, v[None, :], 0), axis=1).astype(jnp.int32)

    def lookup(table, idx):
        hit = idx[None] == experts.reshape((ne,) + (1,) * idx.ndim)
        return jnp.sum(jnp.where(hit, table.reshape((ne,) + (1,) * idx.ndim), 0), axis=0).astype(jnp.int32)

    counts = cnt[:, 0]
    seg_end = running_total(counts)
    seg_start = jnp.concatenate([jnp.zeros((1,), jnp.int32), seg_end])
    dest = lookup(seg_start[:ne], eid) + rank
    n_blocks = n_assign // bm
    n_items = n_blocks + ne - 1
    first_blk = seg_start[:ne] // bm
    last_blk = jnp.where(counts > 0, (seg_end - 1) // bm, first_blk - 1)
    per_expert = jnp.maximum(last_blk - first_blk + 1, 0)
    item_end = running_total(per_expert)
    n_valid = item_end[ne - 1]
    item_ids = jnp.arange(n_items, dtype=jnp.int32)
    item_ids_c = jnp.maximum(jnp.minimum(item_ids, n_valid - 1), 0)
    item_expert = jnp.minimum(jnp.sum(item_ids_c[:, None] >= item_end[None, :], axis=1), ne - 1).astype(jnp.int32)
    item_block = lookup(first_blk - (item_end - per_expert), item_expert) + item_ids_c
    xs = moe_dispatch(x2, mod, gain, dest, tiles_per_batch_of)
    ys = moe_experts(xs, w_in, w_out, item_block, item_expert, seg_start,
                     n_valid.reshape(1).astype(jnp.int32))
    return dest, gate.T, ys, mod


SWA_HEAD_DIM = 64
SWA_Q_HEADS = D_MODEL // SWA_HEAD_DIM
SWA_KV_HEADS = 4
SWA_BLOCK = 128
SWA_Q_WIDTH = SWA_Q_HEADS * SWA_HEAD_DIM
SWA_KV_WIDTH = SWA_KV_HEADS * SWA_HEAD_DIM
HEADS_PER_VREG = LANES // SWA_HEAD_DIM


def _swa_kernel(sink_ref, x_ref, mod_ref, gain_ref, win_ref, qg_ref, kg_ref, ones_ref, wout_ref,
                fgain_ref, rw_ref, rb_ref, before_ref, o_ref, eid_ref, gate_ref, rank_ref, cnt_ref,
                kx_ref, vx_ref, att_ref, base_ref):
    t = pl.program_id(1)
    tm = x_ref.shape[0]
    blk = SWA_BLOCK
    hd = SWA_HEAD_DIM

    @pl.when(t == 0)
    def _():
        kx_ref[0:blk, :] = jnp.zeros((blk, kx_ref.shape[1]), BF16)
        vx_ref[0:blk, :] = jnp.zeros((blk, vx_ref.shape[1]), BF16)

    x = x_ref[...]
    mod = mod_ref[0]
    h = _modulated_norm(x, gain_ref[...], mod[0:1], mod[1:2]).astype(BF16)
    proj = jnp.dot(h, win_ref[...], preferred_element_type=F32)

    def head_rms(a):
        ss = jnp.dot((a * a).astype(BF16), ones_ref[...], preferred_element_type=F32)
        return a * lax.rsqrt(ss * (1.0 / hd) + NORM_EPS)

    lane = lax.broadcasted_iota(jnp.int32, (tm, LANES), 1)
    low = lane < hd
    for c in range(SWA_Q_WIDTH // LANES):
        cols = slice(c * LANES, (c + 1) * LANES)
        att_ref[:, cols] = (head_rms(proj[:, cols]) * qg_ref[:, cols]).astype(BF16)
    for c in range(SWA_KV_WIDTH // LANES):
        cols = slice(c * LANES, (c + 1) * LANES)
        kc = head_rms(proj[:, SWA_Q_WIDTH + c * LANES:SWA_Q_WIDTH + (c + 1) * LANES]) * kg_ref[:, cols]
        vc = proj[:, SWA_Q_WIDTH + SWA_KV_WIDTH + c * LANES:SWA_Q_WIDTH + SWA_KV_WIDTH + (c + 1) * LANES]
        for ref, a in ((kx_ref, kc), (vx_ref, vc)):
            even_lo = jnp.where(low, a, 0.0)
            odd_hi = jnp.where(low, 0.0, a)
            j0 = HEADS_PER_VREG * c
            ref[blk:blk + tm, (2 * j0) * LANES:(2 * j0 + 1) * LANES] = even_lo.astype(BF16)
            ref[blk:blk + tm, (2 * j0 + 1) * LANES:(2 * j0 + 2) * LANES] = pltpu.roll(even_lo, hd, 1).astype(BF16)
            ref[blk:blk + tm, (2 * j0 + 2) * LANES:(2 * j0 + 3) * LANES] = pltpu.roll(odd_hi, hd, 1).astype(BF16)
            ref[blk:blk + tm, (2 * j0 + 3) * LANES:(2 * j0 + 4) * LANES] = odd_hi.astype(BF16)

    qi = lax.broadcasted_iota(jnp.int32, (blk, 2 * blk), 0)
    kj = lax.broadcasted_iota(jnp.int32, (blk, 2 * blk), 1)
    lane_q = lax.broadcasted_iota(jnp.int32, (blk, LANES), 1) < hd
    pairs_per_kv = SWA_Q_HEADS // SWA_KV_HEADS // HEADS_PER_VREG
    for i in range(tm // blk):
        rows = slice(i * blk, (i + 1) * blk)
        keys = slice(i * blk, (i + 2) * blk)
        floor = jnp.where(t == 0, blk - 1, qi) if i == 0 else qi
        mask = jnp.logical_and(kj > floor, kj <= qi + blk)
        heads = [(p, half) for p in range(SWA_Q_HEADS // HEADS_PER_VREG) for half in range(HEADS_PER_VREG)]
        col_of = lambda p, half: (2 * (p // pairs_per_kv) + half) * LANES
        scores = [lax.dot_general(att_ref[rows, p * LANES:(p + 1) * LANES],
                                  kx_ref[keys, col_of(p, half):col_of(p, half) + LANES],
                                  (((1,), (1,)), ((), ())), preferred_element_type=F32) for p, half in heads]
        probs, inv = [], []
        for (p, half), sc in zip(heads, scores):
            sc = jnp.where(mask, sc, -jnp.inf)
            sink = sink_ref[HEADS_PER_VREG * p + half]
            m = jnp.maximum(jnp.max(sc, axis=-1, keepdims=True), sink)
            pr = jnp.exp(sc - m)
            inv.append(1.0 / (jnp.sum(pr, axis=-1, keepdims=True) + jnp.exp(sink - m)))
            probs.append(pr.astype(BF16))
        pvs = [jnp.dot(pr, vx_ref[keys, col_of(p, half):col_of(p, half) + LANES], preferred_element_type=F32)
               for (p, half), pr in zip(heads, probs)]
        for p in range(SWA_Q_HEADS // HEADS_PER_VREG):
            out = (pvs[2 * p] + pvs[2 * p + 1]) * jnp.where(lane_q, inv[2 * p], inv[2 * p + 1])
            att_ref[rows, p * LANES:(p + 1) * LANES] = out.astype(BF16)
    kx_ref[0:blk, :] = kx_ref[tm:tm + blk, :]
    vx_ref[0:blk, :] = vx_ref[tm:tm + blk, :]
    y = jnp.dot(att_ref[...], wout_ref[...], preferred_element_type=F32)
    x_new = x + mod[2:3] * y
    o_ref[...] = x_new
    first = jnp.logical_and(pl.program_id(0) == 0, t == 0)
    eid_ref[...], gate_ref[...], rank_ref[...] = _route(
        x_new, mod[3:4], mod[4:5], first, fgain_ref, rw_ref, rb_ref, before_ref, cnt_ref, base_ref)


def swa_layer(x2, mod, gain, w_in, q_norm, k_norm, sinks, w_out, router, b, s, pending=None):
    t, d = x2.shape
    tm = TOKEN_TILE
    tpb = s // tm
    hd = SWA_HEAD_DIM
    qg = (jnp.tile(q_norm, SWA_Q_HEADS) * hd ** -0.5).reshape(1, SWA_Q_WIDTH)
    kg = jnp.tile(k_norm, SWA_KV_HEADS).reshape(1, SWA_KV_WIDTH)
    ids = jnp.arange(LANES) // hd
    ones = (ids[:, None] == ids[None, :]).astype(BF16)
    proj_w = SWA_Q_WIDTH + 2 * SWA_KV_WIDTH
    ext_w = 2 * SWA_KV_HEADS * LANES
    const2 = lambda bi, ti: (0, 0)
    tile = lambda bi, ti: (bi * tpb + ti, 0)
    r_args, r_in = _router_operands(*router, tm)
    r_out, r_shapes, r_scratch = _router_results(t, pl.BlockSpec((2, tm), lambda bi, ti: (0, bi * tpb + ti)), (2, t))
    body = _swa_kernel
    args = [sinks, x2, mod, gain.reshape(1, d), w_in.astype(BF16), qg, kg, ones, w_out.astype(BF16), *r_args]
    in_specs = [pl.BlockSpec(memory_space=pltpu.SMEM),
                pl.BlockSpec((tm, d), tile),
                pl.BlockSpec((1, 6, d), lambda bi, ti: (bi, 0, 0)),
                pl.BlockSpec((1, d), const2),
                pl.BlockSpec((d, proj_w), const2),
                pl.BlockSpec((1, SWA_Q_WIDTH), const2),
                pl.BlockSpec((1, SWA_KV_WIDTH), const2),
                pl.BlockSpec((LANES, LANES), const2),
                pl.BlockSpec((SWA_Q_WIDTH, d), const2)] + r_in
    scratch = [pltpu.VMEM((tm + SWA_BLOCK, ext_w), BF16), pltpu.VMEM((tm + SWA_BLOCK, ext_w), BF16),
               pltpu.VMEM((tm, SWA_Q_WIDTH), BF16), r_scratch]
    if pending is not None:
        body = _with_pending_moe(body, 1, 2, len(scratch), False)
        p_args, p_specs, p_scratch = _pending_operands(pending, tm, d, lambda bi, ti: bi * tpb + ti,
                                                       lambda bi, ti: bi, t // tm)
        args, in_specs, scratch = p_args + args, p_specs + in_specs, scratch + p_scratch
    out = pl.pallas_call(
        body,
        grid=(b, tpb),
        in_specs=in_specs,
        out_specs=[pl.BlockSpec((tm, d), tile)] + r_out,
        out_shape=[jax.ShapeDtypeStruct((t, d), F32)] + r_shapes,
        scratch_shapes=scratch,
        compiler_params=_params("arbitrary", "arbitrary"),
        name="swa_mixer",
    )(*args)
    return out[0], tuple(out[1:])


DN_QK_HEADS = 4
DN_V_HEADS = 8
DN_HEAD_DIM = D_MODEL // DN_V_HEADS
DN_CONV = 4
DN_CHUNK = 64
DN_KEY_WIDTH = DN_QK_HEADS * DN_HEAD_DIM
DN_VAL_WIDTH = DN_V_HEADS * DN_HEAD_DIM
DN_CONV_WIDTH = 2 * DN_KEY_WIDTH + DN_VAL_WIDTH
DN_PREP_TILE = 128
DN_SCAN_TILE = 256


def _split_bf16(a):
    hi = a.astype(BF16)
    return hi, (a - hi.astype(F32)).astype(BF16)


def _mm(a, b):
    return jnp.dot(a.astype(BF16), b.astype(BF16), preferred_element_type=F32)


def _dn_proj_kernel(x_ref, mod_ref, gain_ref, w_ref, wba_ref, conv_ref, rate_ref, dtb_ref,
                    q_ref, k_ref, v_ref, z_ref, bg_ref, ext_ref):
    t = pl.program_id(1)
    tm = x_ref.shape[0]
    pad = SUBLANES

    @pl.when(t == 0)
    def _():
        ext_ref[0:pad, :] = jnp.zeros((pad, ext_ref.shape[1]), F32)

    mod = mod_ref[0]
    h = _modulated_norm(x_ref[...], gain_ref[...], mod[0:1], mod[1:2])
    hh, hl = _split_bf16(h)
    proj = jnp.dot(hh, w_ref[...], preferred_element_type=F32)
    z_ref[...] = proj[:, DN_CONV_WIDTH:].astype(BF16)

    ext_ref[pad:pad + tm, :] = proj[:, :DN_CONV_WIDTH]
    acc = None
    for j in range(DN_CONV):
        start = pad - (DN_CONV - 1) + j
        term = conv_ref[j:j + 1, :] * ext_ref[start:start + tm, :]
        acc = term if acc is None else acc + term
    ext_ref[0:pad, :] = ext_ref[tm:tm + pad, :]
    qkv = _silu(acc)
    for hq in range(2 * DN_QK_HEADS):
        cols = slice(hq * DN_HEAD_DIM, (hq + 1) * DN_HEAD_DIM)
        a = qkv[:, cols]
        a = a * lax.rsqrt(jnp.sum(a * a, axis=-1, keepdims=True) + NORM_EPS)
        if hq < DN_QK_HEADS:
            q_ref[:, cols] = (a * DN_HEAD_DIM ** -0.5).astype(BF16)
        else:
            k_ref[:, hq * DN_HEAD_DIM - DN_KEY_WIDTH:(hq + 1) * DN_HEAD_DIM - DN_KEY_WIDTH] = a.astype(BF16)
    v_ref[...] = qkv[:, 2 * DN_KEY_WIDTH:].astype(BF16)

    ba = jnp.dot(hh, wba_ref[...], preferred_element_type=F32) + jnp.dot(hl, wba_ref[...], preferred_element_type=F32)
    ba = ba + pltpu.roll(ba, LANES - 2 * DN_V_HEADS, 1)
    lane = lax.broadcasted_iota(jnp.int32, ba.shape, 1)
    beta = 1.0 / (1.0 + jnp.exp(-ba))
    sp = ba + dtb_ref[...]
    g = -rate_ref[...] * (jnp.maximum(sp, 0.0) + jnp.log1p(jnp.exp(-jnp.abs(sp))))
    g = jnp.where(jnp.logical_and(lane >= DN_V_HEADS, lane < 2 * DN_V_HEADS), g, 0.0)
    r = lax.broadcasted_iota(jnp.int32, (tm, tm), 0)
    c = lax.broadcasted_iota(jnp.int32, (tm, tm), 1)
    tri = jnp.logical_and(r // DN_CHUNK == c // DN_CHUNK, c <= r).astype(F32)
    gc = jnp.dot(tri, g, precision=HIGHEST, preferred_element_type=F32)
    bg_ref[...] = jnp.where(lane < DN_V_HEADS, beta, gc)


def _unit_lower_inverses(mats):
    n = mats[0].shape[0]
    r = lax.broadcasted_iota(jnp.int32, (n, n), 0)
    c = lax.broadcasted_iota(jnp.int32, (n, n), 1)
    eye = (r == c).astype(F32)
    size = SUBLANES
    same = (r // size) == (c // size)
    d = [jnp.where(same, a, 0.0) for a in mats]
    d2 = [_mm(v, v) for v in d]
    d4 = [_mm(v, v) for v in d2]
    x = [_mm(eye - v, eye + v2) for v, v2 in zip(d, d2)]
    x = [_mm(v, eye + v4) for v, v4 in zip(x, d4)]
    while size < n:
        wider = (r // (2 * size)) == (c // (2 * size))
        ring = jnp.logical_and(wider, jnp.logical_not(same))
        xl = [_mm(v, jnp.where(ring, a, 0.0)) for v, a in zip(x, mats)]
        x = [v - _mm(vl, v) for v, vl in zip(x, xl)]
        same = wider
        size *= 2
    return x


def _dn_prep_kernel(q_ref, k_ref, v_ref, bg_ref, gct_ref, u_ref, w_ref, qk_ref):
    tm = q_ref.shape[0]
    ck = DN_CHUNK
    hd = DN_HEAD_DIM
    rep = DN_V_HEADS // DN_QK_HEADS
    r = lax.broadcasted_iota(jnp.int32, (ck, ck), 0)
    c = lax.broadcasted_iota(jnp.int32, (ck, ck), 1)
    lower = c <= r
    strict = c < r
    dot = functools.partial(jnp.dot, preferred_element_type=F32)
    chunks = [slice(ci * ck, (ci + 1) * ck) for ci in range(tm // ck)]
    gram = {}
    for rows in chunks:
        for hq in range(DN_QK_HEADS):
            cols = slice(hq * hd, (hq + 1) * hd)
            k = k_ref[rows, cols]
            gram[(hq, rows.start)] = lax.dot_general(jnp.concatenate([k, q_ref[rows, cols]], axis=0), k,
                                                     (((1,), (1,)), ((), ())), preferred_element_type=F32)
    problems = [(hv, rows) for rows in chunks for hv in range(DN_V_HEADS)]
    mats, rhs = [], []
    for hv, rows in problems:
        beta = bg_ref[rows, hv:hv + 1]
        gcc = bg_ref[rows, DN_V_HEADS + hv:DN_V_HEADS + hv + 1]
        gcr = gct_ref[hv:hv + 1, rows]
        decay = jnp.where(lower, jnp.exp(jnp.where(lower, gcc - gcr, 0.0)), 0.0)
        g = gram[(hv // rep, rows.start)]
        mats.append(jnp.where(strict, g[:ck] * beta * decay, 0.0))
        qk_ref[rows, hv * ck:(hv + 1) * ck] = (g[ck:] * decay).astype(BF16)
        kf = k_ref[rows, (hv // rep) * hd:(hv // rep + 1) * hd].astype(F32)
        vf = v_ref[rows, hv * hd:(hv + 1) * hd].astype(F32)
        rhs.append(jnp.concatenate([vf * beta, kf * (beta * jnp.exp(gcc))], axis=1).astype(BF16))
    inverses = _unit_lower_inverses(mats)
    for (hv, rows), tinv, b in zip(problems, inverses, rhs):
        uw = dot(tinv.astype(BF16), b)
        u_ref[rows, hv * hd:(hv + 1) * hd] = uw[:, :hd].astype(BF16)
        w_ref[rows, hv * hd:(hv + 1) * hd] = uw[:, hd:].astype(BF16)


def _dn_scan_kernel(x_ref, mod_ref, u_ref, w_ref, qk_ref, q_ref, k_ref, z_ref, bg_ref, og_ref, wout_ref,
                    fgain_ref, rw_ref, rb_ref, before_ref, o_ref, eid_ref, gate_ref, rank_ref, cnt_ref,
                    state_ref, att_ref, base_ref):
    t = pl.program_id(0)
    nb, tm, _ = x_ref.shape
    ck = DN_CHUNK
    hd = DN_HEAD_DIM
    rep = DN_V_HEADS // DN_QK_HEADS

    @pl.when(t == 0)
    def _():
        state_ref[...] = jnp.zeros_like(state_ref)

    dot = functools.partial(jnp.dot, preferred_element_type=F32)
    chains = [(bi, hv) for bi in range(nb) for hv in range(DN_V_HEADS)]
    for ci in range(tm // ck):
        rows = slice(ci * ck, (ci + 1) * ck)
        gcc, g_end, lhs = [], [], []
        for bi, hv in chains:
            gc = bg_ref[bi, rows, DN_V_HEADS + hv:DN_V_HEADS + hv + 1]
            gcc.append(gc)
            g_end.append(gc[ck - 1:ck, :])
            q_dec = q_ref[bi, rows, (hv // rep) * hd:(hv // rep + 1) * hd].astype(F32) * jnp.exp(gc)
            lhs.append(jnp.concatenate([w_ref[bi, rows, hv * hd:(hv + 1) * hd], q_dec.astype(BF16)], axis=0))
        ws_qs = [dot(a, state_ref[n].astype(BF16)) for n, a in enumerate(lhs)]
        v_new = [(u_ref[bi, rows, hv * hd:(hv + 1) * hd].astype(F32) - m[:ck]).astype(BF16)
                 for (bi, hv), m in zip(chains, ws_qs)]
        outs = [m[ck:] + dot(qk_ref[bi, rows, hv * ck:(hv + 1) * ck], vn)
                for (bi, hv), m, vn in zip(chains, ws_qs, v_new)]
        for n, (bi, hv) in enumerate(chains):
            kf = k_ref[bi, rows, (hv // rep) * hd:(hv // rep + 1) * hd].astype(F32)
            k_dec = (kf * jnp.exp(g_end[n] - gcc[n])).astype(BF16)
            state_ref[n] = state_ref[n] * jnp.exp(g_end[n]) + lax.dot_general(
                k_dec, v_new[n], (((0,), (0,)), ((), ())), preferred_element_type=F32)
        for (bi, hv), o in zip(chains, outs):
            zf = z_ref[bi, rows, hv * hd:(hv + 1) * hd].astype(F32)
            att_ref[bi * tm + ci * ck:bi * tm + (ci + 1) * ck, hv * hd:(hv + 1) * hd] = (
                _rms(o) * og_ref[...] * _silu(zf)).astype(BF16)
    y = dot(att_ref[...], wout_ref[...])
    for bi in range(nb):
        mod = mod_ref[bi]
        x_new = x_ref[bi] + mod[2:3] * y[bi * tm:(bi + 1) * tm]
        o_ref[bi] = x_new
        eid_ref[:, bi, :], gate_ref[:, bi, :], rank_ref[:, bi, :] = _route(
            x_new, mod[3:4], mod[4:5], jnp.logical_and(t == 0, bi == 0), fgain_ref, rw_ref, rb_ref, before_ref, cnt_ref, base_ref)


def deltanet_layer(x2, mod, gain, w_in, conv_w, a_log, dt_bias, o_norm, w_out, router, b, s, pending=None):
    t, d = x2.shape
    nh = DN_V_HEADS
    main_w = DN_CONV_WIDTH + DN_VAL_WIDTH
    w_main = w_in[:, :main_w].astype(BF16)
    w_ba = w_in[:, main_w:]
    w_ba_hi = w_ba.astype(BF16)
    w_ba_lo = (w_ba - w_ba_hi.astype(F32)).astype(BF16)
    w_ba2 = jnp.concatenate([w_ba_hi, w_ba_lo, jnp.zeros((d, LANES - 4 * nh), BF16)], axis=1)
    lanes_pad = lambda v: jnp.concatenate([jnp.zeros((nh,), F32), v.astype(F32),
                                           jnp.zeros((LANES - 2 * nh,), F32)]).reshape(1, LANES)
    rate = lanes_pad(jnp.exp(a_log.astype(F32)))
    dtb = lanes_pad(dt_bias)

    tm = TOKEN_TILE
    tpb = s // tm
    const2 = lambda bi, ti: (0, 0)
    tile = lambda bi, ti: (bi * tpb + ti, 0)
    body = _dn_proj_kernel
    args = [x2, mod, gain.reshape(1, d), w_main, w_ba2, conv_w, rate, dtb]
    in_specs = [pl.BlockSpec((tm, d), tile),
                pl.BlockSpec((1, 6, d), lambda bi, ti: (bi, 0, 0)),
                pl.BlockSpec((1, d), const2),
                pl.BlockSpec((d, main_w), const2),
                pl.BlockSpec((d, LANES), const2),
                pl.BlockSpec((DN_CONV, DN_CONV_WIDTH), const2),
                pl.BlockSpec((1, LANES), const2),
                pl.BlockSpec((1, LANES), const2)]
    out_specs = [pl.BlockSpec((tm, DN_KEY_WIDTH), tile), pl.BlockSpec((tm, DN_KEY_WIDTH), tile),
                 pl.BlockSpec((tm, DN_VAL_WIDTH), tile), pl.BlockSpec((tm, DN_VAL_WIDTH), tile),
                 pl.BlockSpec((tm, LANES), tile)]
    out_shape = [jax.ShapeDtypeStruct((t, DN_KEY_WIDTH), BF16), jax.ShapeDtypeStruct((t, DN_KEY_WIDTH), BF16),
                 jax.ShapeDtypeStruct((t, DN_VAL_WIDTH), BF16), jax.ShapeDtypeStruct((t, DN_VAL_WIDTH), BF16),
                 jax.ShapeDtypeStruct((t, LANES), F32)]
    scratch = [pltpu.VMEM((tm + SUBLANES, DN_CONV_WIDTH), F32)]
    if pending is not None:
        body = _with_pending_moe(body, 0, 2, len(scratch), True)
        p_args, p_specs, p_scratch = _pending_operands(pending, tm, d, lambda bi, ti: bi * tpb + ti,
                                                       lambda bi, ti: bi, t // tm)
        args, in_specs, scratch = p_args + args, p_specs + in_specs, scratch + p_scratch
        out_specs = out_specs + [pl.BlockSpec((tm, d), tile)]
        out_shape = out_shape + [jax.ShapeDtypeStruct((t, d), F32)]
    outs = pl.pallas_call(
        body,
        grid=(b, tpb),
        in_specs=in_specs,
        out_specs=out_specs,
        out_shape=out_shape,
        scratch_shapes=scratch,
        compiler_params=_params("arbitrary", "arbitrary"),
        name="deltanet_proj",
    )(*args)
    q, k, v, z, bg = outs[:5]
    if pending is not None:
        x2 = outs[5]

    gct = bg[:, nh:2 * nh].T
    tp = DN_PREP_TILE
    rows_of = lambda width: pl.BlockSpec((tp, width), lambda i: (i, 0))
    qk_w = nh * DN_CHUNK
    u, w, qk = pl.pallas_call(
        _dn_prep_kernel,
        grid=(t // tp,),
        in_specs=[rows_of(DN_KEY_WIDTH), rows_of(DN_KEY_WIDTH), rows_of(DN_VAL_WIDTH), rows_of(LANES),
                  pl.BlockSpec((nh, tp), lambda i: (0, i))],
        out_specs=[rows_of(DN_VAL_WIDTH), rows_of(DN_VAL_WIDTH), rows_of(qk_w)],
        out_shape=[jax.ShapeDtypeStruct((t, DN_VAL_WIDTH), BF16), jax.ShapeDtypeStruct((t, DN_VAL_WIDTH), BF16),
                   jax.ShapeDtypeStruct((t, qk_w), BF16)],
        compiler_params=_params("arbitrary"),
        name="deltanet_prep",
    )(q, k, v, bg, gct)

    ts = DN_SCAN_TILE
    seq = lambda a: a.reshape(b, s, a.shape[-1])
    both = lambda width: pl.BlockSpec((b, ts, width), lambda i: (0, i, 0))
    r_args, r_in = _router_operands(*router, ts)
    r_out, r_shapes, r_scratch = _router_results(t, pl.BlockSpec((2, b, ts), lambda i: (0, 0, i)), (2, b, s))
    out = pl.pallas_call(
        _dn_scan_kernel,
        grid=(s // ts,),
        in_specs=[both(d),
                  pl.BlockSpec((b, 6, d), lambda i: (0, 0, 0)),
                  both(DN_VAL_WIDTH), both(DN_VAL_WIDTH), both(qk_w), both(DN_KEY_WIDTH), both(DN_KEY_WIDTH),
                  both(DN_VAL_WIDTH), both(LANES),
                  pl.BlockSpec((1, DN_HEAD_DIM), lambda i: (0, 0)),
                  pl.BlockSpec((DN_VAL_WIDTH, d), lambda i: (0, 0))] + r_in,
        out_specs=[both(d)] + r_out,
        out_shape=[jax.ShapeDtypeStruct((b, s, d), F32)] + r_shapes,
        scratch_shapes=[pltpu.VMEM((b * nh, DN_HEAD_DIM, DN_HEAD_DIM), F32),
                        pltpu.VMEM((b * ts, DN_VAL_WIDTH), BF16), r_scratch],
        compiler_params=_params("arbitrary"),
        name="deltanet_scan",
    )(seq(x2), mod, seq(u), seq(w), seq(qk), seq(q), seq(k), seq(z), seq(bg), o_norm.reshape(1, DN_HEAD_DIM),
      w_out.astype(BF16), *r_args)
    eid, gate, rank, cnt = out[1:]
    return out[0].reshape(t, d), (eid.reshape(2, t), gate.reshape(2, t), rank.reshape(2, t), cnt)


def kernel(x, c, l0_norm_mix, l0_norm_ffn, l0_ada_w, l0_ada_b, l0_gm_w_in, l0_gm_v_norm, l0_gm_w_s, l0_gm_b_s, l0_gm_w_out, l0_router_w, l0_router_b, l0_expert_w_in, l0_expert_w_out, l1_norm_mix, l1_norm_ffn, l1_ada_w, l1_ada_b, l1_dn_w_in, l1_dn_conv_w, l1_dn_a_log, l1_dn_dt_bias, l1_dn_o_norm, l1_dn_w_out, l1_router_w, l1_router_b, l1_expert_w_in, l1_expert_w_out, l2_norm_mix, l2_norm_ffn, l2_ada_w, l2_ada_b, l2_swa_w_in, l2_swa_q_norm, l2_swa_k_norm, l2_swa_sinks, l2_swa_w_out, l2_router_w, l2_router_b, l2_expert_w_in, l2_expert_w_out, l3_norm_mix, l3_norm_ffn, l3_ada_w, l3_ada_b, l3_gm_w_in, l3_gm_v_norm, l3_gm_w_s, l3_gm_b_s, l3_gm_w_out, l3_router_w, l3_router_b, l3_expert_w_in, l3_expert_w_out):
    b, s, d = x.shape
    x2 = x.reshape(b * s, d)
    c_pad = jnp.zeros((8, d), F32).at[:b].set(c)
    tiles_per_batch_of = lambda tm: s // tm

    def modulation(ada_w, ada_b):
        return adaln(c_pad, ada_w, ada_b)[:b].reshape(b, 6, d)

    mod = modulation(l0_ada_w, l0_ada_b)
    x2, routing = gmlp_layer(x2, mod, l0_norm_mix, l0_gm_w_in, l0_gm_v_norm, l0_gm_w_s, l0_gm_b_s, l0_gm_w_out,
                             (l0_norm_ffn, l0_router_w, l0_router_b), tiles_per_batch_of(TOKEN_TILE))
    pending = moe_layer(x2, mod, l0_norm_ffn, routing, l0_expert_w_in, l0_expert_w_out, tiles_per_batch_of)

    mod = modulation(l1_ada_w, l1_ada_b)
    x2, routing = deltanet_layer(x2, mod, l1_norm_mix, l1_dn_w_in, l1_dn_conv_w, l1_dn_a_log, l1_dn_dt_bias,
                                 l1_dn_o_norm, l1_dn_w_out, (l1_norm_ffn, l1_router_w, l1_router_b), b, s, pending)
    pending = moe_layer(x2, mod, l1_norm_ffn, routing, l1_expert_w_in, l1_expert_w_out, tiles_per_batch_of)

    mod = modulation(l2_ada_w, l2_ada_b)
    x2, routing = swa_layer(x2, mod, l2_norm_mix, l2_swa_w_in, l2_swa_q_norm, l2_swa_k_norm, l2_swa_sinks,
                            l2_swa_w_out, (l2_norm_ffn, l2_router_w, l2_router_b), b, s, pending)
    pending = moe_layer(x2, mod, l2_norm_ffn, routing, l2_expert_w_in, l2_expert_w_out, tiles_per_batch_of)

    mod = modulation(l3_ada_w, l3_ada_b)
    x2, routing = gmlp_layer(x2, mod, l3_norm_mix, l3_gm_w_in, l3_gm_v_norm, l3_gm_w_s, l3_gm_b_s, l3_gm_w_out,
                             (l3_norm_ffn, l3_router_w, l3_router_b), tiles_per_batch_of(TOKEN_TILE), pending)
    dest, gates_t, ys, _ = moe_layer(x2, mod, l3_norm_ffn, routing, l3_expert_w_in, l3_expert_w_out,
                                     tiles_per_batch_of)
    return moe_combine(x2, mod, gates_t, dest, ys, tiles_per_batch_of).reshape(b, s, d)
```

```python
import functools
import math

import jax
import jax.numpy as jnp
from jax import lax
from jax.experimental import pallas as pl
from jax.experimental.pallas import tpu as pltpu

F32 = jnp.float32
BF16 = jnp.bfloat16

D_MODEL = 1024
NORM_EPS = 1e-6

GM_CHUNK = 128
GM_GROUPS = 8
GM_GROUP_DIM = D_MODEL // GM_GROUPS

MOE_GROUPS = 4
MOE_EXPERTS_PER_GROUP = 8
MOE_EXPERTS = MOE_GROUPS * MOE_EXPERTS_PER_GROUP
MOE_FF = D_MODEL // 2

VMEM_LIMIT_BYTES = 56 * 1024 * 1024

TOKEN_TILE = 512
MOVE_TILE = 512
EXPERT_ROWS = 512
EXPERT_SUB_ROWS = 128
LANES = 128
SUBLANES = 8


def _params(*semantics):
    return pltpu.CompilerParams(dimension_semantics=semantics, vmem_limit_bytes=VMEM_LIMIT_BYTES,
                                disable_bounds_checks=True)


def _rms(xf):
    return xf * lax.rsqrt(jnp.mean(xf * xf, axis=-1, keepdims=True) + NORM_EPS)


def _modulated_norm(x, gain, shift, scale):
    return _rms(x) * gain * (1.0 + scale) + shift


def _gelu_tanh(x):
    return 0.5 * x * (1.0 + jnp.tanh(math.sqrt(2.0 / math.pi) * (x + 0.044715 * (x * x * x))))


def _silu(x):
    return x * (1.0 / (1.0 + jnp.exp(-x)))


def _adaln_kernel(ct_ref, w_ref, b_ref, o_ref):
    act = _silu(ct_ref[...])
    w = w_ref[...]
    o_ref[...] = jnp.concatenate([jnp.sum(act[:, r:r + 1] * w, axis=0, keepdims=True)
                                  for r in range(o_ref.shape[0])], axis=0) + b_ref[...]


def adaln(c, ada_w, ada_b):
    rows, d = c.shape
    n = ada_w.shape[1]
    tn = d
    return pl.pallas_call(
        _adaln_kernel,
        grid=(n // tn,),
        in_specs=[pl.BlockSpec((d, rows), lambda j: (0, 0)),
                  pl.BlockSpec((d, tn), lambda j: (0, j)),
                  pl.BlockSpec((1, tn), lambda j: (0, j))],
        out_specs=pl.BlockSpec((rows, tn), lambda j: (0, j)),
        out_shape=jax.ShapeDtypeStruct((rows, n), F32),
        compiler_params=_params("arbitrary"),
        name="adaln",
    )(c.T, ada_w, ada_b.reshape(1, n))


def _gmlp_kernel(x_ref, mod_ref, gain_ref, win_ref, vn_ref, ws_ref, bs_ref, wout_ref, fgain_ref, rw_ref, rb_ref, before_ref,
                 o_ref, eid_ref, gate_ref, rank_ref, cnt_ref, u_ref, v_ref, g_ref, base_ref):
    width = D_MODEL
    x = x_ref[...]
    mod = mod_ref[0]
    h = _modulated_norm(x, gain_ref[...], mod[0:1], mod[1:2])
    z = _gelu_tanh(jnp.dot(h.astype(BF16), win_ref[...], preferred_element_type=F32))
    u_ref[...] = z[:, :width]
    v_ref[...] = (_rms(z[:, width:]) * vn_ref[...]).astype(BF16)
    cells = [(slice(c * GM_CHUNK, (c + 1) * GM_CHUNK), g, slice(g * GM_GROUP_DIM, (g + 1) * GM_GROUP_DIM))
             for c in range(x.shape[0] // GM_CHUNK) for g in range(GM_GROUPS)]
    mixed = [jnp.dot(ws_ref[g], v_ref[rows, cols], preferred_element_type=F32) for rows, g, cols in cells]
    for (rows, g, cols), sv in zip(cells, mixed):
        g_ref[rows, cols] = (u_ref[rows, cols] * (sv + bs_ref[:, g:g + 1])).astype(BF16)
    y = jnp.dot(g_ref[...], wout_ref[...], preferred_element_type=F32)
    x_new = x + mod[2:3] * y
    o_ref[...] = x_new
    eid_ref[...], gate_ref[...], rank_ref[...] = _route(
        x_new, mod[3:4], mod[4:5], pl.program_id(0) == 0, fgain_ref, rw_ref, rb_ref, before_ref, cnt_ref, base_ref)


def gmlp_layer(x2, mod, gain, w_in, v_norm, w_s, b_s, w_out, router, tiles_per_batch):
    t, d = x2.shape
    tm = TOKEN_TILE
    ws_causal = jnp.where(jnp.tril(jnp.ones((GM_CHUNK, GM_CHUNK), dtype=bool)), w_s, 0).astype(BF16)
    const2 = lambda i: (0, 0)
    r_args, r_in = _router_operands(*router, tm)
    r_out, r_shapes, r_scratch = _router_results(t, pl.BlockSpec((2, tm), lambda i: (0, i)), (2, t))
    out = pl.pallas_call(
        _gmlp_kernel,
        grid=(t // tm,),
        in_specs=[pl.BlockSpec((tm, d), lambda i: (i, 0)),
                  pl.BlockSpec((1, 6, d), lambda i: (i // tiles_per_batch, 0, 0)),
                  pl.BlockSpec((1, d), const2),
                  pl.BlockSpec((d, 2 * d), const2),
                  pl.BlockSpec((1, d), const2),
                  pl.BlockSpec((GM_GROUPS, GM_CHUNK, GM_CHUNK), lambda i: (0, 0, 0)),
                  pl.BlockSpec((GM_CHUNK, GM_GROUPS), const2),
                  pl.BlockSpec((d, d), const2)] + r_in,
        out_specs=[pl.BlockSpec((tm, d), lambda i: (i, 0))] + r_out,
        out_shape=[jax.ShapeDtypeStruct((t, d), F32)] + r_shapes,
        scratch_shapes=[pltpu.VMEM((tm, d), F32), pltpu.VMEM((tm, d), BF16), pltpu.VMEM((tm, d), BF16), r_scratch],
        compiler_params=_params("arbitrary"),
        name="gmlp_mixer",
    )(x2, mod, gain.reshape(1, d), w_in.astype(BF16), v_norm.reshape(1, d), ws_causal, b_s.T,
      w_out.astype(BF16), *r_args)
    return out[0], tuple(out[1:])


def _route(x, shift, scale, first, gain_ref, rw_ref, rb_ref, before_ref, cnt_ref, base_ref):
    tm = x.shape[0]
    ne = MOE_EXPERTS
    npg = MOE_EXPERTS_PER_GROUP

    @pl.when(first)
    def _():
        base_ref[...] = jnp.zeros_like(base_ref)

    hh, hl = _split_bf16(_modulated_norm(x, gain_ref[...], shift, scale))
    nt = (((1,), (1,)), ((), ()))
    nr = ROUTER_ROWS
    by_hi = lax.dot_general(rw_ref[...], hh, nt, preferred_element_type=F32)
    by_lo = lax.dot_general(rw_ref[0:nr, :], hl, nt, preferred_element_type=F32)
    lt = (by_hi[0:nr] + rb_ref[...]) + (by_hi[nr:] + by_lo)
    lg = [lt[ne + g:ne + g + 1, :] for g in range(MOE_GROUPS)]
    gmax = functools.reduce(jnp.maximum, lg)
    gsum = functools.reduce(lambda a, b: a + b, [jnp.exp(l - gmax) for l in lg])
    pg_top = 1.0 / gsum
    g_sel = jnp.full(gmax.shape, MOE_GROUPS - 1, jnp.int32)
    for g in range(MOE_GROUPS - 2, -1, -1):
        g_sel = jnp.where(lg[g] == gmax, g, g_sel)
    sel = lt[(MOE_GROUPS - 1) * npg:MOE_GROUPS * npg, :]
    for g in range(MOE_GROUPS - 2, -1, -1):
        sel = jnp.where(g_sel == g, lt[g * npg:(g + 1) * npg, :], sel)
    row = lax.broadcasted_iota(jnp.int32, sel.shape, 0)
    m1 = jnp.max(sel, axis=0, keepdims=True)
    i1 = jnp.min(jnp.where(sel == m1, row, npg), axis=0, keepdims=True)
    rest = jnp.where(row == i1, -jnp.inf, sel)
    m2 = jnp.max(rest, axis=0, keepdims=True)
    i2 = jnp.min(jnp.where(rest == m2, row, npg), axis=0, keepdims=True)
    e2 = jnp.exp(m2 - m1)
    inv = pg_top / (1.0 + e2)
    eid = jnp.concatenate([g_sel * npg + i1, g_sel * npg + i2], axis=0)
    gates = jnp.concatenate([inv, inv * e2], axis=0)

    erow = lax.broadcasted_iota(jnp.int32, (ne, tm), 0)
    hits = [erow == eid[k:k + 1, :] for k in range(2)]
    prefix = jnp.dot(jnp.concatenate([jnp.where(hit, 1.0, 0.0).astype(BF16) for hit in hits], axis=0),
                     before_ref[...], preferred_element_type=F32)
    base = base_ref[...]
    ranks = []
    for k, hit in enumerate(hits):
        ranks.append(jnp.sum(jnp.where(hit, prefix[k * ne:(k + 1) * ne] + base, 0.0), axis=0, keepdims=True))
        base = base + jnp.sum(jnp.where(hit, 1.0, 0.0), axis=1, keepdims=True)
    base_ref[...] = base
    cnt_ref[...] = jnp.broadcast_to(base, cnt_ref.shape).astype(jnp.int32)
    return eid, gates, jnp.concatenate(ranks, axis=0).astype(jnp.int32)


ROUTER_ROWS = MOE_EXPERTS + 2 * SUBLANES


def _router_operands(gain, router_w, router_b, tm):
    d = router_w.shape[0]
    pad = ROUTER_ROWS - MOE_EXPERTS - MOE_GROUPS
    rw = jnp.concatenate([router_w[:, MOE_GROUPS:], router_w[:, :MOE_GROUPS], jnp.zeros((d, pad), F32)], axis=1).T
    rw_hi = rw.astype(BF16)
    rw2 = jnp.concatenate([rw_hi, (rw - rw_hi.astype(F32)).astype(BF16)], axis=0)
    rb = jnp.concatenate([router_b[MOE_GROUPS:], router_b[:MOE_GROUPS], jnp.zeros((pad,), F32)])
    pos = jnp.arange(tm)
    before = (pos[:, None] < pos[None, :]).astype(BF16)
    const = lambda *_: (0, 0)
    specs = [pl.BlockSpec((1, d), const), pl.BlockSpec((2 * ROUTER_ROWS, d), const),
             pl.BlockSpec((ROUTER_ROWS, 1), const), pl.BlockSpec((tm, tm), const)]
    return [gain.reshape(1, d), rw2, rb.reshape(ROUTER_ROWS, 1), before], specs


def _router_results(t, slot_spec, slot_shape):
    specs = [slot_spec, slot_spec, slot_spec, pl.BlockSpec((MOE_EXPERTS, LANES), lambda *_: (0, 0))]
    shapes = [jax.ShapeDtypeStruct(slot_shape, jnp.int32), jax.ShapeDtypeStruct(slot_shape, F32),
              jax.ShapeDtypeStruct(slot_shape, jnp.int32), jax.ShapeDtypeStruct((MOE_EXPERTS, LANES), jnp.int32)]
    return specs, shapes, pltpu.VMEM((MOE_EXPERTS, 1), F32)


def _wait_rows(hbm_ref, n_rows, sem, times):
    rows = hbm_ref.at[pl.ds(0, n_rows)]
    for _ in range(times):
        pltpu.make_async_copy(rows, rows, sem).wait()


def _tiled_shape(n, d):
    return (n // SUBLANES, d // LANES, SUBLANES, LANES)


def _store_tiled(ref, a):
    n, d = a.shape
    for j in range(d // LANES):
        ref[:, j, :, :] = a[:, j * LANES:(j + 1) * LANES].reshape(n // SUBLANES, SUBLANES, LANES)


def _load_tiled(ref):
    g, nj, _, _ = ref.shape
    return jnp.concatenate([ref[:, j, :, :].reshape(g * SUBLANES, LANES) for j in range(nj)], axis=1)


def _dispatch_kernel(d0_ref, d1_ref, x_ref, mod_ref, gain_ref, xs_ref, h_ref, sem):
    i = pl.program_id(0)
    last = pl.num_programs(0) - 1
    tm, d = x_ref.shape
    mod = mod_ref[0]
    h = _modulated_norm(x_ref[...], gain_ref[...], mod[3:4], mod[4:5])

    def step(buf):
        _store_tiled(h_ref.at[buf], h)

        def issue(g, carry):
            for u in range(SUBLANES):
                for k, dest_ref in enumerate((d0_ref, d1_ref)):
                    pltpu.make_async_copy(h_ref.at[buf, g, :, u, :], xs_ref.at[dest_ref[g * SUBLANES + u]],
                                          sem.at[buf]).start(priority=k)
            return carry

        lax.fori_loop(0, tm // SUBLANES, issue, 0)

        @pl.when(i > 0)
        def _():
            _wait_rows(xs_ref, tm, sem.at[1 - buf], 2)

        @pl.when(i == last)
        def _():
            _wait_rows(xs_ref, tm, sem.at[buf], 2)

    for buf in range(2):
        pl.when(i % 2 == buf)(functools.partial(step, buf))


def moe_dispatch(x2, mod, gain, dest, tiles_per_batch_of):
    t, d = x2.shape
    tm = MOVE_TILE
    tpb = tiles_per_batch_of(tm)
    const2 = lambda i: (0, 0)
    slot_spec = pl.BlockSpec((tm,), lambda i: (i,), memory_space=pltpu.SMEM)
    return pl.pallas_call(
        _dispatch_kernel,
        grid=(t // tm,),
        in_specs=[slot_spec, slot_spec,
                  pl.BlockSpec((tm, d), lambda i: (i, 0)),
                  pl.BlockSpec((1, 6, d), lambda i: (i // tpb, 0, 0)),
                  pl.BlockSpec((1, d), const2)],
        out_specs=pl.BlockSpec(memory_space=pl.ANY),
        out_shape=jax.ShapeDtypeStruct((2 * t, d // LANES, LANES), F32),
        scratch_shapes=[pltpu.VMEM((2,) + _tiled_shape(tm, d), F32), pltpu.SemaphoreType.DMA((2,))],
        compiler_params=_params("arbitrary"),
        name="moe_dispatch",
    )(dest[0], dest[1], x2, mod, gain.reshape(1, d))


def _expert_kernel(blk_ref, exp_ref, start_ref, nvalid_ref, xs_ref, win_ref, wout_ref, ys_ref,
                   winb_ref, woutb_ref, xbuf_ref, ybuf_ref, in_sem, out_sem):
    i = pl.program_id(0)
    n_items = pl.num_programs(0)
    bm = EXPERT_ROWS
    groups = bm // SUBLANES
    n_blocks = xs_ref.shape[0] // groups
    e = exp_ref[i]
    blk = blk_ref[i]
    prev = jnp.maximum(i - 1, 0)
    nxt = jnp.minimum(i + 1, n_items - 1)
    first = i == 0
    final = i == n_items - 1
    lo = start_ref[e]
    hi = start_ref[e + 1]
    row0 = blk * bm
    live = i < nvalid_ref[0]
    slot = blk % 2

    def block_copies(hbm_ref, buf_ref, b, s, sem, to_hbm):
        copies = []
        for u in range(SUBLANES):
            hbm = hbm_ref.at[pl.ds(b * groups, groups), u]
            vmem = buf_ref.at[s, :, :, u, :]
            copies.append(pltpu.make_async_copy(vmem, hbm, sem.at[s]) if to_hbm
                          else pltpu.make_async_copy(hbm, vmem, sem.at[s]))
        return copies

    def fetch(b, s):
        for c in block_copies(xs_ref, xbuf_ref, b, s, in_sem, False):
            c.start()

    @pl.when(jnp.logical_or(first, exp_ref[prev] != e))
    def _():
        winb_ref[...] = win_ref[0].astype(BF16)
        woutb_ref[...] = wout_ref[0].astype(BF16)

    new_block = jnp.logical_or(first, blk_ref[prev] != blk)
    last_of_block = jnp.logical_or(final, blk_ref[nxt] != blk)
    whole = jnp.logical_and(lo <= row0, hi >= row0 + bm)

    @pl.when(first)
    def _():
        fetch(0, 0)

    @pl.when(new_block)
    def _():
        for c in block_copies(xs_ref, xbuf_ref, blk, slot, in_sem, False):
            c.wait()

        @pl.when(blk + 1 < n_blocks)
        def _():
            fetch(blk + 1, 1 - slot)

        @pl.when(blk >= 2)
        def _():
            for c in block_copies(ys_ref, ybuf_ref, blk - 2, slot, out_sem, True):
                c.wait()

    def ffn(x):
        a_gl = jnp.dot(x.astype(BF16), winb_ref[...], preferred_element_type=F32)
        mid = (_silu(a_gl[:, :MOE_FF]) * a_gl[:, MOE_FF:]).astype(BF16)
        return jnp.dot(mid, woutb_ref[...], preferred_element_type=F32)

    @pl.when(jnp.logical_and(live, whole))
    def _():
        _store_tiled(ybuf_ref.at[slot], ffn(_load_tiled(xbuf_ref.at[slot])))

    @pl.when(jnp.logical_and(new_block, jnp.logical_not(whole)))
    def _():
        ybuf_ref[slot] = jnp.zeros(ybuf_ref.shape[1:], F32)

    sub_groups = EXPERT_SUB_ROWS // SUBLANES
    for sub in range(bm // EXPERT_SUB_ROWS):
        sub0 = row0 + sub * EXPERT_SUB_ROWS
        touched = jnp.logical_and(hi > sub0, lo < sub0 + EXPERT_SUB_ROWS)

        @pl.when(jnp.logical_and(jnp.logical_and(live, jnp.logical_not(whole)), touched))
        def _():
            gs = pl.ds(sub * sub_groups, sub_groups)
            rows = sub0 + lax.broadcasted_iota(jnp.int32, (EXPERT_SUB_ROWS, 1), 0)
            mine = jnp.logical_and(rows >= lo, rows < hi)
            y = jnp.where(mine, ffn(_load_tiled(xbuf_ref.at[slot, gs])), 0.0)
            _store_tiled(ybuf_ref.at[slot, gs], _load_tiled(ybuf_ref.at[slot, gs]) + y)

    @pl.when(last_of_block)
    def _():
        for c in block_copies(ys_ref, ybuf_ref, blk, slot, out_sem, True):
            c.start()

    @pl.when(final)
    def _():
        for c in block_copies(ys_ref, ybuf_ref, blk, slot, out_sem, True):
            c.wait()

        @pl.when(blk >= 1)
        def _():
            for c in block_copies(ys_ref, ybuf_ref, blk - 1, 1 - slot, out_sem, True):
                c.wait()


def moe_experts(xs, w_in, w_out, item_block, item_expert, seg_start, n_valid):
    n, nj, _ = xs.shape
    d = nj * LANES
    bm = EXPERT_ROWS
    n_items = item_block.shape[0]
    by_group = (n // SUBLANES, SUBLANES, nj, LANES)
    grid_spec = pltpu.PrefetchScalarGridSpec(
        num_scalar_prefetch=4,
        grid=(n_items,),
        in_specs=[pl.BlockSpec(memory_space=pl.ANY),
                  pl.BlockSpec((1, d, 2 * MOE_FF), lambda i, b, e, s, v: (e[i], 0, 0)),
                  pl.BlockSpec((1, MOE_FF, d), lambda i, b, e, s, v: (e[i], 0, 0))],
        out_specs=pl.BlockSpec(memory_space=pl.ANY),
        scratch_shapes=[pltpu.VMEM((d, 2 * MOE_FF), BF16), pltpu.VMEM((MOE_FF, d), BF16),
                        pltpu.VMEM((2,) + _tiled_shape(bm, d), F32), pltpu.VMEM((2,) + _tiled_shape(bm, d), F32),
                        pltpu.SemaphoreType.DMA((2,)), pltpu.SemaphoreType.DMA((2,))],
    )
    ys = pl.pallas_call(
        _expert_kernel,
        grid_spec=grid_spec,
        out_shape=jax.ShapeDtypeStruct(by_group, F32),
        compiler_params=_params("arbitrary"),
        name="moe_experts",
    )(item_block, item_expert, seg_start, n_valid, xs.reshape(by_group), w_in, w_out)
    return ys.reshape(n, nj, LANES)


def _combine_kernel(d0_ref, d1_ref, n0_ref, n1_ref, x_ref, mod_ref, gate_ref, ys_ref, o_ref, y_ref, sem):
    i = pl.program_id(0)
    last = pl.num_programs(0) - 1
    tm, d = x_ref.shape

    def gather(dest_refs, buf):
        def issue(g, carry):
            for u in range(SUBLANES):
                for k, dest_ref in enumerate(dest_refs):
                    pltpu.make_async_copy(ys_ref.at[dest_ref[g * SUBLANES + u]],
                                          y_ref.at[buf, k, g, :, u, :], sem.at[buf]).start(priority=k)
            return carry

        lax.fori_loop(0, tm // SUBLANES, issue, 0)

    def step(buf):
        @pl.when(i == 0)
        def _():
            gather((d0_ref, d1_ref), buf)

        @pl.when(i < last)
        def _():
            gather((n0_ref, n1_ref), 1 - buf)

        _wait_rows(ys_ref, tm, sem.at[buf], 2)
        gates = gate_ref[...]
        moe = gates[:, 0:1] * _load_tiled(y_ref.at[buf, 0]) + gates[:, 1:2] * _load_tiled(y_ref.at[buf, 1])
        o_ref[...] = x_ref[...] + mod_ref[0][5:6] * moe

    for buf in range(2):
        pl.when(i % 2 == buf)(functools.partial(step, buf))


def moe_combine(x2, mod, gates_t, dest, ys, tiles_per_batch_of):
    t, d = x2.shape
    tm = MOVE_TILE
    tpb = tiles_per_batch_of(tm)
    n_tiles = t // tm
    slot_spec = pl.BlockSpec((tm,), lambda i: (i,), memory_space=pltpu.SMEM)
    next_spec = pl.BlockSpec((tm,), lambda i: (jnp.minimum(i + 1, n_tiles - 1),), memory_space=pltpu.SMEM)
    return pl.pallas_call(
        _combine_kernel,
        grid=(n_tiles,),
        in_specs=[slot_spec, slot_spec, next_spec, next_spec,
                  pl.BlockSpec((tm, d), lambda i: (i, 0)),
                  pl.BlockSpec((1, 6, d), lambda i: (i // tpb, 0, 0)),
                  pl.BlockSpec((tm, 2), lambda i: (i, 0)),
                  pl.BlockSpec(memory_space=pl.ANY)],
        out_specs=pl.BlockSpec((tm, d), lambda i: (i, 0)),
        out_shape=jax.ShapeDtypeStruct((t, d), F32),
        scratch_shapes=[pltpu.VMEM((2, 2) + _tiled_shape(tm, d), F32), pltpu.SemaphoreType.DMA((2,))],
        compiler_params=_params("arbitrary"),
        name="moe_combine",
    )(dest[0], dest[1], dest[0], dest[1], x2, mod, gates_t, ys)


def moe_layer(x2, mod, gain, routing, w_in, w_out, tiles_per_batch_of):
    t, d = x2.shape
    bm = EXPERT_ROWS
    n_assign = 2 * t
    eid, gate, rank, cnt = routing
    ne = MOE_EXPERTS
    experts = jnp.arange(ne, dtype=jnp.int32)
    upto = experts[None, :] <= experts[:, None]

    def running_total(v):
        return jnp.sum(jnp.where(upto, v[None, :], 0), axis=1).astype(jnp.int32)

    def lookup(table, idx):
        hit = idx[None] == experts.reshape((ne,) + (1,) * idx.ndim)
        return jnp.sum(jnp.where(hit, table.reshape((ne,) + (1,) * idx.ndim), 0), axis=0).astype(jnp.int32)

    counts = cnt[:, 0]
    seg_end = running_total(counts)
    seg_start = jnp.concatenate([jnp.zeros((1,), jnp.int32), seg_end])
    dest = lookup(seg_start[:ne], eid) + rank
    n_blocks = n_assign // bm
    n_items = n_blocks + ne - 1
    first_blk = seg_start[:ne] // bm
    last_blk = jnp.where(counts > 0, (seg_end - 1) // bm, first_blk - 1)
    per_expert = jnp.maximum(last_blk - first_blk + 1, 0)
    item_end = running_total(per_expert)
    n_valid = item_end[ne - 1]
    item_ids = jnp.arange(n_items, dtype=jnp.int32)
    item_ids_c = jnp.maximum(jnp.minimum(item_ids, n_valid - 1), 0)
    item_expert = jnp.minimum(jnp.sum(item_ids_c[:, None] >= item_end[None, :], axis=1), ne - 1).astype(jnp.int32)
    item_block = lookup(first_blk - (item_end - per_expert), item_expert) + item_ids_c
    xs = moe_dispatch(x2, mod, gain, dest, tiles_per_batch_of)
    ys = moe_experts(xs, w_in, w_out, item_block, item_expert, seg_start,
                     n_valid.reshape(1).astype(jnp.int32))
    return moe_combine(x2, mod, gate.T, dest, ys, tiles_per_batch_of)


SWA_HEAD_DIM = 64
SWA_Q_HEADS = D_MODEL // SWA_HEAD_DIM
SWA_KV_HEADS = 4
SWA_BLOCK = 128
SWA_Q_WIDTH = SWA_Q_HEADS * SWA_HEAD_DIM
SWA_KV_WIDTH = SWA_KV_HEADS * SWA_HEAD_DIM
HEADS_PER_VREG = LANES // SWA_HEAD_DIM


def _swa_kernel(sink_ref, x_ref, mod_ref, gain_ref, win_ref, qg_ref, kg_ref, ones_ref, wout_ref,
                fgain_ref, rw_ref, rb_ref, before_ref, o_ref, eid_ref, gate_ref, rank_ref, cnt_ref,
                kx_ref, vx_ref, att_ref, base_ref):
    t = pl.program_id(1)
    tm = x_ref.shape[0]
    blk = SWA_BLOCK
    hd = SWA_HEAD_DIM

    @pl.when(t == 0)
    def _():
        kx_ref[0:blk, :] = jnp.zeros((blk, kx_ref.shape[1]), BF16)
        vx_ref[0:blk, :] = jnp.zeros((blk, vx_ref.shape[1]), BF16)

    x = x_ref[...]
    mod = mod_ref[0]
    h = _modulated_norm(x, gain_ref[...], mod[0:1], mod[1:2]).astype(BF16)
    proj = jnp.dot(h, win_ref[...], preferred_element_type=F32)

    def head_rms(a):
        ss = jnp.dot((a * a).astype(BF16), ones_ref[...], preferred_element_type=F32)
        return a * lax.rsqrt(ss * (1.0 / hd) + NORM_EPS)

    lane = lax.broadcasted_iota(jnp.int32, (tm, LANES), 1)
    low = lane < hd
    for c in range(SWA_Q_WIDTH // LANES):
        cols = slice(c * LANES, (c + 1) * LANES)
        att_ref[:, cols] = (head_rms(proj[:, cols]) * qg_ref[:, cols]).astype(BF16)
    for c in range(SWA_KV_WIDTH // LANES):
        cols = slice(c * LANES, (c + 1) * LANES)
        kc = head_rms(proj[:, SWA_Q_WIDTH + c * LANES:SWA_Q_WIDTH + (c + 1) * LANES]) * kg_ref[:, cols]
        vc = proj[:, SWA_Q_WIDTH + SWA_KV_WIDTH + c * LANES:SWA_Q_WIDTH + SWA_KV_WIDTH + (c + 1) * LANES]
        for ref, a in ((kx_ref, kc), (vx_ref, vc)):
            even_lo = jnp.where(low, a, 0.0)
            odd_hi = jnp.where(low, 0.0, a)
            j0 = HEADS_PER_VREG * c
            ref[blk:blk + tm, (2 * j0) * LANES:(2 * j0 + 1) * LANES] = even_lo.astype(BF16)
            ref[blk:blk + tm, (2 * j0 + 1) * LANES:(2 * j0 + 2) * LANES] = pltpu.roll(even_lo, hd, 1).astype(BF16)
            ref[blk:blk + tm, (2 * j0 + 2) * LANES:(2 * j0 + 3) * LANES] = pltpu.roll(odd_hi, hd, 1).astype(BF16)
            ref[blk:blk + tm, (2 * j0 + 3) * LANES:(2 * j0 + 4) * LANES] = odd_hi.astype(BF16)

    qi = lax.broadcasted_iota(jnp.int32, (blk, 2 * blk), 0)
    kj = lax.broadcasted_iota(jnp.int32, (blk, 2 * blk), 1)
    lane_q = lax.broadcasted_iota(jnp.int32, (blk, LANES), 1) < hd
    pairs_per_kv = SWA_Q_HEADS // SWA_KV_HEADS // HEADS_PER_VREG
    for i in range(tm // blk):
        rows = slice(i * blk, (i + 1) * blk)
        keys = slice(i * blk, (i + 2) * blk)
        floor = jnp.where(t == 0, blk - 1, qi) if i == 0 else qi
        mask = jnp.logical_and(kj > floor, kj <= qi + blk)
        heads = [(p, half) for p in range(SWA_Q_HEADS // HEADS_PER_VREG) for half in range(HEADS_PER_VREG)]
        col_of = lambda p, half: (2 * (p // pairs_per_kv) + half) * LANES
        scores = [lax.dot_general(att_ref[rows, p * LANES:(p + 1) * LANES],
                                  kx_ref[keys, col_of(p, half):col_of(p, half) + LANES],
                                  (((1,), (1,)), ((), ())), preferred_element_type=F32) for p, half in heads]
        probs, inv = [], []
        for (p, half), sc in zip(heads, scores):
            sc = jnp.where(mask, sc, -jnp.inf)
            sink = sink_ref[HEADS_PER_VREG * p + half]
            m = jnp.maximum(jnp.max(sc, axis=-1, keepdims=True), sink)
            pr = jnp.exp(sc - m)
            inv.append(1.0 / (jnp.sum(pr, axis=-1, keepdims=True) + jnp.exp(sink - m)))
            probs.append(pr.astype(BF16))
        pvs = [jnp.dot(pr, vx_ref[keys, col_of(p, half):col_of(p, half) + LANES], preferred_element_type=F32)
               for (p, half), pr in zip(heads, probs)]
        for p in range(SWA_Q_HEADS // HEADS_PER_VREG):
            out = (pvs[2 * p] + pvs[2 * p + 1]) * jnp.where(lane_q, inv[2 * p], inv[2 * p + 1])
            att_ref[rows, p * LANES:(p + 1) * LANES] = out.astype(BF16)
    kx_ref[0:blk, :] = kx_ref[tm:tm + blk, :]
    vx_ref[0:blk, :] = vx_ref[tm:tm + blk, :]
    y = jnp.dot(att_ref[...], wout_ref[...], preferred_element_type=F32)
    x_new = x + mod[2:3] * y
    o_ref[...] = x_new
    first = jnp.logical_and(pl.program_id(0) == 0, t == 0)
    eid_ref[...], gate_ref[...], rank_ref[...] = _route(
        x_new, mod[3:4], mod[4:5], first, fgain_ref, rw_ref, rb_ref, before_ref, cnt_ref, base_ref)


def swa_layer(x2, mod, gain, w_in, q_norm, k_norm, sinks, w_out, router, b, s):
    t, d = x2.shape
    tm = TOKEN_TILE
    tpb = s // tm
    hd = SWA_HEAD_DIM
    qg = (jnp.tile(q_norm, SWA_Q_HEADS) * hd ** -0.5).reshape(1, SWA_Q_WIDTH)
    kg = jnp.tile(k_norm, SWA_KV_HEADS).reshape(1, SWA_KV_WIDTH)
    ids = jnp.arange(LANES) // hd
    ones = (ids[:, None] == ids[None, :]).astype(BF16)
    proj_w = SWA_Q_WIDTH + 2 * SWA_KV_WIDTH
    ext_w = 2 * SWA_KV_HEADS * LANES
    const2 = lambda bi, ti: (0, 0)
    tile = lambda bi, ti: (bi * tpb + ti, 0)
    r_args, r_in = _router_operands(*router, tm)
    r_out, r_shapes, r_scratch = _router_results(t, pl.BlockSpec((2, tm), lambda bi, ti: (0, bi * tpb + ti)), (2, t))
    out = pl.pallas_call(
        _swa_kernel,
        grid=(b, tpb),
        in_specs=[pl.BlockSpec(memory_space=pltpu.SMEM),
                  pl.BlockSpec((tm, d), tile),
                  pl.BlockSpec((1, 6, d), lambda bi, ti: (bi, 0, 0)),
                  pl.BlockSpec((1, d), const2),
                  pl.BlockSpec((d, proj_w), const2),
                  pl.BlockSpec((1, SWA_Q_WIDTH), const2),
                  pl.BlockSpec((1, SWA_KV_WIDTH), const2),
                  pl.BlockSpec((LANES, LANES), const2),
                  pl.BlockSpec((SWA_Q_WIDTH, d), const2)] + r_in,
        out_specs=[pl.BlockSpec((tm, d), tile)] + r_out,
        out_shape=[jax.ShapeDtypeStruct((t, d), F32)] + r_shapes,
        scratch_shapes=[pltpu.VMEM((tm + SWA_BLOCK, ext_w), BF16), pltpu.VMEM((tm + SWA_BLOCK, ext_w), BF16),
                        pltpu.VMEM((tm, SWA_Q_WIDTH), BF16), r_scratch],
        compiler_params=_params("arbitrary", "arbitrary"),
        name="swa_mixer",
    )(sinks, x2, mod, gain.reshape(1, d), w_in.astype(BF16), qg, kg, ones, w_out.astype(BF16), *r_args)
    return out[0], tuple(out[1:])


DN_QK_HEADS = 4
DN_V_HEADS = 8
DN_HEAD_DIM = D_MODEL // DN_V_HEADS
DN_CONV = 4
DN_CHUNK = 64
DN_KEY_WIDTH = DN_QK_HEADS * DN_HEAD_DIM
DN_VAL_WIDTH = DN_V_HEADS * DN_HEAD_DIM
DN_CONV_WIDTH = 2 * DN_KEY_WIDTH + DN_VAL_WIDTH
DN_PREP_TILE = 128
DN_SCAN_TILE = 256


def _split_bf16(a):
    hi = a.astype(BF16)
    return hi, (a - hi.astype(F32)).astype(BF16)


def _mm(a, b):
    return jnp.dot(a.astype(BF16), b.astype(BF16), preferred_element_type=F32)


def _dn_proj_kernel(x_ref, mod_ref, gain_ref, w_ref, wba_ref, conv_ref, rate_ref, dtb_ref,
                    q_ref, k_ref, v_ref, z_ref, bg_ref, ext_ref):
    t = pl.program_id(1)
    tm = x_ref.shape[0]
    pad = SUBLANES

    @pl.when(t == 0)
    def _():
        ext_ref[0:pad, :] = jnp.zeros((pad, ext_ref.shape[1]), F32)

    mod = mod_ref[0]
    h = _modulated_norm(x_ref[...], gain_ref[...], mod[0:1], mod[1:2])
    hh, hl = _split_bf16(h)
    proj = jnp.dot(hh, w_ref[...], preferred_element_type=F32)
    z_ref[...] = proj[:, DN_CONV_WIDTH:].astype(BF16)

    ext_ref[pad:pad + tm, :] = proj[:, :DN_CONV_WIDTH]
    acc = None
    for j in range(DN_CONV):
        start = pad - (DN_CONV - 1) + j
        term = conv_ref[j:j + 1, :] * ext_ref[start:start + tm, :]
        acc = term if acc is None else acc + term
    ext_ref[0:pad, :] = ext_ref[tm:tm + pad, :]
    qkv = _silu(acc)
    for hq in range(2 * DN_QK_HEADS):
        cols = slice(hq * DN_HEAD_DIM, (hq + 1) * DN_HEAD_DIM)
        a = qkv[:, cols]
        a = a * lax.rsqrt(jnp.sum(a * a, axis=-1, keepdims=True) + NORM_EPS)
        if hq < DN_QK_HEADS:
            q_ref[:, cols] = (a * DN_HEAD_DIM ** -0.5).astype(BF16)
        else:
            k_ref[:, hq * DN_HEAD_DIM - DN_KEY_WIDTH:(hq + 1) * DN_HEAD_DIM - DN_KEY_WIDTH] = a.astype(BF16)
    v_ref[...] = qkv[:, 2 * DN_KEY_WIDTH:].astype(BF16)

    ba = jnp.dot(hh, wba_ref[...], preferred_element_type=F32) + jnp.dot(hl, wba_ref[...], preferred_element_type=F32)
    ba = ba + pltpu.roll(ba, LANES - 2 * DN_V_HEADS, 1)
    lane = lax.broadcasted_iota(jnp.int32, ba.shape, 1)
    beta = 1.0 / (1.0 + jnp.exp(-ba))
    sp = ba + dtb_ref[...]
    g = -rate_ref[...] * (jnp.maximum(sp, 0.0) + jnp.log1p(jnp.exp(-jnp.abs(sp))))
    g = jnp.where(jnp.logical_and(lane >= DN_V_HEADS, lane < 2 * DN_V_HEADS), g, 0.0)
    r = lax.broadcasted_iota(jnp.int32, (tm, tm), 0)
    c = lax.broadcasted_iota(jnp.int32, (tm, tm), 1)
    tri = jnp.where(jnp.logical_and(r // DN_CHUNK == c // DN_CHUNK, c <= r), 1.0, 0.0).astype(BF16)
    g_hi = g.astype(BF16)
    g_rest = g - g_hi.astype(F32)
    g_mid = g_rest.astype(BF16)
    g_lo = (g_rest - g_mid.astype(F32)).astype(BF16)
    gc = sum(jnp.dot(tri, piece, preferred_element_type=F32) for piece in (g_hi, g_mid, g_lo))
    bg_ref[...] = jnp.where(lane < DN_V_HEADS, beta, gc)


def _unit_lower_inverses(mats):
    n = mats[0].shape[0]
    r = lax.broadcasted_iota(jnp.int32, (n, n), 0)
    c = lax.broadcasted_iota(jnp.int32, (n, n), 1)
    eye = (r == c).astype(F32)
    size = SUBLANES
    same = (r // size) == (c // size)
    d = [jnp.where(same, a, 0.0) for a in mats]
    d2 = [_mm(v, v) for v in d]
    d4 = [_mm(v, v) for v in d2]
    x = [_mm(eye - v, eye + v2) for v, v2 in zip(d, d2)]
    x = [_mm(v, eye + v4) for v, v4 in zip(x, d4)]
    while size < n:
        wider = (r // (2 * size)) == (c // (2 * size))
        ring = jnp.logical_and(wider, jnp.logical_not(same))
        xl = [_mm(v, jnp.where(ring, a, 0.0)) for v, a in zip(x, mats)]
        x = [v - _mm(vl, v) for v, vl in zip(x, xl)]
        same = wider
        size *= 2
    return x


def _dn_prep_kernel(q_ref, k_ref, v_ref, bg_ref, gct_ref, u_ref, w_ref, qk_ref):
    tm = q_ref.shape[0]
    ck = DN_CHUNK
    hd = DN_HEAD_DIM
    rep = DN_V_HEADS // DN_QK_HEADS
    r = lax.broadcasted_iota(jnp.int32, (ck, ck), 0)
    c = lax.broadcasted_iota(jnp.int32, (ck, ck), 1)
    lower = c <= r
    strict = c < r
    dot = functools.partial(jnp.dot, preferred_element_type=F32)
    chunks = [slice(ci * ck, (ci + 1) * ck) for ci in range(tm // ck)]
    gram = {}
    for rows in chunks:
        for hq in range(DN_QK_HEADS):
            cols = slice(hq * hd, (hq + 1) * hd)
            k = k_ref[rows, cols]
            gram[(hq, rows.start)] = lax.dot_general(jnp.concatenate([k, q_ref[rows, cols]], axis=0), k,
                                                     (((1,), (1,)), ((), ())), preferred_element_type=F32)
    problems = [(hv, rows) for rows in chunks for hv in range(DN_V_HEADS)]
    mats, rhs = [], []
    for hv, rows in problems:
        beta = bg_ref[rows, hv:hv + 1]
        gcc = bg_ref[rows, DN_V_HEADS + hv:DN_V_HEADS + hv + 1]
        gcr = gct_ref[hv:hv + 1, rows]
        decay = jnp.where(lower, jnp.exp(jnp.where(lower, gcc - gcr, 0.0)), 0.0)
        g = gram[(hv // rep, rows.start)]
        mats.append(jnp.where(strict, g[:ck] * beta * decay, 0.0))
        qk_ref[rows, hv * ck:(hv + 1) * ck] = (g[ck:] * decay).astype(BF16)
        kf = k_ref[rows, (hv // rep) * hd:(hv // rep + 1) * hd].astype(F32)
        vf = v_ref[rows, hv * hd:(hv + 1) * hd].astype(F32)
        rhs.append(jnp.concatenate([vf * beta, kf * (beta * jnp.exp(gcc))], axis=1).astype(BF16))
    inverses = _unit_lower_inverses(mats)
    for (hv, rows), tinv, b in zip(problems, inverses, rhs):
        uw = dot(tinv.astype(BF16), b)
        u_ref[rows, hv * hd:(hv + 1) * hd] = uw[:, :hd].astype(BF16)
        w_ref[rows, hv * hd:(hv + 1) * hd] = uw[:, hd:].astype(BF16)


def _dn_scan_kernel(x_ref, mod_ref, u_ref, w_ref, qk_ref, q_ref, k_ref, z_ref, bg_ref, og_ref, wout_ref,
                    fgain_ref, rw_ref, rb_ref, before_ref, o_ref, eid_ref, gate_ref, rank_ref, cnt_ref,
                    state_ref, att_ref, base_ref):
    t = pl.program_id(0)
    nb, tm, _ = x_ref.shape
    ck = DN_CHUNK
    hd = DN_HEAD_DIM
    rep = DN_V_HEADS // DN_QK_HEADS

    @pl.when(t == 0)
    def _():
        state_ref[...] = jnp.zeros_like(state_ref)

    dot = functools.partial(jnp.dot, preferred_element_type=F32)
    chains = [(bi, hv) for bi in range(nb) for hv in range(DN_V_HEADS)]
    for ci in range(tm // ck):
        rows = slice(ci * ck, (ci + 1) * ck)
        gcc, g_end, lhs = [], [], []
        for bi, hv in chains:
            gc = bg_ref[bi, rows, DN_V_HEADS + hv:DN_V_HEADS + hv + 1]
            gcc.append(gc)
            g_end.append(gc[ck - 1:ck, :])
            q_dec = q_ref[bi, rows, (hv // rep) * hd:(hv // rep + 1) * hd].astype(F32) * jnp.exp(gc)
            lhs.append(jnp.concatenate([w_ref[bi, rows, hv * hd:(hv + 1) * hd], q_dec.astype(BF16)], axis=0))
        ws_qs = [dot(a, state_ref[n].astype(BF16)) for n, a in enumerate(lhs)]
        v_new = [(u_ref[bi, rows, hv * hd:(hv + 1) * hd].astype(F32) - m[:ck]).astype(BF16)
                 for (bi, hv), m in zip(chains, ws_qs)]
        outs = [m[ck:] + dot(qk_ref[bi, rows, hv * ck:(hv + 1) * ck], vn)
                for (bi, hv), m, vn in zip(chains, ws_qs, v_new)]
        for n, (bi, hv) in enumerate(chains):
            kf = k_ref[bi, rows, (hv // rep) * hd:(hv // rep + 1) * hd].astype(F32)
            k_dec = (kf * jnp.exp(g_end[n] - gcc[n])).astype(BF16)
            state_ref[n] = state_ref[n] * jnp.exp(g_end[n]) + lax.dot_general(
                k_dec, v_new[n], (((0,), (0,)), ((), ())), preferred_element_type=F32)
        for (bi, hv), o in zip(chains, outs):
            zf = z_ref[bi, rows, hv * hd:(hv + 1) * hd].astype(F32)
            att_ref[bi * tm + ci * ck:bi * tm + (ci + 1) * ck, hv * hd:(hv + 1) * hd] = (
                _rms(o) * og_ref[...] * _silu(zf)).astype(BF16)
    y = dot(att_ref[...], wout_ref[...])
    x_new, shift, scale = [], [], []
    for bi in range(nb):
        mod = mod_ref[bi]
        x_new.append(x_ref[bi] + mod[2:3] * y[bi * tm:(bi + 1) * tm])
        o_ref[bi] = x_new[bi]
        shift.append(jnp.broadcast_to(mod[3:4], x_new[bi].shape))
        scale.append(jnp.broadcast_to(mod[4:5], x_new[bi].shape))
    eid, gates, rank = _route(jnp.concatenate(x_new, axis=0), jnp.concatenate(shift, axis=0),
                              jnp.concatenate(scale, axis=0), t == 0, fgain_ref, rw_ref, rb_ref, before_ref,
                              cnt_ref, base_ref)
    for bi in range(nb):
        eid_ref[:, bi, :] = eid[:, bi * tm:(bi + 1) * tm]
        gate_ref[:, bi, :] = gates[:, bi * tm:(bi + 1) * tm]
        rank_ref[:, bi, :] = rank[:, bi * tm:(bi + 1) * tm]


def deltanet_layer(x2, mod, gain, w_in, conv_w, a_log, dt_bias, o_norm, w_out, router, b, s):
    t, d = x2.shape
    nh = DN_V_HEADS
    main_w = DN_CONV_WIDTH + DN_VAL_WIDTH
    w_main = w_in[:, :main_w].astype(BF16)
    w_ba = w_in[:, main_w:]
    w_ba_hi = w_ba.astype(BF16)
    w_ba_lo = (w_ba - w_ba_hi.astype(F32)).astype(BF16)
    w_ba2 = jnp.concatenate([w_ba_hi, w_ba_lo, jnp.zeros((d, LANES - 4 * nh), BF16)], axis=1)
    lanes_pad = lambda v: jnp.concatenate([jnp.zeros((nh,), F32), v.astype(F32),
                                           jnp.zeros((LANES - 2 * nh,), F32)]).reshape(1, LANES)
    rate = lanes_pad(jnp.exp(a_log.astype(F32)))
    dtb = lanes_pad(dt_bias)

    tm = TOKEN_TILE
    tpb = s // tm
    const2 = lambda bi, ti: (0, 0)
    tile = lambda bi, ti: (bi * tpb + ti, 0)
    q, k, v, z, bg = pl.pallas_call(
        _dn_proj_kernel,
        grid=(b, tpb),
        in_specs=[pl.BlockSpec((tm, d), tile),
                  pl.BlockSpec((1, 6, d), lambda bi, ti: (bi, 0, 0)),
                  pl.BlockSpec((1, d), const2),
                  pl.BlockSpec((d, main_w), const2),
                  pl.BlockSpec((d, LANES), const2),
                  pl.BlockSpec((DN_CONV, DN_CONV_WIDTH), const2),
                  pl.BlockSpec((1, LANES), const2),
                  pl.BlockSpec((1, LANES), const2)],
        out_specs=[pl.BlockSpec((tm, DN_KEY_WIDTH), tile), pl.BlockSpec((tm, DN_KEY_WIDTH), tile),
                   pl.BlockSpec((tm, DN_VAL_WIDTH), tile), pl.BlockSpec((tm, DN_VAL_WIDTH), tile),
                   pl.BlockSpec((tm, LANES), tile)],
        out_shape=[jax.ShapeDtypeStruct((t, DN_KEY_WIDTH), BF16), jax.ShapeDtypeStruct((t, DN_KEY_WIDTH), BF16),
                   jax.ShapeDtypeStruct((t, DN_VAL_WIDTH), BF16), jax.ShapeDtypeStruct((t, DN_VAL_WIDTH), BF16),
                   jax.ShapeDtypeStruct((t, LANES), F32)],
        scratch_shapes=[pltpu.VMEM((tm + SUBLANES, DN_CONV_WIDTH), F32)],
        compiler_params=_params("arbitrary", "arbitrary"),
        name="deltanet_proj",
    )(x2, mod, gain.reshape(1, d), w_main, w_ba2, conv_w, rate, dtb)

    gct = bg[:, nh:2 * nh].T
    tp = DN_PREP_TILE
    rows_of = lambda width: pl.BlockSpec((tp, width), lambda i: (i, 0))
    qk_w = nh * DN_CHUNK
    u, w, qk = pl.pallas_call(
        _dn_prep_kernel,
        grid=(t // tp,),
        in_specs=[rows_of(DN_KEY_WIDTH), rows_of(DN_KEY_WIDTH), rows_of(DN_VAL_WIDTH), rows_of(LANES),
                  pl.BlockSpec((nh, tp), lambda i: (0, i))],
        out_specs=[rows_of(DN_VAL_WIDTH), rows_of(DN_VAL_WIDTH), rows_of(qk_w)],
        out_shape=[jax.ShapeDtypeStruct((t, DN_VAL_WIDTH), BF16), jax.ShapeDtypeStruct((t, DN_VAL_WIDTH), BF16),
                   jax.ShapeDtypeStruct((t, qk_w), BF16)],
        compiler_params=_params("arbitrary"),
        name="deltanet_prep",
    )(q, k, v, bg, gct)

    ts = DN_SCAN_TILE
    seq = lambda a: a.reshape(b, s, a.shape[-1])
    both = lambda width: pl.BlockSpec((b, ts, width), lambda i: (0, i, 0))
    r_args, r_in = _router_operands(*router, b * ts)
    r_out, r_shapes, r_scratch = _router_results(t, pl.BlockSpec((2, b, ts), lambda i: (0, 0, i)), (2, b, s))
    out = pl.pallas_call(
        _dn_scan_kernel,
        grid=(s // ts,),
        in_specs=[both(d),
                  pl.BlockSpec((b, 6, d), lambda i: (0, 0, 0)),
                  both(DN_VAL_WIDTH), both(DN_VAL_WIDTH), both(qk_w), both(DN_KEY_WIDTH), both(DN_KEY_WIDTH),
                  both(DN_VAL_WIDTH), both(LANES),
                  pl.BlockSpec((1, DN_HEAD_DIM), lambda i: (0, 0)),
                  pl.BlockSpec((DN_VAL_WIDTH, d), lambda i: (0, 0))] + r_in,
        out_specs=[both(d)] + r_out,
        out_shape=[jax.ShapeDtypeStruct((b, s, d), F32)] + r_shapes,
        scratch_shapes=[pltpu.VMEM((b * nh, DN_HEAD_DIM, DN_HEAD_DIM), F32),
                        pltpu.VMEM((b * ts, DN_VAL_WIDTH), BF16), r_scratch],
        compiler_params=_params("arbitrary"),
        name="deltanet_scan",
    )(seq(x2), mod, seq(u), seq(w), seq(qk), seq(q), seq(k), seq(z), seq(bg), o_norm.reshape(1, DN_HEAD_DIM),
      w_out.astype(BF16), *r_args)
    eid, gate, rank, cnt = out[1:]
    return out[0].reshape(t, d), (eid.reshape(2, t), gate.reshape(2, t), rank.reshape(2, t), cnt)


def kernel(x, c, l0_norm_mix, l0_norm_ffn, l0_ada_w, l0_ada_b, l0_gm_w_in, l0_gm_v_norm, l0_gm_w_s, l0_gm_b_s, l0_gm_w_out, l0_router_w, l0_router_b, l0_expert_w_in, l0_expert_w_out, l1_norm_mix, l1_norm_ffn, l1_ada_w, l1_ada_b, l1_dn_w_in, l1_dn_conv_w, l1_dn_a_log, l1_dn_dt_bias, l1_dn_o_norm, l1_dn_w_out, l1_router_w, l1_router_b, l1_expert_w_in, l1_expert_w_out, l2_norm_mix, l2_norm_ffn, l2_ada_w, l2_ada_b, l2_swa_w_in, l2_swa_q_norm, l2_swa_k_norm, l2_swa_sinks, l2_swa_w_out, l2_router_w, l2_router_b, l2_expert_w_in, l2_expert_w_out, l3_norm_mix, l3_norm_ffn, l3_ada_w, l3_ada_b, l3_gm_w_in, l3_gm_v_norm, l3_gm_w_s, l3_gm_b_s, l3_gm_w_out, l3_router_w, l3_router_b, l3_expert_w_in, l3_expert_w_out):
    b, s, d = x.shape
    x2 = x.reshape(b * s, d)
    tiles_per_batch_of = lambda tm: s // tm

    def modulation(ada_w, ada_b):
        return adaln(c, ada_w, ada_b).reshape(b, 6, d)

    mod = modulation(l0_ada_w, l0_ada_b)
    x2, routing = gmlp_layer(x2, mod, l0_norm_mix, l0_gm_w_in, l0_gm_v_norm, l0_gm_w_s, l0_gm_b_s, l0_gm_w_out,
                             (l0_norm_ffn, l0_router_w, l0_router_b), tiles_per_batch_of(TOKEN_TILE))
    x2 = moe_layer(x2, mod, l0_norm_ffn, routing, l0_expert_w_in, l0_expert_w_out, tiles_per_batch_of)

    mod = modulation(l1_ada_w, l1_ada_b)
    x2, routing = deltanet_layer(x2, mod, l1_norm_mix, l1_dn_w_in, l1_dn_conv_w, l1_dn_a_log, l1_dn_dt_bias,
                                 l1_dn_o_norm, l1_dn_w_out, (l1_norm_ffn, l1_router_w, l1_router_b), b, s)
    x2 = moe_layer(x2, mod, l1_norm_ffn, routing, l1_expert_w_in, l1_expert_w_out, tiles_per_batch_of)

    mod = modulation(l2_ada_w, l2_ada_b)
    x2, routing = swa_layer(x2, mod, l2_norm_mix, l2_swa_w_in, l2_swa_q_norm, l2_swa_k_norm, l2_swa_sinks,
                            l2_swa_w_out, (l2_norm_ffn, l2_router_w, l2_router_b), b, s)
    x2 = moe_layer(x2, mod, l2_norm_ffn, routing, l2_expert_w_in, l2_expert_w_out, tiles_per_batch_of)

    mod = modulation(l3_ada_w, l3_ada_b)
    x2, routing = gmlp_layer(x2, mod, l3_norm_mix, l3_gm_w_in, l3_gm_v_norm, l3_gm_w_s, l3_gm_b_s, l3_gm_w_out,
                             (l3_norm_ffn, l3_router_w, l3_router_b), tiles_per_batch_of(TOKEN_TILE))
    x2 = moe_layer(x2, mod, l3_norm_ffn, routing, l3_expert_w_in, l3_expert_w_out, tiles_per_batch_of)
    return x2.reshape(b, s, d)
```

```python
import functools
import math

import jax
import jax.numpy as jnp
from jax import lax
from jax.experimental import pallas as pl
from jax.experimental.pallas import tpu as pltpu

F32 = jnp.float32
BF16 = jnp.bfloat16

D_MODEL = 1024
NORM_EPS = 1e-6

GM_CHUNK = 128
GM_GROUPS = 8
GM_GROUP_DIM = D_MODEL // GM_GROUPS

MOE_GROUPS = 4
MOE_EXPERTS_PER_GROUP = 8
MOE_EXPERTS = MOE_GROUPS * MOE_EXPERTS_PER_GROUP
MOE_FF = D_MODEL // 2

VMEM_LIMIT_BYTES = 56 * 1024 * 1024

TOKEN_TILE = 512
MOVE_TILE = 512
EXPERT_ROWS = 512
EXPERT_SUB_ROWS = 128
LANES = 128
SUBLANES = 8


def _params(*semantics):
    return pltpu.CompilerParams(dimension_semantics=semantics, vmem_limit_bytes=VMEM_LIMIT_BYTES,
                                disable_bounds_checks=True)


def _rms(xf):
    return xf * lax.rsqrt(jnp.mean(xf * xf, axis=-1, keepdims=True) + NORM_EPS)


def _modulated_norm(x, gain, shift, scale):
    return _rms(x) * gain * (1.0 + scale) + shift


def _gelu_tanh(x):
    return 0.5 * x * (1.0 + jnp.tanh(math.sqrt(2.0 / math.pi) * (x + 0.044715 * (x * x * x))))


def _silu(x):
    return x * (1.0 / (1.0 + jnp.exp(-x)))


def _adaln_kernel(ct_ref, w_ref, b_ref, o_ref):
    act = _silu(ct_ref[...])
    w = w_ref[...]
    o_ref[...] = jnp.concatenate([jnp.sum(act[:, r:r + 1] * w, axis=0, keepdims=True)
                                  for r in range(o_ref.shape[0])], axis=0) + b_ref[...]


def adaln(c, ada_w, ada_b):
    rows, d = c.shape
    n = ada_w.shape[1]
    tn = d
    return pl.pallas_call(
        _adaln_kernel,
        grid=(n // tn,),
        in_specs=[pl.BlockSpec((d, rows), lambda j: (0, 0)),
                  pl.BlockSpec((d, tn), lambda j: (0, j)),
                  pl.BlockSpec((1, tn), lambda j: (0, j))],
        out_specs=pl.BlockSpec((rows, tn), lambda j: (0, j)),
        out_shape=jax.ShapeDtypeStruct((rows, n), F32),
        compiler_params=_params("arbitrary"),
        name="adaln",
    )(c.T, ada_w, ada_b.reshape(1, n))


def _gmlp_kernel(x_ref, mod_ref, gain_ref, win_ref, vn_ref, ws_ref, bs_ref, wout_ref, fgain_ref, rw_ref, rb_ref, before_ref,
                 o_ref, eid_ref, gate_ref, rank_ref, cnt_ref, u_ref, v_ref, g_ref, base_ref):
    width = D_MODEL
    x = x_ref[...]
    mod = mod_ref[0]
    h = _modulated_norm(x, gain_ref[...], mod[0:1], mod[1:2])
    z = _gelu_tanh(jnp.dot(h.astype(BF16), win_ref[...], preferred_element_type=F32))
    u_ref[...] = z[:, :width]
    v_ref[...] = (_rms(z[:, width:]) * vn_ref[...]).astype(BF16)
    cells = [(slice(c * GM_CHUNK, (c + 1) * GM_CHUNK), g, slice(g * GM_GROUP_DIM, (g + 1) * GM_GROUP_DIM))
             for c in range(x.shape[0] // GM_CHUNK) for g in range(GM_GROUPS)]
    mixed = [jnp.dot(ws_ref[g], v_ref[rows, cols], preferred_element_type=F32) for rows, g, cols in cells]
    for (rows, g, cols), sv in zip(cells, mixed):
        g_ref[rows, cols] = (u_ref[rows, cols] * (sv + bs_ref[:, g:g + 1])).astype(BF16)
    y = jnp.dot(g_ref[...], wout_ref[...], preferred_element_type=F32)
    x_new = x + mod[2:3] * y
    o_ref[...] = x_new
    eid_ref[...], gate_ref[...], rank_ref[...] = _route(
        x_new, mod[3:4], mod[4:5], pl.program_id(0) == 0, fgain_ref, rw_ref, rb_ref, before_ref, cnt_ref, base_ref)


def gmlp_layer(x2, mod, gain, w_in, v_norm, w_s, b_s, w_out, router, tiles_per_batch):
    t, d = x2.shape
    tm = TOKEN_TILE
    ws_causal = jnp.where(jnp.tril(jnp.ones((GM_CHUNK, GM_CHUNK), dtype=bool)), w_s, 0).astype(BF16)
    const2 = lambda i: (0, 0)
    r_args, r_in = _router_operands(*router, tm)
    r_out, r_shapes, r_scratch = _router_results(t, pl.BlockSpec((2, tm), lambda i: (0, i)), (2, t))
    out = pl.pallas_call(
        _gmlp_kernel,
        grid=(t // tm,),
        in_specs=[pl.BlockSpec((tm, d), lambda i: (i, 0)),
                  pl.BlockSpec((1, 6, d), lambda i: (i // tiles_per_batch, 0, 0)),
                  pl.BlockSpec((1, d), const2),
                  pl.BlockSpec((d, 2 * d), const2),
                  pl.BlockSpec((1, d), const2),
                  pl.BlockSpec((GM_GROUPS, GM_CHUNK, GM_CHUNK), lambda i: (0, 0, 0)),
                  pl.BlockSpec((GM_CHUNK, GM_GROUPS), const2),
                  pl.BlockSpec((d, d), const2)] + r_in,
        out_specs=[pl.BlockSpec((tm, d), lambda i: (i, 0))] + r_out,
        out_shape=[jax.ShapeDtypeStruct((t, d), F32)] + r_shapes,
        scratch_shapes=[pltpu.VMEM((tm, d), F32), pltpu.VMEM((tm, d), BF16), pltpu.VMEM((tm, d), BF16), r_scratch],
        compiler_params=_params("arbitrary"),
        name="gmlp_mixer",
    )(x2, mod, gain.reshape(1, d), w_in.astype(BF16), v_norm.reshape(1, d), ws_causal, b_s.T,
      w_out.astype(BF16), *r_args)
    return out[0], tuple(out[1:])


def _route(x, shift, scale, first, gain_ref, rw_ref, rb_ref, before_ref, cnt_ref, base_ref):
    tm = x.shape[0]
    ne = MOE_EXPERTS
    npg = MOE_EXPERTS_PER_GROUP

    @pl.when(first)
    def _():
        base_ref[...] = jnp.zeros_like(base_ref)

    hh, hl = _split_bf16(_modulated_norm(x, gain_ref[...], shift, scale))
    nt = (((1,), (1,)), ((), ()))
    nr = ROUTER_ROWS
    by_hi = lax.dot_general(rw_ref[...], hh, nt, preferred_element_type=F32)
    by_lo = lax.dot_general(rw_ref[0:nr, :], hl, nt, preferred_element_type=F32)
    lt = (by_hi[0:nr] + rb_ref[...]) + (by_hi[nr:] + by_lo)
    lg = [lt[ne + g:ne + g + 1, :] for g in range(MOE_GROUPS)]
    gmax = functools.reduce(jnp.maximum, lg)
    gsum = functools.reduce(lambda a, b: a + b, [jnp.exp(l - gmax) for l in lg])
    pg_top = 1.0 / gsum
    g_sel = jnp.full(gmax.shape, MOE_GROUPS - 1, jnp.int32)
    for g in range(MOE_GROUPS - 2, -1, -1):
        g_sel = jnp.where(lg[g] == gmax, g, g_sel)
    sel = lt[(MOE_GROUPS - 1) * npg:MOE_GROUPS * npg, :]
    for g in range(MOE_GROUPS - 2, -1, -1):
        sel = jnp.where(g_sel == g, lt[g * npg:(g + 1) * npg, :], sel)
    row = lax.broadcasted_iota(jnp.int32, sel.shape, 0)
    m1 = jnp.max(sel, axis=0, keepdims=True)
    i1 = jnp.min(jnp.where(sel == m1, row, npg), axis=0, keepdims=True)
    rest = jnp.where(row == i1, -jnp.inf, sel)
    m2 = jnp.max(rest, axis=0, keepdims=True)
    i2 = jnp.min(jnp.where(rest == m2, row, npg), axis=0, keepdims=True)
    e2 = jnp.exp(m2 - m1)
    inv = pg_top / (1.0 + e2)
    eid = jnp.concatenate([g_sel * npg + i1, g_sel * npg + i2], axis=0)
    gates = jnp.concatenate([inv, inv * e2], axis=0)

    erow = lax.broadcasted_iota(jnp.int32, (ne, tm), 0)
    hits = [erow == eid[k:k + 1, :] for k in range(2)]
    prefix = jnp.dot(jnp.concatenate([jnp.where(hit, 1.0, 0.0).astype(BF16) for hit in hits], axis=0),
                     before_ref[...], preferred_element_type=F32)
    base = base_ref[...]
    ranks = []
    for k, hit in enumerate(hits):
        ranks.append(jnp.sum(jnp.where(hit, prefix[k * ne:(k + 1) * ne] + base, 0.0), axis=0, keepdims=True))
        base = base + jnp.sum(jnp.where(hit, 1.0, 0.0), axis=1, keepdims=True)
    base_ref[...] = base
    cnt_ref[...] = jnp.broadcast_to(base, cnt_ref.shape).astype(jnp.int32)
    return eid, gates, jnp.concatenate(ranks, axis=0).astype(jnp.int32)


ROUTER_ROWS = MOE_EXPERTS + 2 * SUBLANES


def _router_operands(gain, router_w, router_b, tm):
    d = router_w.shape[0]
    pad = ROUTER_ROWS - MOE_EXPERTS - MOE_GROUPS
    rw = jnp.concatenate([router_w[:, MOE_GROUPS:], router_w[:, :MOE_GROUPS], jnp.zeros((d, pad), F32)], axis=1).T
    rw_hi = rw.astype(BF16)
    rw2 = jnp.concatenate([rw_hi, (rw - rw_hi.astype(F32)).astype(BF16)], axis=0)
    rb = jnp.concatenate([router_b[MOE_GROUPS:], router_b[:MOE_GROUPS], jnp.zeros((pad,), F32)])
    pos = jnp.arange(tm)
    before = (pos[:, None] < pos[None, :]).astype(BF16)
    const = lambda *_: (0, 0)
    specs = [pl.BlockSpec((1, d), const), pl.BlockSpec((2 * ROUTER_ROWS, d), const),
             pl.BlockSpec((ROUTER_ROWS, 1), const), pl.BlockSpec((tm, tm), const)]
    return [gain.reshape(1, d), rw2, rb.reshape(ROUTER_ROWS, 1), before], specs


def _router_results(t, slot_spec, slot_shape):
    specs = [slot_spec, slot_spec, slot_spec, pl.BlockSpec((MOE_EXPERTS, LANES), lambda *_: (0, 0))]
    shapes = [jax.ShapeDtypeStruct(slot_shape, jnp.int32), jax.ShapeDtypeStruct(slot_shape, F32),
              jax.ShapeDtypeStruct(slot_shape, jnp.int32), jax.ShapeDtypeStruct((MOE_EXPERTS, LANES), jnp.int32)]
    return specs, shapes, pltpu.VMEM((MOE_EXPERTS, 1), F32)


def _wait_rows(hbm_ref, n_rows, sem, times):
    rows = hbm_ref.at[pl.ds(0, n_rows)]
    for _ in range(times):
        pltpu.make_async_copy(rows, rows, sem).wait()


def _tiled_shape(n, d):
    return (n // SUBLANES, d // LANES, SUBLANES, LANES)


def _store_tiled(ref, a):
    n, d = a.shape
    for j in range(d // LANES):
        ref[:, j, :, :] = a[:, j * LANES:(j + 1) * LANES].reshape(n // SUBLANES, SUBLANES, LANES)


def _load_tiled(ref):
    g, nj, _, _ = ref.shape
    return jnp.concatenate([ref[:, j, :, :].reshape(g * SUBLANES, LANES) for j in range(nj)], axis=1)


def _dispatch_kernel(d0_ref, d1_ref, x_ref, mod_ref, gain_ref, xs_ref, h_ref, sem):
    i = pl.program_id(0)
    last = pl.num_programs(0) - 1
    tm, d = x_ref.shape
    mod = mod_ref[0]
    h = _modulated_norm(x_ref[...], gain_ref[...], mod[3:4], mod[4:5])

    def step(buf):
        _store_tiled(h_ref.at[buf], h)

        def issue(g, carry):
            for u in range(SUBLANES):
                for k, dest_ref in enumerate((d0_ref, d1_ref)):
                    pltpu.make_async_copy(h_ref.at[buf, g, :, u, :], xs_ref.at[dest_ref[g * SUBLANES + u]],
                                          sem.at[buf]).start(priority=k)
            return carry

        lax.fori_loop(0, tm // SUBLANES, issue, 0)

        @pl.when(i > 0)
        def _():
            _wait_rows(xs_ref, tm, sem.at[1 - buf], 2)

        @pl.when(i == last)
        def _():
            _wait_rows(xs_ref, tm, sem.at[buf], 2)

    for buf in range(2):
        pl.when(i % 2 == buf)(functools.partial(step, buf))


def moe_dispatch(x2, mod, gain, dest, tiles_per_batch_of):
    t, d = x2.shape
    tm = MOVE_TILE
    tpb = tiles_per_batch_of(tm)
    const2 = lambda i: (0, 0)
    slot_spec = pl.BlockSpec((tm,), lambda i: (i,), memory_space=pltpu.SMEM)
    return pl.pallas_call(
        _dispatch_kernel,
        grid=(t // tm,),
        in_specs=[slot_spec, slot_spec,
                  pl.BlockSpec((tm, d), lambda i: (i, 0)),
                  pl.BlockSpec((1, 6, d), lambda i: (i // tpb, 0, 0)),
                  pl.BlockSpec((1, d), const2)],
        out_specs=pl.BlockSpec(memory_space=pl.ANY),
        out_shape=jax.ShapeDtypeStruct((2 * t, d // LANES, LANES), F32),
        scratch_shapes=[pltpu.VMEM((2,) + _tiled_shape(tm, d), F32), pltpu.SemaphoreType.DMA((2,))],
        compiler_params=_params("arbitrary"),
        name="moe_dispatch",
    )(dest[0], dest[1], x2, mod, gain.reshape(1, d))


def _expert_kernel(blk_ref, exp_ref, start_ref, nvalid_ref, xs_ref, win_ref, wout_ref, ys_ref,
                   winb_ref, woutb_ref, xbuf_ref, ybuf_ref, in_sem, out_sem):
    i = pl.program_id(0)
    n_items = pl.num_programs(0)
    bm = EXPERT_ROWS
    groups = bm // SUBLANES
    n_blocks = xs_ref.shape[0] // groups
    e = exp_ref[i]
    blk = blk_ref[i]
    prev = jnp.maximum(i - 1, 0)
    nxt = jnp.minimum(i + 1, n_items - 1)
    first = i == 0
    final = i == n_items - 1
    lo = start_ref[e]
    hi = start_ref[e + 1]
    row0 = blk * bm
    live = i < nvalid_ref[0]
    slot = blk % 2

    def block_copies(hbm_ref, buf_ref, b, s, sem, to_hbm):
        copies = []
        for u in range(SUBLANES):
            hbm = hbm_ref.at[pl.ds(b * groups, groups), u]
            vmem = buf_ref.at[s, :, :, u, :]
            copies.append(pltpu.make_async_copy(vmem, hbm, sem.at[s]) if to_hbm
                          else pltpu.make_async_copy(hbm, vmem, sem.at[s]))
        return copies

    def fetch(b, s):
        for c in block_copies(xs_ref, xbuf_ref, b, s, in_sem, False):
            c.start()

    @pl.when(jnp.logical_or(first, exp_ref[prev] != e))
    def _():
        winb_ref[...] = win_ref[0].astype(BF16)
        woutb_ref[...] = wout_ref[0].astype(BF16)

    new_block = jnp.logical_or(first, blk_ref[prev] != blk)
    last_of_block = jnp.logical_or(final, blk_ref[nxt] != blk)
    whole = jnp.logical_and(lo <= row0, hi >= row0 + bm)

    @pl.when(first)
    def _():
        fetch(0, 0)

    @pl.when(new_block)
    def _():
        for c in block_copies(xs_ref, xbuf_ref, blk, slot, in_sem, False):
            c.wait()

        @pl.when(blk + 1 < n_blocks)
        def _():
            fetch(blk + 1, 1 - slot)

        @pl.when(blk >= 2)
        def _():
            for c in block_copies(ys_ref, ybuf_ref, blk - 2, slot, out_sem, True):
                c.wait()

    def ffn(x):
        a_gl = jnp.dot(x.astype(BF16), winb_ref[...], preferred_element_type=F32)
        mid = (_silu(a_gl[:, :MOE_FF]) * a_gl[:, MOE_FF:]).astype(BF16)
        return jnp.dot(mid, woutb_ref[...], preferred_element_type=F32)

    @pl.when(jnp.logical_and(live, whole))
    def _():
        _store_tiled(ybuf_ref.at[slot], ffn(_load_tiled(xbuf_ref.at[slot])))

    @pl.when(jnp.logical_and(new_block, jnp.logical_not(whole)))
    def _():
        ybuf_ref[slot] = jnp.zeros(ybuf_ref.shape[1:], F32)

    sub_groups = EXPERT_SUB_ROWS // SUBLANES
    for sub in range(bm // EXPERT_SUB_ROWS):
        sub0 = row0 + sub * EXPERT_SUB_ROWS
        touched = jnp.logical_and(hi > sub0, lo < sub0 + EXPERT_SUB_ROWS)

        @pl.when(jnp.logical_and(jnp.logical_and(live, jnp.logical_not(whole)), touched))
        def _():
            gs = pl.ds(sub * sub_groups, sub_groups)
            rows = sub0 + lax.broadcasted_iota(jnp.int32, (EXPERT_SUB_ROWS, 1), 0)
            mine = jnp.logical_and(rows >= lo, rows < hi)
            y = jnp.where(mine, ffn(_load_tiled(xbuf_ref.at[slot, gs])), 0.0)
            _store_tiled(ybuf_ref.at[slot, gs], _load_tiled(ybuf_ref.at[slot, gs]) + y)

    @pl.when(last_of_block)
    def _():
        for c in block_copies(ys_ref, ybuf_ref, blk, slot, out_sem, True):
            c.start()

    @pl.when(final)
    def _():
        for c in block_copies(ys_ref, ybuf_ref, blk, slot, out_sem, True):
            c.wait()

        @pl.when(blk >= 1)
        def _():
            for c in block_copies(ys_ref, ybuf_ref, blk - 1, 1 - slot, out_sem, True):
                c.wait()


def moe_experts(xs, w_in, w_out, item_block, item_expert, seg_start, n_valid):
    n, nj, _ = xs.shape
    d = nj * LANES
    bm = EXPERT_ROWS
    n_items = item_block.shape[0]
    by_group = (n // SUBLANES, SUBLANES, nj, LANES)
    grid_spec = pltpu.PrefetchScalarGridSpec(
        num_scalar_prefetch=4,
        grid=(n_items,),
        in_specs=[pl.BlockSpec(memory_space=pl.ANY),
                  pl.BlockSpec((1, d, 2 * MOE_FF), lambda i, b, e, s, v: (e[i], 0, 0)),
                  pl.BlockSpec((1, MOE_FF, d), lambda i, b, e, s, v: (e[i], 0, 0))],
        out_specs=pl.BlockSpec(memory_space=pl.ANY),
        scratch_shapes=[pltpu.VMEM((d, 2 * MOE_FF), BF16), pltpu.VMEM((MOE_FF, d), BF16),
                        pltpu.VMEM((2,) + _tiled_shape(bm, d), F32), pltpu.VMEM((2,) + _tiled_shape(bm, d), F32),
                        pltpu.SemaphoreType.DMA((2,)), pltpu.SemaphoreType.DMA((2,))],
    )
    ys = pl.pallas_call(
        _expert_kernel,
        grid_spec=grid_spec,
        out_shape=jax.ShapeDtypeStruct(by_group, F32),
        compiler_params=_params("arbitrary"),
        name="moe_experts",
    )(item_block, item_expert, seg_start, n_valid, xs.reshape(by_group), w_in, w_out)
    return ys.reshape(n, nj, LANES)


def _combine_kernel(d0_ref, d1_ref, n0_ref, n1_ref, x_ref, mod_ref, gate_ref, ys_ref, o_ref, y_ref, sem):
    i = pl.program_id(0)
    last = pl.num_programs(0) - 1
    tm, d = x_ref.shape

    def gather(dest_refs, buf):
        def issue(g, carry):
            for u in range(SUBLANES):
                for k, dest_ref in enumerate(dest_refs):
                    pltpu.make_async_copy(ys_ref.at[dest_ref[g * SUBLANES + u]],
                                          y_ref.at[buf, k, g, :, u, :], sem.at[buf]).start(priority=k)
            return carry

        lax.fori_loop(0, tm // SUBLANES, issue, 0)

    def step(buf):
        @pl.when(i == 0)
        def _():
            gather((d0_ref, d1_ref), buf)

        @pl.when(i < last)
        def _():
            gather((n0_ref, n1_ref), 1 - buf)

        _wait_rows(ys_ref, tm, sem.at[buf], 2)
        gates = gate_ref[...]
        moe = gates[:, 0:1] * _load_tiled(y_ref.at[buf, 0]) + gates[:, 1:2] * _load_tiled(y_ref.at[buf, 1])
        o_ref[...] = x_ref[...] + mod_ref[0][5:6] * moe

    for buf in range(2):
        pl.when(i % 2 == buf)(functools.partial(step, buf))


def moe_combine(x2, mod, gates_t, dest, ys, tiles_per_batch_of):
    t, d = x2.shape
    tm = MOVE_TILE
    tpb = tiles_per_batch_of(tm)
    n_tiles = t // tm
    slot_spec = pl.BlockSpec((tm,), lambda i: (i,), memory_space=pltpu.SMEM)
    next_spec = pl.BlockSpec((tm,), lambda i: (jnp.minimum(i + 1, n_tiles - 1),), memory_space=pltpu.SMEM)
    return pl.pallas_call(
        _combine_kernel,
        grid=(n_tiles,),
        in_specs=[slot_spec, slot_spec, next_spec, next_spec,
                  pl.BlockSpec((tm, d), lambda i: (i, 0)),
                  pl.BlockSpec((1, 6, d), lambda i: (i // tpb, 0, 0)),
                  pl.BlockSpec((tm, 2), lambda i: (i, 0)),
                  pl.BlockSpec(memory_space=pl.ANY)],
        out_specs=pl.BlockSpec((tm, d), lambda i: (i, 0)),
        out_shape=jax.ShapeDtypeStruct((t, d), F32),
        scratch_shapes=[pltpu.VMEM((2, 2) + _tiled_shape(tm, d), F32), pltpu.SemaphoreType.DMA((2,))],
        compiler_params=_params("arbitrary"),
        name="moe_combine",
    )(dest[0], dest[1], dest[0], dest[1], x2, mod, gates_t, ys)


def moe_layer(x2, mod, gain, routing, w_in, w_out, tiles_per_batch_of):
    t, d = x2.shape
    bm = EXPERT_ROWS
    n_assign = 2 * t
    eid, gate, rank, cnt = routing
    ne = MOE_EXPERTS
    experts = jnp.arange(ne, dtype=jnp.int32)
    upto = experts[None, :] <= experts[:, None]

    def running_total(v):
        return jnp.sum(jnp.where(upto, v[None, :], 0), axis=1).astype(jnp.int32)

    def lookup(table, idx):
        hit = idx[None] == experts.reshape((ne,) + (1,) * idx.ndim)
        return jnp.sum(jnp.where(hit, table.reshape((ne,) + (1,) * idx.ndim), 0), axis=0).astype(jnp.int32)

    counts = cnt[:, 0]
    seg_end = running_total(counts)
    seg_start = jnp.concatenate([jnp.zeros((1,), jnp.int32), seg_end])
    dest = lookup(seg_start[:ne], eid) + rank
    n_blocks = n_assign // bm
    n_items = n_blocks + ne - 1
    first_blk = seg_start[:ne] // bm
    last_blk = jnp.where(counts > 0, (seg_end - 1) // bm, first_blk - 1)
    per_expert = jnp.maximum(last_blk - first_blk + 1, 0)
    item_end = running_total(per_expert)
    n_valid = item_end[ne - 1]
    item_ids = jnp.arange(n_items, dtype=jnp.int32)
    item_ids_c = jnp.maximum(jnp.minimum(item_ids, n_valid - 1), 0)
    item_expert = jnp.minimum(jnp.sum(item_ids_c[:, None] >= item_end[None, :], axis=1), ne - 1).astype(jnp.int32)
    item_block = lookup(first_blk - (item_end - per_expert), item_expert) + item_ids_c
    xs = moe_dispatch(x2, mod, gain, dest, tiles_per_batch_of)
    ys = moe_experts(xs, w_in, w_out, item_block, item_expert, seg_start,
                     n_valid.reshape(1).astype(jnp.int32))
    return moe_combine(x2, mod, gate.T, dest, ys, tiles_per_batch_of)


SWA_HEAD_DIM = 64
SWA_Q_HEADS = D_MODEL // SWA_HEAD_DIM
SWA_KV_HEADS = 4
SWA_BLOCK = 128
SWA_Q_WIDTH = SWA_Q_HEADS * SWA_HEAD_DIM
SWA_KV_WIDTH = SWA_KV_HEADS * SWA_HEAD_DIM
HEADS_PER_VREG = LANES // SWA_HEAD_DIM


def _swa_kernel(sink_ref, x_ref, mod_ref, gain_ref, win_ref, qg_ref, kg_ref, ones_ref, wout_ref,
                fgain_ref, rw_ref, rb_ref, before_ref, o_ref, eid_ref, gate_ref, rank_ref, cnt_ref,
                kx_ref, vx_ref, att_ref, base_ref):
    t = pl.program_id(1)
    tm = x_ref.shape[0]
    blk = SWA_BLOCK
    hd = SWA_HEAD_DIM

    @pl.when(t == 0)
    def _():
        kx_ref[0:blk, :] = jnp.zeros((blk, kx_ref.shape[1]), BF16)
        vx_ref[0:blk, :] = jnp.zeros((blk, vx_ref.shape[1]), BF16)

    x = x_ref[...]
    mod = mod_ref[0]
    h = _modulated_norm(x, gain_ref[...], mod[0:1], mod[1:2]).astype(BF16)
    proj = jnp.dot(h, win_ref[...], preferred_element_type=F32)

    def head_rms(a):
        ss = jnp.dot((a * a).astype(BF16), ones_ref[...], preferred_element_type=F32)
        return a * lax.rsqrt(ss * (1.0 / hd) + NORM_EPS)

    lane = lax.broadcasted_iota(jnp.int32, (tm, LANES), 1)
    low = lane < hd
    for c in range(SWA_Q_WIDTH // LANES):
        cols = slice(c * LANES, (c + 1) * LANES)
        att_ref[:, cols] = (head_rms(proj[:, cols]) * qg_ref[:, cols]).astype(BF16)
    for c in range(SWA_KV_WIDTH // LANES):
        cols = slice(c * LANES, (c + 1) * LANES)
        kc = head_rms(proj[:, SWA_Q_WIDTH + c * LANES:SWA_Q_WIDTH + (c + 1) * LANES]) * kg_ref[:, cols]
        vc = proj[:, SWA_Q_WIDTH + SWA_KV_WIDTH + c * LANES:SWA_Q_WIDTH + SWA_KV_WIDTH + (c + 1) * LANES]
        for ref, a in ((kx_ref, kc), (vx_ref, vc)):
            even_lo = jnp.where(low, a, 0.0)
            odd_hi = jnp.where(low, 0.0, a)
            j0 = HEADS_PER_VREG * c
            ref[blk:blk + tm, (2 * j0) * LANES:(2 * j0 + 1) * LANES] = even_lo.astype(BF16)
            ref[blk:blk + tm, (2 * j0 + 1) * LANES:(2 * j0 + 2) * LANES] = pltpu.roll(even_lo, hd, 1).astype(BF16)
            ref[blk:blk + tm, (2 * j0 + 2) * LANES:(2 * j0 + 3) * LANES] = pltpu.roll(odd_hi, hd, 1).astype(BF16)
            ref[blk:blk + tm, (2 * j0 + 3) * LANES:(2 * j0 + 4) * LANES] = odd_hi.astype(BF16)

    qi = lax.broadcasted_iota(jnp.int32, (blk, 2 * blk), 0)
    kj = lax.broadcasted_iota(jnp.int32, (blk, 2 * blk), 1)
    lane_q = lax.broadcasted_iota(jnp.int32, (blk, LANES), 1) < hd
    pairs_per_kv = SWA_Q_HEADS // SWA_KV_HEADS // HEADS_PER_VREG
    for i in range(tm // blk):
        rows = slice(i * blk, (i + 1) * blk)
        keys = slice(i * blk, (i + 2) * blk)
        floor = jnp.where(t == 0, blk - 1, qi) if i == 0 else qi
        mask = jnp.logical_and(kj > floor, kj <= qi + blk)
        heads = [(p, half) for p in range(SWA_Q_HEADS // HEADS_PER_VREG) for half in range(HEADS_PER_VREG)]
        col_of = lambda p, half: (2 * (p // pairs_per_kv) + half) * LANES
        scores = [lax.dot_general(att_ref[rows, p * LANES:(p + 1) * LANES],
                                  kx_ref[keys, col_of(p, half):col_of(p, half) + LANES],
                                  (((1,), (1,)), ((), ())), preferred_element_type=F32) for p, half in heads]
        probs, inv = [], []
        for (p, half), sc in zip(heads, scores):
            sc = jnp.where(mask, sc, -jnp.inf)
            sink = sink_ref[HEADS_PER_VREG * p + half]
            m = jnp.maximum(jnp.max(sc, axis=-1, keepdims=True), sink)
            pr = jnp.exp(sc - m)
            inv.append(1.0 / (jnp.sum(pr, axis=-1, keepdims=True) + jnp.exp(sink - m)))
            probs.append(pr.astype(BF16))
        pvs = [jnp.dot(pr, vx_ref[keys, col_of(p, half):col_of(p, half) + LANES], preferred_element_type=F32)
               for (p, half), pr in zip(heads, probs)]
        for p in range(SWA_Q_HEADS // HEADS_PER_VREG):
            out = (pvs[2 * p] + pvs[2 * p + 1]) * jnp.where(lane_q, inv[2 * p], inv[2 * p + 1])
            att_ref[rows, p * LANES:(p + 1) * LANES] = out.astype(BF16)
    kx_ref[0:blk, :] = kx_ref[tm:tm + blk, :]
    vx_ref[0:blk, :] = vx_ref[tm:tm + blk, :]
    y = jnp.dot(att_ref[...], wout_ref[...], preferred_element_type=F32)
    x_new = x + mod[2:3] * y
    o_ref[...] = x_new
    first = jnp.logical_and(pl.program_id(0) == 0, t == 0)
    eid_ref[...], gate_ref[...], rank_ref[...] = _route(
        x_new, mod[3:4], mod[4:5], first, fgain_ref, rw_ref, rb_ref, before_ref, cnt_ref, base_ref)


def swa_layer(x2, mod, gain, w_in, q_norm, k_norm, sinks, w_out, router, b, s):
    t, d = x2.shape
    tm = TOKEN_TILE
    tpb = s // tm
    hd = SWA_HEAD_DIM
    qg = (jnp.tile(q_norm, SWA_Q_HEADS) * hd ** -0.5).reshape(1, SWA_Q_WIDTH)
    kg = jnp.tile(k_norm, SWA_KV_HEADS).reshape(1, SWA_KV_WIDTH)
    ids = jnp.arange(LANES) // hd
    ones = (ids[:, None] == ids[None, :]).astype(BF16)
    proj_w = SWA_Q_WIDTH + 2 * SWA_KV_WIDTH
    ext_w = 2 * SWA_KV_HEADS * LANES
    const2 = lambda bi, ti: (0, 0)
    tile = lambda bi, ti: (bi * tpb + ti, 0)
    r_args, r_in = _router_operands(*router, tm)
    r_out, r_shapes, r_scratch = _router_results(t, pl.BlockSpec((2, tm), lambda bi, ti: (0, bi * tpb + ti)), (2, t))
    out = pl.pallas_call(
        _swa_kernel,
        grid=(b, tpb),
        in_specs=[pl.BlockSpec(memory_space=pltpu.SMEM),
                  pl.BlockSpec((tm, d), tile),
                  pl.BlockSpec((1, 6, d), lambda bi, ti: (bi, 0, 0)),
                  pl.BlockSpec((1, d), const2),
                  pl.BlockSpec((d, proj_w), const2),
                  pl.BlockSpec((1, SWA_Q_WIDTH), const2),
                  pl.BlockSpec((1, SWA_KV_WIDTH), const2),
                  pl.BlockSpec((LANES, LANES), const2),
                  pl.BlockSpec((SWA_Q_WIDTH, d), const2)] + r_in,
        out_specs=[pl.BlockSpec((tm, d), tile)] + r_out,
        out_shape=[jax.ShapeDtypeStruct((t, d), F32)] + r_shapes,
        scratch_shapes=[pltpu.VMEM((tm + SWA_BLOCK, ext_w), BF16), pltpu.VMEM((tm + SWA_BLOCK, ext_w), BF16),
                        pltpu.VMEM((tm, SWA_Q_WIDTH), BF16), r_scratch],
        compiler_params=_params("arbitrary", "arbitrary"),
        name="swa_mixer",
    )(sinks, x2, mod, gain.reshape(1, d), w_in.astype(BF16), qg, kg, ones, w_out.astype(BF16), *r_args)
    return out[0], tuple(out[1:])


DN_QK_HEADS = 4
DN_V_HEADS = 8
DN_HEAD_DIM = D_MODEL // DN_V_HEADS
DN_CONV = 4
DN_CHUNK = 64
DN_KEY_WIDTH = DN_QK_HEADS * DN_HEAD_DIM
DN_VAL_WIDTH = DN_V_HEADS * DN_HEAD_DIM
DN_CONV_WIDTH = 2 * DN_KEY_WIDTH + DN_VAL_WIDTH
DN_PREP_TILE = 128
DN_SCAN_TILE = 256


def _split_bf16(a):
    hi = a.astype(BF16)
    return hi, (a - hi.astype(F32)).astype(BF16)


def _mm(a, b):
    return jnp.dot(a.astype(BF16), b.astype(BF16), preferred_element_type=F32)


def _lane_sums(a):
    return jnp.dot(a.astype(BF16), jnp.ones((LANES, LANES), BF16), preferred_element_type=F32)


def _dn_proj_kernel(x_ref, mod_ref, gain_ref, w_ref, wba_ref, conv_ref, rate_ref, dtb_ref,
                    q_ref, k_ref, v_ref, z_ref, bg_ref, ext_ref):
    t = pl.program_id(1)
    tm = x_ref.shape[0]
    pad = SUBLANES

    @pl.when(t == 0)
    def _():
        ext_ref[0:pad, :] = jnp.zeros((pad, ext_ref.shape[1]), F32)

    mod = mod_ref[0]
    h = _modulated_norm(x_ref[...], gain_ref[...], mod[0:1], mod[1:2])
    hh, hl = _split_bf16(h)
    proj = jnp.dot(hh, w_ref[...], preferred_element_type=F32)
    z_ref[...] = proj[:, DN_CONV_WIDTH:].astype(BF16)

    ext_ref[pad:pad + tm, :] = proj[:, :DN_CONV_WIDTH]
    acc = None
    for j in range(DN_CONV):
        start = pad - (DN_CONV - 1) + j
        term = conv_ref[j:j + 1, :] * ext_ref[start:start + tm, :]
        acc = term if acc is None else acc + term
    ext_ref[0:pad, :] = ext_ref[tm:tm + pad, :]
    qkv = _silu(acc)
    for hq in range(2 * DN_QK_HEADS):
        cols = slice(hq * DN_HEAD_DIM, (hq + 1) * DN_HEAD_DIM)
        a = qkv[:, cols]
        a = a * lax.rsqrt(jnp.sum(a * a, axis=-1, keepdims=True) + NORM_EPS)
        if hq < DN_QK_HEADS:
            q_ref[:, cols] = (a * DN_HEAD_DIM ** -0.5).astype(BF16)
        else:
            k_ref[:, hq * DN_HEAD_DIM - DN_KEY_WIDTH:(hq + 1) * DN_HEAD_DIM - DN_KEY_WIDTH] = a.astype(BF16)
    v_ref[...] = qkv[:, 2 * DN_KEY_WIDTH:].astype(BF16)

    ba = jnp.dot(hh, wba_ref[...], preferred_element_type=F32) + jnp.dot(hl, wba_ref[...], preferred_element_type=F32)
    ba = ba + pltpu.roll(ba, LANES - 2 * DN_V_HEADS, 1)
    lane = lax.broadcasted_iota(jnp.int32, ba.shape, 1)
    beta = 1.0 / (1.0 + jnp.exp(-ba))
    sp = ba + dtb_ref[...]
    g = -rate_ref[...] * (jnp.maximum(sp, 0.0) + jnp.log1p(jnp.exp(-jnp.abs(sp))))
    g = jnp.where(jnp.logical_and(lane >= DN_V_HEADS, lane < 2 * DN_V_HEADS), g, 0.0)
    r = lax.broadcasted_iota(jnp.int32, (tm, tm), 0)
    c = lax.broadcasted_iota(jnp.int32, (tm, tm), 1)
    tri = jnp.where(jnp.logical_and(r // DN_CHUNK == c // DN_CHUNK, c <= r), 1.0, 0.0).astype(BF16)
    g_hi = g.astype(BF16)
    g_rest = g - g_hi.astype(F32)
    g_mid = g_rest.astype(BF16)
    g_lo = (g_rest - g_mid.astype(F32)).astype(BF16)
    gc = sum(jnp.dot(tri, piece, preferred_element_type=F32) for piece in (g_hi, g_mid, g_lo))
    bg_ref[...] = jnp.where(lane < DN_V_HEADS, beta, gc)


def _unit_lower_inverses(mats):
    n = mats[0].shape[0]
    r = lax.broadcasted_iota(jnp.int32, (n, n), 0)
    c = lax.broadcasted_iota(jnp.int32, (n, n), 1)
    eye = (r == c).astype(F32)
    size = SUBLANES
    same = (r // size) == (c // size)
    d = [jnp.where(same, a, 0.0) for a in mats]
    d2 = [_mm(v, v) for v in d]
    d4 = [_mm(v, v) for v in d2]
    x = [_mm(eye - v, eye + v2) for v, v2 in zip(d, d2)]
    x = [_mm(v, eye + v4) for v, v4 in zip(x, d4)]
    while size < n:
        wider = (r // (2 * size)) == (c // (2 * size))
        ring = jnp.logical_and(wider, jnp.logical_not(same))
        xl = [_mm(v, jnp.where(ring, a, 0.0)) for v, a in zip(x, mats)]
        x = [v - _mm(vl, v) for v, vl in zip(x, xl)]
        same = wider
        size *= 2
    return x


def _dn_prep_kernel(q_ref, k_ref, v_ref, bg_ref, gct_ref, u_ref, w_ref, qk_ref, qd_ref):
    tm = q_ref.shape[0]
    ck = DN_CHUNK
    hd = DN_HEAD_DIM
    rep = DN_V_HEADS // DN_QK_HEADS
    r = lax.broadcasted_iota(jnp.int32, (ck, ck), 0)
    c = lax.broadcasted_iota(jnp.int32, (ck, ck), 1)
    lower = c <= r
    strict = c < r
    dot = functools.partial(jnp.dot, preferred_element_type=F32)
    chunks = [slice(ci * ck, (ci + 1) * ck) for ci in range(tm // ck)]
    gram = {}
    for rows in chunks:
        for hq in range(DN_QK_HEADS):
            cols = slice(hq * hd, (hq + 1) * hd)
            k = k_ref[rows, cols]
            gram[(hq, rows.start)] = lax.dot_general(jnp.concatenate([k, q_ref[rows, cols]], axis=0), k,
                                                     (((1,), (1,)), ((), ())), preferred_element_type=F32)
    problems = [(hv, rows) for rows in chunks for hv in range(DN_V_HEADS)]
    mats, rhs = [], []
    for hv, rows in problems:
        beta = bg_ref[rows, hv:hv + 1]
        gcc = bg_ref[rows, DN_V_HEADS + hv:DN_V_HEADS + hv + 1]
        gcr = gct_ref[hv:hv + 1, rows]
        decay = jnp.where(lower, jnp.exp(jnp.where(lower, gcc - gcr, 0.0)), 0.0)
        g = gram[(hv // rep, rows.start)]
        mats.append(jnp.where(strict, g[:ck] * beta * decay, 0.0))
        qk_ref[rows, hv * ck:(hv + 1) * ck] = (g[ck:] * decay).astype(BF16)
        qcols = slice((hv // rep) * hd, (hv // rep + 1) * hd)
        kf = k_ref[rows, qcols].astype(F32)
        vf = v_ref[rows, hv * hd:(hv + 1) * hd].astype(F32)
        egc = jnp.exp(gcc)
        rhs.append(jnp.concatenate([vf * beta, kf * (beta * egc)], axis=1).astype(BF16))
        qd_ref[rows, hv * hd:(hv + 1) * hd] = (q_ref[rows, qcols].astype(F32) * egc).astype(BF16)
    inverses = _unit_lower_inverses(mats)
    for (hv, rows), tinv, b in zip(problems, inverses, rhs):
        uw = dot(tinv.astype(BF16), b)
        u_ref[rows, hv * hd:(hv + 1) * hd] = uw[:, :hd].astype(BF16)
        w_ref[rows, hv * hd:(hv + 1) * hd] = uw[:, hd:].astype(BF16)


def _dn_scan_kernel(x_ref, mod_ref, u_ref, w_ref, qk_ref, qd_ref, k_ref, z_ref, bg_ref, og_ref, wout_ref,
                    fgain_ref, rw_ref, rb_ref, before_ref, o_ref, eid_ref, gate_ref, rank_ref, cnt_ref,
                    state_ref, att_ref, base_ref):
    t = pl.program_id(0)
    nb, tm, _ = x_ref.shape
    ck = DN_CHUNK
    hd = DN_HEAD_DIM
    rep = DN_V_HEADS // DN_QK_HEADS

    @pl.when(t == 0)
    def _():
        state_ref[...] = jnp.zeros_like(state_ref)

    dot = functools.partial(jnp.dot, preferred_element_type=F32)
    chains = [(bi, hv) for bi in range(nb) for hv in range(DN_V_HEADS)]
    for ci in range(tm // ck):
        rows = slice(ci * ck, (ci + 1) * ck)
        gcc, g_end, lhs = [], [], []
        for bi, hv in chains:
            gc = bg_ref[bi, rows, DN_V_HEADS + hv:DN_V_HEADS + hv + 1]
            gcc.append(gc)
            g_end.append(gc[ck - 1:ck, :])
            vcols = slice(hv * hd, (hv + 1) * hd)
            lhs.append(jnp.concatenate([w_ref[bi, rows, vcols], qd_ref[bi, rows, vcols]], axis=0))
        ws_qs = [dot(a, state_ref[n].astype(BF16)) for n, a in enumerate(lhs)]
        v_new = [(u_ref[bi, rows, hv * hd:(hv + 1) * hd].astype(F32) - m[:ck]).astype(BF16)
                 for (bi, hv), m in zip(chains, ws_qs)]
        outs = [m[ck:] + dot(qk_ref[bi, rows, hv * ck:(hv + 1) * ck], vn)
                for (bi, hv), m, vn in zip(chains, ws_qs, v_new)]
        for n, (bi, hv) in enumerate(chains):
            kf = k_ref[bi, rows, (hv // rep) * hd:(hv // rep + 1) * hd].astype(F32)
            k_dec = (kf * jnp.exp(g_end[n] - gcc[n])).astype(BF16)
            state_ref[n] = state_ref[n] * jnp.exp(g_end[n]) + lax.dot_general(
                k_dec, v_new[n], (((0,), (0,)), ((), ())), preferred_element_type=F32)
        for (bi, hv), o in zip(chains, outs):
            zf = z_ref[bi, rows, hv * hd:(hv + 1) * hd].astype(F32)
            normed = o * lax.rsqrt(_lane_sums(o * o) * (1.0 / hd) + NORM_EPS)
            att_ref[bi * tm + ci * ck:bi * tm + (ci + 1) * ck, hv * hd:(hv + 1) * hd] = (
                normed * og_ref[...] * _silu(zf)).astype(BF16)
    y = dot(att_ref[...], wout_ref[...])
    x_new, shift, scale = [], [], []
    for bi in range(nb):
        mod = mod_ref[bi]
        x_new.append(x_ref[bi] + mod[2:3] * y[bi * tm:(bi + 1) * tm])
        o_ref[bi] = x_new[bi]
        shift.append(jnp.broadcast_to(mod[3:4], x_new[bi].shape))
        scale.append(jnp.broadcast_to(mod[4:5], x_new[bi].shape))
    eid, gates, rank = _route(jnp.concatenate(x_new, axis=0), jnp.concatenate(shift, axis=0),
                              jnp.concatenate(scale, axis=0), t == 0, fgain_ref, rw_ref, rb_ref, before_ref,
                              cnt_ref, base_ref)
    for bi in range(nb):
        eid_ref[:, bi, :] = eid[:, bi * tm:(bi + 1) * tm]
        gate_ref[:, bi, :] = gates[:, bi * tm:(bi + 1) * tm]
        rank_ref[:, bi, :] = rank[:, bi * tm:(bi + 1) * tm]


def deltanet_layer(x2, mod, gain, w_in, conv_w, a_log, dt_bias, o_norm, w_out, router, b, s):
    t, d = x2.shape
    nh = DN_V_HEADS
    main_w = DN_CONV_WIDTH + DN_VAL_WIDTH
    w_main = w_in[:, :main_w].astype(BF16)
    w_ba = w_in[:, main_w:]
    w_ba_hi = w_ba.astype(BF16)
    w_ba_lo = (w_ba - w_ba_hi.astype(F32)).astype(BF16)
    w_ba2 = jnp.concatenate([w_ba_hi, w_ba_lo, jnp.zeros((d, LANES - 4 * nh), BF16)], axis=1)
    lanes_pad = lambda v: jnp.concatenate([jnp.zeros((nh,), F32), v.astype(F32),
                                           jnp.zeros((LANES - 2 * nh,), F32)]).reshape(1, LANES)
    rate = lanes_pad(jnp.exp(a_log.astype(F32)))
    dtb = lanes_pad(dt_bias)

    tm = TOKEN_TILE
    tpb = s // tm
    const2 = lambda bi, ti: (0, 0)
    tile = lambda bi, ti: (bi * tpb + ti, 0)
    q, k, v, z, bg = pl.pallas_call(
        _dn_proj_kernel,
        grid=(b, tpb),
        in_specs=[pl.BlockSpec((tm, d), tile),
                  pl.BlockSpec((1, 6, d), lambda bi, ti: (bi, 0, 0)),
                  pl.BlockSpec((1, d), const2),
                  pl.BlockSpec((d, main_w), const2),
                  pl.BlockSpec((d, LANES), const2),
                  pl.BlockSpec((DN_CONV, DN_CONV_WIDTH), const2),
                  pl.BlockSpec((1, LANES), const2),
                  pl.BlockSpec((1, LANES), const2)],
        out_specs=[pl.BlockSpec((tm, DN_KEY_WIDTH), tile), pl.BlockSpec((tm, DN_KEY_WIDTH), tile),
                   pl.BlockSpec((tm, DN_VAL_WIDTH), tile), pl.BlockSpec((tm, DN_VAL_WIDTH), tile),
                   pl.BlockSpec((tm, LANES), tile)],
        out_shape=[jax.ShapeDtypeStruct((t, DN_KEY_WIDTH), BF16), jax.ShapeDtypeStruct((t, DN_KEY_WIDTH), BF16),
                   jax.ShapeDtypeStruct((t, DN_VAL_WIDTH), BF16), jax.ShapeDtypeStruct((t, DN_VAL_WIDTH), BF16),
                   jax.ShapeDtypeStruct((t, LANES), F32)],
        scratch_shapes=[pltpu.VMEM((tm + SUBLANES, DN_CONV_WIDTH), F32)],
        compiler_params=_params("arbitrary", "arbitrary"),
        name="deltanet_proj",
    )(x2, mod, gain.reshape(1, d), w_main, w_ba2, conv_w, rate, dtb)

    gct = bg[:, nh:2 * nh].T
    tp = DN_PREP_TILE
    rows_of = lambda width: pl.BlockSpec((tp, width), lambda i: (i, 0))
    qk_w = nh * DN_CHUNK
    u, w, qk, qd = pl.pallas_call(
        _dn_prep_kernel,
        grid=(t // tp,),
        in_specs=[rows_of(DN_KEY_WIDTH), rows_of(DN_KEY_WIDTH), rows_of(DN_VAL_WIDTH), rows_of(LANES),
                  pl.BlockSpec((nh, tp), lambda i: (0, i))],
        out_specs=[rows_of(DN_VAL_WIDTH), rows_of(DN_VAL_WIDTH), rows_of(qk_w), rows_of(DN_VAL_WIDTH)],
        out_shape=[jax.ShapeDtypeStruct((t, DN_VAL_WIDTH), BF16), jax.ShapeDtypeStruct((t, DN_VAL_WIDTH), BF16),
                   jax.ShapeDtypeStruct((t, qk_w), BF16), jax.ShapeDtypeStruct((t, DN_VAL_WIDTH), BF16)],
        compiler_params=_params("arbitrary"),
        name="deltanet_prep",
    )(q, k, v, bg, gct)

    ts = DN_SCAN_TILE
    seq = lambda a: a.reshape(b, s, a.shape[-1])
    both = lambda width: pl.BlockSpec((b, ts, width), lambda i: (0, i, 0))
    r_args, r_in = _router_operands(*router, b * ts)
    r_out, r_shapes, r_scratch = _router_results(t, pl.BlockSpec((2, b, ts), lambda i: (0, 0, i)), (2, b, s))
    out = pl.pallas_call(
        _dn_scan_kernel,
        grid=(s // ts,),
        in_specs=[both(d),
                  pl.BlockSpec((b, 6, d), lambda i: (0, 0, 0)),
                  both(DN_VAL_WIDTH), both(DN_VAL_WIDTH), both(qk_w), both(DN_VAL_WIDTH), both(DN_KEY_WIDTH),
                  both(DN_VAL_WIDTH), both(LANES),
                  pl.BlockSpec((1, DN_HEAD_DIM), lambda i: (0, 0)),
                  pl.BlockSpec((DN_VAL_WIDTH, d), lambda i: (0, 0))] + r_in,
        out_specs=[both(d)] + r_out,
        out_shape=[jax.ShapeDtypeStruct((b, s, d), F32)] + r_shapes,
        scratch_shapes=[pltpu.VMEM((b * nh, DN_HEAD_DIM, DN_HEAD_DIM), F32),
                        pltpu.VMEM((b * ts, DN_VAL_WIDTH), BF16), r_scratch],
        compiler_params=_params("arbitrary"),
        name="deltanet_scan",
    )(seq(x2), mod, seq(u), seq(w), seq(qk), seq(qd), seq(k), seq(z), seq(bg), o_norm.reshape(1, DN_HEAD_DIM),
      w_out.astype(BF16), *r_args)
    eid, gate, rank, cnt = out[1:]
    return out[0].reshape(t, d), (eid.reshape(2, t), gate.reshape(2, t), rank.reshape(2, t), cnt)


def kernel(x, c, l0_norm_mix, l0_norm_ffn, l0_ada_w, l0_ada_b, l0_gm_w_in, l0_gm_v_norm, l0_gm_w_s, l0_gm_b_s, l0_gm_w_out, l0_router_w, l0_router_b, l0_expert_w_in, l0_expert_w_out, l1_norm_mix, l1_norm_ffn, l1_ada_w, l1_ada_b, l1_dn_w_in, l1_dn_conv_w, l1_dn_a_log, l1_dn_dt_bias, l1_dn_o_norm, l1_dn_w_out, l1_router_w, l1_router_b, l1_expert_w_in, l1_expert_w_out, l2_norm_mix, l2_norm_ffn, l2_ada_w, l2_ada_b, l2_swa_w_in, l2_swa_q_norm, l2_swa_k_norm, l2_swa_sinks, l2_swa_w_out, l2_router_w, l2_router_b, l2_expert_w_in, l2_expert_w_out, l3_norm_mix, l3_norm_ffn, l3_ada_w, l3_ada_b, l3_gm_w_in, l3_gm_v_norm, l3_gm_w_s, l3_gm_b_s, l3_gm_w_out, l3_router_w, l3_router_b, l3_expert_w_in, l3_expert_w_out):
    b, s, d = x.shape
    x2 = x.reshape(b * s, d)
    tiles_per_batch_of = lambda tm: s // tm

    def modulation(ada_w, ada_b):
        return adaln(c, ada_w, ada_b).reshape(b, 6, d)

    mod = modulation(l0_ada_w, l0_ada_b)
    x2, routing = gmlp_layer(x2, mod, l0_norm_mix, l0_gm_w_in, l0_gm_v_norm, l0_gm_w_s, l0_gm_b_s, l0_gm_w_out,
                             (l0_norm_ffn, l0_router_w, l0_router_b), tiles_per_batch_of(TOKEN_TILE))
    x2 = moe_layer(x2, mod, l0_norm_ffn, routing, l0_expert_w_in, l0_expert_w_out, tiles_per_batch_of)

    mod = modulation(l1_ada_w, l1_ada_b)
    x2, routing = deltanet_layer(x2, mod, l1_norm_mix, l1_dn_w_in, l1_dn_conv_w, l1_dn_a_log, l1_dn_dt_bias,
                                 l1_dn_o_norm, l1_dn_w_out, (l1_norm_ffn, l1_router_w, l1_router_b), b, s)
    x2 = moe_layer(x2, mod, l1_norm_ffn, routing, l1_expert_w_in, l1_expert_w_out, tiles_per_batch_of)

    mod = modulation(l2_ada_w, l2_ada_b)
    x2, routing = swa_layer(x2, mod, l2_norm_mix, l2_swa_w_in, l2_swa_q_norm, l2_swa_k_norm, l2_swa_sinks,
                            l2_swa_w_out, (l2_norm_ffn, l2_router_w, l2_router_b), b, s)
    x2 = moe_layer(x2, mod, l2_norm_ffn, routing, l2_expert_w_in, l2_expert_w_out, tiles_per_batch_of)

    mod = modulation(l3_ada_w, l3_ada_b)
    x2, routing = gmlp_layer(x2, mod, l3_norm_mix, l3_gm_w_in, l3_gm_v_norm, l3_gm_w_s, l3_gm_b_s, l3_gm_w_out,
                             (l3_norm_ffn, l3_router_w, l3_router_b), tiles_per_batch_of(TOKEN_TILE))
    x2 = moe_layer(x2, mod, l3_norm_ffn, routing, l3_expert_w_in, l3_expert_w_out, tiles_per_batch_of)
    return x2.reshape(b, s, d)
```

```python
import functools
import math

import jax
import jax.numpy as jnp
from jax import lax
from jax.experimental import pallas as pl
from jax.experimental.pallas import tpu as pltpu

F32 = jnp.float32
BF16 = jnp.bfloat16

D_MODEL = 1024
NORM_EPS = 1e-6

GM_CHUNK = 128
GM_GROUPS = 8
GM_GROUP_DIM = D_MODEL // GM_GROUPS

MOE_GROUPS = 4
MOE_EXPERTS_PER_GROUP = 8
MOE_EXPERTS = MOE_GROUPS * MOE_EXPERTS_PER_GROUP
MOE_FF = D_MODEL // 2

VMEM_LIMIT_BYTES = 56 * 1024 * 1024

TOKEN_TILE = 512
MOVE_TILE = 512
EXPERT_ROWS = 512
EXPERT_SUB_ROWS = 128
LANES = 128
SUBLANES = 8


def _params(*semantics):
    return pltpu.CompilerParams(dimension_semantics=semantics, vmem_limit_bytes=VMEM_LIMIT_BYTES,
                                disable_bounds_checks=True)


def _rms(xf):
    return xf * lax.rsqrt(jnp.mean(xf * xf, axis=-1, keepdims=True) + NORM_EPS)


def _modulated_norm(x, gain, shift, scale):
    return _rms(x) * gain * (1.0 + scale) + shift


def _gelu_tanh(x):
    return 0.5 * x * (1.0 + jnp.tanh(math.sqrt(2.0 / math.pi) * (x + 0.044715 * (x * x * x))))


def _silu(x):
    return x * (1.0 / (1.0 + jnp.exp(-x)))


def _adaln_kernel(ct_ref, w_ref, b_ref, o_ref):
    act = _silu(ct_ref[...])
    w = w_ref[...]
    o_ref[...] = jnp.concatenate([jnp.sum(act[:, r:r + 1] * w, axis=0, keepdims=True)
                                  for r in range(o_ref.shape[0])], axis=0) + b_ref[...]


def adaln(c, ada_w, ada_b):
    rows, d = c.shape
    n = ada_w.shape[1]
    tn = d
    return pl.pallas_call(
        _adaln_kernel,
        grid=(n // tn,),
        in_specs=[pl.BlockSpec((d, rows), lambda j: (0, 0)),
                  pl.BlockSpec((d, tn), lambda j: (0, j)),
                  pl.BlockSpec((1, tn), lambda j: (0, j))],
        out_specs=pl.BlockSpec((rows, tn), lambda j: (0, j)),
        out_shape=jax.ShapeDtypeStruct((rows, n), F32),
        compiler_params=_params("arbitrary"),
        name="adaln",
    )(c.T, ada_w, ada_b.reshape(1, n))


def _gmlp_kernel(x_ref, mod_ref, gain_ref, win_ref, vn_ref, ws_ref, bs_ref, wout_ref, fgain_ref, rw_ref, rb_ref, before_ref,
                 o_ref, eid_ref, gate_ref, rank_ref, cnt_ref, u_ref, v_ref, g_ref, base_ref):
    width = D_MODEL
    x = x_ref[...]
    mod = mod_ref[0]
    h = _modulated_norm(x, gain_ref[...], mod[0:1], mod[1:2])
    z = _gelu_tanh(jnp.dot(h.astype(BF16), win_ref[...], preferred_element_type=F32))
    u_ref[...] = z[:, :width]
    v_ref[...] = (_rms(z[:, width:]) * vn_ref[...]).astype(BF16)
    cells = [(slice(c * GM_CHUNK, (c + 1) * GM_CHUNK), g, slice(g * GM_GROUP_DIM, (g + 1) * GM_GROUP_DIM))
             for c in range(x.shape[0] // GM_CHUNK) for g in range(GM_GROUPS)]
    mixed = [jnp.dot(ws_ref[g], v_ref[rows, cols], preferred_element_type=F32) for rows, g, cols in cells]
    for (rows, g, cols), sv in zip(cells, mixed):
        g_ref[rows, cols] = (u_ref[rows, cols] * (sv + bs_ref[:, g:g + 1])).astype(BF16)
    y = jnp.dot(g_ref[...], wout_ref[...], preferred_element_type=F32)
    x_new = x + mod[2:3] * y
    o_ref[...] = x_new
    eid_ref[...], gate_ref[...], rank_ref[...] = _route(
        x_new, mod[3:4], mod[4:5], pl.program_id(0) == 0, fgain_ref, rw_ref, rb_ref, before_ref, cnt_ref, base_ref)


def gmlp_layer(x2, mod, gain, w_in, v_norm, w_s, b_s, w_out, router, tiles_per_batch):
    t, d = x2.shape
    tm = TOKEN_TILE
    ws_causal = jnp.where(jnp.tril(jnp.ones((GM_CHUNK, GM_CHUNK), dtype=bool)), w_s, 0).astype(BF16)
    const2 = lambda i: (0, 0)
    r_args, r_in = _router_operands(*router, tm)
    r_out, r_shapes, r_scratch = _router_results(t, pl.BlockSpec((2, tm), lambda i: (0, i)), (2, t))
    out = pl.pallas_call(
        _gmlp_kernel,
        grid=(t // tm,),
        in_specs=[pl.BlockSpec((tm, d), lambda i: (i, 0)),
                  pl.BlockSpec((1, 6, d), lambda i: (i // tiles_per_batch, 0, 0)),
                  pl.BlockSpec((1, d), const2),
                  pl.BlockSpec((d, 2 * d), const2),
                  pl.BlockSpec((1, d), const2),
                  pl.BlockSpec((GM_GROUPS, GM_CHUNK, GM_CHUNK), lambda i: (0, 0, 0)),
                  pl.BlockSpec((GM_CHUNK, GM_GROUPS), const2),
                  pl.BlockSpec((d, d), const2)] + r_in,
        out_specs=[pl.BlockSpec((tm, d), lambda i: (i, 0))] + r_out,
        out_shape=[jax.ShapeDtypeStruct((t, d), F32)] + r_shapes,
        scratch_shapes=[pltpu.VMEM((tm, d), F32), pltpu.VMEM((tm, d), BF16), pltpu.VMEM((tm, d), BF16), r_scratch],
        compiler_params=_params("arbitrary"),
        name="gmlp_mixer",
    )(x2, mod, gain.reshape(1, d), w_in.astype(BF16), v_norm.reshape(1, d), ws_causal, b_s.T,
      w_out.astype(BF16), *r_args)
    return out[0], tuple(out[1:])


def _route(x, shift, scale, first, gain_ref, rw_ref, rb_ref, before_ref, cnt_ref, base_ref):
    tm = x.shape[0]
    ne = MOE_EXPERTS
    npg = MOE_EXPERTS_PER_GROUP

    @pl.when(first)
    def _():
        base_ref[...] = jnp.zeros_like(base_ref)

    hh, hl = _split_bf16(_modulated_norm(x, gain_ref[...], shift, scale))
    nt = (((1,), (1,)), ((), ()))
    nr = ROUTER_ROWS
    by_hi = lax.dot_general(rw_ref[...], hh, nt, preferred_element_type=F32)
    by_lo = lax.dot_general(rw_ref[0:nr, :], hl, nt, preferred_element_type=F32)
    lt = (by_hi[0:nr] + rb_ref[...]) + (by_hi[nr:] + by_lo)
    lg = [lt[ne + g:ne + g + 1, :] for g in range(MOE_GROUPS)]
    gmax = functools.reduce(jnp.maximum, lg)
    gsum = functools.reduce(lambda a, b: a + b, [jnp.exp(l - gmax) for l in lg])
    pg_top = 1.0 / gsum
    g_sel = jnp.full(gmax.shape, MOE_GROUPS - 1, jnp.int32)
    for g in range(MOE_GROUPS - 2, -1, -1):
        g_sel = jnp.where(lg[g] == gmax, g, g_sel)
    sel = lt[(MOE_GROUPS - 1) * npg:MOE_GROUPS * npg, :]
    for g in range(MOE_GROUPS - 2, -1, -1):
        sel = jnp.where(g_sel == g, lt[g * npg:(g + 1) * npg, :], sel)
    row = lax.broadcasted_iota(jnp.int32, sel.shape, 0)
    m1 = jnp.max(sel, axis=0, keepdims=True)
    i1 = jnp.min(jnp.where(sel == m1, row, npg), axis=0, keepdims=True)
    rest = jnp.where(row == i1, -jnp.inf, sel)
    m2 = jnp.max(rest, axis=0, keepdims=True)
    i2 = jnp.min(jnp.where(rest == m2, row, npg), axis=0, keepdims=True)
    e2 = jnp.exp(m2 - m1)
    inv = pg_top / (1.0 + e2)
    eid = jnp.concatenate([g_sel * npg + i1, g_sel * npg + i2], axis=0)
    gates = jnp.concatenate([inv, inv * e2], axis=0)

    erow = lax.broadcasted_iota(jnp.int32, (ne, tm), 0)
    hits = [erow == eid[k:k + 1, :] for k in range(2)]
    prefix = jnp.dot(jnp.concatenate([jnp.where(hit, 1.0, 0.0).astype(BF16) for hit in hits], axis=0),
                     before_ref[...], preferred_element_type=F32)
    base = base_ref[...]
    ranks = []
    for k, hit in enumerate(hits):
        ranks.append(jnp.sum(jnp.where(hit, prefix[k * ne:(k + 1) * ne] + base, 0.0), axis=0, keepdims=True))
        base = base + jnp.sum(jnp.where(hit, 1.0, 0.0), axis=1, keepdims=True)
    base_ref[...] = base
    cnt_ref[...] = jnp.broadcast_to(base, cnt_ref.shape).astype(jnp.int32)
    return eid, gates, jnp.concatenate(ranks, axis=0).astype(jnp.int32)


ROUTER_ROWS = MOE_EXPERTS + 2 * SUBLANES


def _router_operands(gain, router_w, router_b, tm):
    d = router_w.shape[0]
    pad = ROUTER_ROWS - MOE_EXPERTS - MOE_GROUPS
    rw = jnp.concatenate([router_w[:, MOE_GROUPS:], router_w[:, :MOE_GROUPS], jnp.zeros((d, pad), F32)], axis=1).T
    rw_hi = rw.astype(BF16)
    rw2 = jnp.concatenate([rw_hi, (rw - rw_hi.astype(F32)).astype(BF16)], axis=0)
    rb = jnp.concatenate([router_b[MOE_GROUPS:], router_b[:MOE_GROUPS], jnp.zeros((pad,), F32)])
    pos = jnp.arange(tm)
    before = (pos[:, None] < pos[None, :]).astype(BF16)
    const = lambda *_: (0, 0)
    specs = [pl.BlockSpec((1, d), const), pl.BlockSpec((2 * ROUTER_ROWS, d), const),
             pl.BlockSpec((ROUTER_ROWS, 1), const), pl.BlockSpec((tm, tm), const)]
    return [gain.reshape(1, d), rw2, rb.reshape(ROUTER_ROWS, 1), before], specs


def _router_results(t, slot_spec, slot_shape):
    specs = [slot_spec, slot_spec, slot_spec, pl.BlockSpec((MOE_EXPERTS, LANES), lambda *_: (0, 0))]
    shapes = [jax.ShapeDtypeStruct(slot_shape, jnp.int32), jax.ShapeDtypeStruct(slot_shape, F32),
              jax.ShapeDtypeStruct(slot_shape, jnp.int32), jax.ShapeDtypeStruct((MOE_EXPERTS, LANES), jnp.int32)]
    return specs, shapes, pltpu.VMEM((MOE_EXPERTS, 1), F32)


def _wait_rows(hbm_ref, n_rows, sem, times):
    rows = hbm_ref.at[pl.ds(0, n_rows)]
    for _ in range(times):
        pltpu.make_async_copy(rows, rows, sem).wait()


def _tiled_shape(n, d):
    return (n // SUBLANES, d // LANES, SUBLANES, LANES)


def _store_tiled(ref, a):
    n, d = a.shape
    for j in range(d // LANES):
        ref[:, j, :, :] = a[:, j * LANES:(j + 1) * LANES].reshape(n // SUBLANES, SUBLANES, LANES)


def _load_tiled(ref):
    g, nj, _, _ = ref.shape
    return jnp.concatenate([ref[:, j, :, :].reshape(g * SUBLANES, LANES) for j in range(nj)], axis=1)


def _dispatch_kernel(d0_ref, d1_ref, x_ref, mod_ref, gain_ref, xs_ref, h_ref, sem):
    i = pl.program_id(0)
    last = pl.num_programs(0) - 1
    tm, d = x_ref.shape
    mod = mod_ref[0]
    h = _modulated_norm(x_ref[...], gain_ref[...], mod[3:4], mod[4:5])

    def step(buf):
        _store_tiled(h_ref.at[buf], h)

        def issue(g, carry):
            for u in range(SUBLANES):
                for k, dest_ref in enumerate((d0_ref, d1_ref)):
                    pltpu.make_async_copy(h_ref.at[buf, g, :, u, :], xs_ref.at[dest_ref[g * SUBLANES + u]],
                                          sem.at[buf]).start(priority=k)
            return carry

        lax.fori_loop(0, tm // SUBLANES, issue, 0)

        @pl.when(i > 0)
        def _():
            _wait_rows(xs_ref, tm, sem.at[1 - buf], 2)

        @pl.when(i == last)
        def _():
            _wait_rows(xs_ref, tm, sem.at[buf], 2)

    for buf in range(2):
        pl.when(i % 2 == buf)(functools.partial(step, buf))


def moe_dispatch(x2, mod, gain, dest, tiles_per_batch_of):
    t, d = x2.shape
    tm = MOVE_TILE
    tpb = tiles_per_batch_of(tm)
    const2 = lambda i: (0, 0)
    slot_spec = pl.BlockSpec((tm,), lambda i: (i,), memory_space=pltpu.SMEM)
    return pl.pallas_call(
        _dispatch_kernel,
        grid=(t // tm,),
        in_specs=[slot_spec, slot_spec,
                  pl.BlockSpec((tm, d), lambda i: (i, 0)),
                  pl.BlockSpec((1, 6, d), lambda i: (i // tpb, 0, 0)),
                  pl.BlockSpec((1, d), const2)],
        out_specs=pl.BlockSpec(memory_space=pl.ANY),
        out_shape=jax.ShapeDtypeStruct((2 * t, d // LANES, LANES), F32),
        scratch_shapes=[pltpu.VMEM((2,) + _tiled_shape(tm, d), F32), pltpu.SemaphoreType.DMA((2,))],
        compiler_params=_params("arbitrary"),
        name="moe_dispatch",
    )(dest[0], dest[1], x2, mod, gain.reshape(1, d))


def _expert_kernel(blk_ref, exp_ref, start_ref, nvalid_ref, wslot_ref, wnext_ref, xs_ref, win_ref, wout_ref, ys_ref,
                   winb_ref, woutb_ref, xbuf_ref, ybuf_ref, winf_ref, woutf_ref, in_sem, out_sem, w_sem):
    i = pl.program_id(0)
    n_items = pl.num_programs(0)
    bm = EXPERT_ROWS
    groups = bm // SUBLANES
    n_blocks = xs_ref.shape[0] // groups
    e = exp_ref[i]
    blk = blk_ref[i]
    prev = jnp.maximum(i - 1, 0)
    nxt = jnp.minimum(i + 1, n_items - 1)
    first = i == 0
    final = i == n_items - 1
    lo = start_ref[e]
    hi = start_ref[e + 1]
    row0 = blk * bm
    live = i < nvalid_ref[0]
    slot = blk % 2

    def block_copies(hbm_ref, buf_ref, b, s, sem, to_hbm):
        copies = []
        for u in range(SUBLANES):
            hbm = hbm_ref.at[pl.ds(b * groups, groups), u]
            vmem = buf_ref.at[s, :, :, u, :]
            copies.append(pltpu.make_async_copy(vmem, hbm, sem.at[s]) if to_hbm
                          else pltpu.make_async_copy(hbm, vmem, sem.at[s]))
        return copies

    def fetch(b, s):
        for c in block_copies(xs_ref, xbuf_ref, b, s, in_sem, False):
            c.start()

    def weight_copies(ex, s):
        return (pltpu.make_async_copy(win_ref.at[ex], winf_ref.at[s], w_sem.at[s]),
                pltpu.make_async_copy(wout_ref.at[ex], woutf_ref.at[s], w_sem.at[s]))

    wslot = wslot_ref[e]

    @pl.when(first)
    def _():
        for c in weight_copies(e, wslot):
            c.start()

    @pl.when(jnp.logical_or(first, exp_ref[prev] != e))
    def _():
        for c in weight_copies(e, wslot):
            c.wait()
        winb_ref[...] = winf_ref[wslot].astype(BF16)
        woutb_ref[...] = woutf_ref[wslot].astype(BF16)

        @pl.when(wnext_ref[e] != e)
        def _():
            for c in weight_copies(wnext_ref[e], 1 - wslot):
                c.start()

    new_block = jnp.logical_or(first, blk_ref[prev] != blk)
    last_of_block = jnp.logical_or(final, blk_ref[nxt] != blk)
    whole = jnp.logical_and(lo <= row0, hi >= row0 + bm)

    @pl.when(first)
    def _():
        fetch(0, 0)

    @pl.when(new_block)
    def _():
        for c in block_copies(xs_ref, xbuf_ref, blk, slot, in_sem, False):
            c.wait()

        @pl.when(blk + 1 < n_blocks)
        def _():
            fetch(blk + 1, 1 - slot)

        @pl.when(blk >= 2)
        def _():
            for c in block_copies(ys_ref, ybuf_ref, blk - 2, slot, out_sem, True):
                c.wait()

    def ffn(x):
        a_gl = jnp.dot(x.astype(BF16), winb_ref[...], preferred_element_type=F32)
        mid = (_silu(a_gl[:, :MOE_FF]) * a_gl[:, MOE_FF:]).astype(BF16)
        return jnp.dot(mid, woutb_ref[...], preferred_element_type=F32)

    @pl.when(jnp.logical_and(live, whole))
    def _():
        _store_tiled(ybuf_ref.at[slot], ffn(_load_tiled(xbuf_ref.at[slot])))

    @pl.when(jnp.logical_and(new_block, jnp.logical_not(whole)))
    def _():
        ybuf_ref[slot] = jnp.zeros(ybuf_ref.shape[1:], F32)

    sub_groups = EXPERT_SUB_ROWS // SUBLANES
    for sub in range(bm // EXPERT_SUB_ROWS):
        sub0 = row0 + sub * EXPERT_SUB_ROWS
        touched = jnp.logical_and(hi > sub0, lo < sub0 + EXPERT_SUB_ROWS)

        @pl.when(jnp.logical_and(jnp.logical_and(live, jnp.logical_not(whole)), touched))
        def _():
            gs = pl.ds(sub * sub_groups, sub_groups)
            rows = sub0 + lax.broadcasted_iota(jnp.int32, (EXPERT_SUB_ROWS, 1), 0)
            mine = jnp.logical_and(rows >= lo, rows < hi)
            y = jnp.where(mine, ffn(_load_tiled(xbuf_ref.at[slot, gs])), 0.0)
            _store_tiled(ybuf_ref.at[slot, gs], _load_tiled(ybuf_ref.at[slot, gs]) + y)

    @pl.when(last_of_block)
    def _():
        for c in block_copies(ys_ref, ybuf_ref, blk, slot, out_sem, True):
            c.start()

    @pl.when(final)
    def _():
        for c in block_copies(ys_ref, ybuf_ref, blk, slot, out_sem, True):
            c.wait()

        @pl.when(blk >= 1)
        def _():
            for c in block_copies(ys_ref, ybuf_ref, blk - 1, 1 - slot, out_sem, True):
                c.wait()


def moe_experts(xs, w_in, w_out, item_block, item_expert, seg_start, n_valid, w_slot, w_next):
    n, nj, _ = xs.shape
    d = nj * LANES
    bm = EXPERT_ROWS
    n_items = item_block.shape[0]
    by_group = (n // SUBLANES, SUBLANES, nj, LANES)
    in_hbm = pl.BlockSpec(memory_space=pl.ANY)
    grid_spec = pltpu.PrefetchScalarGridSpec(
        num_scalar_prefetch=6,
        grid=(n_items,),
        in_specs=[in_hbm, in_hbm, in_hbm],
        out_specs=pl.BlockSpec(memory_space=pl.ANY),
        scratch_shapes=[pltpu.VMEM((d, 2 * MOE_FF), BF16), pltpu.VMEM((MOE_FF, d), BF16),
                        pltpu.VMEM((2,) + _tiled_shape(bm, d), F32), pltpu.VMEM((2,) + _tiled_shape(bm, d), F32),
                        pltpu.VMEM((2, d, 2 * MOE_FF), F32), pltpu.VMEM((2, MOE_FF, d), F32),
                        pltpu.SemaphoreType.DMA((2,)), pltpu.SemaphoreType.DMA((2,)), pltpu.SemaphoreType.DMA((2,))],
    )
    ys = pl.pallas_call(
        _expert_kernel,
        grid_spec=grid_spec,
        out_shape=jax.ShapeDtypeStruct(by_group, F32),
        compiler_params=_params("arbitrary"),
        name="moe_experts",
    )(item_block, item_expert, seg_start, n_valid, w_slot, w_next, xs.reshape(by_group), w_in, w_out)
    return ys.reshape(n, nj, LANES)


def _combine_kernel(d0_ref, d1_ref, n0_ref, n1_ref, x_ref, mod_ref, gate_ref, ys_ref, o_ref, y_ref, sem):
    i = pl.program_id(0)
    last = pl.num_programs(0) - 1
    tm, d = x_ref.shape

    def gather(dest_refs, buf):
        def issue(g, carry):
            for u in range(SUBLANES):
                for k, dest_ref in enumerate(dest_refs):
                    pltpu.make_async_copy(ys_ref.at[dest_ref[g * SUBLANES + u]],
                                          y_ref.at[buf, k, g, :, u, :], sem.at[buf]).start(priority=k)
            return carry

        lax.fori_loop(0, tm // SUBLANES, issue, 0)

    def step(buf):
        @pl.when(i == 0)
        def _():
            gather((d0_ref, d1_ref), buf)

        @pl.when(i < last)
        def _():
            gather((n0_ref, n1_ref), 1 - buf)

        _wait_rows(ys_ref, tm, sem.at[buf], 2)
        gates = gate_ref[...]
        moe = gates[:, 0:1] * _load_tiled(y_ref.at[buf, 0]) + gates[:, 1:2] * _load_tiled(y_ref.at[buf, 1])
        o_ref[...] = x_ref[...] + mod_ref[0][5:6] * moe

    for buf in range(2):
        pl.when(i % 2 == buf)(functools.partial(step, buf))


def moe_combine(x2, mod, gates_t, dest, ys, tiles_per_batch_of):
    t, d = x2.shape
    tm = MOVE_TILE
    tpb = tiles_per_batch_of(tm)
    n_tiles = t // tm
    slot_spec = pl.BlockSpec((tm,), lambda i: (i,), memory_space=pltpu.SMEM)
    next_spec = pl.BlockSpec((tm,), lambda i: (jnp.minimum(i + 1, n_tiles - 1),), memory_space=pltpu.SMEM)
    return pl.pallas_call(
        _combine_kernel,
        grid=(n_tiles,),
        in_specs=[slot_spec, slot_spec, next_spec, next_spec,
                  pl.BlockSpec((tm, d), lambda i: (i, 0)),
                  pl.BlockSpec((1, 6, d), lambda i: (i // tpb, 0, 0)),
                  pl.BlockSpec((tm, 2), lambda i: (i, 0)),
                  pl.BlockSpec(memory_space=pl.ANY)],
        out_specs=pl.BlockSpec((tm, d), lambda i: (i, 0)),
        out_shape=jax.ShapeDtypeStruct((t, d), F32),
        scratch_shapes=[pltpu.VMEM((2, 2) + _tiled_shape(tm, d), F32), pltpu.SemaphoreType.DMA((2,))],
        compiler_params=_params("arbitrary"),
        name="moe_combine",
    )(dest[0], dest[1], dest[0], dest[1], x2, mod, gates_t, ys)


def moe_layer(x2, mod, gain, routing, w_in, w_out, tiles_per_batch_of):
    t, d = x2.shape
    bm = EXPERT_ROWS
    n_assign = 2 * t
    eid, gate, rank, cnt = routing
    ne = MOE_EXPERTS
    experts = jnp.arange(ne, dtype=jnp.int32)
    upto = experts[None, :] <= experts[:, None]

    def running_total(v):
        return jnp.sum(jnp.where(upto, v[None, :], 0), axis=1).astype(jnp.int32)

    def lookup(table, idx):
        hit = idx[None] == experts.reshape((ne,) + (1,) * idx.ndim)
        return jnp.sum(jnp.where(hit, table.reshape((ne,) + (1,) * idx.ndim), 0), axis=0).astype(jnp.int32)

    counts = cnt[:, 0]
    seg_end = running_total(counts)
    seg_start = jnp.concatenate([jnp.zeros((1,), jnp.int32), seg_end])
    dest = lookup(seg_start[:ne], eid) + rank
    n_blocks = n_assign // bm
    n_items = n_blocks + ne - 1
    first_blk = seg_start[:ne] // bm
    last_blk = jnp.where(counts > 0, (seg_end - 1) // bm, first_blk - 1)
    per_expert = jnp.maximum(last_blk - first_blk + 1, 0)
    item_end = running_total(per_expert)
    n_valid = item_end[ne - 1]
    item_ids = jnp.arange(n_items, dtype=jnp.int32)
    item_ids_c = jnp.maximum(jnp.minimum(item_ids, n_valid - 1), 0)
    item_expert = jnp.minimum(jnp.sum(item_ids_c[:, None] >= item_end[None, :], axis=1), ne - 1).astype(jnp.int32)
    item_block = lookup(first_blk - (item_end - per_expert), item_expert) + item_ids_c
    has_rows = counts > 0
    w_slot = (running_total(has_rows.astype(jnp.int32)) + 1) % 2
    later = jnp.logical_and(experts[None, :] > experts[:, None], has_rows[None, :])
    w_next = jnp.min(jnp.where(later, experts[None, :], ne), axis=1)
    w_next = jnp.where(w_next == ne, experts, w_next).astype(jnp.int32)
    xs = moe_dispatch(x2, mod, gain, dest, tiles_per_batch_of)
    ys = moe_experts(xs, w_in, w_out, item_block, item_expert, seg_start,
                     n_valid.reshape(1).astype(jnp.int32), w_slot.astype(jnp.int32), w_next)
    return moe_combine(x2, mod, gate.T, dest, ys, tiles_per_batch_of)


SWA_HEAD_DIM = 64
SWA_Q_HEADS = D_MODEL // SWA_HEAD_DIM
SWA_KV_HEADS = 4
SWA_BLOCK = 128
SWA_Q_WIDTH = SWA_Q_HEADS * SWA_HEAD_DIM
SWA_KV_WIDTH = SWA_KV_HEADS * SWA_HEAD_DIM
HEADS_PER_VREG = LANES // SWA_HEAD_DIM


def _swa_kernel(sink_ref, x_ref, mod_ref, gain_ref, win_ref, qg_ref, kg_ref, ones_ref, wout_ref,
                fgain_ref, rw_ref, rb_ref, before_ref, o_ref, eid_ref, gate_ref, rank_ref, cnt_ref,
                kx_ref, vx_ref, att_ref, base_ref):
    t = pl.program_id(1)
    tm = x_ref.shape[0]
    blk = SWA_BLOCK
    hd = SWA_HEAD_DIM

    @pl.when(t == 0)
    def _():
        kx_ref[0:blk, :] = jnp.zeros((blk, kx_ref.shape[1]), BF16)
        vx_ref[0:blk, :] = jnp.zeros((blk, vx_ref.shape[1]), BF16)

    x = x_ref[...]
    mod = mod_ref[0]
    h = _modulated_norm(x, gain_ref[...], mod[0:1], mod[1:2]).astype(BF16)
    proj = jnp.dot(h, win_ref[...], preferred_element_type=F32)

    def head_rms(a):
        ss = jnp.dot((a * a).astype(BF16), ones_ref[...], preferred_element_type=F32)
        return a * lax.rsqrt(ss * (1.0 / hd) + NORM_EPS)

    lane = lax.broadcasted_iota(jnp.int32, (tm, LANES), 1)
    low = lane < hd
    for c in range(SWA_Q_WIDTH // LANES):
        cols = slice(c * LANES, (c + 1) * LANES)
        att_ref[:, cols] = (head_rms(proj[:, cols]) * qg_ref[:, cols]).astype(BF16)
    for c in range(SWA_KV_WIDTH // LANES):
        cols = slice(c * LANES, (c + 1) * LANES)
        kc = head_rms(proj[:, SWA_Q_WIDTH + c * LANES:SWA_Q_WIDTH + (c + 1) * LANES]) * kg_ref[:, cols]
        vc = proj[:, SWA_Q_WIDTH + SWA_KV_WIDTH + c * LANES:SWA_Q_WIDTH + SWA_KV_WIDTH + (c + 1) * LANES]
        for ref, a in ((kx_ref, kc), (vx_ref, vc)):
            even_lo = jnp.where(low, a, 0.0)
            odd_hi = jnp.where(low, 0.0, a)
            j0 = HEADS_PER_VREG * c
            ref[blk:blk + tm, (2 * j0) * LANES:(2 * j0 + 1) * LANES] = even_lo.astype(BF16)
            ref[blk:blk + tm, (2 * j0 + 1) * LANES:(2 * j0 + 2) * LANES] = pltpu.roll(even_lo, hd, 1).astype(BF16)
            ref[blk:blk + tm, (2 * j0 + 2) * LANES:(2 * j0 + 3) * LANES] = pltpu.roll(odd_hi, hd, 1).astype(BF16)
            ref[blk:blk + tm, (2 * j0 + 3) * LANES:(2 * j0 + 4) * LANES] = odd_hi.astype(BF16)

    qi = lax.broadcasted_iota(jnp.int32, (blk, 2 * blk), 0)
    kj = lax.broadcasted_iota(jnp.int32, (blk, 2 * blk), 1)
    lane_q = lax.broadcasted_iota(jnp.int32, (blk, LANES), 1) < hd
    pairs_per_kv = SWA_Q_HEADS // SWA_KV_HEADS // HEADS_PER_VREG
    for i in range(tm // blk):
        rows = slice(i * blk, (i + 1) * blk)
        keys = slice(i * blk, (i + 2) * blk)
        floor = jnp.where(t == 0, blk - 1, qi) if i == 0 else qi
        mask = jnp.logical_and(kj > floor, kj <= qi + blk)
        heads = [(p, half) for p in range(SWA_Q_HEADS // HEADS_PER_VREG) for half in range(HEADS_PER_VREG)]
        col_of = lambda p, half: (2 * (p // pairs_per_kv) + half) * LANES
        scores = [lax.dot_general(att_ref[rows, p * LANES:(p + 1) * LANES],
                                  kx_ref[keys, col_of(p, half):col_of(p, half) + LANES],
                                  (((1,), (1,)), ((), ())), preferred_element_type=F32) for p, half in heads]
        probs, inv = [], []
        for (p, half), sc in zip(heads, scores):
            sc = jnp.where(mask, sc, -jnp.inf)
            sink = sink_ref[HEADS_PER_VREG * p + half]
            m = jnp.maximum(jnp.max(sc, axis=-1, keepdims=True), sink)
            pr = jnp.exp(sc - m)
            inv.append(1.0 / (jnp.sum(pr, axis=-1, keepdims=True) + jnp.exp(sink - m)))
            probs.append(pr.astype(BF16))
        pvs = [jnp.dot(pr, vx_ref[keys, col_of(p, half):col_of(p, half) + LANES], preferred_element_type=F32)
               for (p, half), pr in zip(heads, probs)]
        for p in range(SWA_Q_HEADS // HEADS_PER_VREG):
            out = (pvs[2 * p] + pvs[2 * p + 1]) * jnp.where(lane_q, inv[2 * p], inv[2 * p + 1])
            att_ref[rows, p * LANES:(p + 1) * LANES] = out.astype(BF16)
    kx_ref[0:blk, :] = kx_ref[tm:tm + blk, :]
    vx_ref[0:blk, :] = vx_ref[tm:tm + blk, :]
    y = jnp.dot(att_ref[...], wout_ref[...], preferred_element_type=F32)
    x_new = x + mod[2:3] * y
    o_ref[...] = x_new
    first = jnp.logical_and(pl.program_id(0) == 0, t == 0)
    eid_ref[...], gate_ref[...], rank_ref[...] = _route(
        x_new, mod[3:4], mod[4:5], first, fgain_ref, rw_ref, rb_ref, before_ref, cnt_ref, base_ref)


def swa_layer(x2, mod, gain, w_in, q_norm, k_norm, sinks, w_out, router, b, s):
    t, d = x2.shape
    tm = TOKEN_TILE
    tpb = s // tm
    hd = SWA_HEAD_DIM
    qg = (jnp.tile(q_norm, SWA_Q_HEADS) * hd ** -0.5).reshape(1, SWA_Q_WIDTH)
    kg = jnp.tile(k_norm, SWA_KV_HEADS).reshape(1, SWA_KV_WIDTH)
    ids = jnp.arange(LANES) // hd
    ones = (ids[:, None] == ids[None, :]).astype(BF16)
    proj_w = SWA_Q_WIDTH + 2 * SWA_KV_WIDTH
    ext_w = 2 * SWA_KV_HEADS * LANES
    const2 = lambda bi, ti: (0, 0)
    tile = lambda bi, ti: (bi * tpb + ti, 0)
    r_args, r_in = _router_operands(*router, tm)
    r_out, r_shapes, r_scratch = _router_results(t, pl.BlockSpec((2, tm), lambda bi, ti: (0, bi * tpb + ti)), (2, t))
    out = pl.pallas_call(
        _swa_kernel,
        grid=(b, tpb),
        in_specs=[pl.BlockSpec(memory_space=pltpu.SMEM),
                  pl.BlockSpec((tm, d), tile),
                  pl.BlockSpec((1, 6, d), lambda bi, ti: (bi, 0, 0)),
                  pl.BlockSpec((1, d), const2),
                  pl.BlockSpec((d, proj_w), const2),
                  pl.BlockSpec((1, SWA_Q_WIDTH), const2),
                  pl.BlockSpec((1, SWA_KV_WIDTH), const2),
                  pl.BlockSpec((LANES, LANES), const2),
                  pl.BlockSpec((SWA_Q_WIDTH, d), const2)] + r_in,
        out_specs=[pl.BlockSpec((tm, d), tile)] + r_out,
        out_shape=[jax.ShapeDtypeStruct((t, d), F32)] + r_shapes,
        scratch_shapes=[pltpu.VMEM((tm + SWA_BLOCK, ext_w), BF16), pltpu.VMEM((tm + SWA_BLOCK, ext_w), BF16),
                        pltpu.VMEM((tm, SWA_Q_WIDTH), BF16), r_scratch],
        compiler_params=_params("arbitrary", "arbitrary"),
        name="swa_mixer",
    )(sinks, x2, mod, gain.reshape(1, d), w_in.astype(BF16), qg, kg, ones, w_out.astype(BF16), *r_args)
    return out[0], tuple(out[1:])


DN_QK_HEADS = 4
DN_V_HEADS = 8
DN_HEAD_DIM = D_MODEL // DN_V_HEADS
DN_CONV = 4
DN_CHUNK = 64
DN_KEY_WIDTH = DN_QK_HEADS * DN_HEAD_DIM
DN_VAL_WIDTH = DN_V_HEADS * DN_HEAD_DIM
DN_CONV_WIDTH = 2 * DN_KEY_WIDTH + DN_VAL_WIDTH
DN_PREP_TILE = 128
DN_SCAN_TILE = 256


def _split_bf16(a):
    hi = a.astype(BF16)
    return hi, (a - hi.astype(F32)).astype(BF16)


def _mm(a, b):
    return jnp.dot(a.astype(BF16), b.astype(BF16), preferred_element_type=F32)


def _lane_sums(a):
    return jnp.dot(a.astype(BF16), jnp.ones((LANES, LANES), BF16), preferred_element_type=F32)


def _dn_proj_kernel(x_ref, mod_ref, gain_ref, w_ref, wba_ref, conv_ref, rate_ref, dtb_ref,
                    q_ref, k_ref, v_ref, z_ref, bg_ref, ext_ref):
    t = pl.program_id(1)
    tm = x_ref.shape[0]
    pad = SUBLANES

    @pl.when(t == 0)
    def _():
        ext_ref[0:pad, :] = jnp.zeros((pad, ext_ref.shape[1]), F32)

    mod = mod_ref[0]
    h = _modulated_norm(x_ref[...], gain_ref[...], mod[0:1], mod[1:2])
    hh, hl = _split_bf16(h)
    proj = jnp.dot(hh, w_ref[...], preferred_element_type=F32)
    z_ref[...] = proj[:, DN_CONV_WIDTH:].astype(BF16)

    ext_ref[pad:pad + tm, :] = proj[:, :DN_CONV_WIDTH]
    acc = None
    for j in range(DN_CONV):
        start = pad - (DN_CONV - 1) + j
        term = conv_ref[j:j + 1, :] * ext_ref[start:start + tm, :]
        acc = term if acc is None else acc + term
    ext_ref[0:pad, :] = ext_ref[tm:tm + pad, :]
    qkv = _silu(acc)
    for hq in range(2 * DN_QK_HEADS):
        cols = slice(hq * DN_HEAD_DIM, (hq + 1) * DN_HEAD_DIM)
        a = qkv[:, cols]
        a = a * lax.rsqrt(jnp.sum(a * a, axis=-1, keepdims=True) + NORM_EPS)
        if hq < DN_QK_HEADS:
            q_ref[:, cols] = (a * DN_HEAD_DIM ** -0.5).astype(BF16)
        else:
            k_ref[:, hq * DN_HEAD_DIM - DN_KEY_WIDTH:(hq + 1) * DN_HEAD_DIM - DN_KEY_WIDTH] = a.astype(BF16)
    v_ref[...] = qkv[:, 2 * DN_KEY_WIDTH:].astype(BF16)

    ba = jnp.dot(hh, wba_ref[...], preferred_element_type=F32) + jnp.dot(hl, wba_ref[...], preferred_element_type=F32)
    ba = ba + pltpu.roll(ba, LANES - 2 * DN_V_HEADS, 1)
    lane = lax.broadcasted_iota(jnp.int32, ba.shape, 1)
    beta = 1.0 / (1.0 + jnp.exp(-ba))
    sp = ba + dtb_ref[...]
    g = -rate_ref[...] * (jnp.maximum(sp, 0.0) + jnp.log1p(jnp.exp(-jnp.abs(sp))))
    g = jnp.where(jnp.logical_and(lane >= DN_V_HEADS, lane < 2 * DN_V_HEADS), g, 0.0)
    r = lax.broadcasted_iota(jnp.int32, (tm, tm), 0)
    c = lax.broadcasted_iota(jnp.int32, (tm, tm), 1)
    tri = jnp.where(jnp.logical_and(r // DN_CHUNK == c // DN_CHUNK, c <= r), 1.0, 0.0).astype(BF16)
    g_hi = g.astype(BF16)
    g_rest = g - g_hi.astype(F32)
    g_mid = g_rest.astype(BF16)
    g_lo = (g_rest - g_mid.astype(F32)).astype(BF16)
    gc = sum(jnp.dot(tri, piece, preferred_element_type=F32) for piece in (g_hi, g_mid, g_lo))
    bg_ref[...] = jnp.where(lane < DN_V_HEADS, beta, gc)


def _unit_lower_inverses(mats):
    n = mats[0].shape[0]
    r = lax.broadcasted_iota(jnp.int32, (n, n), 0)
    c = lax.broadcasted_iota(jnp.int32, (n, n), 1)
    eye = (r == c).astype(F32)
    size = SUBLANES
    same = (r // size) == (c // size)
    d = [jnp.where(same, a, 0.0) for a in mats]
    d2 = [_mm(v, v) for v in d]
    d4 = [_mm(v, v) for v in d2]
    x = [_mm(eye - v, eye + v2) for v, v2 in zip(d, d2)]
    x = [_mm(v, eye + v4) for v, v4 in zip(x, d4)]
    while size < n:
        wider = (r // (2 * size)) == (c // (2 * size))
        ring = jnp.logical_and(wider, jnp.logical_not(same))
        xl = [_mm(v, jnp.where(ring, a, 0.0)) for v, a in zip(x, mats)]
        x = [v - _mm(vl, v) for v, vl in zip(x, xl)]
        same = wider
        size *= 2
    return x


def _dn_prep_kernel(q_ref, k_ref, v_ref, bg_ref, gct_ref, u_ref, w_ref, qk_ref, qd_ref):
    tm = q_ref.shape[0]
    ck = DN_CHUNK
    hd = DN_HEAD_DIM
    rep = DN_V_HEADS // DN_QK_HEADS
    r = lax.broadcasted_iota(jnp.int32, (ck, ck), 0)
    c = lax.broadcasted_iota(jnp.int32, (ck, ck), 1)
    lower = c <= r
    strict = c < r
    dot = functools.partial(jnp.dot, preferred_element_type=F32)
    chunks = [slice(ci * ck, (ci + 1) * ck) for ci in range(tm // ck)]
    gram = {}
    for rows in chunks:
        for hq in range(DN_QK_HEADS):
            cols = slice(hq * hd, (hq + 1) * hd)
            k = k_ref[rows, cols]
            gram[(hq, rows.start)] = lax.dot_general(jnp.concatenate([k, q_ref[rows, cols]], axis=0), k,
                                                     (((1,), (1,)), ((), ())), preferred_element_type=F32)
    problems = [(hv, rows) for rows in chunks for hv in range(DN_V_HEADS)]
    mats, rhs = [], []
    for hv, rows in problems:
        beta = bg_ref[rows, hv:hv + 1]
        gcc = bg_ref[rows, DN_V_HEADS + hv:DN_V_HEADS + hv + 1]
        gcr = gct_ref[hv:hv + 1, rows]
        decay = jnp.where(lower, jnp.exp(jnp.where(lower, gcc - gcr, 0.0)), 0.0)
        g = gram[(hv // rep, rows.start)]
        mats.append(jnp.where(strict, g[:ck] * beta * decay, 0.0))
        qk_ref[rows, hv * ck:(hv + 1) * ck] = (g[ck:] * decay).astype(BF16)
        qcols = slice((hv // rep) * hd, (hv // rep + 1) * hd)
        kf = k_ref[rows, qcols].astype(F32)
        vf = v_ref[rows, hv * hd:(hv + 1) * hd].astype(F32)
        egc = jnp.exp(gcc)
        rhs.append(jnp.concatenate([vf * beta, kf * (beta * egc)], axis=1).astype(BF16))
        qd_ref[rows, hv * hd:(hv + 1) * hd] = (q_ref[rows, qcols].astype(F32) * egc).astype(BF16)
    inverses = _unit_lower_inverses(mats)
    for (hv, rows), tinv, b in zip(problems, inverses, rhs):
        uw = dot(tinv.astype(BF16), b)
        u_ref[rows, hv * hd:(hv + 1) * hd] = uw[:, :hd].astype(BF16)
        w_ref[rows, hv * hd:(hv + 1) * hd] = uw[:, hd:].astype(BF16)


def _dn_scan_kernel(x_ref, mod_ref, u_ref, w_ref, qk_ref, qd_ref, k_ref, z_ref, bg_ref, og_ref, wout_ref,
                    fgain_ref, rw_ref, rb_ref, before_ref, o_ref, eid_ref, gate_ref, rank_ref, cnt_ref,
                    state_ref, att_ref, base_ref):
    t = pl.program_id(0)
    nb, tm, _ = x_ref.shape
    ck = DN_CHUNK
    hd = DN_HEAD_DIM
    rep = DN_V_HEADS // DN_QK_HEADS

    @pl.when(t == 0)
    def _():
        state_ref[...] = jnp.zeros_like(state_ref)

    dot = functools.partial(jnp.dot, preferred_element_type=F32)
    chains = [(bi, hv) for bi in range(nb) for hv in range(DN_V_HEADS)]
    for ci in range(tm // ck):
        rows = slice(ci * ck, (ci + 1) * ck)
        gcc, g_end, lhs = [], [], []
        for bi, hv in chains:
            gc = bg_ref[bi, rows, DN_V_HEADS + hv:DN_V_HEADS + hv + 1]
            gcc.append(gc)
            g_end.append(gc[ck - 1:ck, :])
            vcols = slice(hv * hd, (hv + 1) * hd)
            lhs.append(jnp.concatenate([w_ref[bi, rows, vcols], qd_ref[bi, rows, vcols]], axis=0))
        ws_qs = [dot(a, state_ref[n].astype(BF16)) for n, a in enumerate(lhs)]
        v_new = [(u_ref[bi, rows, hv * hd:(hv + 1) * hd].astype(F32) - m[:ck]).astype(BF16)
                 for (bi, hv), m in zip(chains, ws_qs)]
        outs = [m[ck:] + dot(qk_ref[bi, rows, hv * ck:(hv + 1) * ck], vn)
                for (bi, hv), m, vn in zip(chains, ws_qs, v_new)]
        for n, (bi, hv) in enumerate(chains):
            kf = k_ref[bi, rows, (hv // rep) * hd:(hv // rep + 1) * hd].astype(F32)
            k_dec = (kf * jnp.exp(g_end[n] - gcc[n])).astype(BF16)
            state_ref[n] = state_ref[n] * jnp.exp(g_end[n]) + lax.dot_general(
                k_dec, v_new[n], (((0,), (0,)), ((), ())), preferred_element_type=F32)
        for (bi, hv), o in zip(chains, outs):
            zf = z_ref[bi, rows, hv * hd:(hv + 1) * hd].astype(F32)
            normed = o * lax.rsqrt(_lane_sums(o * o) * (1.0 / hd) + NORM_EPS)
            att_ref[bi * tm + ci * ck:bi * tm + (ci + 1) * ck, hv * hd:(hv + 1) * hd] = (
                normed * og_ref[...] * _silu(zf)).astype(BF16)
    y = dot(att_ref[...], wout_ref[...])
    x_new, shift, scale = [], [], []
    for bi in range(nb):
        mod = mod_ref[bi]
        x_new.append(x_ref[bi] + mod[2:3] * y[bi * tm:(bi + 1) * tm])
        o_ref[bi] = x_new[bi]
        shift.append(jnp.broadcast_to(mod[3:4], x_new[bi].shape))
        scale.append(jnp.broadcast_to(mod[4:5], x_new[bi].shape))
    eid, gates, rank = _route(jnp.concatenate(x_new, axis=0), jnp.concatenate(shift, axis=0),
                              jnp.concatenate(scale, axis=0), t == 0, fgain_ref, rw_ref, rb_ref, before_ref,
                              cnt_ref, base_ref)
    for bi in range(nb):
        eid_ref[:, bi, :] = eid[:, bi * tm:(bi + 1) * tm]
        gate_ref[:, bi, :] = gates[:, bi * tm:(bi + 1) * tm]
        rank_ref[:, bi, :] = rank[:, bi * tm:(bi + 1) * tm]


def deltanet_layer(x2, mod, gain, w_in, conv_w, a_log, dt_bias, o_norm, w_out, router, b, s):
    t, d = x2.shape
    nh = DN_V_HEADS
    main_w = DN_CONV_WIDTH + DN_VAL_WIDTH
    w_main = w_in[:, :main_w].astype(BF16)
    w_ba = w_in[:, main_w:]
    w_ba_hi = w_ba.astype(BF16)
    w_ba_lo = (w_ba - w_ba_hi.astype(F32)).astype(BF16)
    w_ba2 = jnp.concatenate([w_ba_hi, w_ba_lo, jnp.zeros((d, LANES - 4 * nh), BF16)], axis=1)
    lanes_pad = lambda v: jnp.concatenate([jnp.zeros((nh,), F32), v.astype(F32),
                                           jnp.zeros((LANES - 2 * nh,), F32)]).reshape(1, LANES)
    rate = lanes_pad(jnp.exp(a_log.astype(F32)))
    dtb = lanes_pad(dt_bias)

    tm = TOKEN_TILE
    tpb = s // tm
    const2 = lambda bi, ti: (0, 0)
    tile = lambda bi, ti: (bi * tpb + ti, 0)
    q, k, v, z, bg = pl.pallas_call(
        _dn_proj_kernel,
        grid=(b, tpb),
        in_specs=[pl.BlockSpec((tm, d), tile),
                  pl.BlockSpec((1, 6, d), lambda bi, ti: (bi, 0, 0)),
                  pl.BlockSpec((1, d), const2),
                  pl.BlockSpec((d, main_w), const2),
                  pl.BlockSpec((d, LANES), const2),
                  pl.BlockSpec((DN_CONV, DN_CONV_WIDTH), const2),
                  pl.BlockSpec((1, LANES), const2),
                  pl.BlockSpec((1, LANES), const2)],
        out_specs=[pl.BlockSpec((tm, DN_KEY_WIDTH), tile), pl.BlockSpec((tm, DN_KEY_WIDTH), tile),
                   pl.BlockSpec((tm, DN_VAL_WIDTH), tile), pl.BlockSpec((tm, DN_VAL_WIDTH), tile),
                   pl.BlockSpec((tm, LANES), tile)],
        out_shape=[jax.ShapeDtypeStruct((t, DN_KEY_WIDTH), BF16), jax.ShapeDtypeStruct((t, DN_KEY_WIDTH), BF16),
                   jax.ShapeDtypeStruct((t, DN_VAL_WIDTH), BF16), jax.ShapeDtypeStruct((t, DN_VAL_WIDTH), BF16),
                   jax.ShapeDtypeStruct((t, LANES), F32)],
        scratch_shapes=[pltpu.VMEM((tm + SUBLANES, DN_CONV_WIDTH), F32)],
        compiler_params=_params("arbitrary", "arbitrary"),
        name="deltanet_proj",
    )(x2, mod, gain.reshape(1, d), w_main, w_ba2, conv_w, rate, dtb)

    gct = bg[:, nh:2 * nh].T
    tp = DN_PREP_TILE
    rows_of = lambda width: pl.BlockSpec((tp, width), lambda i: (i, 0))
    qk_w = nh * DN_CHUNK
    u, w, qk, qd = pl.pallas_call(
        _dn_prep_kernel,
        grid=(t // tp,),
        in_specs=[rows_of(DN_KEY_WIDTH), rows_of(DN_KEY_WIDTH), rows_of(DN_VAL_WIDTH), rows_of(LANES),
                  pl.BlockSpec((nh, tp), lambda i: (0, i))],
        out_specs=[rows_of(DN_VAL_WIDTH), rows_of(DN_VAL_WIDTH), rows_of(qk_w), rows_of(DN_VAL_WIDTH)],
        out_shape=[jax.ShapeDtypeStruct((t, DN_VAL_WIDTH), BF16), jax.ShapeDtypeStruct((t, DN_VAL_WIDTH), BF16),
                   jax.ShapeDtypeStruct((t, qk_w), BF16), jax.ShapeDtypeStruct((t, DN_VAL_WIDTH), BF16)],
        compiler_params=_params("arbitrary"),
        name="deltanet_prep",
    )(q, k, v, bg, gct)

    ts = DN_SCAN_TILE
    seq = lambda a: a.reshape(b, s, a.shape[-1])
    both = lambda width: pl.BlockSpec((b, ts, width), lambda i: (0, i, 0))
    r_args, r_in = _router_operands(*router, b * ts)
    r_out, r_shapes, r_scratch = _router_results(t, pl.BlockSpec((2, b, ts), lambda i: (0, 0, i)), (2, b, s))
    out = pl.pallas_call(
        _dn_scan_kernel,
        grid=(s // ts,),
        in_specs=[both(d),
                  pl.BlockSpec((b, 6, d), lambda i: (0, 0, 0)),
                  both(DN_VAL_WIDTH), both(DN_VAL_WIDTH), both(qk_w), both(DN_VAL_WIDTH), both(DN_KEY_WIDTH),
                  both(DN_VAL_WIDTH), both(LANES),
                  pl.BlockSpec((1, DN_HEAD_DIM), lambda i: (0, 0)),
                  pl.BlockSpec((DN_VAL_WIDTH, d), lambda i: (0, 0))] + r_in,
        out_specs=[both(d)] + r_out,
        out_shape=[jax.ShapeDtypeStruct((b, s, d), F32)] + r_shapes,
        scratch_shapes=[pltpu.VMEM((b * nh, DN_HEAD_DIM, DN_HEAD_DIM), F32),
                        pltpu.VMEM((b * ts, DN_VAL_WIDTH), BF16), r_scratch],
        compiler_params=_params("arbitrary"),
        name="deltanet_scan",
    )(seq(x2), mod, seq(u), seq(w), seq(qk), seq(qd), seq(k), seq(z), seq(bg), o_norm.reshape(1, DN_HEAD_DIM),
      w_out.astype(BF16), *r_args)
    eid, gate, rank, cnt = out[1:]
    return out[0].reshape(t, d), (eid.reshape(2, t), gate.reshape(2, t), rank.reshape(2, t), cnt)


def kernel(x, c, l0_norm_mix, l0_norm_ffn, l0_ada_w, l0_ada_b, l0_gm_w_in, l0_gm_v_norm, l0_gm_w_s, l0_gm_b_s, l0_gm_w_out, l0_router_w, l0_router_b, l0_expert_w_in, l0_expert_w_out, l1_norm_mix, l1_norm_ffn, l1_ada_w, l1_ada_b, l1_dn_w_in, l1_dn_conv_w, l1_dn_a_log, l1_dn_dt_bias, l1_dn_o_norm, l1_dn_w_out, l1_router_w, l1_router_b, l1_expert_w_in, l1_expert_w_out, l2_norm_mix, l2_norm_ffn, l2_ada_w, l2_ada_b, l2_swa_w_in, l2_swa_q_norm, l2_swa_k_norm, l2_swa_sinks, l2_swa_w_out, l2_router_w, l2_router_b, l2_expert_w_in, l2_expert_w_out, l3_norm_mix, l3_norm_ffn, l3_ada_w, l3_ada_b, l3_gm_w_in, l3_gm_v_norm, l3_gm_w_s, l3_gm_b_s, l3_gm_w_out, l3_router_w, l3_router_b, l3_expert_w_in, l3_expert_w_out):
    b, s, d = x.shape
    x2 = x.reshape(b * s, d)
    tiles_per_batch_of = lambda tm: s // tm

    def modulation(ada_w, ada_b):
        return adaln(c, ada_w, ada_b).reshape(b, 6, d)

    mod = modulation(l0_ada_w, l0_ada_b)
    x2, routing = gmlp_layer(x2, mod, l0_norm_mix, l0_gm_w_in, l0_gm_v_norm, l0_gm_w_s, l0_gm_b_s, l0_gm_w_out,
                             (l0_norm_ffn, l0_router_w, l0_router_b), tiles_per_batch_of(TOKEN_TILE))
    x2 = moe_layer(x2, mod, l0_norm_ffn, routing, l0_expert_w_in, l0_expert_w_out, tiles_per_batch_of)

    mod = modulation(l1_ada_w, l1_ada_b)
    x2, routing = deltanet_layer(x2, mod, l1_norm_mix, l1_dn_w_in, l1_dn_conv_w, l1_dn_a_log, l1_dn_dt_bias,
                                 l1_dn_o_norm, l1_dn_w_out, (l1_norm_ffn, l1_router_w, l1_router_b), b, s)
    x2 = moe_layer(x2, mod, l1_norm_ffn, routing, l1_expert_w_in, l1_expert_w_out, tiles_per_batch_of)

    mod = modulation(l2_ada_w, l2_ada_b)
    x2, routing = swa_layer(x2, mod, l2_norm_mix, l2_swa_w_in, l2_swa_q_norm, l2_swa_k_norm, l2_swa_sinks,
                            l2_swa_w_out, (l2_norm_ffn, l2_router_w, l2_router_b), b, s)
    x2 = moe_layer(x2, mod, l2_norm_ffn, routing, l2_expert_w_in, l2_expert_w_out, tiles_per_batch_of)

    mod = modulation(l3_ada_w, l3_ada_b)
    x2, routing = gmlp_layer(x2, mod, l3_norm_mix, l3_gm_w_in, l3_gm_v_norm, l3_gm_w_s, l3_gm_b_s, l3_gm_w_out,
                             (l3_norm_ffn, l3_router_w, l3_router_b), tiles_per_batch_of(TOKEN_TILE))
    x2 = moe_layer(x2, mod, l3_norm_ffn, routing, l3_expert_w_in, l3_expert_w_out, tiles_per_batch_of)
    return x2.reshape(b, s, d)
```

```python
import functools
import math

import jax
import jax.numpy as jnp
from jax import lax
from jax.experimental import pallas as pl
from jax.experimental.pallas import tpu as pltpu

F32 = jnp.float32
BF16 = jnp.bfloat16

D_MODEL = 1024
NORM_EPS = 1e-6

GM_CHUNK = 128
GM_GROUPS = 8
GM_GROUP_DIM = D_MODEL // GM_GROUPS

MOE_GROUPS = 4
MOE_EXPERTS_PER_GROUP = 8
MOE_EXPERTS = MOE_GROUPS * MOE_EXPERTS_PER_GROUP
MOE_FF = D_MODEL // 2

VMEM_LIMIT_BYTES = 56 * 1024 * 1024

TOKEN_TILE = 512
GMLP_TILE = 1024
MOVE_TILE = 512
EXPERT_ROWS = 512
EXPERT_SUB_ROWS = 128
LANES = 128
SUBLANES = 8


def _params(*semantics):
    return pltpu.CompilerParams(dimension_semantics=semantics, vmem_limit_bytes=VMEM_LIMIT_BYTES,
                                disable_bounds_checks=True)


def _rms(xf):
    return xf * lax.rsqrt(jnp.mean(xf * xf, axis=-1, keepdims=True) + NORM_EPS)


def _modulated_norm(x, gain, shift, scale):
    return _rms(x) * gain * (1.0 + scale) + shift


def _gelu_tanh(x):
    return 0.5 * x * (1.0 + jnp.tanh(math.sqrt(2.0 / math.pi) * (x + 0.044715 * (x * x * x))))


def _silu(x):
    return x * (1.0 / (1.0 + jnp.exp(-x)))


def _adaln_kernel(ct_ref, w_ref, b_ref, o_ref):
    act = _silu(ct_ref[...])
    w = w_ref[...]
    o_ref[...] = jnp.concatenate([jnp.sum(act[:, r:r + 1] * w, axis=0, keepdims=True)
                                  for r in range(o_ref.shape[0])], axis=0) + b_ref[...]


def adaln(c, ada_w, ada_b):
    rows, d = c.shape
    n = ada_w.shape[1]
    tn = d
    return pl.pallas_call(
        _adaln_kernel,
        grid=(n // tn,),
        in_specs=[pl.BlockSpec((d, rows), lambda j: (0, 0)),
                  pl.BlockSpec((d, tn), lambda j: (0, j)),
                  pl.BlockSpec((1, tn), lambda j: (0, j))],
        out_specs=pl.BlockSpec((rows, tn), lambda j: (0, j)),
        out_shape=jax.ShapeDtypeStruct((rows, n), F32),
        compiler_params=_params("arbitrary"),
        name="adaln",
    )(c.T, ada_w, ada_b.reshape(1, n))


def _gmlp_kernel(x_ref, mod_ref, gain_ref, win_ref, vn_ref, ws_ref, bs_ref, wout_ref, fgain_ref, rw_ref, rb_ref, before_ref,
                 o_ref, eid_ref, gate_ref, rank_ref, cnt_ref, u_ref, v_ref, g_ref, base_ref):
    width = D_MODEL
    x = x_ref[...]
    mod = mod_ref[0]
    h = _modulated_norm(x, gain_ref[...], mod[0:1], mod[1:2])
    z = _gelu_tanh(jnp.dot(h.astype(BF16), win_ref[...], preferred_element_type=F32))
    u_ref[...] = z[:, :width]
    v_ref[...] = (_rms(z[:, width:]) * vn_ref[...]).astype(BF16)
    cells = [(slice(c * GM_CHUNK, (c + 1) * GM_CHUNK), g, slice(g * GM_GROUP_DIM, (g + 1) * GM_GROUP_DIM))
             for c in range(x.shape[0] // GM_CHUNK) for g in range(GM_GROUPS)]
    mixed = [jnp.dot(ws_ref[g], v_ref[rows, cols], preferred_element_type=F32) for rows, g, cols in cells]
    for (rows, g, cols), sv in zip(cells, mixed):
        g_ref[rows, cols] = (u_ref[rows, cols] * (sv + bs_ref[:, g:g + 1])).astype(BF16)
    y = jnp.dot(g_ref[...], wout_ref[...], preferred_element_type=F32)
    x_new = x + mod[2:3] * y
    o_ref[...] = x_new
    eid_ref[...], gate_ref[...], rank_ref[...] = _route(
        x_new, mod[3:4], mod[4:5], pl.program_id(0) == 0, fgain_ref, rw_ref, rb_ref, before_ref, cnt_ref, base_ref)


def gmlp_layer(x2, mod, gain, w_in, v_norm, w_s, b_s, w_out, router, tiles_per_batch):
    t, d = x2.shape
    tm = GMLP_TILE
    ws_causal =jnp.where(jnp.tril(jnp.ones((GM_CHUNK, GM_CHUNK), dtype=bool)), w_s, 0).astype(BF16)
    const2 = lambda i: (0, 0)
    r_args, r_in = _router_operands(*router, tm)
    r_out, r_shapes, r_scratch = _router_results(t, pl.BlockSpec((2, tm), lambda i: (0, i)), (2, t))
    out = pl.pallas_call(
        _gmlp_kernel,
        grid=(t // tm,),
        in_specs=[pl.BlockSpec((tm, d), lambda i: (i, 0)),
                  pl.BlockSpec((1, 6, d), lambda i: (i // tiles_per_batch, 0, 0)),
                  pl.BlockSpec((1, d), const2),
                  pl.BlockSpec((d, 2 * d), const2),
                  pl.BlockSpec((1, d), const2),
                  pl.BlockSpec((GM_GROUPS, GM_CHUNK, GM_CHUNK), lambda i: (0, 0, 0)),
                  pl.BlockSpec((GM_CHUNK, GM_GROUPS), const2),
                  pl.BlockSpec((d, d), const2)] + r_in,
        out_specs=[pl.BlockSpec((tm, d), lambda i: (i, 0))] + r_out,
        out_shape=[jax.ShapeDtypeStruct((t, d), F32)] + r_shapes,
        scratch_shapes=[pltpu.VMEM((tm, d), F32), pltpu.VMEM((tm, d), BF16), pltpu.VMEM((tm, d), BF16), r_scratch],
        compiler_params=_params("arbitrary"),
        name="gmlp_mixer",
    )(x2, mod, gain.reshape(1, d), w_in.astype(BF16), v_norm.reshape(1, d), ws_causal, b_s.T,
      w_out.astype(BF16), *r_args)
    return out[0], tuple(out[1:])


def _route(x, shift, scale, first, gain_ref, rw_ref, rb_ref, before_ref, cnt_ref, base_ref):
    tm = x.shape[0]
    ne = MOE_EXPERTS
    npg = MOE_EXPERTS_PER_GROUP

    @pl.when(first)
    def _():
        base_ref[...] = jnp.zeros_like(base_ref)

    hh, hl = _split_bf16(_modulated_norm(x, gain_ref[...], shift, scale))
    nt = (((1,), (1,)), ((), ()))
    nr = ROUTER_ROWS
    by_hi = lax.dot_general(rw_ref[...], hh, nt, preferred_element_type=F32)
    by_lo = lax.dot_general(rw_ref[0:nr, :], hl, nt, preferred_element_type=F32)
    lt = (by_hi[0:nr] + rb_ref[...]) + (by_hi[nr:] + by_lo)
    lg = [lt[ne + g:ne + g + 1, :] for g in range(MOE_GROUPS)]
    gmax = functools.reduce(jnp.maximum, lg)
    gsum = functools.reduce(lambda a, b: a + b, [jnp.exp(l - gmax) for l in lg])
    pg_top = 1.0 / gsum
    g_sel = jnp.full(gmax.shape, MOE_GROUPS - 1, jnp.int32)
    for g in range(MOE_GROUPS - 2, -1, -1):
        g_sel = jnp.where(lg[g] == gmax, g, g_sel)
    sel = lt[(MOE_GROUPS - 1) * npg:MOE_GROUPS * npg, :]
    for g in range(MOE_GROUPS - 2, -1, -1):
        sel = jnp.where(g_sel == g, lt[g * npg:(g + 1) * npg, :], sel)
    row = lax.broadcasted_iota(jnp.int32, sel.shape, 0)
    m1 = jnp.max(sel, axis=0, keepdims=True)
    i1 = jnp.min(jnp.where(sel == m1, row, npg), axis=0, keepdims=True)
    rest = jnp.where(row == i1, -jnp.inf, sel)
    m2 = jnp.max(rest, axis=0, keepdims=True)
    i2 = jnp.min(jnp.where(rest == m2, row, npg), axis=0, keepdims=True)
    e2 = jnp.exp(m2 - m1)
    inv = pg_top / (1.0 + e2)
    eid = jnp.concatenate([g_sel * npg + i1, g_sel * npg + i2], axis=0)
    gates = jnp.concatenate([inv, inv * e2], axis=0)

    erow = lax.broadcasted_iota(jnp.int32, (ne, tm), 0)
    hits = [erow == eid[k:k + 1, :] for k in range(2)]
    prefix = jnp.dot(jnp.concatenate([jnp.where(hit, 1.0, 0.0).astype(BF16) for hit in hits], axis=0),
                     before_ref[...], preferred_element_type=F32)
    base = base_ref[...]
    ranks = []
    for k, hit in enumerate(hits):
        ranks.append(jnp.sum(jnp.where(hit, prefix[k * ne:(k + 1) * ne] + base, 0.0), axis=0, keepdims=True))
        base = base + jnp.sum(jnp.where(hit, 1.0, 0.0), axis=1, keepdims=True)
    base_ref[...] = base
    cnt_ref[...] = jnp.broadcast_to(base, cnt_ref.shape).astype(jnp.int32)
    return eid, gates, jnp.concatenate(ranks, axis=0).astype(jnp.int32)


ROUTER_ROWS = MOE_EXPERTS + 2 * SUBLANES


def _router_operands(gain, router_w, router_b, tm):
    d = router_w.shape[0]
    pad = ROUTER_ROWS - MOE_EXPERTS - MOE_GROUPS
    rw = jnp.concatenate([router_w[:, MOE_GROUPS:], router_w[:, :MOE_GROUPS], jnp.zeros((d, pad), F32)], axis=1).T
    rw_hi = rw.astype(BF16)
    rw2 = jnp.concatenate([rw_hi, (rw - rw_hi.astype(F32)).astype(BF16)], axis=0)
    rb = jnp.concatenate([router_b[MOE_GROUPS:], router_b[:MOE_GROUPS], jnp.zeros((pad,), F32)])
    pos = jnp.arange(tm)
    before = (pos[:, None] < pos[None, :]).astype(BF16)
    const = lambda *_: (0, 0)
    specs = [pl.BlockSpec((1, d), const), pl.BlockSpec((2 * ROUTER_ROWS, d), const),
             pl.BlockSpec((ROUTER_ROWS, 1), const), pl.BlockSpec((tm, tm), const)]
    return [gain.reshape(1, d), rw2, rb.reshape(ROUTER_ROWS, 1), before], specs


def _router_results(t, slot_spec, slot_shape):
    specs = [slot_spec, slot_spec, slot_spec, pl.BlockSpec((MOE_EXPERTS, LANES), lambda *_: (0, 0))]
    shapes = [jax.ShapeDtypeStruct(slot_shape, jnp.int32), jax.ShapeDtypeStruct(slot_shape, F32),
              jax.ShapeDtypeStruct(slot_shape, jnp.int32), jax.ShapeDtypeStruct((MOE_EXPERTS, LANES), jnp.int32)]
    return specs, shapes, pltpu.VMEM((MOE_EXPERTS, 1), F32)


def _wait_rows(hbm_ref, n_rows, sem, times):
    rows = hbm_ref.at[pl.ds(0, n_rows)]
    for _ in range(times):
        pltpu.make_async_copy(rows, rows, sem).wait()


def _tiled_shape(n, d):
    return (n // SUBLANES, d // LANES, SUBLANES, LANES)


def _store_tiled(ref, a):
    n, d = a.shape
    for j in range(d // LANES):
        ref[:, j, :, :] = a[:, j * LANES:(j + 1) * LANES].reshape(n // SUBLANES, SUBLANES, LANES)


def _load_tiled(ref):
    g, nj, _, _ = ref.shape
    return jnp.concatenate([ref[:, j, :, :].reshape(g * SUBLANES, LANES) for j in range(nj)], axis=1)


def _dispatch_kernel(d0_ref, d1_ref, x_ref, mod_ref, gain_ref, xs_ref, h_ref, sem):
    i = pl.program_id(0)
    last = pl.num_programs(0) - 1
    tm, d = x_ref.shape
    mod = mod_ref[0]
    h = _modulated_norm(x_ref[...], gain_ref[...], mod[3:4], mod[4:5])

    def step(buf):
        _store_tiled(h_ref.at[buf], h)

        def issue(g, carry):
            for u in range(SUBLANES):
                for k, dest_ref in enumerate((d0_ref, d1_ref)):
                    pltpu.make_async_copy(h_ref.at[buf, g, :, u, :], xs_ref.at[dest_ref[g * SUBLANES + u]],
                                          sem.at[buf]).start(priority=k)
            return carry

        lax.fori_loop(0, tm // SUBLANES, issue, 0)

        @pl.when(i > 0)
        def _():
            _wait_rows(xs_ref, tm, sem.at[1 - buf], 2)

        @pl.when(i == last)
        def _():
            _wait_rows(xs_ref, tm, sem.at[buf], 2)

    for buf in range(2):
        pl.when(i % 2 == buf)(functools.partial(step, buf))


def moe_dispatch(x2, mod, gain, dest, tiles_per_batch_of):
    t, d = x2.shape
    tm = MOVE_TILE
    tpb = tiles_per_batch_of(tm)
    const2 = lambda i: (0, 0)
    slot_spec = pl.BlockSpec((tm,), lambda i: (i,), memory_space=pltpu.SMEM)
    return pl.pallas_call(
        _dispatch_kernel,
        grid=(t // tm,),
        in_specs=[slot_spec, slot_spec,
                  pl.BlockSpec((tm, d), lambda i: (i, 0)),
                  pl.BlockSpec((1, 6, d), lambda i: (i // tpb, 0, 0)),
                  pl.BlockSpec((1, d), const2)],
        out_specs=pl.BlockSpec(memory_space=pl.ANY),
        out_shape=jax.ShapeDtypeStruct((2 * t, d // LANES, LANES), F32),
        scratch_shapes=[pltpu.VMEM((2,) + _tiled_shape(tm, d), F32), pltpu.SemaphoreType.DMA((2,))],
        compiler_params=_params("arbitrary"),
        name="moe_dispatch",
    )(dest[0], dest[1], x2, mod, gain.reshape(1, d))


def _expert_kernel(blk_ref, exp_ref, start_ref, nvalid_ref, wslot_ref, wnext_ref, xs_ref, win_ref, wout_ref, ys_ref,
                   winb_ref, woutb_ref, xbuf_ref, ybuf_ref, winf_ref, woutf_ref, in_sem, out_sem, w_sem):
    i = pl.program_id(0)
    n_items = pl.num_programs(0)
    bm = EXPERT_ROWS
    groups = bm // SUBLANES
    n_blocks = xs_ref.shape[0] // groups
    e = exp_ref[i]
    blk = blk_ref[i]
    prev = jnp.maximum(i - 1, 0)
    nxt = jnp.minimum(i + 1, n_items - 1)
    first = i == 0
    final = i == n_items - 1
    lo = start_ref[e]
    hi = start_ref[e + 1]
    row0 = blk * bm
    live = i < nvalid_ref[0]
    slot = blk % 2

    def block_copies(hbm_ref, buf_ref, b, s, sem, to_hbm):
        copies = []
        for u in range(SUBLANES):
            hbm = hbm_ref.at[pl.ds(b * groups, groups), u]
            vmem = buf_ref.at[s, :, :, u, :]
            copies.append(pltpu.make_async_copy(vmem, hbm, sem.at[s]) if to_hbm
                          else pltpu.make_async_copy(hbm, vmem, sem.at[s]))
        return copies

    def fetch(b, s):
        for c in block_copies(xs_ref, xbuf_ref, b, s, in_sem, False):
            c.start()

    def weight_copies(ex, s):
        return (pltpu.make_async_copy(win_ref.at[ex], winf_ref.at[s], w_sem.at[s]),
                pltpu.make_async_copy(wout_ref.at[ex], woutf_ref.at[s], w_sem.at[s]))

    wslot = wslot_ref[e]

    @pl.when(first)
    def _():
        for c in weight_copies(e, wslot):
            c.start()

    @pl.when(jnp.logical_or(first, exp_ref[prev] != e))
    def _():
        for c in weight_copies(e, wslot):
            c.wait()
        winb_ref[...] = winf_ref[wslot].astype(BF16)
        woutb_ref[...] = woutf_ref[wslot].astype(BF16)

        @pl.when(wnext_ref[e] != e)
        def _():
            for c in weight_copies(wnext_ref[e], 1 - wslot):
                c.start()

    new_block = jnp.logical_or(first, blk_ref[prev] != blk)
    last_of_block = jnp.logical_or(final, blk_ref[nxt] != blk)
    whole = jnp.logical_and(lo <= row0, hi >= row0 + bm)

    @pl.when(first)
    def _():
        fetch(0, 0)

    @pl.when(new_block)
    def _():
        for c in block_copies(xs_ref, xbuf_ref, blk, slot, in_sem, False):
            c.wait()

        @pl.when(blk + 1 < n_blocks)
        def _():
            fetch(blk + 1, 1 - slot)

        @pl.when(blk >= 2)
        def _():
            for c in block_copies(ys_ref, ybuf_ref, blk - 2, slot, out_sem, True):
                c.wait()

    def ffn(x):
        a_gl = jnp.dot(x.astype(BF16), winb_ref[...], preferred_element_type=F32)
        mid = (_silu(a_gl[:, :MOE_FF]) * a_gl[:, MOE_FF:]).astype(BF16)
        return jnp.dot(mid, woutb_ref[...], preferred_element_type=F32)

    @pl.when(jnp.logical_and(live, whole))
    def _():
        _store_tiled(ybuf_ref.at[slot], ffn(_load_tiled(xbuf_ref.at[slot])))

    @pl.when(jnp.logical_and(new_block, jnp.logical_not(whole)))
    def _():
        ybuf_ref[slot] = jnp.zeros(ybuf_ref.shape[1:], F32)

    sub_groups = EXPERT_SUB_ROWS // SUBLANES
    for sub in range(bm // EXPERT_SUB_ROWS):
        sub0 = row0 + sub * EXPERT_SUB_ROWS
        touched = jnp.logical_and(hi > sub0, lo < sub0 + EXPERT_SUB_ROWS)

        @pl.when(jnp.logical_and(jnp.logical_and(live, jnp.logical_not(whole)), touched))
        def _():
            gs = pl.ds(sub * sub_groups, sub_groups)
            rows = sub0 + lax.broadcasted_iota(jnp.int32, (EXPERT_SUB_ROWS, 1), 0)
            mine = jnp.logical_and(rows >= lo, rows < hi)
            y = jnp.where(mine, ffn(_load_tiled(xbuf_ref.at[slot, gs])), 0.0)
            _store_tiled(ybuf_ref.at[slot, gs], _load_tiled(ybuf_ref.at[slot, gs]) + y)

    @pl.when(last_of_block)
    def _():
        for c in block_copies(ys_ref, ybuf_ref, blk, slot, out_sem, True):
            c.start()

    @pl.when(final)
    def _():
        for c in block_copies(ys_ref, ybuf_ref, blk, slot, out_sem, True):
            c.wait()

        @pl.when(blk >= 1)
        def _():
            for c in block_copies(ys_ref, ybuf_ref, blk - 1, 1 - slot, out_sem, True):
                c.wait()


def moe_experts(xs, w_in, w_out, item_block, item_expert, seg_start, n_valid, w_slot, w_next):
    n, nj, _ = xs.shape
    d = nj * LANES
    bm = EXPERT_ROWS
    n_items = item_block.shape[0]
    by_group = (n // SUBLANES, SUBLANES, nj, LANES)
    in_hbm = pl.BlockSpec(memory_space=pl.ANY)
    grid_spec = pltpu.PrefetchScalarGridSpec(
        num_scalar_prefetch=6,
        grid=(n_items,),
        in_specs=[in_hbm, in_hbm, in_hbm],
        out_specs=pl.BlockSpec(memory_space=pl.ANY),
        scratch_shapes=[pltpu.VMEM((d, 2 * MOE_FF), BF16), pltpu.VMEM((MOE_FF, d), BF16),
                        pltpu.VMEM((2,) + _tiled_shape(bm, d), F32), pltpu.VMEM((2,) + _tiled_shape(bm, d), F32),
                        pltpu.VMEM((2, d, 2 * MOE_FF), F32), pltpu.VMEM((2, MOE_FF, d), F32),
                        pltpu.SemaphoreType.DMA((2,)), pltpu.SemaphoreType.DMA((2,)), pltpu.SemaphoreType.DMA((2,))],
    )
    ys = pl.pallas_call(
        _expert_kernel,
        grid_spec=grid_spec,
        out_shape=jax.ShapeDtypeStruct(by_group, F32),
        compiler_params=_params("arbitrary"),
        name="moe_experts",
    )(item_block, item_expert, seg_start, n_valid, w_slot, w_next, xs.reshape(by_group), w_in, w_out)
    return ys.reshape(n, nj, LANES)


def _combine_kernel(d0_ref, d1_ref, n0_ref, n1_ref, x_ref, mod_ref, gate_ref, ys_ref, o_ref, y_ref, sem):
    i = pl.program_id(0)
    last = pl.num_programs(0) - 1
    tm, d = x_ref.shape

    def gather(dest_refs, buf):
        def issue(g, carry):
            for u in range(SUBLANES):
                for k, dest_ref in enumerate(dest_refs):
                    pltpu.make_async_copy(ys_ref.at[dest_ref[g * SUBLANES + u]],
                                          y_ref.at[buf, k, g, :, u, :], sem.at[buf]).start(priority=k)
            return carry

        lax.fori_loop(0, tm // SUBLANES, issue, 0)

    def step(buf):
        @pl.when(i == 0)
        def _():
            gather((d0_ref, d1_ref), buf)

        @pl.when(i < last)
        def _():
            gather((n0_ref, n1_ref), 1 - buf)

        _wait_rows(ys_ref, tm, sem.at[buf], 2)
        gates = gate_ref[...]
        moe = gates[:, 0:1] * _load_tiled(y_ref.at[buf, 0]) + gates[:, 1:2] * _load_tiled(y_ref.at[buf, 1])
        o_ref[...] = x_ref[...] + mod_ref[0][5:6] * moe

    for buf in range(2):
        pl.when(i % 2 == buf)(functools.partial(step, buf))


def moe_combine(x2, mod, gates_t, dest, ys, tiles_per_batch_of):
    t, d = x2.shape
    tm = MOVE_TILE
    tpb = tiles_per_batch_of(tm)
    n_tiles = t // tm
    slot_spec = pl.BlockSpec((tm,), lambda i: (i,), memory_space=pltpu.SMEM)
    next_spec = pl.BlockSpec((tm,), lambda i: (jnp.minimum(i + 1, n_tiles - 1),), memory_space=pltpu.SMEM)
    return pl.pallas_call(
        _combine_kernel,
        grid=(n_tiles,),
        in_specs=[slot_spec, slot_spec, next_spec, next_spec,
                  pl.BlockSpec((tm, d), lambda i: (i, 0)),
                  pl.BlockSpec((1, 6, d), lambda i: (i // tpb, 0, 0)),
                  pl.BlockSpec((tm, 2), lambda i: (i, 0)),
                  pl.BlockSpec(memory_space=pl.ANY)],
        out_specs=pl.BlockSpec((tm, d), lambda i: (i, 0)),
        out_shape=jax.ShapeDtypeStruct((t, d), F32),
        scratch_shapes=[pltpu.VMEM((2, 2) + _tiled_shape(tm, d), F32), pltpu.SemaphoreType.DMA((2,))],
        compiler_params=_params("arbitrary"),
        name="moe_combine",
    )(dest[0], dest[1], dest[0], dest[1], x2, mod, gates_t, ys)


def moe_layer(x2, mod, gain, routing, w_in, w_out, tiles_per_batch_of):
    t, d = x2.shape
    bm = EXPERT_ROWS
    n_assign = 2 * t
    eid, gate, rank, cnt = routing
    ne = MOE_EXPERTS
    experts = jnp.arange(ne, dtype=jnp.int32)
    upto = experts[None, :] <= experts[:, None]

    def running_total(v):
        return jnp.sum(jnp.where(upto, v[None, :], 0), axis=1).astype(jnp.int32)

    def lookup(table, idx):
        hit = idx[None] == experts.reshape((ne,) + (1,) * idx.ndim)
        return jnp.sum(jnp.where(hit, table.reshape((ne,) + (1,) * idx.ndim), 0), axis=0).astype(jnp.int32)

    counts = cnt[:, 0]
    seg_end = running_total(counts)
    seg_start = jnp.concatenate([jnp.zeros((1,), jnp.int32), seg_end])
    dest = lookup(seg_start[:ne], eid) + rank
    n_blocks = n_assign // bm
    n_items = n_blocks + ne - 1
    first_blk = seg_start[:ne] // bm
    last_blk = jnp.where(counts > 0, (seg_end - 1) // bm, first_blk - 1)
    per_expert = jnp.maximum(last_blk - first_blk + 1, 0)
    item_end = running_total(per_expert)
    n_valid = item_end[ne - 1]
    item_ids = jnp.arange(n_items, dtype=jnp.int32)
    item_ids_c = jnp.maximum(jnp.minimum(item_ids, n_valid - 1), 0)
    item_expert = jnp.minimum(jnp.sum(item_ids_c[:, None] >= item_end[None, :], axis=1), ne - 1).astype(jnp.int32)
    item_block = lookup(first_blk - (item_end - per_expert), item_expert) + item_ids_c
    has_rows = counts > 0
    w_slot = (running_total(has_rows.astype(jnp.int32)) + 1) % 2
    later = jnp.logical_and(experts[None, :] > experts[:, None], has_rows[None, :])
    w_next = jnp.min(jnp.where(later, experts[None, :], ne), axis=1)
    w_next = jnp.where(w_next == ne, experts, w_next).astype(jnp.int32)
    xs = moe_dispatch(x2, mod, gain, dest, tiles_per_batch_of)
    ys = moe_experts(xs, w_in, w_out, item_block, item_expert, seg_start,
                     n_valid.reshape(1).astype(jnp.int32), w_slot.astype(jnp.int32), w_next)
    return moe_combine(x2, mod, gate.T, dest, ys, tiles_per_batch_of)


SWA_HEAD_DIM = 64
SWA_Q_HEADS = D_MODEL // SWA_HEAD_DIM
SWA_KV_HEADS = 4
SWA_BLOCK = 128
SWA_Q_WIDTH = SWA_Q_HEADS * SWA_HEAD_DIM
SWA_KV_WIDTH = SWA_KV_HEADS * SWA_HEAD_DIM
HEADS_PER_VREG = LANES // SWA_HEAD_DIM


def _swa_kernel(sink_ref, x_ref, mod_ref, gain_ref, win_ref, qg_ref, kg_ref, ones_ref, wout_ref,
                fgain_ref, rw_ref, rb_ref, before_ref, o_ref, eid_ref, gate_ref, rank_ref, cnt_ref,
                kx_ref, vx_ref, att_ref, base_ref):
    t = pl.program_id(1)
    tm = x_ref.shape[0]
    blk = SWA_BLOCK
    hd = SWA_HEAD_DIM

    @pl.when(t == 0)
    def _():
        kx_ref[0:blk, :] = jnp.zeros((blk, kx_ref.shape[1]), BF16)
        vx_ref[0:blk, :] = jnp.zeros((blk, vx_ref.shape[1]), BF16)

    x = x_ref[...]
    mod = mod_ref[0]
    h = _modulated_norm(x, gain_ref[...], mod[0:1], mod[1:2]).astype(BF16)
    proj = jnp.dot(h, win_ref[...], preferred_element_type=F32)

    def head_rms(a):
        ss = jnp.dot((a * a).astype(BF16), ones_ref[...], preferred_element_type=F32)
        return a * lax.rsqrt(ss * (1.0 / hd) + NORM_EPS)

    lane = lax.broadcasted_iota(jnp.int32, (tm, LANES), 1)
    low = lane < hd
    for c in range(SWA_Q_WIDTH // LANES):
        cols = slice(c * LANES, (c + 1) * LANES)
        att_ref[:, cols] = (head_rms(proj[:, cols]) * qg_ref[:, cols]).astype(BF16)
    for c in range(SWA_KV_WIDTH // LANES):
        cols = slice(c * LANES, (c + 1) * LANES)
        kc = head_rms(proj[:, SWA_Q_WIDTH + c * LANES:SWA_Q_WIDTH + (c + 1) * LANES]) * kg_ref[:, cols]
        vc = proj[:, SWA_Q_WIDTH + SWA_KV_WIDTH + c * LANES:SWA_Q_WIDTH + SWA_KV_WIDTH + (c + 1) * LANES]
        for ref, a in ((kx_ref, kc), (vx_ref, vc)):
            even_lo = jnp.where(low, a, 0.0)
            odd_hi = jnp.where(low, 0.0, a)
            j0 = HEADS_PER_VREG * c
            ref[blk:blk + tm, (2 * j0) * LANES:(2 * j0 + 1) * LANES] = even_lo.astype(BF16)
            ref[blk:blk + tm, (2 * j0 + 1) * LANES:(2 * j0 + 2) * LANES] = pltpu.roll(even_lo, hd, 1).astype(BF16)
            ref[blk:blk + tm, (2 * j0 + 2) * LANES:(2 * j0 + 3) * LANES] = pltpu.roll(odd_hi, hd, 1).astype(BF16)
            ref[blk:blk + tm, (2 * j0 + 3) * LANES:(2 * j0 + 4) * LANES] = odd_hi.astype(BF16)

    qi = lax.broadcasted_iota(jnp.int32, (blk, 2 * blk), 0)
    kj = lax.broadcasted_iota(jnp.int32, (blk, 2 * blk), 1)
    lane_q = lax.broadcasted_iota(jnp.int32, (blk, LANES), 1) < hd
    pairs_per_kv = SWA_Q_HEADS // SWA_KV_HEADS // HEADS_PER_VREG
    for i in range(tm // blk):
        rows = slice(i * blk, (i + 1) * blk)
        keys = slice(i * blk, (i + 2) * blk)
        floor = jnp.where(t == 0, blk - 1, qi) if i == 0 else qi
        mask = jnp.logical_and(kj > floor, kj <= qi + blk)
        heads = [(p, half) for p in range(SWA_Q_HEADS // HEADS_PER_VREG) for half in range(HEADS_PER_VREG)]
        col_of = lambda p, half: (2 * (p // pairs_per_kv) + half) * LANES
        scores = [lax.dot_general(att_ref[rows, p * LANES:(p + 1) * LANES],
                                  kx_ref[keys, col_of(p, half):col_of(p, half) + LANES],
                                  (((1,), (1,)), ((), ())), preferred_element_type=F32) for p, half in heads]
        probs, inv = [], []
        for (p, half), sc in zip(heads, scores):
            sc = jnp.where(mask, sc, -jnp.inf)
            sink = sink_ref[HEADS_PER_VREG * p + half]
            m = jnp.maximum(jnp.max(sc, axis=-1, keepdims=True), sink)
            pr = jnp.exp(sc - m)
            inv.append(1.0 / (jnp.sum(pr, axis=-1, keepdims=True) + jnp.exp(sink - m)))
            probs.append(pr.astype(BF16))
        pvs = [jnp.dot(pr, vx_ref[keys, col_of(p, half):col_of(p, half) + LANES], preferred_element_type=F32)
               for (p, half), pr in zip(heads, probs)]
        for p in range(SWA_Q_HEADS // HEADS_PER_VREG):
            out = (pvs[2 * p] + pvs[2 * p + 1]) * jnp.where(lane_q, inv[2 * p], inv[2 * p + 1])
            att_ref[rows, p * LANES:(p + 1) * LANES] = out.astype(BF16)
    kx_ref[0:blk, :] = kx_ref[tm:tm + blk, :]
    vx_ref[0:blk, :] = vx_ref[tm:tm + blk, :]
    y = jnp.dot(att_ref[...], wout_ref[...], preferred_element_type=F32)
    x_new = x + mod[2:3] * y
    o_ref[...] = x_new
    first = jnp.logical_and(pl.program_id(0) == 0, t == 0)
    eid_ref[...], gate_ref[...], rank_ref[...] = _route(
        x_new, mod[3:4], mod[4:5], first, fgain_ref, rw_ref, rb_ref, before_ref, cnt_ref, base_ref)


def swa_layer(x2, mod, gain, w_in, q_norm, k_norm, sinks, w_out, router, b, s):
    t, d = x2.shape
    tm = TOKEN_TILE
    tpb = s // tm
    hd = SWA_HEAD_DIM
    qg = (jnp.tile(q_norm, SWA_Q_HEADS) * hd ** -0.5).reshape(1, SWA_Q_WIDTH)
    kg = jnp.tile(k_norm, SWA_KV_HEADS).reshape(1, SWA_KV_WIDTH)
    ids = jnp.arange(LANES) // hd
    ones = (ids[:, None] == ids[None, :]).astype(BF16)
    proj_w = SWA_Q_WIDTH + 2 * SWA_KV_WIDTH
    ext_w = 2 * SWA_KV_HEADS * LANES
    const2 = lambda bi, ti: (0, 0)
    tile = lambda bi, ti: (bi * tpb + ti, 0)
    r_args, r_in = _router_operands(*router, tm)
    r_out, r_shapes, r_scratch = _router_results(t, pl.BlockSpec((2, tm), lambda bi, ti: (0, bi * tpb + ti)), (2, t))
    out = pl.pallas_call(
        _swa_kernel,
        grid=(b, tpb),
        in_specs=[pl.BlockSpec(memory_space=pltpu.SMEM),
                  pl.BlockSpec((tm, d), tile),
                  pl.BlockSpec((1, 6, d), lambda bi, ti: (bi, 0, 0)),
                  pl.BlockSpec((1, d), const2),
                  pl.BlockSpec((d, proj_w), const2),
                  pl.BlockSpec((1, SWA_Q_WIDTH), const2),
                  pl.BlockSpec((1, SWA_KV_WIDTH), const2),
                  pl.BlockSpec((LANES, LANES), const2),
                  pl.BlockSpec((SWA_Q_WIDTH, d), const2)] + r_in,
        out_specs=[pl.BlockSpec((tm, d), tile)] + r_out,
        out_shape=[jax.ShapeDtypeStruct((t, d), F32)] + r_shapes,
        scratch_shapes=[pltpu.VMEM((tm + SWA_BLOCK, ext_w), BF16), pltpu.VMEM((tm + SWA_BLOCK, ext_w), BF16),
                        pltpu.VMEM((tm, SWA_Q_WIDTH), BF16), r_scratch],
        compiler_params=_params("arbitrary", "arbitrary"),
        name="swa_mixer",
    )(sinks, x2, mod, gain.reshape(1, d), w_in.astype(BF16), qg, kg, ones, w_out.astype(BF16), *r_args)
    return out[0], tuple(out[1:])


DN_QK_HEADS = 4
DN_V_HEADS = 8
DN_HEAD_DIM = D_MODEL // DN_V_HEADS
DN_CONV = 4
DN_CHUNK = 64
DN_KEY_WIDTH = DN_QK_HEADS * DN_HEAD_DIM
DN_VAL_WIDTH = DN_V_HEADS * DN_HEAD_DIM
DN_CONV_WIDTH = 2 * DN_KEY_WIDTH + DN_VAL_WIDTH
DN_PREP_TILE = 128
DN_SCAN_TILE = 256


def _split_bf16(a):
    hi = a.astype(BF16)
    return hi, (a - hi.astype(F32)).astype(BF16)


def _mm(a, b):
    return jnp.dot(a.astype(BF16), b.astype(BF16), preferred_element_type=F32)


def _lane_sums(a):
    return jnp.dot(a.astype(BF16), jnp.ones((LANES, LANES), BF16), preferred_element_type=F32)


def _dn_proj_kernel(x_ref, mod_ref, gain_ref, w_ref, wba_ref, conv_ref, rate_ref, dtb_ref,
                    q_ref, k_ref, v_ref, z_ref, bg_ref, ext_ref):
    t = pl.program_id(1)
    tm = x_ref.shape[0]
    pad = SUBLANES

    @pl.when(t == 0)
    def _():
        ext_ref[0:pad, :] = jnp.zeros((pad, ext_ref.shape[1]), F32)

    mod = mod_ref[0]
    h = _modulated_norm(x_ref[...], gain_ref[...], mod[0:1], mod[1:2])
    hh, hl = _split_bf16(h)
    proj = jnp.dot(hh, w_ref[...], preferred_element_type=F32)
    z_ref[...] = proj[:, DN_CONV_WIDTH:].astype(BF16)

    ext_ref[pad:pad + tm, :] = proj[:, :DN_CONV_WIDTH]
    acc = None
    for j in range(DN_CONV):
        start = pad - (DN_CONV - 1) + j
        term = conv_ref[j:j + 1, :] * ext_ref[start:start + tm, :]
        acc = term if acc is None else acc + term
    ext_ref[0:pad, :] = ext_ref[tm:tm + pad, :]
    qkv = _silu(acc)
    for hq in range(2 * DN_QK_HEADS):
        cols = slice(hq * DN_HEAD_DIM, (hq + 1) * DN_HEAD_DIM)
        a = qkv[:, cols]
        a = a * lax.rsqrt(jnp.sum(a * a, axis=-1, keepdims=True) + NORM_EPS)
        if hq < DN_QK_HEADS:
            q_ref[:, cols] = (a * DN_HEAD_DIM ** -0.5).astype(BF16)
        else:
            k_ref[:, hq * DN_HEAD_DIM - DN_KEY_WIDTH:(hq + 1) * DN_HEAD_DIM - DN_KEY_WIDTH] = a.astype(BF16)
    v_ref[...] = qkv[:, 2 * DN_KEY_WIDTH:].astype(BF16)

    ba = jnp.dot(hh, wba_ref[...], preferred_element_type=F32) + jnp.dot(hl, wba_ref[...], preferred_element_type=F32)
    ba = ba + pltpu.roll(ba, LANES - 2 * DN_V_HEADS, 1)
    lane = lax.broadcasted_iota(jnp.int32, ba.shape, 1)
    beta = 1.0 / (1.0 + jnp.exp(-ba))
    sp = ba + dtb_ref[...]
    g = -rate_ref[...] * (jnp.maximum(sp, 0.0) + jnp.log1p(jnp.exp(-jnp.abs(sp))))
    g = jnp.where(jnp.logical_and(lane >= DN_V_HEADS, lane < 2 * DN_V_HEADS), g, 0.0)
    r = lax.broadcasted_iota(jnp.int32, (tm, tm), 0)
    c = lax.broadcasted_iota(jnp.int32, (tm, tm), 1)
    tri = jnp.where(jnp.logical_and(r // DN_CHUNK == c // DN_CHUNK, c <= r), 1.0, 0.0).astype(BF16)
    g_hi = g.astype(BF16)
    g_rest = g - g_hi.astype(F32)
    g_mid = g_rest.astype(BF16)
    g_lo = (g_rest - g_mid.astype(F32)).astype(BF16)
    gc = sum(jnp.dot(tri, piece, preferred_element_type=F32) for piece in (g_hi, g_mid, g_lo))
    bg_ref[...] = jnp.where(lane < DN_V_HEADS, beta, gc)


def _unit_lower_inverses(mats):
    n = mats[0].shape[0]
    r = lax.broadcasted_iota(jnp.int32, (n, n), 0)
    c = lax.broadcasted_iota(jnp.int32, (n, n), 1)
    eye = (r == c).astype(F32)
    size = SUBLANES
    same = (r // size) == (c // size)
    d = [jnp.where(same, a, 0.0) for a in mats]
    d2 = [_mm(v, v) for v in d]
    d4 = [_mm(v, v) for v in d2]
    x = [_mm(eye - v, eye + v2) for v, v2 in zip(d, d2)]
    x = [_mm(v, eye + v4) for v, v4 in zip(x, d4)]
    while size < n:
        wider = (r // (2 * size)) == (c // (2 * size))
        ring = jnp.logical_and(wider, jnp.logical_not(same))
        xl = [_mm(v, jnp.where(ring, a, 0.0)) for v, a in zip(x, mats)]
        x = [v - _mm(vl, v) for v, vl in zip(x, xl)]
        same = wider
        size *= 2
    return x


def _dn_prep_kernel(q_ref, k_ref, v_ref, bg_ref, gct_ref, u_ref, w_ref, qk_ref, qd_ref):
    tm = q_ref.shape[0]
    ck = DN_CHUNK
    hd = DN_HEAD_DIM
    rep = DN_V_HEADS // DN_QK_HEADS
    r = lax.broadcasted_iota(jnp.int32, (ck, ck), 0)
    c = lax.broadcasted_iota(jnp.int32, (ck, ck), 1)
    lower = c <= r
    strict = c < r
    dot = functools.partial(jnp.dot, preferred_element_type=F32)
    chunks = [slice(ci * ck, (ci + 1) * ck) for ci in range(tm // ck)]
    gram = {}
    for rows in chunks:
        for hq in range(DN_QK_HEADS):
            cols = slice(hq * hd, (hq + 1) * hd)
            k = k_ref[rows, cols]
            gram[(hq, rows.start)] = lax.dot_general(jnp.concatenate([k, q_ref[rows, cols]], axis=0), k,
                                                     (((1,), (1,)), ((), ())), preferred_element_type=F32)
    problems = [(hv, rows) for rows in chunks for hv in range(DN_V_HEADS)]
    mats, rhs = [], []
    for hv, rows in problems:
        beta = bg_ref[rows, hv:hv + 1]
        gcc = bg_ref[rows, DN_V_HEADS + hv:DN_V_HEADS + hv + 1]
        gcr = gct_ref[hv:hv + 1, rows]
        decay = jnp.where(lower, jnp.exp(jnp.where(lower, gcc - gcr, 0.0)), 0.0)
        g = gram[(hv // rep, rows.start)]
        mats.append(jnp.where(strict, g[:ck] * beta * decay, 0.0))
        qk_ref[rows, hv * ck:(hv + 1) * ck] = (g[ck:] * decay).astype(BF16)
        qcols = slice((hv // rep) * hd, (hv // rep + 1) * hd)
        kf = k_ref[rows, qcols].astype(F32)
        vf = v_ref[rows, hv * hd:(hv + 1) * hd].astype(F32)
        egc = jnp.exp(gcc)
        rhs.append(jnp.concatenate([vf * beta, kf * (beta * egc)], axis=1).astype(BF16))
        qd_ref[rows, hv * hd:(hv + 1) * hd] = (q_ref[rows, qcols].astype(F32) * egc).astype(BF16)
    inverses = _unit_lower_inverses(mats)
    for (hv, rows), tinv, b in zip(problems, inverses, rhs):
        uw = dot(tinv.astype(BF16), b)
        u_ref[rows, hv * hd:(hv + 1) * hd] = uw[:, :hd].astype(BF16)
        w_ref[rows, hv * hd:(hv + 1) * hd] = uw[:, hd:].astype(BF16)


def _dn_scan_kernel(x_ref, mod_ref, u_ref, w_ref, qk_ref, qd_ref, k_ref, z_ref, bg_ref, og_ref, wout_ref,
                    fgain_ref, rw_ref, rb_ref, before_ref, o_ref, eid_ref, gate_ref, rank_ref, cnt_ref,
                    state_ref, att_ref, base_ref):
    t = pl.program_id(0)
    nb, tm, _ = x_ref.shape
    ck = DN_CHUNK
    hd = DN_HEAD_DIM
    rep = DN_V_HEADS // DN_QK_HEADS

    @pl.when(t == 0)
    def _():
        state_ref[...] = jnp.zeros_like(state_ref)

    dot = functools.partial(jnp.dot, preferred_element_type=F32)
    chains = [(bi, hv) for bi in range(nb) for hv in range(DN_V_HEADS)]
    for ci in range(tm // ck):
        rows = slice(ci * ck, (ci + 1) * ck)
        gcc, g_end, lhs = [], [], []
        for bi, hv in chains:
            gc = bg_ref[bi, rows, DN_V_HEADS + hv:DN_V_HEADS + hv + 1]
            gcc.append(gc)
            g_end.append(gc[ck - 1:ck, :])
            vcols = slice(hv * hd, (hv + 1) * hd)
            lhs.append(jnp.concatenate([w_ref[bi, rows, vcols], qd_ref[bi, rows, vcols]], axis=0))
        ws_qs = [dot(a, state_ref[n].astype(BF16)) for n, a in enumerate(lhs)]
        v_new = [(u_ref[bi, rows, hv * hd:(hv + 1) * hd].astype(F32) - m[:ck]).astype(BF16)
                 for (bi, hv), m in zip(chains, ws_qs)]
        outs = [m[ck:] + dot(qk_ref[bi, rows, hv * ck:(hv + 1) * ck], vn)
                for (bi, hv), m, vn in zip(chains, ws_qs, v_new)]
        for n, (bi, hv) in enumerate(chains):
            kf = k_ref[bi, rows, (hv // rep) * hd:(hv // rep + 1) * hd].astype(F32)
            k_dec = (kf * jnp.exp(g_end[n] - gcc[n])).astype(BF16)
            state_ref[n] = state_ref[n] * jnp.exp(g_end[n]) + lax.dot_general(
                k_dec, v_new[n], (((0,), (0,)), ((), ())), preferred_element_type=F32)
        for (bi, hv), o in zip(chains, outs):
            zf = z_ref[bi, rows, hv * hd:(hv + 1) * hd].astype(F32)
            normed = o * lax.rsqrt(_lane_sums(o * o) * (1.0 / hd) + NORM_EPS)
            att_ref[bi * tm + ci * ck:bi * tm + (ci + 1) * ck, hv * hd:(hv + 1) * hd] = (
                normed * og_ref[...] * _silu(zf)).astype(BF16)
    y = dot(att_ref[...], wout_ref[...])
    x_new, shift, scale = [], [], []
    for bi in range(nb):
        mod = mod_ref[bi]
        x_new.append(x_ref[bi] + mod[2:3] * y[bi * tm:(bi + 1) * tm])
        o_ref[bi] = x_new[bi]
        shift.append(jnp.broadcast_to(mod[3:4], x_new[bi].shape))
        scale.append(jnp.broadcast_to(mod[4:5], x_new[bi].shape))
    eid, gates, rank = _route(jnp.concatenate(x_new, axis=0), jnp.concatenate(shift, axis=0),
                              jnp.concatenate(scale, axis=0), t == 0, fgain_ref, rw_ref, rb_ref, before_ref,
                              cnt_ref, base_ref)
    for bi in range(nb):
        eid_ref[:, bi, :] = eid[:, bi * tm:(bi + 1) * tm]
        gate_ref[:, bi, :] = gates[:, bi * tm:(bi + 1) * tm]
        rank_ref[:, bi, :] = rank[:, bi * tm:(bi + 1) * tm]


def deltanet_layer(x2, mod, gain, w_in, conv_w, a_log, dt_bias, o_norm, w_out, router, b, s):
    t, d = x2.shape
    nh = DN_V_HEADS
    main_w = DN_CONV_WIDTH + DN_VAL_WIDTH
    w_main = w_in[:, :main_w].astype(BF16)
    w_ba = w_in[:, main_w:]
    w_ba_hi = w_ba.astype(BF16)
    w_ba_lo = (w_ba - w_ba_hi.astype(F32)).astype(BF16)
    w_ba2 = jnp.concatenate([w_ba_hi, w_ba_lo, jnp.zeros((d, LANES - 4 * nh), BF16)], axis=1)
    lanes_pad = lambda v: jnp.concatenate([jnp.zeros((nh,), F32), v.astype(F32),
                                           jnp.zeros((LANES - 2 * nh,), F32)]).reshape(1, LANES)
    rate = lanes_pad(jnp.exp(a_log.astype(F32)))
    dtb = lanes_pad(dt_bias)

    tm = TOKEN_TILE
    tpb = s // tm
    const2 = lambda bi, ti: (0, 0)
    tile = lambda bi, ti: (bi * tpb + ti, 0)
    q, k, v, z, bg = pl.pallas_call(
        _dn_proj_kernel,
        grid=(b, tpb),
        in_specs=[pl.BlockSpec((tm, d), tile),
                  pl.BlockSpec((1, 6, d), lambda bi, ti: (bi, 0, 0)),
                  pl.BlockSpec((1, d), const2),
                  pl.BlockSpec((d, main_w), const2),
                  pl.BlockSpec((d, LANES), const2),
                  pl.BlockSpec((DN_CONV, DN_CONV_WIDTH), const2),
                  pl.BlockSpec((1, LANES), const2),
                  pl.BlockSpec((1, LANES), const2)],
        out_specs=[pl.BlockSpec((tm, DN_KEY_WIDTH), tile), pl.BlockSpec((tm, DN_KEY_WIDTH), tile),
                   pl.BlockSpec((tm, DN_VAL_WIDTH), tile), pl.BlockSpec((tm, DN_VAL_WIDTH), tile),
                   pl.BlockSpec((tm, LANES), tile)],
        out_shape=[jax.ShapeDtypeStruct((t, DN_KEY_WIDTH), BF16), jax.ShapeDtypeStruct((t, DN_KEY_WIDTH), BF16),
                   jax.ShapeDtypeStruct((t, DN_VAL_WIDTH), BF16), jax.ShapeDtypeStruct((t, DN_VAL_WIDTH), BF16),
                   jax.ShapeDtypeStruct((t, LANES), F32)],
        scratch_shapes=[pltpu.VMEM((tm + SUBLANES, DN_CONV_WIDTH), F32)],
        compiler_params=_params("arbitrary", "arbitrary"),
        name="deltanet_proj",
    )(x2, mod, gain.reshape(1, d), w_main, w_ba2, conv_w, rate, dtb)

    gct = bg[:, nh:2 * nh].T
    tp = DN_PREP_TILE
    rows_of = lambda width: pl.BlockSpec((tp, width), lambda i: (i, 0))
    qk_w = nh * DN_CHUNK
    u, w, qk, qd = pl.pallas_call(
        _dn_prep_kernel,
        grid=(t // tp,),
        in_specs=[rows_of(DN_KEY_WIDTH), rows_of(DN_KEY_WIDTH), rows_of(DN_VAL_WIDTH), rows_of(LANES),
                  pl.BlockSpec((nh, tp), lambda i: (0, i))],
        out_specs=[rows_of(DN_VAL_WIDTH), rows_of(DN_VAL_WIDTH), rows_of(qk_w), rows_of(DN_VAL_WIDTH)],
        out_shape=[jax.ShapeDtypeStruct((t, DN_VAL_WIDTH), BF16), jax.ShapeDtypeStruct((t, DN_VAL_WIDTH), BF16),
                   jax.ShapeDtypeStruct((t, qk_w), BF16), jax.ShapeDtypeStruct((t, DN_VAL_WIDTH), BF16)],
        compiler_params=_params("arbitrary"),
        name="deltanet_prep",
    )(q, k, v, bg, gct)

    ts = DN_SCAN_TILE
    seq = lambda a: a.reshape(b, s, a.shape[-1])
    both = lambda width: pl.BlockSpec((b, ts, width), lambda i: (0, i, 0))
    r_args, r_in = _router_operands(*router, b * ts)
    r_out, r_shapes, r_scratch = _router_results(t, pl.BlockSpec((2, b, ts), lambda i: (0, 0, i)), (2, b, s))
    out = pl.pallas_call(
        _dn_scan_kernel,
        grid=(s // ts,),
        in_specs=[both(d),
                  pl.BlockSpec((b, 6, d), lambda i: (0, 0, 0)),
                  both(DN_VAL_WIDTH), both(DN_VAL_WIDTH), both(qk_w), both(DN_VAL_WIDTH), both(DN_KEY_WIDTH),
                  both(DN_VAL_WIDTH), both(LANES),
                  pl.BlockSpec((1, DN_HEAD_DIM), lambda i: (0, 0)),
                  pl.BlockSpec((DN_VAL_WIDTH, d), lambda i: (0, 0))] + r_in,
        out_specs=[both(d)] + r_out,
        out_shape=[jax.ShapeDtypeStruct((b, s, d), F32)] + r_shapes,
        scratch_shapes=[pltpu.VMEM((b * nh, DN_HEAD_DIM, DN_HEAD_DIM), F32),
                        pltpu.VMEM((b * ts, DN_VAL_WIDTH), BF16), r_scratch],
        compiler_params=_params("arbitrary"),
        name="deltanet_scan",
    )(seq(x2), mod, seq(u), seq(w), seq(qk), seq(qd), seq(k), seq(z), seq(bg), o_norm.reshape(1, DN_HEAD_DIM),
      w_out.astype(BF16), *r_args)
    eid, gate, rank, cnt = out[1:]
    return out[0].reshape(t, d), (eid.reshape(2, t), gate.reshape(2, t), rank.reshape(2, t), cnt)


def kernel(x, c, l0_norm_mix, l0_norm_ffn, l0_ada_w, l0_ada_b, l0_gm_w_in, l0_gm_v_norm, l0_gm_w_s, l0_gm_b_s, l0_gm_w_out, l0_router_w, l0_router_b, l0_expert_w_in, l0_expert_w_out, l1_norm_mix, l1_norm_ffn, l1_ada_w, l1_ada_b, l1_dn_w_in, l1_dn_conv_w, l1_dn_a_log, l1_dn_dt_bias, l1_dn_o_norm, l1_dn_w_out, l1_router_w, l1_router_b, l1_expert_w_in, l1_expert_w_out, l2_norm_mix, l2_norm_ffn, l2_ada_w, l2_ada_b, l2_swa_w_in, l2_swa_q_norm, l2_swa_k_norm, l2_swa_sinks, l2_swa_w_out, l2_router_w, l2_router_b, l2_expert_w_in, l2_expert_w_out, l3_norm_mix, l3_norm_ffn, l3_ada_w, l3_ada_b, l3_gm_w_in, l3_gm_v_norm, l3_gm_w_s, l3_gm_b_s, l3_gm_w_out, l3_router_w, l3_router_b, l3_expert_w_in, l3_expert_w_out):
    b, s, d = x.shape
    x2 = x.reshape(b * s, d)
    tiles_per_batch_of = lambda tm: s // tm

    def modulation(ada_w, ada_b):
        return adaln(c, ada_w, ada_b).reshape(b, 6, d)

    mod = modulation(l0_ada_w, l0_ada_b)
    x2, routing = gmlp_layer(x2, mod, l0_norm_mix, l0_gm_w_in, l0_gm_v_norm, l0_gm_w_s, l0_gm_b_s, l0_gm_w_out,
                             (l0_norm_ffn, l0_router_w, l0_router_b), tiles_per_batch_of(GMLP_TILE))
    x2 = moe_layer(x2, mod, l0_norm_ffn, routing, l0_expert_w_in, l0_expert_w_out, tiles_per_batch_of)

    mod = modulation(l1_ada_w, l1_ada_b)
    x2, routing = deltanet_layer(x2, mod, l1_norm_mix, l1_dn_w_in, l1_dn_conv_w, l1_dn_a_log, l1_dn_dt_bias,
                                 l1_dn_o_norm, l1_dn_w_out, (l1_norm_ffn, l1_router_w, l1_router_b), b, s)
    x2 = moe_layer(x2, mod, l1_norm_ffn, routing, l1_expert_w_in, l1_expert_w_out, tiles_per_batch_of)

    mod = modulation(l2_ada_w, l2_ada_b)
    x2, routing = swa_layer(x2, mod, l2_norm_mix, l2_swa_w_in, l2_swa_q_norm, l2_swa_k_norm, l2_swa_sinks,
                            l2_swa_w_out, (l2_norm_ffn, l2_router_w, l2_router_b), b, s)
    x2 = moe_layer(x2, mod, l2_norm_ffn, routing, l2_expert_w_in, l2_expert_w_out, tiles_per_batch_of)

    mod = modulation(l3_ada_w, l3_ada_b)
    x2, routing = gmlp_layer(x2, mod, l3_norm_mix, l3_gm_w_in, l3_gm_v_norm, l3_gm_w_s, l3_gm_b_s, l3_gm_w_out,
                             (l3_norm_ffn, l3_router_w, l3_router_b), tiles_per_batch_of(GMLP_TILE))
    x2 = moe_layer(x2, mod, l3_norm_ffn, routing, l3_expert_w_in, l3_expert_w_out, tiles_per_batch_of)
    return x2.reshape(b, s, d)
```

```python
import functools
import math

import jax
import jax.numpy as jnp
from jax import lax
from jax.experimental import pallas as pl
from jax.experimental.pallas import tpu as pltpu

F32 = jnp.float32
BF16 = jnp.bfloat16

D_MODEL = 1024
NORM_EPS = 1e-6

GM_CHUNK = 128
GM_GROUPS = 8
GM_GROUP_DIM = D_MODEL // GM_GROUPS

MOE_GROUPS = 4
MOE_EXPERTS_PER_GROUP = 8
MOE_EXPERTS = MOE_GROUPS * MOE_EXPERTS_PER_GROUP
MOE_FF = D_MODEL // 2

VMEM_LIMIT_BYTES = 56 * 1024 * 1024

TOKEN_TILE = 512
GMLP_TILE = 1024
MOVE_TILE = 512
EXPERT_ROWS = 512
EXPERT_SUB_ROWS = 128
LANES = 128
SUBLANES = 8


def _params(*semantics):
    return pltpu.CompilerParams(dimension_semantics=semantics, vmem_limit_bytes=VMEM_LIMIT_BYTES,
                                disable_bounds_checks=True)


def _rms(xf):
    return xf * lax.rsqrt(jnp.mean(xf * xf, axis=-1, keepdims=True) + NORM_EPS)


def _modulated_norm(x, gain, shift, scale):
    return _rms(x) * gain * (1.0 + scale) + shift


def _gelu_tanh(x):
    return 0.5 * x * (1.0 + jnp.tanh(math.sqrt(2.0 / math.pi) * (x + 0.044715 * (x * x * x))))


def _silu(x):
    return x * (1.0 / (1.0 + jnp.exp(-x)))


def _adaln_kernel(ct_ref, w_ref, b_ref, o_ref):
    act = _silu(ct_ref[...])
    w = w_ref[...]
    o_ref[...] = jnp.concatenate([jnp.sum(act[:, r:r + 1] * w, axis=0, keepdims=True)
                                  for r in range(o_ref.shape[0])], axis=0) + b_ref[...]


def adaln(c, ada_w, ada_b):
    rows, d = c.shape
    n = ada_w.shape[1]
    tn = d
    return pl.pallas_call(
        _adaln_kernel,
        grid=(n // tn,),
        in_specs=[pl.BlockSpec((d, rows), lambda j: (0, 0)),
                  pl.BlockSpec((d, tn), lambda j: (0, j)),
                  pl.BlockSpec((1, tn), lambda j: (0, j))],
        out_specs=pl.BlockSpec((rows, tn), lambda j: (0, j)),
        out_shape=jax.ShapeDtypeStruct((rows, n), F32),
        compiler_params=_params("arbitrary"),
        name="adaln",
    )(c.T, ada_w, ada_b.reshape(1, n))


def _gmlp_kernel(x_ref, mod_ref, gain_ref, win_ref, vn_ref, ws_ref, bs_ref, wout_ref, fgain_ref, rw_ref, rb_ref, before_ref,
                 o_ref, eid_ref, gate_ref, rank_ref, cnt_ref, u_ref, v_ref, g_ref, base_ref):
    width = D_MODEL
    x = x_ref[...]
    mod = mod_ref[0]
    h = _modulated_norm(x, gain_ref[...], mod[0:1], mod[1:2])
    z = _gelu_tanh(jnp.dot(h.astype(BF16), win_ref[...], preferred_element_type=F32))
    u_ref[...] = z[:, :width]
    v_ref[...] = (_rms(z[:, width:]) * vn_ref[...]).astype(BF16)
    cells = [(slice(c * GM_CHUNK, (c + 1) * GM_CHUNK), g, slice(g * GM_GROUP_DIM, (g + 1) * GM_GROUP_DIM))
             for c in range(x.shape[0] // GM_CHUNK) for g in range(GM_GROUPS)]
    mixed = [jnp.dot(ws_ref[g], v_ref[rows, cols], preferred_element_type=F32) for rows, g, cols in cells]
    for (rows, g, cols), sv in zip(cells, mixed):
        g_ref[rows, cols] = (u_ref[rows, cols] * (sv + bs_ref[:, g:g + 1])).astype(BF16)
    y = jnp.dot(g_ref[...], wout_ref[...], preferred_element_type=F32)
    x_new = x + mod[2:3] * y
    o_ref[...] = x_new
    eid_ref[...], gate_ref[...], rank_ref[...] = _route(
        x_new, mod[3:4], mod[4:5], pl.program_id(0) == 0, fgain_ref, rw_ref, rb_ref, before_ref, cnt_ref, base_ref)


def gmlp_layer(x2, mod, gain, w_in, v_norm, w_s, b_s, w_out, router, tiles_per_batch):
    t, d = x2.shape
    tm = GMLP_TILE
    ws_causal =jnp.where(jnp.tril(jnp.ones((GM_CHUNK, GM_CHUNK), dtype=bool)), w_s, 0).astype(BF16)
    const2 = lambda i: (0, 0)
    r_args, r_in = _router_operands(*router, tm)
    r_out, r_shapes, r_scratch = _router_results(t, pl.BlockSpec((2, tm), lambda i: (0, i)), (2, t))
    out = pl.pallas_call(
        _gmlp_kernel,
        grid=(t // tm,),
        in_specs=[pl.BlockSpec((tm, d), lambda i: (i, 0)),
                  pl.BlockSpec((1, 6, d), lambda i: (i // tiles_per_batch, 0, 0)),
                  pl.BlockSpec((1, d), const2),
                  pl.BlockSpec((d, 2 * d), const2),
                  pl.BlockSpec((1, d), const2),
                  pl.BlockSpec((GM_GROUPS, GM_CHUNK, GM_CHUNK), lambda i: (0, 0, 0)),
                  pl.BlockSpec((GM_CHUNK, GM_GROUPS), const2),
                  pl.BlockSpec((d, d), const2)] + r_in,
        out_specs=[pl.BlockSpec((tm, d), lambda i: (i, 0))] + r_out,
        out_shape=[jax.ShapeDtypeStruct((t, d), F32)] + r_shapes,
        scratch_shapes=[pltpu.VMEM((tm, d), F32), pltpu.VMEM((tm, d), BF16), pltpu.VMEM((tm, d), BF16), r_scratch],
        compiler_params=_params("arbitrary"),
        name="gmlp_mixer",
    )(x2, mod, gain.reshape(1, d), w_in.astype(BF16), v_norm.reshape(1, d), ws_causal, b_s.T,
      w_out.astype(BF16), *r_args)
    return out[0], tuple(out[1:])


def _route(x, shift, scale, first, gain_ref, rw_ref, rb_ref, before_ref, cnt_ref, base_ref):
    tm = x.shape[0]
    ne = MOE_EXPERTS
    npg = MOE_EXPERTS_PER_GROUP

    @pl.when(first)
    def _():
        base_ref[...] = jnp.zeros_like(base_ref)

    hh, hl = _split_bf16(_modulated_norm(x, gain_ref[...], shift, scale))
    nt = (((1,), (1,)), ((), ()))
    nr = ROUTER_ROWS
    by_hi = lax.dot_general(rw_ref[...], hh, nt, preferred_element_type=F32)
    by_lo = lax.dot_general(rw_ref[0:nr, :], hl, nt, preferred_element_type=F32)
    lt = (by_hi[0:nr] + rb_ref[...]) + (by_hi[nr:] + by_lo)
    lg = [lt[ne + g:ne + g + 1, :] for g in range(MOE_GROUPS)]
    gmax = functools.reduce(jnp.maximum, lg)
    gsum = functools.reduce(lambda a, b: a + b, [jnp.exp(l - gmax) for l in lg])
    pg_top = 1.0 / gsum
    g_sel = jnp.full(gmax.shape, MOE_GROUPS - 1, jnp.int32)
    for g in range(MOE_GROUPS - 2, -1, -1):
        g_sel = jnp.where(lg[g] == gmax, g, g_sel)
    sel = lt[(MOE_GROUPS - 1) * npg:MOE_GROUPS * npg, :]
    for g in range(MOE_GROUPS - 2, -1, -1):
        sel = jnp.where(g_sel == g, lt[g * npg:(g + 1) * npg, :], sel)
    row = lax.broadcasted_iota(jnp.int32, sel.shape, 0)
    m1 = jnp.max(sel, axis=0, keepdims=True)
    i1 = jnp.min(jnp.where(sel == m1, row, npg), axis=0, keepdims=True)
    rest = jnp.where(row == i1, -jnp.inf, sel)
    m2 = jnp.max(rest, axis=0, keepdims=True)
    i2 = jnp.min(jnp.where(rest == m2, row, npg), axis=0, keepdims=True)
    e2 = jnp.exp(m2 - m1)
    inv = pg_top / (1.0 + e2)
    eid = jnp.concatenate([g_sel * npg + i1, g_sel * npg + i2], axis=0)
    gates = jnp.concatenate([inv, inv * e2], axis=0)

    erow = lax.broadcasted_iota(jnp.int32, (ne, tm), 0)
    hits = [erow == eid[k:k + 1, :] for k in range(2)]
    prefix = jnp.dot(jnp.concatenate([jnp.where(hit, 1.0, 0.0).astype(BF16) for hit in hits], axis=0),
                     before_ref[...], preferred_element_type=F32)
    base = base_ref[...]
    ranks = []
    for k, hit in enumerate(hits):
        ranks.append(jnp.sum(jnp.where(hit, prefix[k * ne:(k + 1) * ne] + base, 0.0), axis=0, keepdims=True))
        base = base + jnp.sum(jnp.where(hit, 1.0, 0.0), axis=1, keepdims=True)
    base_ref[...] = base
    cnt_ref[...] = jnp.broadcast_to(base, cnt_ref.shape).astype(jnp.int32)
    return eid, gates, jnp.concatenate(ranks, axis=0).astype(jnp.int32)


ROUTER_ROWS = MOE_EXPERTS + 2 * SUBLANES


def _router_operands(gain, router_w, router_b, tm):
    d = router_w.shape[0]
    pad = ROUTER_ROWS - MOE_EXPERTS - MOE_GROUPS
    rw = jnp.concatenate([router_w[:, MOE_GROUPS:], router_w[:, :MOE_GROUPS], jnp.zeros((d, pad), F32)], axis=1).T
    rw_hi = rw.astype(BF16)
    rw2 = jnp.concatenate([rw_hi, (rw - rw_hi.astype(F32)).astype(BF16)], axis=0)
    rb = jnp.concatenate([router_b[MOE_GROUPS:], router_b[:MOE_GROUPS], jnp.zeros((pad,), F32)])
    pos = jnp.arange(tm)
    before = (pos[:, None] < pos[None, :]).astype(BF16)
    const = lambda *_: (0, 0)
    specs = [pl.BlockSpec((1, d), const), pl.BlockSpec((2 * ROUTER_ROWS, d), const),
             pl.BlockSpec((ROUTER_ROWS, 1), const), pl.BlockSpec((tm, tm), const)]
    return [gain.reshape(1, d), rw2, rb.reshape(ROUTER_ROWS, 1), before], specs


def _router_results(t, slot_spec, slot_shape):
    specs = [slot_spec, slot_spec, slot_spec, pl.BlockSpec((MOE_EXPERTS, LANES), lambda *_: (0, 0))]
    shapes = [jax.ShapeDtypeStruct(slot_shape, jnp.int32), jax.ShapeDtypeStruct(slot_shape, F32),
              jax.ShapeDtypeStruct(slot_shape, jnp.int32), jax.ShapeDtypeStruct((MOE_EXPERTS, LANES), jnp.int32)]
    return specs, shapes, pltpu.VMEM((MOE_EXPERTS, 1), F32)


def _wait_rows(hbm_ref, n_rows, sem, times):
    rows = hbm_ref.at[pl.ds(0, n_rows)]
    for _ in range(times):
        pltpu.make_async_copy(rows, rows, sem).wait()


def _tiled_shape(n, d):
    return (n // SUBLANES, d // LANES, SUBLANES, LANES)


def _store_tiled(ref, a):
    n, d = a.shape
    for j in range(d // LANES):
        ref[:, j, :, :] = a[:, j * LANES:(j + 1) * LANES].reshape(n // SUBLANES, SUBLANES, LANES)


def _load_tiled(ref):
    g, nj, _, _ = ref.shape
    return jnp.concatenate([ref[:, j, :, :].reshape(g * SUBLANES, LANES) for j in range(nj)], axis=1)


def _dispatch_kernel(d0_ref, d1_ref, x_ref, mod_ref, gain_ref, xs_ref, h_ref, sem):
    i = pl.program_id(0)
    last = pl.num_programs(0) - 1
    tm, d = x_ref.shape
    mod = mod_ref[0]
    h = _modulated_norm(x_ref[...], gain_ref[...], mod[3:4], mod[4:5])

    def step(buf):
        _store_tiled(h_ref.at[buf], h)

        def issue(g, carry):
            for u in range(SUBLANES):
                for k, dest_ref in enumerate((d0_ref, d1_ref)):
                    pltpu.make_async_copy(h_ref.at[buf, g, :, u, :], xs_ref.at[dest_ref[g * SUBLANES + u]],
                                          sem.at[buf]).start(priority=k)
            return carry

        lax.fori_loop(0, tm // SUBLANES, issue, 0)

        @pl.when(i > 0)
        def _():
            _wait_rows(xs_ref, tm, sem.at[1 - buf], 2)

        @pl.when(i == last)
        def _():
            _wait_rows(xs_ref, tm, sem.at[buf], 2)

    for buf in range(2):
        pl.when(i % 2 == buf)(functools.partial(step, buf))


def moe_dispatch(x2, mod, gain, dest, tiles_per_batch_of):
    t, d = x2.shape
    tm = MOVE_TILE
    tpb = tiles_per_batch_of(tm)
    const2 = lambda i: (0, 0)
    slot_spec = pl.BlockSpec((tm,), lambda i: (i,), memory_space=pltpu.SMEM)
    return pl.pallas_call(
        _dispatch_kernel,
        grid=(t // tm,),
        in_specs=[slot_spec, slot_spec,
                  pl.BlockSpec((tm, d), lambda i: (i, 0)),
                  pl.BlockSpec((1, 6, d), lambda i: (i // tpb, 0, 0)),
                  pl.BlockSpec((1, d), const2)],
        out_specs=pl.BlockSpec(memory_space=pl.ANY),
        out_shape=jax.ShapeDtypeStruct((2 * t, d // LANES, LANES), F32),
        scratch_shapes=[pltpu.VMEM((2,) + _tiled_shape(tm, d), F32), pltpu.SemaphoreType.DMA((2,))],
        compiler_params=_params("arbitrary"),
        name="moe_dispatch",
    )(dest[0], dest[1], x2, mod, gain.reshape(1, d))


def _expert_kernel(blk_ref, exp_ref, start_ref, nvalid_ref, wslot_ref, wnext_ref, xs_ref, win_ref, wout_ref, ys_ref,
                   winb_ref, woutb_ref, xbuf_ref, ybuf_ref, winf_ref, woutf_ref, in_sem, out_sem, w_sem):
    i = pl.program_id(0)
    n_items = pl.num_programs(0)
    bm = EXPERT_ROWS
    groups = bm // SUBLANES
    n_blocks = xs_ref.shape[0] // groups
    e = exp_ref[i]
    blk = blk_ref[i]
    prev = jnp.maximum(i - 1, 0)
    nxt = jnp.minimum(i + 1, n_items - 1)
    first = i == 0
    final = i == n_items - 1
    lo = start_ref[e]
    hi = start_ref[e + 1]
    row0 = blk * bm
    live = i < nvalid_ref[0]
    slot = blk % 2

    def block_copies(hbm_ref, buf_ref, b, s, sem, to_hbm):
        copies = []
        for u in range(SUBLANES):
            hbm = hbm_ref.at[pl.ds(b * groups, groups), u]
            vmem = buf_ref.at[s, :, :, u, :]
            copies.append(pltpu.make_async_copy(vmem, hbm, sem.at[s]) if to_hbm
                          else pltpu.make_async_copy(hbm, vmem, sem.at[s]))
        return copies

    def fetch(b, s):
        for c in block_copies(xs_ref, xbuf_ref, b, s, in_sem, False):
            c.start()

    def weight_copies(ex, s):
        return (pltpu.make_async_copy(win_ref.at[ex], winf_ref.at[s], w_sem.at[s]),
                pltpu.make_async_copy(wout_ref.at[ex], woutf_ref.at[s], w_sem.at[s]))

    wslot = wslot_ref[e]

    @pl.when(first)
    def _():
        for c in weight_copies(e, wslot):
            c.start()

    @pl.when(jnp.logical_or(first, exp_ref[prev] != e))
    def _():
        for c in weight_copies(e, wslot):
            c.wait()
        winb_ref[...] = winf_ref[wslot].astype(BF16)
        woutb_ref[...] = woutf_ref[wslot].astype(BF16)

        @pl.when(wnext_ref[e] != e)
        def _():
            for c in weight_copies(wnext_ref[e], 1 - wslot):
                c.start()

    new_block = jnp.logical_or(first, blk_ref[prev] != blk)
    last_of_block = jnp.logical_or(final, blk_ref[nxt] != blk)
    whole = jnp.logical_and(lo <= row0, hi >= row0 + bm)

    @pl.when(first)
    def _():
        fetch(0, 0)

    @pl.when(new_block)
    def _():
        for c in block_copies(xs_ref, xbuf_ref, blk, slot, in_sem, False):
            c.wait()

        @pl.when(blk + 1 < n_blocks)
        def _():
            fetch(blk + 1, 1 - slot)

        @pl.when(blk >= 2)
        def _():
            for c in block_copies(ys_ref, ybuf_ref, blk - 2, slot, out_sem, True):
                c.wait()

    def ffn(x):
        a_gl = jnp.dot(x.astype(BF16), winb_ref[...], preferred_element_type=F32)
        mid = (_silu(a_gl[:, :MOE_FF]) * a_gl[:, MOE_FF:]).astype(BF16)
        return jnp.dot(mid, woutb_ref[...], preferred_element_type=F32)

    @pl.when(jnp.logical_and(live, whole))
    def _():
        _store_tiled(ybuf_ref.at[slot], ffn(_load_tiled(xbuf_ref.at[slot])))

    @pl.when(jnp.logical_and(new_block, jnp.logical_not(whole)))
    def _():
        ybuf_ref[slot] = jnp.zeros(ybuf_ref.shape[1:], F32)

    sub_groups = EXPERT_SUB_ROWS // SUBLANES
    for sub in range(bm // EXPERT_SUB_ROWS):
        sub0 = row0 + sub * EXPERT_SUB_ROWS
        touched = jnp.logical_and(hi > sub0, lo < sub0 + EXPERT_SUB_ROWS)

        @pl.when(jnp.logical_and(jnp.logical_and(live, jnp.logical_not(whole)), touched))
        def _():
            gs = pl.ds(sub * sub_groups, sub_groups)
            rows = sub0 + lax.broadcasted_iota(jnp.int32, (EXPERT_SUB_ROWS, 1), 0)
            mine = jnp.logical_and(rows >= lo, rows < hi)
            y = jnp.where(mine, ffn(_load_tiled(xbuf_ref.at[slot, gs])), 0.0)
            _store_tiled(ybuf_ref.at[slot, gs], _load_tiled(ybuf_ref.at[slot, gs]) + y)

    @pl.when(last_of_block)
    def _():
        for c in block_copies(ys_ref, ybuf_ref, blk, slot, out_sem, True):
            c.start()

    @pl.when(final)
    def _():
        for c in block_copies(ys_ref, ybuf_ref, blk, slot, out_sem, True):
            c.wait()

        @pl.when(blk >= 1)
        def _():
            for c in block_copies(ys_ref, ybuf_ref, blk - 1, 1 - slot, out_sem, True):
                c.wait()


def moe_experts(xs, w_in, w_out, item_block, item_expert, seg_start, n_valid, w_slot, w_next):
    n, nj, _ = xs.shape
    d = nj * LANES
    bm = EXPERT_ROWS
    n_items = item_block.shape[0]
    by_group = (n // SUBLANES, SUBLANES, nj, LANES)
    in_hbm = pl.BlockSpec(memory_space=pl.ANY)
    grid_spec = pltpu.PrefetchScalarGridSpec(
        num_scalar_prefetch=6,
        grid=(n_items,),
        in_specs=[in_hbm, in_hbm, in_hbm],
        out_specs=pl.BlockSpec(memory_space=pl.ANY),
        scratch_shapes=[pltpu.VMEM((d, 2 * MOE_FF), BF16), pltpu.VMEM((MOE_FF, d), BF16),
                        pltpu.VMEM((2,) + _tiled_shape(bm, d), F32), pltpu.VMEM((2,) + _tiled_shape(bm, d), F32),
                        pltpu.VMEM((2, d, 2 * MOE_FF), F32), pltpu.VMEM((2, MOE_FF, d), F32),
                        pltpu.SemaphoreType.DMA((2,)), pltpu.SemaphoreType.DMA((2,)), pltpu.SemaphoreType.DMA((2,))],
    )
    ys = pl.pallas_call(
        _expert_kernel,
        grid_spec=grid_spec,
        out_shape=jax.ShapeDtypeStruct(by_group, F32),
        compiler_params=_params("arbitrary"),
        name="moe_experts",
    )(item_block, item_expert, seg_start, n_valid, w_slot, w_next, xs.reshape(by_group), w_in, w_out)
    return ys.reshape(n, nj, LANES)


def _combine_kernel(d0_ref, d1_ref, n0_ref, n1_ref, x_ref, mod_ref, gate_ref, ys_ref, o_ref, y_ref, sem):
    i = pl.program_id(0)
    last = pl.num_programs(0) - 1
    tm, d = x_ref.shape

    def gather(dest_refs, buf):
        def issue(g, carry):
            for u in range(SUBLANES):
                for k, dest_ref in enumerate(dest_refs):
                    pltpu.make_async_copy(ys_ref.at[dest_ref[g * SUBLANES + u]],
                                          y_ref.at[buf, k, g, :, u, :], sem.at[buf]).start(priority=k)
            return carry

        lax.fori_loop(0, tm // SUBLANES, issue, 0)

    def step(buf):
        @pl.when(i == 0)
        def _():
            gather((d0_ref, d1_ref), buf)

        @pl.when(i < last)
        def _():
            gather((n0_ref, n1_ref), 1 - buf)

        _wait_rows(ys_ref, tm, sem.at[buf], 2)
        gates = gate_ref[...]
        moe = gates[:, 0:1] * _load_tiled(y_ref.at[buf, 0]) + gates[:, 1:2] * _load_tiled(y_ref.at[buf, 1])
        o_ref[...] = x_ref[...] + mod_ref[0][5:6] * moe

    for buf in range(2):
        pl.when(i % 2 == buf)(functools.partial(step, buf))


def moe_combine(x2, mod, gates_t, dest, ys, tiles_per_batch_of):
    t, d = x2.shape
    tm = MOVE_TILE
    tpb = tiles_per_batch_of(tm)
    n_tiles = t // tm
    slot_spec = pl.BlockSpec((tm,), lambda i: (i,), memory_space=pltpu.SMEM)
    next_spec = pl.BlockSpec((tm,), lambda i: (jnp.minimum(i + 1, n_tiles - 1),), memory_space=pltpu.SMEM)
    return pl.pallas_call(
        _combine_kernel,
        grid=(n_tiles,),
        in_specs=[slot_spec, slot_spec, next_spec, next_spec,
                  pl.BlockSpec((tm, d), lambda i: (i, 0)),
                  pl.BlockSpec((1, 6, d), lambda i: (i // tpb, 0, 0)),
                  pl.BlockSpec((tm, 2), lambda i: (i, 0)),
                  pl.BlockSpec(memory_space=pl.ANY)],
        out_specs=pl.BlockSpec((tm, d), lambda i: (i, 0)),
        out_shape=jax.ShapeDtypeStruct((t, d), F32),
        scratch_shapes=[pltpu.VMEM((2, 2) + _tiled_shape(tm, d), F32), pltpu.SemaphoreType.DMA((2,))],
        compiler_params=_params("arbitrary"),
        name="moe_combine",
    )(dest[0], dest[1], dest[0], dest[1], x2, mod, gates_t, ys)


def moe_layer(x2, mod, gain, routing, w_in, w_out, tiles_per_batch_of):
    t, d = x2.shape
    bm = EXPERT_ROWS
    n_assign = 2 * t
    eid, gate, rank, cnt = routing
    ne = MOE_EXPERTS
    experts = jnp.arange(ne, dtype=jnp.int32)
    upto = experts[None, :] <= experts[:, None]

    def running_total(v):
        return jnp.sum(jnp.where(upto, v[None, :], 0), axis=1).astype(jnp.int32)

    def lookup(table, idx):
        hit = idx[None] == experts.reshape((ne,) + (1,) * idx.ndim)
        return jnp.sum(jnp.where(hit, table.reshape((ne,) + (1,) * idx.ndim), 0), axis=0).astype(jnp.int32)

    counts = cnt[:, 0]
    seg_end = running_total(counts)
    seg_start = jnp.concatenate([jnp.zeros((1,), jnp.int32), seg_end])
    dest = lookup(seg_start[:ne], eid) + rank
    n_blocks = n_assign // bm
    n_items = n_blocks + ne - 1
    first_blk = seg_start[:ne] // bm
    last_blk = jnp.where(counts > 0, (seg_end - 1) // bm, first_blk - 1)
    per_expert = jnp.maximum(last_blk - first_blk + 1, 0)
    item_end = running_total(per_expert)
    n_valid = item_end[ne - 1]
    item_ids = jnp.arange(n_items, dtype=jnp.int32)
    item_ids_c = jnp.maximum(jnp.minimum(item_ids, n_valid - 1), 0)
    item_expert = jnp.minimum(jnp.sum(item_ids_c[:, None] >= item_end[None, :], axis=1), ne - 1).astype(jnp.int32)
    item_block = lookup(first_blk - (item_end - per_expert), item_expert) + item_ids_c
    has_rows = counts > 0
    w_slot = (running_total(has_rows.astype(jnp.int32)) + 1) % 2
    later = jnp.logical_and(experts[None, :] > experts[:, None], has_rows[None, :])
    w_next = jnp.min(jnp.where(later, experts[None, :], ne), axis=1)
    w_next = jnp.where(w_next == ne, experts, w_next).astype(jnp.int32)
    xs = moe_dispatch(x2, mod, gain, dest, tiles_per_batch_of)
    ys = moe_experts(xs, w_in, w_out, item_block, item_expert, seg_start,
                     n_valid.reshape(1).astype(jnp.int32), w_slot.astype(jnp.int32), w_next)
    return moe_combine(x2, mod, gate.T, dest, ys, tiles_per_batch_of)


SWA_HEAD_DIM = 64
SWA_Q_HEADS = D_MODEL // SWA_HEAD_DIM
SWA_KV_HEADS = 4
SWA_BLOCK = 128
SWA_Q_WIDTH = SWA_Q_HEADS * SWA_HEAD_DIM
SWA_KV_WIDTH = SWA_KV_HEADS * SWA_HEAD_DIM
HEADS_PER_VREG = LANES // SWA_HEAD_DIM


def _swa_kernel(sink_ref, x_ref, mod_ref, gain_ref, win_ref, qg_ref, kg_ref, ones_ref, wout_ref,
                fgain_ref, rw_ref, rb_ref, before_ref, o_ref, eid_ref, gate_ref, rank_ref, cnt_ref,
                kx_ref, vx_ref, att_ref, base_ref):
    t = pl.program_id(1)
    tm = x_ref.shape[0]
    blk = SWA_BLOCK
    hd = SWA_HEAD_DIM

    @pl.when(t == 0)
    def _():
        kx_ref[0:blk, :] = jnp.zeros((blk, kx_ref.shape[1]), BF16)
        vx_ref[0:blk, :] = jnp.zeros((blk, vx_ref.shape[1]), BF16)

    x = x_ref[...]
    mod = mod_ref[0]
    h = _modulated_norm(x, gain_ref[...], mod[0:1], mod[1:2]).astype(BF16)
    proj = jnp.dot(h, win_ref[...], preferred_element_type=F32)

    def head_rms(a):
        ss = jnp.dot((a * a).astype(BF16), ones_ref[...], preferred_element_type=F32)
        return a * lax.rsqrt(ss * (1.0 / hd) + NORM_EPS)

    lane = lax.broadcasted_iota(jnp.int32, (tm, LANES), 1)
    low = lane < hd
    for c in range(SWA_Q_WIDTH // LANES):
        cols = slice(c * LANES, (c + 1) * LANES)
        att_ref[:, cols] = (head_rms(proj[:, cols]) * qg_ref[:, cols]).astype(BF16)
    for c in range(SWA_KV_WIDTH // LANES):
        cols = slice(c * LANES, (c + 1) * LANES)
        kc = head_rms(proj[:, SWA_Q_WIDTH + c * LANES:SWA_Q_WIDTH + (c + 1) * LANES]) * kg_ref[:, cols]
        vc = proj[:, SWA_Q_WIDTH + SWA_KV_WIDTH + c * LANES:SWA_Q_WIDTH + SWA_KV_WIDTH + (c + 1) * LANES]
        for ref, a in ((kx_ref, kc), (vx_ref, vc)):
            even_lo = jnp.where(low, a, 0.0)
            odd_hi = jnp.where(low, 0.0, a)
            j0 = HEADS_PER_VREG * c
            ref[blk:blk + tm, (2 * j0) * LANES:(2 * j0 + 1) * LANES] = even_lo.astype(BF16)
            ref[blk:blk + tm, (2 * j0 + 1) * LANES:(2 * j0 + 2) * LANES] = pltpu.roll(even_lo, hd, 1).astype(BF16)
            ref[blk:blk + tm, (2 * j0 + 2) * LANES:(2 * j0 + 3) * LANES] = pltpu.roll(odd_hi, hd, 1).astype(BF16)
            ref[blk:blk + tm, (2 * j0 + 3) * LANES:(2 * j0 + 4) * LANES] = odd_hi.astype(BF16)

    qi = lax.broadcasted_iota(jnp.int32, (blk, 2 * blk), 0)
    kj = lax.broadcasted_iota(jnp.int32, (blk, 2 * blk), 1)
    lane_q = lax.broadcasted_iota(jnp.int32, (blk, LANES), 1) < hd
    pairs_per_kv = SWA_Q_HEADS // SWA_KV_HEADS // HEADS_PER_VREG
    for i in range(tm // blk):
        rows = slice(i * blk, (i + 1) * blk)
        keys = slice(i * blk, (i + 2) * blk)
        floor = jnp.where(t == 0, blk - 1, qi) if i == 0 else qi
        mask = jnp.logical_and(kj > floor, kj <= qi + blk)
        heads = [(p, half) for p in range(SWA_Q_HEADS // HEADS_PER_VREG) for half in range(HEADS_PER_VREG)]
        col_of = lambda p, half: (2 * (p // pairs_per_kv) + half) * LANES
        scores = [lax.dot_general(att_ref[rows, p * LANES:(p + 1) * LANES],
                                  kx_ref[keys, col_of(p, half):col_of(p, half) + LANES],
                                  (((1,), (1,)), ((), ())), preferred_element_type=F32) for p, half in heads]
        probs, inv = [], []
        for (p, half), sc in zip(heads, scores):
            sc = jnp.where(mask, sc, -jnp.inf)
            sink = sink_ref[HEADS_PER_VREG * p + half]
            m = jnp.maximum(jnp.max(sc, axis=-1, keepdims=True), sink)
            pr = jnp.exp(sc - m)
            inv.append(1.0 / (jnp.sum(pr, axis=-1, keepdims=True) + jnp.exp(sink - m)))
            probs.append(pr.astype(BF16))
        pvs = [jnp.dot(pr, vx_ref[keys, col_of(p, half):col_of(p, half) + LANES], preferred_element_type=F32)
               for (p, half), pr in zip(heads, probs)]
        for p in range(SWA_Q_HEADS // HEADS_PER_VREG):
            out = (pvs[2 * p] + pvs[2 * p + 1]) * jnp.where(lane_q, inv[2 * p], inv[2 * p + 1])
            att_ref[rows, p * LANES:(p + 1) * LANES] = out.astype(BF16)
    kx_ref[0:blk, :] = kx_ref[tm:tm + blk, :]
    vx_ref[0:blk, :] = vx_ref[tm:tm + blk, :]
    y = jnp.dot(att_ref[...], wout_ref[...], preferred_element_type=F32)
    x_new = x + mod[2:3] * y
    o_ref[...] = x_new
    first = jnp.logical_and(pl.program_id(0) == 0, t == 0)
    eid_ref[...], gate_ref[...], rank_ref[...] = _route(
        x_new, mod[3:4], mod[4:5], first, fgain_ref, rw_ref, rb_ref, before_ref, cnt_ref, base_ref)


def swa_layer(x2, mod, gain, w_in, q_norm, k_norm, sinks, w_out, router, b, s):
    t, d = x2.shape
    tm = TOKEN_TILE
    tpb = s // tm
    hd = SWA_HEAD_DIM
    qg = (jnp.tile(q_norm, SWA_Q_HEADS) * hd ** -0.5).reshape(1, SWA_Q_WIDTH)
    kg = jnp.tile(k_norm, SWA_KV_HEADS).reshape(1, SWA_KV_WIDTH)
    ids = jnp.arange(LANES) // hd
    ones = (ids[:, None] == ids[None, :]).astype(BF16)
    proj_w = SWA_Q_WIDTH + 2 * SWA_KV_WIDTH
    ext_w = 2 * SWA_KV_HEADS * LANES
    const2 = lambda bi, ti: (0, 0)
    tile = lambda bi, ti: (bi * tpb + ti, 0)
    r_args, r_in = _router_operands(*router, tm)
    r_out, r_shapes, r_scratch = _router_results(t, pl.BlockSpec((2, tm), lambda bi, ti: (0, bi * tpb + ti)), (2, t))
    out = pl.pallas_call(
        _swa_kernel,
        grid=(b, tpb),
        in_specs=[pl.BlockSpec(memory_space=pltpu.SMEM),
                  pl.BlockSpec((tm, d), tile),
                  pl.BlockSpec((1, 6, d), lambda bi, ti: (bi, 0, 0)),
                  pl.BlockSpec((1, d), const2),
                  pl.BlockSpec((d, proj_w), const2),
                  pl.BlockSpec((1, SWA_Q_WIDTH), const2),
                  pl.BlockSpec((1, SWA_KV_WIDTH), const2),
                  pl.BlockSpec((LANES, LANES), const2),
                  pl.BlockSpec((SWA_Q_WIDTH, d), const2)] + r_in,
        out_specs=[pl.BlockSpec((tm, d), tile)] + r_out,
        out_shape=[jax.ShapeDtypeStruct((t, d), F32)] + r_shapes,
        scratch_shapes=[pltpu.VMEM((tm + SWA_BLOCK, ext_w), BF16), pltpu.VMEM((tm + SWA_BLOCK, ext_w), BF16),
                        pltpu.VMEM((tm, SWA_Q_WIDTH), BF16), r_scratch],
        compiler_params=_params("arbitrary", "arbitrary"),
        name="swa_mixer",
    )(sinks, x2, mod, gain.reshape(1, d), w_in.astype(BF16), qg, kg, ones, w_out.astype(BF16), *r_args)
    return out[0], tuple(out[1:])


DN_QK_HEADS = 4
DN_V_HEADS = 8
DN_HEAD_DIM = D_MODEL // DN_V_HEADS
DN_CONV = 4
DN_CHUNK = 64
DN_KEY_WIDTH = DN_QK_HEADS * DN_HEAD_DIM
DN_VAL_WIDTH = DN_V_HEADS * DN_HEAD_DIM
DN_CONV_WIDTH = 2 * DN_KEY_WIDTH + DN_VAL_WIDTH
DN_PREP_TILE = 128
DN_SCAN_TILE = 256


def _split_bf16(a):
    hi = a.astype(BF16)
    return hi, (a - hi.astype(F32)).astype(BF16)


def _mm(a, b):
    return jnp.dot(a.astype(BF16), b.astype(BF16), preferred_element_type=F32)


def _lane_sums(a):
    return jnp.dot(a.astype(BF16), jnp.ones((LANES, LANES), BF16), preferred_element_type=F32)


def _dn_proj_kernel(x_ref, mod_ref, gain_ref, w_ref, wba_ref, conv_ref, rate_ref, dtb_ref,
                    q_ref, k_ref, v_ref, z_ref, bg_ref, ext_ref):
    t = pl.program_id(1)
    tm = x_ref.shape[0]
    pad = SUBLANES

    @pl.when(t == 0)
    def _():
        ext_ref[0:pad, :] = jnp.zeros((pad, ext_ref.shape[1]), F32)

    mod = mod_ref[0]
    h = _modulated_norm(x_ref[...], gain_ref[...], mod[0:1], mod[1:2])
    hh, hl = _split_bf16(h)
    proj = jnp.dot(hh, w_ref[...], preferred_element_type=F32)
    z_ref[...] = proj[:, DN_CONV_WIDTH:].astype(BF16)

    cur = proj[:, :DN_CONV_WIDTH]
    prev = ext_ref[...]
    row = lax.broadcasted_iota(jnp.int32, (pad, 1), 0)
    acc = conv_ref[DN_CONV - 1:DN_CONV, :] * cur
    for k in range(1, DN_CONV):
        back = pltpu.roll(cur, k, 0)
        head = jnp.where(row < k, pltpu.roll(prev, k, 0), back[0:pad, :])
        back = jnp.concatenate([head, back[pad:, :]], axis=0)
        acc = acc + conv_ref[DN_CONV - 1 - k:DN_CONV - k, :] * back
    ext_ref[...] = cur[tm - pad:tm, :]
    qkv = _silu(acc)
    for hq in range(2 * DN_QK_HEADS):
        cols = slice(hq * DN_HEAD_DIM, (hq + 1) * DN_HEAD_DIM)
        a = qkv[:, cols]
        a = a * lax.rsqrt(jnp.sum(a * a, axis=-1, keepdims=True) + NORM_EPS)
        if hq < DN_QK_HEADS:
            q_ref[:, cols] = (a * DN_HEAD_DIM ** -0.5).astype(BF16)
        else:
            k_ref[:, hq * DN_HEAD_DIM - DN_KEY_WIDTH:(hq + 1) * DN_HEAD_DIM - DN_KEY_WIDTH] = a.astype(BF16)
    v_ref[...] = qkv[:, 2 * DN_KEY_WIDTH:].astype(BF16)

    ba = jnp.dot(hh, wba_ref[...], preferred_element_type=F32) + jnp.dot(hl, wba_ref[...], preferred_element_type=F32)
    ba = ba + pltpu.roll(ba, LANES - 2 * DN_V_HEADS, 1)
    lane = lax.broadcasted_iota(jnp.int32, ba.shape, 1)
    beta = 1.0 / (1.0 + jnp.exp(-ba))
    sp = ba + dtb_ref[...]
    g = -rate_ref[...] * (jnp.maximum(sp, 0.0) + jnp.log1p(jnp.exp(-jnp.abs(sp))))
    g = jnp.where(jnp.logical_and(lane >= DN_V_HEADS, lane < 2 * DN_V_HEADS), g, 0.0)
    r = lax.broadcasted_iota(jnp.int32, (tm, tm), 0)
    c = lax.broadcasted_iota(jnp.int32, (tm, tm), 1)
    tri = jnp.where(jnp.logical_and(r // DN_CHUNK == c // DN_CHUNK, c <= r), 1.0, 0.0).astype(BF16)
    g_hi = g.astype(BF16)
    g_rest = g - g_hi.astype(F32)
    g_mid = g_rest.astype(BF16)
    g_lo = (g_rest - g_mid.astype(F32)).astype(BF16)
    gc = sum(jnp.dot(tri, piece, preferred_element_type=F32) for piece in (g_hi, g_mid, g_lo))
    bg_ref[...] = jnp.where(lane < DN_V_HEADS, beta, gc)


def _unit_lower_inverses(mats):
    n = mats[0].shape[0]
    r = lax.broadcasted_iota(jnp.int32, (n, n), 0)
    c = lax.broadcasted_iota(jnp.int32, (n, n), 1)
    eye = (r == c).astype(F32)
    size = SUBLANES
    same = (r // size) == (c // size)
    d = [jnp.where(same, a, 0.0) for a in mats]
    d2 = [_mm(v, v) for v in d]
    d4 = [_mm(v, v) for v in d2]
    x = [_mm(eye - v, eye + v2) for v, v2 in zip(d, d2)]
    x = [_mm(v, eye + v4) for v, v4 in zip(x, d4)]
    while size < n:
        wider = (r // (2 * size)) == (c // (2 * size))
        ring = jnp.logical_and(wider, jnp.logical_not(same))
        xl = [_mm(v, jnp.where(ring, a, 0.0)) for v, a in zip(x, mats)]
        x = [v - _mm(vl, v) for v, vl in zip(x, xl)]
        same = wider
        size *= 2
    return x


def _dn_prep_kernel(q_ref, k_ref, v_ref, bg_ref, gct_ref, u_ref, w_ref, qk_ref, qd_ref):
    tm = q_ref.shape[0]
    ck = DN_CHUNK
    hd = DN_HEAD_DIM
    rep = DN_V_HEADS // DN_QK_HEADS
    r = lax.broadcasted_iota(jnp.int32, (ck, ck), 0)
    c = lax.broadcasted_iota(jnp.int32, (ck, ck), 1)
    lower = c <= r
    strict = c < r
    dot = functools.partial(jnp.dot, preferred_element_type=F32)
    chunks = [slice(ci * ck, (ci + 1) * ck) for ci in range(tm // ck)]
    gram = {}
    for rows in chunks:
        for hq in range(DN_QK_HEADS):
            cols = slice(hq * hd, (hq + 1) * hd)
            k = k_ref[rows, cols]
            gram[(hq, rows.start)] = lax.dot_general(jnp.concatenate([k, q_ref[rows, cols]], axis=0), k,
                                                     (((1,), (1,)), ((), ())), preferred_element_type=F32)
    problems = [(hv, rows) for rows in chunks for hv in range(DN_V_HEADS)]
    mats, rhs = [], []
    for hv, rows in problems:
        beta = bg_ref[rows, hv:hv + 1]
        gcc = bg_ref[rows, DN_V_HEADS + hv:DN_V_HEADS + hv + 1]
        gcr = gct_ref[hv:hv + 1, rows]
        decay = jnp.where(lower, jnp.exp(jnp.where(lower, gcc - gcr, 0.0)), 0.0)
        g = gram[(hv // rep, rows.start)]
        mats.append(jnp.where(strict, g[:ck] * beta * decay, 0.0))
        qk_ref[rows, hv * ck:(hv + 1) * ck] = (g[ck:] * decay).astype(BF16)
        qcols = slice((hv // rep) * hd, (hv // rep + 1) * hd)
        kf = k_ref[rows, qcols].astype(F32)
        vf = v_ref[rows, hv * hd:(hv + 1) * hd].astype(F32)
        egc = jnp.exp(gcc)
        rhs.append(jnp.concatenate([vf * beta, kf * (beta * egc)], axis=1).astype(BF16))
        qd_ref[rows, hv * hd:(hv + 1) * hd] = (q_ref[rows, qcols].astype(F32) * egc).astype(BF16)
    inverses = _unit_lower_inverses(mats)
    for (hv, rows), tinv, b in zip(problems, inverses, rhs):
        uw = dot(tinv.astype(BF16), b)
        u_ref[rows, hv * hd:(hv + 1) * hd] = uw[:, :hd].astype(BF16)
        w_ref[rows, hv * hd:(hv + 1) * hd] = uw[:, hd:].astype(BF16)


def _dn_scan_kernel(x_ref, mod_ref, u_ref, w_ref, qk_ref, qd_ref, k_ref, z_ref, bg_ref, og_ref, wout_ref,
                    fgain_ref, rw_ref, rb_ref, before_ref, o_ref, eid_ref, gate_ref, rank_ref, cnt_ref,
                    state_ref, att_ref, base_ref):
    t = pl.program_id(0)
    nb, tm, _ = x_ref.shape
    ck = DN_CHUNK
    hd = DN_HEAD_DIM
    rep = DN_V_HEADS // DN_QK_HEADS

    @pl.when(t == 0)
    def _():
        state_ref[...] = jnp.zeros_like(state_ref)

    dot = functools.partial(jnp.dot, preferred_element_type=F32)
    chains = [(bi, hv) for bi in range(nb) for hv in range(DN_V_HEADS)]
    for ci in range(tm // ck):
        rows = slice(ci * ck, (ci + 1) * ck)
        gcc, g_end, lhs = [], [], []
        for bi, hv in chains:
            gc = bg_ref[bi, rows, DN_V_HEADS + hv:DN_V_HEADS + hv + 1]
            gcc.append(gc)
            g_end.append(gc[ck - 1:ck, :])
            vcols = slice(hv * hd, (hv + 1) * hd)
            lhs.append(jnp.concatenate([w_ref[bi, rows, vcols], qd_ref[bi, rows, vcols]], axis=0))
        ws_qs = [dot(a, state_ref[n].astype(BF16)) for n, a in enumerate(lhs)]
        v_new = [(u_ref[bi, rows, hv * hd:(hv + 1) * hd].astype(F32) - m[:ck]).astype(BF16)
                 for (bi, hv), m in zip(chains, ws_qs)]
        outs = [m[ck:] + dot(qk_ref[bi, rows, hv * ck:(hv + 1) * ck], vn)
                for (bi, hv), m, vn in zip(chains, ws_qs, v_new)]
        for n, (bi, hv) in enumerate(chains):
            kf = k_ref[bi, rows, (hv // rep) * hd:(hv // rep + 1) * hd].astype(F32)
            k_dec = (kf * jnp.exp(g_end[n] - gcc[n])).astype(BF16)
            state_ref[n] = state_ref[n] * jnp.exp(g_end[n]) + lax.dot_general(
                k_dec, v_new[n], (((0,), (0,)), ((), ())), preferred_element_type=F32)
        for (bi, hv), o in zip(chains, outs):
            zf = z_ref[bi, rows, hv * hd:(hv + 1) * hd].astype(F32)
            normed = o * lax.rsqrt(_lane_sums(o * o) * (1.0 / hd) + NORM_EPS)
            att_ref[bi * tm + ci * ck:bi * tm + (ci + 1) * ck, hv * hd:(hv + 1) * hd] = (
                normed * og_ref[...] * _silu(zf)).astype(BF16)
    y = dot(att_ref[...], wout_ref[...])
    x_new, shift, scale = [], [], []
    for bi in range(nb):
        mod = mod_ref[bi]
        x_new.append(x_ref[bi] + mod[2:3] * y[bi * tm:(bi + 1) * tm])
        o_ref[bi] = x_new[bi]
        shift.append(jnp.broadcast_to(mod[3:4], x_new[bi].shape))
        scale.append(jnp.broadcast_to(mod[4:5], x_new[bi].shape))
    eid, gates, rank = _route(jnp.concatenate(x_new, axis=0), jnp.concatenate(shift, axis=0),
                              jnp.concatenate(scale, axis=0), t == 0, fgain_ref, rw_ref, rb_ref, before_ref,
                              cnt_ref, base_ref)
    for bi in range(nb):
        eid_ref[:, bi, :] = eid[:, bi * tm:(bi + 1) * tm]
        gate_ref[:, bi, :] = gates[:, bi * tm:(bi + 1) * tm]
        rank_ref[:, bi, :] = rank[:, bi * tm:(bi + 1) * tm]


def deltanet_layer(x2, mod, gain, w_in, conv_w, a_log, dt_bias, o_norm, w_out, router, b, s):
    t, d = x2.shape
    nh = DN_V_HEADS
    main_w = DN_CONV_WIDTH + DN_VAL_WIDTH
    w_main = w_in[:, :main_w].astype(BF16)
    w_ba = w_in[:, main_w:]
    w_ba_hi = w_ba.astype(BF16)
    w_ba_lo = (w_ba - w_ba_hi.astype(F32)).astype(BF16)
    w_ba2 = jnp.concatenate([w_ba_hi, w_ba_lo, jnp.zeros((d, LANES - 4 * nh), BF16)], axis=1)
    lanes_pad = lambda v: jnp.concatenate([jnp.zeros((nh,), F32), v.astype(F32),
                                           jnp.zeros((LANES - 2 * nh,), F32)]).reshape(1, LANES)
    rate = lanes_pad(jnp.exp(a_log.astype(F32)))
    dtb = lanes_pad(dt_bias)

    tm = TOKEN_TILE
    tpb = s // tm
    const2 = lambda bi, ti: (0, 0)
    tile = lambda bi, ti: (bi * tpb + ti, 0)
    q, k, v, z, bg = pl.pallas_call(
        _dn_proj_kernel,
        grid=(b, tpb),
        in_specs=[pl.BlockSpec((tm, d), tile),
                  pl.BlockSpec((1, 6, d), lambda bi, ti: (bi, 0, 0)),
                  pl.BlockSpec((1, d), const2),
                  pl.BlockSpec((d, main_w), const2),
                  pl.BlockSpec((d, LANES), const2),
                  pl.BlockSpec((DN_CONV, DN_CONV_WIDTH), const2),
                  pl.BlockSpec((1, LANES), const2),
                  pl.BlockSpec((1, LANES), const2)],
        out_specs=[pl.BlockSpec((tm, DN_KEY_WIDTH), tile), pl.BlockSpec((tm, DN_KEY_WIDTH), tile),
                   pl.BlockSpec((tm, DN_VAL_WIDTH), tile), pl.BlockSpec((tm, DN_VAL_WIDTH), tile),
                   pl.BlockSpec((tm, LANES), tile)],
        out_shape=[jax.ShapeDtypeStruct((t, DN_KEY_WIDTH), BF16), jax.ShapeDtypeStruct((t, DN_KEY_WIDTH), BF16),
                   jax.ShapeDtypeStruct((t, DN_VAL_WIDTH), BF16), jax.ShapeDtypeStruct((t, DN_VAL_WIDTH), BF16),
                   jax.ShapeDtypeStruct((t, LANES), F32)],
        scratch_shapes=[pltpu.VMEM((SUBLANES, DN_CONV_WIDTH), F32)],
        compiler_params=_params("arbitrary", "arbitrary"),
        name="deltanet_proj",
    )(x2, mod, gain.reshape(1, d), w_main, w_ba2, conv_w, rate, dtb)

    gct = bg[:, nh:2 * nh].T
    tp = DN_PREP_TILE
    rows_of = lambda width: pl.BlockSpec((tp, width), lambda i: (i, 0))
    qk_w = nh * DN_CHUNK
    u, w, qk, qd = pl.pallas_call(
        _dn_prep_kernel,
        grid=(t // tp,),
        in_specs=[rows_of(DN_KEY_WIDTH), rows_of(DN_KEY_WIDTH), rows_of(DN_VAL_WIDTH), rows_of(LANES),
                  pl.BlockSpec((nh, tp), lambda i: (0, i))],
        out_specs=[rows_of(DN_VAL_WIDTH), rows_of(DN_VAL_WIDTH), rows_of(qk_w), rows_of(DN_VAL_WIDTH)],
        out_shape=[jax.ShapeDtypeStruct((t, DN_VAL_WIDTH), BF16), jax.ShapeDtypeStruct((t, DN_VAL_WIDTH), BF16),
                   jax.ShapeDtypeStruct((t, qk_w), BF16), jax.ShapeDtypeStruct((t, DN_VAL_WIDTH), BF16)],
        compiler_params=_params("arbitrary"),
        name="deltanet_prep",
    )(q, k, v, bg, gct)

    ts = DN_SCAN_TILE
    seq = lambda a: a.reshape(b, s, a.shape[-1])
    both = lambda width: pl.BlockSpec((b, ts, width), lambda i: (0, i, 0))
    r_args, r_in = _router_operands(*router, b * ts)
    r_out, r_shapes, r_scratch = _router_results(t, pl.BlockSpec((2, b, ts), lambda i: (0, 0, i)), (2, b, s))
    out = pl.pallas_call(
        _dn_scan_kernel,
        grid=(s // ts,),
        in_specs=[both(d),
                  pl.BlockSpec((b, 6, d), lambda i: (0, 0, 0)),
                  both(DN_VAL_WIDTH), both(DN_VAL_WIDTH), both(qk_w), both(DN_VAL_WIDTH), both(DN_KEY_WIDTH),
                  both(DN_VAL_WIDTH), both(LANES),
                  pl.BlockSpec((1, DN_HEAD_DIM), lambda i: (0, 0)),
                  pl.BlockSpec((DN_VAL_WIDTH, d), lambda i: (0, 0))] + r_in,
        out_specs=[both(d)] + r_out,
        out_shape=[jax.ShapeDtypeStruct((b, s, d), F32)] + r_shapes,
        scratch_shapes=[pltpu.VMEM((b * nh, DN_HEAD_DIM, DN_HEAD_DIM), F32),
                        pltpu.VMEM((b * ts, DN_VAL_WIDTH), BF16), r_scratch],
        compiler_params=_params("arbitrary"),
        name="deltanet_scan",
    )(seq(x2), mod, seq(u), seq(w), seq(qk), seq(qd), seq(k), seq(z), seq(bg), o_norm.reshape(1, DN_HEAD_DIM),
      w_out.astype(BF16), *r_args)
    eid, gate, rank, cnt = out[1:]
    return out[0].reshape(t, d), (eid.reshape(2, t), gate.reshape(2, t), rank.reshape(2, t), cnt)


def kernel(x, c, l0_norm_mix, l0_norm_ffn, l0_ada_w, l0_ada_b, l0_gm_w_in, l0_gm_v_norm, l0_gm_w_s, l0_gm_b_s, l0_gm_w_out, l0_router_w, l0_router_b, l0_expert_w_in, l0_expert_w_out, l1_norm_mix, l1_norm_ffn, l1_ada_w, l1_ada_b, l1_dn_w_in, l1_dn_conv_w, l1_dn_a_log, l1_dn_dt_bias, l1_dn_o_norm, l1_dn_w_out, l1_router_w, l1_router_b, l1_expert_w_in, l1_expert_w_out, l2_norm_mix, l2_norm_ffn, l2_ada_w, l2_ada_b, l2_swa_w_in, l2_swa_q_norm, l2_swa_k_norm, l2_swa_sinks, l2_swa_w_out, l2_router_w, l2_router_b, l2_expert_w_in, l2_expert_w_out, l3_norm_mix, l3_norm_ffn, l3_ada_w, l3_ada_b, l3_gm_w_in, l3_gm_v_norm, l3_gm_w_s, l3_gm_b_s, l3_gm_w_out, l3_router_w, l3_router_b, l3_expert_w_in, l3_expert_w_out):
    b, s, d = x.shape
    x2 = x.reshape(b * s, d)
    tiles_per_batch_of = lambda tm: s // tm

    def modulation(ada_w, ada_b):
        return adaln(c, ada_w, ada_b).reshape(b, 6, d)

    mod = modulation(l0_ada_w, l0_ada_b)
    x2, routing = gmlp_layer(x2, mod, l0_norm_mix, l0_gm_w_in, l0_gm_v_norm, l0_gm_w_s, l0_gm_b_s, l0_gm_w_out,
                             (l0_norm_ffn, l0_router_w, l0_router_b), tiles_per_batch_of(GMLP_TILE))
    x2 = moe_layer(x2, mod, l0_norm_ffn, routing, l0_expert_w_in, l0_expert_w_out, tiles_per_batch_of)

    mod = modulation(l1_ada_w, l1_ada_b)
    x2, routing = deltanet_layer(x2, mod, l1_norm_mix, l1_dn_w_in, l1_dn_conv_w, l1_dn_a_log, l1_dn_dt_bias,
                                 l1_dn_o_norm, l1_dn_w_out, (l1_norm_ffn, l1_router_w, l1_router_b), b, s)
    x2 = moe_layer(x2, mod, l1_norm_ffn, routing, l1_expert_w_in, l1_expert_w_out, tiles_per_batch_of)

    mod = modulation(l2_ada_w, l2_ada_b)
    x2, routing = swa_layer(x2, mod, l2_norm_mix, l2_swa_w_in, l2_swa_q_norm, l2_swa_k_norm, l2_swa_sinks,
                            l2_swa_w_out, (l2_norm_ffn, l2_router_w, l2_router_b), b, s)
    x2 = moe_layer(x2, mod, l2_norm_ffn, routing, l2_expert_w_in, l2_expert_w_out, tiles_per_batch_of)

    mod = modulation(l3_ada_w, l3_ada_b)
    x2, routing = gmlp_layer(x2, mod, l3_norm_mix, l3_gm_w_in, l3_gm_v_norm, l3_gm_w_s, l3_gm_b_s, l3_gm_w_out,
                             (l3_norm_ffn, l3_router_w, l3_router_b), tiles_per_batch_of(GMLP_TILE))
    x2 = moe_layer(x2, mod, l3_norm_ffn, routing, l3_expert_w_in, l3_expert_w_out, tiles_per_batch_of)
    return x2.reshape(b, s, d)
```
